```python
import jax, jax.numpy as jnp
from jax import lax
import numpy as np

D_MODEL = 4096
BATCH = 2
SEQ = 4096
DEPTH = 2

HEAD_DIM = 128
ROPE_THETA = 10000.0
NORM_EPS = 1e-6
NEG = -1e30
MAX_POS_OFFSET = 4096
N_BRANCH = 4
BRANCH_WIDTH = D_MODEL // 4

MOBA_HEADS = BRANCH_WIDTH // HEAD_DIM
MOBA_BLOCK = 256
MOBA_TOPK = 3
MOBA_Q_BLOCK = 64
LRU_WIDTH = BRANCH_WIDTH
LRU_HEADS = LRU_WIDTH // HEAD_DIM
LRU_CONV = 4
LRU_C = 8.0
LRU_A_MIN = 0.9
LRU_A_MAX = 0.999
SC_WIDTH = BRANCH_WIDTH
SC_CONV = 3
NSA_HEADS = BRANCH_WIDTH // HEAD_DIM
NSA_KV_HEADS = NSA_HEADS // 4
NSA_CMP_BLOCK = 32
NSA_CMP_STRIDE = 16
NSA_CMP_HIDDEN = HEAD_DIM
NSA_SEL_BLOCK = 64
NSA_SEL_TOPK = 16
NSA_WINDOW = 512
NSA_FORCE_BONUS = 1e4
Q_BLOCK = 128
MEM_LEN = 256
XA_HEADS = 4
XA_WIDTH = XA_HEADS * HEAD_DIM
N_GROUPS = 4
EXPERTS_PER_GROUP = 8
N_EXPERTS = N_GROUPS * EXPERTS_PER_GROUP
TOPK_IN_GROUP = 2
EXPERT_FF = D_MODEL // 8

MOBA_W = MOBA_HEADS * HEAD_DIM
NSA_Q_W = NSA_HEADS * HEAD_DIM
NSA_KV_W = NSA_KV_HEADS * HEAD_DIM
NSA_GATE_W = NSA_HEADS * 3
MIX_SPLITS = (MOBA_W, MOBA_W, MOBA_W,
              NSA_Q_W, NSA_KV_W, NSA_KV_W, NSA_KV_W, NSA_KV_W, NSA_KV_W, NSA_KV_W, NSA_GATE_W,
              LRU_WIDTH, LRU_WIDTH,
              SC_WIDTH, SC_WIDTH, SC_WIDTH)
MIX_IN_WIDTH = 3 * MOBA_W + NSA_Q_W + 6 * NSA_KV_W + NSA_GATE_W + 2 * LRU_WIDTH + 3 * SC_WIDTH

kernel_name = 'hybrid_moba_rglru_shortconv_nsa_hiermoe'

F32 = jnp.float32


def rms_norm(x, g):
    xf = x.astype(F32)
    y = xf * lax.rsqrt(jnp.mean(xf * xf, axis=-1, keepdims=True) + NORM_EPS)
    return (y * g.astype(F32)).astype(x.dtype)


def rope_tables(positions):
    inv = ROPE_THETA ** (-jnp.arange(0, HEAD_DIM, 2, dtype=F32) / HEAD_DIM)
    ang = positions.astype(F32)[..., None] * inv
    return jnp.cos(ang)[:, :, None, :], jnp.sin(ang)[:, :, None, :]


def rope(x, cos, sin):
    x1, x2 = jnp.split(x.astype(F32), 2, axis=-1)
    return jnp.concatenate([x1 * cos - x2 * sin, x2 * cos + x1 * sin], axis=-1).astype(x.dtype)


def split_heads(t):
    return t.reshape(t.shape[0], t.shape[1], -1, HEAD_DIM)


def split_cols(z):
    return jnp.split(z, [int(v) for v in np.cumsum(MIX_SPLITS)[:-1]], axis=-1)


def causal_depthwise_conv(x, w):
    width, c = w.shape
    xp = jnp.pad(x, ((0, 0), (width - 1, 0), (0, 0)))
    return lax.conv_general_dilated(xp, w[:, None, :].astype(x.dtype), (1,), 'VALID',
                                    dimension_numbers=('NWC', 'WIO', 'NWC'),
                                    feature_group_count=c)


def moba_attention(q, k, v):
    B, S, H, dh = q.shape
    bs = MOBA_BLOCK
    nb = -(-S // bs)
    scale = dh ** -0.5
    qh, kh, vh = (t.transpose(0, 2, 1, 3) for t in (q, k, v))
    pad = nb * bs - S
    kp = jnp.pad(kh, ((0, 0), (0, 0), (0, pad), (0, 0)))
    vp = jnp.pad(vh, ((0, 0), (0, 0), (0, pad), (0, 0)))
    kb = kp.reshape(B, H, nb, bs, dh)
    vb = vp.reshape(B, H, nb, bs, dh)
    k_mean = jnp.mean(kb.astype(F32), axis=3)
    t = jnp.arange(S)
    q_blk = t // bs
    gate = jnp.einsum('bhsd,bhnd->bhsn', qh.astype(F32), k_mean)
    past = jnp.arange(nb)[None, :] < q_blk[:, None]
    gate = jnp.where(past, gate, NEG)
    kk = min(MOBA_TOPK, nb)
    _, sel = lax.top_k(gate, kk)
    sel_ok = sel < q_blk[:, None]
    qb = MOBA_Q_BLOCK
    nqb = S // qb
    q_c = qh.reshape(B, H, nqb, qb, dh).transpose(2, 0, 1, 3, 4)
    sel_c = sel.reshape(B, H, nqb, qb, kk).transpose(2, 0, 1, 3, 4)
    ok_c = sel_ok.reshape(B, H, nqb, qb, kk).transpose(2, 0, 1, 3, 4)
    bi = jnp.arange(B)[:, None, None, None]
    hi = jnp.arange(H)[None, :, None, None]

    def chunk(args):
        qc, selc, okc, ci = args
        tq = ci * qb + jnp.arange(qb)
        own = (ci * qb) // bs * bs
        k_own = lax.dynamic_slice_in_dim(kp, own, bs, axis=2)
        v_own = lax.dynamic_slice_in_dim(vp, own, bs, axis=2)
        own_mask = (own + jnp.arange(bs))[None, :] <= tq[:, None]
        k_s = kb[bi, hi, selc].reshape(B, H, qb, kk * bs, dh)
        v_s = vb[bi, hi, selc].reshape(B, H, qb, kk * bs, dh)
        s_sel = jnp.einsum('bhqd,bhqkd->bhqk', qc, k_s, preferred_element_type=F32) * scale
        s_sel = jnp.where(jnp.repeat(okc, bs, axis=-1), s_sel, NEG)
        s_own = jnp.einsum('bhqd,bhkd->bhqk', qc, k_own, preferred_element_type=F32) * scale
        s_own = jnp.where(own_mask, s_own, NEG)
        p = jax.nn.softmax(jnp.concatenate([s_sel, s_own], axis=-1), axis=-1).astype(v.dtype)
        return (jnp.einsum('bhqk,bhqkd->bhqd', p[..., :kk * bs], v_s)
                + jnp.einsum('bhqk,bhkd->bhqd', p[..., kk * bs:], v_own))

    o = lax.map(chunk, (q_c, sel_c, ok_c, jnp.arange(nqb)))
    return o.transpose(1, 0, 3, 2, 4).reshape(B, S, H * dh)


def compress_tokens(x, pe, w1, w2):
    B, S, G, dh = x.shape
    nc = (S - NSA_CMP_BLOCK) // NSA_CMP_STRIDE + 1
    idx = jnp.arange(nc)[:, None] * NSA_CMP_STRIDE + jnp.arange(NSA_CMP_BLOCK)[None, :]
    blocks = x[:, idx] + pe[:, None, :]
    flat = blocks.transpose(0, 1, 3, 2, 4).reshape(B, nc, G, NSA_CMP_BLOCK * dh)
    return jax.nn.gelu(flat @ w1) @ w2


def cmp_to_sel_weights(nc, ns):
    r = NSA_SEL_BLOCK // NSA_CMP_STRIDE
    m = NSA_CMP_BLOCK // NSA_CMP_STRIDE
    c = jnp.arange(nc)[:, None] - r * jnp.arange(ns)[None, :]
    w = jnp.minimum(jnp.minimum(c + 1, r + m - 1 - c), min(r, m))
    return jnp.clip(w, 0, None).astype(F32)


def nsa_attention(q, k_cmp, v_cmp, k_sel, v_sel, k_win, v_win, gate_logits,
                  pe_k, w1_k, w2_k, pe_v, w1_v, w2_v):
    B, S, H, dh = q.shape
    G = k_sel.shape[2]
    R = H // G
    scale = dh ** -0.5
    t = jnp.arange(S)
    qg = q.reshape(B, S, G, R, dh)
    kc = compress_tokens(k_cmp, pe_k, w1_k, w2_k)
    vc = compress_tokens(v_cmp, pe_v, w1_v, w2_v)
    nc = kc.shape[1]
    s_c = jnp.einsum('bsgrd,bngd->bgrsn', qg, kc, preferred_element_type=F32) * scale
    c_end = jnp.arange(nc) * NSA_CMP_STRIDE + NSA_CMP_BLOCK - 1
    c_mask = c_end[None, :] <= t[:, None]
    p_c = jax.nn.softmax(jnp.where(c_mask, s_c, NEG), axis=-1) * c_mask
    o_cmp = jnp.einsum('bgrsn,bngd->bsgrd', p_c.astype(vc.dtype), vc).reshape(B, S, H, dh)
    ns = S // NSA_SEL_BLOCK
    imp = jnp.einsum('bgrsn,nm->bgsm', p_c, cmp_to_sel_weights(nc, ns))
    q_blk = t // NSA_SEL_BLOCK
    j = jnp.arange(ns)[None, :]
    valid = j <= q_blk[:, None]
    forced = (j == 0) | (j >= q_blk[:, None] - 1)
    imp = jnp.where(valid, imp + jnp.where(forced, NSA_FORCE_BONUS, 0.0), NEG)
    kk = min(NSA_SEL_TOPK, ns)
    _, sel = lax.top_k(imp, kk)
    sel_ok = sel <= q_blk[:, None]
    sb = NSA_SEL_BLOCK
    k_blocks = k_sel.transpose(0, 2, 1, 3).reshape(B, G, ns, sb, dh)
    v_blocks = v_sel.transpose(0, 2, 1, 3).reshape(B, G, ns, sb, dh)
    wpad = ((0, 0), (0, 0), (NSA_WINDOW, 0), (0, 0))
    k_wp = jnp.pad(k_win.transpose(0, 2, 1, 3), wpad)
    v_wp = jnp.pad(v_win.transpose(0, 2, 1, 3), wpad)
    nqb = S // Q_BLOCK
    q_c = qg.transpose(0, 2, 3, 1, 4).reshape(B, G, R, nqb, Q_BLOCK, dh).transpose(3, 0, 1, 2, 4, 5)
    sel_c = sel.reshape(B, G, nqb, Q_BLOCK, kk).transpose(2, 0, 1, 3, 4)
    ok_c = sel_ok.reshape(B, G, nqb, Q_BLOCK, kk).transpose(2, 0, 1, 3, 4)
    bi = jnp.arange(B)[:, None, None, None]
    gi = jnp.arange(G)[None, :, None, None]

    def chunk(args):
        qc, selc, okc, ci = args
        tq = ci * Q_BLOCK + jnp.arange(Q_BLOCK)
        ks = k_blocks[bi, gi, selc].reshape(B, G, Q_BLOCK, kk * sb, dh)
        vs = v_blocks[bi, gi, selc].reshape(B, G, Q_BLOCK, kk * sb, dh)
        kpos = (selc[..., None] * sb + jnp.arange(sb)).reshape(B, G, Q_BLOCK, kk * sb)
        m = (kpos <= tq[:, None]) & jnp.repeat(okc, sb, axis=-1)
        s = jnp.einsum('bgrqd,bgqkd->bgrqk', qc, ks, preferred_element_type=F32) * scale
        p = jax.nn.softmax(jnp.where(m[:, :, None], s, NEG), axis=-1)
        o_s = jnp.einsum('bgrqk,bgqkd->bgrqd', p.astype(vs.dtype), vs)
        kw = lax.dynamic_slice_in_dim(k_wp, ci * Q_BLOCK, Q_BLOCK + NSA_WINDOW, axis=2)
        vw = lax.dynamic_slice_in_dim(v_wp, ci * Q_BLOCK, Q_BLOCK + NSA_WINDOW, axis=2)
        wpos = ci * Q_BLOCK - NSA_WINDOW + jnp.arange(Q_BLOCK + NSA_WINDOW)
        dist = tq[:, None] - wpos[None, :]
        wm = (dist >= 0) & (dist < NSA_WINDOW) & (wpos[None, :] >= 0)
        s_w = jnp.einsum('bgrqd,bgkd->bgrqk', qc, kw, preferred_element_type=F32) * scale
        p_w = jax.nn.softmax(jnp.where(wm, s_w, NEG), axis=-1)
        o_w = jnp.einsum('bgrqk,bgkd->bgrqd', p_w.astype(vw.dtype), vw)
        return o_s, o_w

    o_sel, o_win = lax.map(chunk, (q_c, sel_c, ok_c, jnp.arange(nqb)))
    o_sel = o_sel.transpose(1, 0, 4, 2, 3, 5).reshape(B, S, H, dh)
    o_win = o_win.transpose(1, 0, 4, 2, 3, 5).reshape(B, S, H, dh)
    g = jax.nn.sigmoid(gate_logits.astype(F32)).reshape(B, S, H, 3)
    out = g[..., 0:1] * o_cmp + g[..., 1:2] * o_sel + g[..., 2:3] * o_win
    return out.reshape(B, S, H * dh).astype(q.dtype)


def rg_lru_branch(x_in, gate_in, conv_w, conv_b, w_a, b_a, w_i, b_i, lam):
    B, S, C = x_in.shape
    xc = (causal_depthwise_conv(x_in, conv_w) + conv_b).astype(F32)
    xh = xc.reshape(B, S, LRU_HEADS, C // LRU_HEADS)
    r = jax.nn.sigmoid(jnp.einsum('bshi,hij->bshj', xh, w_a.astype(F32)) + b_a).reshape(B, S, C)
    i = jax.nn.sigmoid(jnp.einsum('bshi,hij->bshj', xh, w_i.astype(F32)) + b_i).reshape(B, S, C)
    log_a = -LRU_C * r * jax.nn.softplus(-lam.astype(F32))
    a = jnp.exp(log_a)
    b = jnp.sqrt(-jnp.expm1(2.0 * log_a)) * (i * xc)

    def combine(c1, c2):
        a1, b1 = c1
        a2, b2 = c2
        return a1 * a2, a2 * b1 + b2

    _, h = lax.associative_scan(combine, (a, b), axis=1)
    return (h * jax.nn.gelu(gate_in.astype(F32))).astype(x_in.dtype)


def short_conv_branch(x_in, b_gate, c_gate, conv_w):
    return b_gate * causal_depthwise_conv(c_gate * x_in, conv_w)


def cross_attention(h, mem_n, w_q, w_k, w_v, w_o):
    B, S, _ = h.shape
    M = mem_n.shape[1]
    q = (h @ w_q).reshape(B, S, XA_HEADS, HEAD_DIM)
    k = (mem_n @ w_k).reshape(B, M, XA_HEADS, HEAD_DIM)
    v = (mem_n @ w_v).reshape(B, M, XA_HEADS, HEAD_DIM)
    s = jnp.einsum('bshd,bmhd->bhsm', q, k, preferred_element_type=F32) * HEAD_DIM ** -0.5
    p = jax.nn.softmax(s, axis=-1).astype(v.dtype)
    o = jnp.einsum('bhsm,bmhd->bshd', p, v).reshape(B, S, XA_WIDTH)
    return o @ w_o


def hierarchical_moe(h, w_group, b_group, w_expert, b_expert, w_gate, w_up, w_down):
    B, S, D = h.shape
    t = h.reshape(B * S, D)
    g_logits = (t @ w_group + b_group).astype(F32)
    g_prob = jax.nn.softmax(g_logits, axis=-1)
    g_sel = jnp.argmax(g_logits, axis=-1)
    e_logits = (t @ w_expert + b_expert).astype(F32).reshape(-1, N_GROUPS, EXPERTS_PER_GROUP)
    e_in = jnp.take_along_axis(e_logits, g_sel[:, None, None], axis=1)[:, 0]
    top_v, top_i = lax.top_k(e_in, TOPK_IN_GROUP)
    p_g = jnp.take_along_axis(g_prob, g_sel[:, None], axis=1)
    w_top = jax.nn.softmax(top_v, axis=-1) * p_g
    eid = g_sel[:, None] * EXPERTS_PER_GROUP + top_i
    combine = jnp.sum(jax.nn.one_hot(eid, N_EXPERTS, dtype=F32) * w_top[..., None], axis=1)
    hid = jax.nn.silu(jnp.einsum('td,edf->tef', t, w_gate)) * jnp.einsum('td,edf->tef', t, w_up)
    out = jnp.einsum('tef,efd->td', hid * combine[:, :, None].astype(hid.dtype), w_down)
    return out.reshape(B, S, D)


def setup_inputs(seed: int = 0) -> dict:
    key = jax.random.key(seed)
    keys = iter(jax.random.split(key, 48))

    def normal(shape, scale):
        return jax.random.normal(next(keys), shape, F32) * scale

    def gain(shape):
        return 1.0 + normal(shape, 0.02)

    L, D, dh = DEPTH, D_MODEL, HEAD_DIM
    lb = LRU_WIDTH // LRU_HEADS
    a0 = jax.random.uniform(next(keys), (L, LRU_WIDTH), F32, LRU_A_MIN, LRU_A_MAX)
    a_base = a0 ** (1.0 / LRU_C)
    start = jax.random.randint(next(keys), (BATCH, 1), 0, MAX_POS_OFFSET, dtype=jnp.int32)
    cmp_in = NSA_CMP_BLOCK * dh
    return {
        'x': normal((BATCH, SEQ, D), 1.0),
        'mem': normal((BATCH, MEM_LEN, D), 1.0),
        'positions': start + jnp.arange(SEQ, dtype=jnp.int32)[None, :],
        'norm_mix': gain((L, D)),
        'w_mix_in': normal((L, D, MIX_IN_WIDTH), D ** -0.5),
        'lru_conv_w': normal((L, LRU_CONV, LRU_WIDTH), LRU_CONV ** -0.5),
        'lru_conv_b': normal((L, LRU_WIDTH), 0.02),
        'lru_w_a': normal((L, LRU_HEADS, lb, lb), lb ** -0.5),
        'lru_b_a': normal((L, LRU_HEADS, lb), 0.02),
        'lru_w_i': normal((L, LRU_HEADS, lb, lb), lb ** -0.5),
        'lru_b_i': normal((L, LRU_HEADS, lb), 0.02),
        'lru_lambda': jnp.log(a_base) - jnp.log1p(-a_base),
        'sc_conv_w': normal((L, SC_CONV, SC_WIDTH), SC_CONV ** -0.5),
        'nsa_pe_k': normal((L, NSA_CMP_BLOCK, dh), 0.02),
        'nsa_w1_k': normal((L, cmp_in, NSA_CMP_HIDDEN), cmp_in ** -0.5),
        'nsa_w2_k': normal((L, NSA_CMP_HIDDEN, dh), NSA_CMP_HIDDEN ** -0.5),
        'nsa_pe_v': normal((L, NSA_CMP_BLOCK, dh), 0.02),
        'nsa_w1_v': normal((L, cmp_in, NSA_CMP_HIDDEN), cmp_in ** -0.5),
        'nsa_w2_v': normal((L, NSA_CMP_HIDDEN, dh), NSA_CMP_HIDDEN ** -0.5),
        'w_merge_gate': normal((L, N_BRANCH, D, D), D ** -0.5),
        'b_merge_gate': normal((L, N_BRANCH, D), 0.02),
        'w_branch_out': normal((L, N_BRANCH, BRANCH_WIDTH, D), BRANCH_WIDTH ** -0.5),
        'w_mix_out': normal((L, D, D), D ** -0.5),
        'norm_xattn': gain((L, D)),
        'norm_mem': gain((L, D)),
        'xa_w_q': normal((L, D, XA_WIDTH), D ** -0.5),
        'xa_w_k': normal((L, D, XA_WIDTH), D ** -0.5),
        'xa_w_v': normal((L, D, XA_WIDTH), D ** -0.5),
        'xa_w_o': normal((L, XA_WIDTH, D), XA_WIDTH ** -0.5),
        'norm_moe': gain((L, D)),
        'moe_w_group': normal((L, D, N_GROUPS), D ** -0.5),
        'moe_b_group': normal((L, N_GROUPS), 0.01),
        'moe_w_expert': normal((L, D, N_EXPERTS), D ** -0.5),
        'moe_b_expert': normal((L, N_EXPERTS), 0.01),
        'moe_w_gate': normal((L, N_EXPERTS, D, EXPERT_FF), D ** -0.5),
        'moe_w_up': normal((L, N_EXPERTS, D, EXPERT_FF), D ** -0.5),
        'moe_w_down': normal((L, N_EXPERTS, EXPERT_FF, D), EXPERT_FF ** -0.5),
        'norm_final': gain((D,)),
    }


def reference(x, mem, positions, norm_mix, w_mix_in, lru_conv_w, lru_conv_b, lru_w_a, lru_b_a,
              lru_w_i, lru_b_i, lru_lambda, sc_conv_w, nsa_pe_k, nsa_w1_k, nsa_w2_k, nsa_pe_v,
              nsa_w1_v, nsa_w2_v, w_merge_gate, b_merge_gate, w_branch_out, w_mix_out,
              norm_xattn, norm_mem, xa_w_q, xa_w_k, xa_w_v, xa_w_o, norm_moe, moe_w_group,
              moe_b_group, moe_w_expert, moe_b_expert, moe_w_gate, moe_w_up, moe_w_down,
              norm_final):
    cos, sin = rope_tables(positions)
    for l in range(DEPTH):
        h = rms_norm(x, norm_mix[l])
        (a_q, a_k, a_v, n_q, n_kc, n_vc, n_ks, n_vs, n_kw, n_vw, n_g,
         r_x, r_g, c_b, c_c, c_x) = split_cols(h @ w_mix_in[l])
        o_moba = moba_attention(rope(split_heads(a_q), cos, sin),
                                rope(split_heads(a_k), cos, sin), split_heads(a_v))
        o_lru = rg_lru_branch(r_x, r_g, lru_conv_w[l], lru_conv_b[l], lru_w_a[l], lru_b_a[l],
                              lru_w_i[l], lru_b_i[l], lru_lambda[l])
        o_conv = short_conv_branch(c_x, c_b, c_c, sc_conv_w[l])
        o_nsa = nsa_attention(rope(split_heads(n_q), cos, sin),
                              rope(split_heads(n_kc), cos, sin), split_heads(n_vc),
                              rope(split_heads(n_ks), cos, sin), split_heads(n_vs),
                              rope(split_heads(n_kw), cos, sin), split_heads(n_vw), n_g,
                              nsa_pe_k[l], nsa_w1_k[l], nsa_w2_k[l],
                              nsa_pe_v[l], nsa_w1_v[l], nsa_w2_v[l])
        merged = jnp.zeros_like(x)
        for n, o in enumerate((o_moba, o_lru, o_conv, o_nsa)):
            gate = jax.nn.sigmoid(h @ w_merge_gate[l, n] + b_merge_gate[l, n])
            merged = merged + gate * (o @ w_branch_out[l, n])
        x = x + (merged @ w_mix_out[l]).astype(x.dtype)
        x = x + cross_attention(rms_norm(x, norm_xattn[l]), rms_norm(mem, norm_mem[l]),
                                xa_w_q[l], xa_w_k[l], xa_w_v[l], xa_w_o[l]).astype(x.dtype)
        x = x + hierarchical_moe(rms_norm(x, norm_moe[l]), moe_w_group[l], moe_b_group[l],
                                 moe_w_expert[l], moe_b_expert[l], moe_w_gate[l],
                                 moe_w_up[l], moe_w_down[l]).astype(x.dtype)
    return rms_norm(x, norm_final)
```

```python
import functools

import jax
import jax.numpy as jnp
import numpy as np
from jax import lax
from jax.experimental import pallas as pl
from jax.experimental.pallas import tpu as pltpu

F32 = jnp.float32
BF16 = jnp.bfloat16

D_MODEL = 4096
DEPTH = 2
HEAD_DIM = 128
ROPE_THETA = 10000.0
NORM_EPS = 1e-6
NEG = -1e30
N_BRANCH = 4
BRANCH_WIDTH = D_MODEL // 4
MOBA_HEADS = BRANCH_WIDTH // HEAD_DIM
MOBA_BLOCK = 256
MOBA_TOPK = 3
MOBA_Q_BLOCK = 64
LRU_WIDTH = BRANCH_WIDTH
LRU_HEADS = LRU_WIDTH // HEAD_DIM
LRU_C = 8.0
SC_WIDTH = BRANCH_WIDTH
NSA_HEADS = BRANCH_WIDTH // HEAD_DIM
NSA_KV_HEADS = NSA_HEADS // 4
NSA_CMP_BLOCK = 32
NSA_CMP_STRIDE = 16
NSA_SEL_BLOCK = 64
NSA_SEL_TOPK = 16
NSA_WINDOW = 512
NSA_FORCE_BONUS = 1e4
Q_BLOCK = 128
XA_HEADS = 4
XA_WIDTH = XA_HEADS * HEAD_DIM
N_GROUPS = 4
EXPERTS_PER_GROUP = 8
N_EXPERTS = N_GROUPS * EXPERTS_PER_GROUP
TOPK_IN_GROUP = 2
EXPERT_FF = D_MODEL // 8
MOBA_W = MOBA_HEADS * HEAD_DIM
NSA_Q_W = NSA_HEADS * HEAD_DIM
NSA_KV_W = NSA_KV_HEADS * HEAD_DIM
NSA_GATE_W = NSA_HEADS * 3
MIX_SPLITS = (MOBA_W, MOBA_W, MOBA_W,
              NSA_Q_W, NSA_KV_W, NSA_KV_W, NSA_KV_W, NSA_KV_W, NSA_KV_W, NSA_KV_W, NSA_GATE_W,
              LRU_WIDTH, LRU_WIDTH,
              SC_WIDTH, SC_WIDTH, SC_WIDTH)

VMEM_LIMIT_BYTES = 56 * 1024 * 1024


def _mm_kernel(a_ref, w_ref, o_ref):
    o_ref[...] = jnp.dot(a_ref[...].astype(BF16), w_ref[...].astype(BF16),
                         preferred_element_type=F32)


def matmul(a, w, *, tm=1024, tn=512):
    m, k = a.shape
    k2, n = w.shape
    assert k == k2
    tm = min(tm, m)
    tn = min(tn, n)
    return pl.pallas_call(
        _mm_kernel,
        grid=(pl.cdiv(m, tm), pl.cdiv(n, tn)),
        in_specs=[pl.BlockSpec((tm, k), lambda i, j: (i, 0)),
                  pl.BlockSpec((k, tn), lambda i, j: (0, j))],
        out_specs=pl.BlockSpec((tm, tn), lambda i, j: (i, j)),
        out_shape=jax.ShapeDtypeStruct((m, n), F32),
        compiler_params=pltpu.CompilerParams(
            dimension_semantics=("parallel", "parallel"),
            vmem_limit_bytes=VMEM_LIMIT_BYTES),
        name="matmul",
    )(a, w)


def dense(x, w):
    lead = x.shape[:-1]
    y = matmul(x.reshape(-1, x.shape[-1]).astype(BF16), w.astype(BF16))
    return y.reshape(*lead, w.shape[-1])


def rms_norm(x, g):
    xf = x.astype(F32)
    y = xf * lax.rsqrt(jnp.mean(xf * xf, axis=-1, keepdims=True) + NORM_EPS)
    return (y * g.astype(F32)).astype(x.dtype)


def rope_tables(positions):
    inv = ROPE_THETA ** (-jnp.arange(0, HEAD_DIM, 2, dtype=F32) / HEAD_DIM)
    ang = positions.astype(F32)[..., None] * inv
    return jnp.cos(ang)[:, :, None, :], jnp.sin(ang)[:, :, None, :]


def rope(x, cos, sin):
    x1, x2 = jnp.split(x.astype(F32), 2, axis=-1)
    return jnp.concatenate([x1 * cos - x2 * sin, x2 * cos + x1 * sin], axis=-1).astype(x.dtype)


def split_heads(t):
    return t.reshape(t.shape[0], t.shape[1], -1, HEAD_DIM)


def split_cols(z):
    return jnp.split(z, [int(v) for v in np.cumsum(MIX_SPLITS)[:-1]], axis=-1)


def causal_depthwise_conv(x, w):
    width, c = w.shape
    xp = jnp.pad(x, ((0, 0), (width - 1, 0), (0, 0)))
    return lax.conv_general_dilated(xp, w[:, None, :].astype(x.dtype), (1,), 'VALID',
                                    dimension_numbers=('NWC', 'WIO', 'NWC'),
                                    feature_group_count=c)


def moba_attention(q, k, v):
    B, S, H, dh = q.shape
    bs = MOBA_BLOCK
    nb = -(-S // bs)
    scale = dh ** -0.5
    qh, kh, vh = (t.transpose(0, 2, 1, 3) for t in (q, k, v))
    pad = nb * bs - S
    kp = jnp.pad(kh, ((0, 0), (0, 0), (0, pad), (0, 0)))
    vp = jnp.pad(vh, ((0, 0), (0, 0), (0, pad), (0, 0)))
    kb = kp.reshape(B, H, nb, bs, dh)
    vb = vp.reshape(B, H, nb, bs, dh)
    k_mean = jnp.mean(kb.astype(F32), axis=3)
    t = jnp.arange(S)
    q_blk = t // bs
    gate = jnp.einsum('bhsd,bhnd->bhsn', qh.astype(F32), k_mean)
    past = jnp.arange(nb)[None, :] < q_blk[:, None]
    gate = jnp.where(past, gate, NEG)
    kk = min(MOBA_TOPK, nb)
    _, sel = lax.top_k(gate, kk)
    sel_ok = sel < q_blk[:, None]
    qb = MOBA_Q_BLOCK
    nqb = S // qb
    q_c = qh.reshape(B, H, nqb, qb, dh).transpose(2, 0, 1, 3, 4)
    sel_c = sel.reshape(B, H, nqb, qb, kk).transpose(2, 0, 1, 3, 4)
    ok_c = sel_ok.reshape(B, H, nqb, qb, kk).transpose(2, 0, 1, 3, 4)
    bi = jnp.arange(B)[:, None, None, None]
    hi = jnp.arange(H)[None, :, None, None]

    def chunk(args):
        qc, selc, okc, ci = args
        tq = ci * qb + jnp.arange(qb)
        own = (ci * qb) // bs * bs
        k_own = lax.dynamic_slice_in_dim(kp, own, bs, axis=2)
        v_own = lax.dynamic_slice_in_dim(vp, own, bs, axis=2)
        own_mask = (own + jnp.arange(bs))[None, :] <= tq[:, None]
        k_s = kb[bi, hi, selc].reshape(B, H, qb, kk * bs, dh)
        v_s = vb[bi, hi, selc].reshape(B, H, qb, kk * bs, dh)
        s_sel = jnp.einsum('bhqd,bhqkd->bhqk', qc, k_s, preferred_element_type=F32) * scale
        s_sel = jnp.where(jnp.repeat(okc, bs, axis=-1), s_sel, NEG)
        s_own = jnp.einsum('bhqd,bhkd->bhqk', qc, k_own, preferred_element_type=F32) * scale
        s_own = jnp.where(own_mask, s_own, NEG)
        p = jax.nn.softmax(jnp.concatenate([s_sel, s_own], axis=-1), axis=-1).astype(v.dtype)
        return (jnp.einsum('bhqk,bhqkd->bhqd', p[..., :kk * bs], v_s)
                + jnp.einsum('bhqk,bhkd->bhqd', p[..., kk * bs:], v_own))

    o = lax.map(chunk, (q_c, sel_c, ok_c, jnp.arange(nqb)))
    return o.transpose(1, 0, 3, 2, 4).reshape(B, S, H * dh)


def compress_tokens(x, pe, w1, w2):
    B, S, G, dh = x.shape
    nc = (S - NSA_CMP_BLOCK) // NSA_CMP_STRIDE + 1
    idx = jnp.arange(nc)[:, None] * NSA_CMP_STRIDE + jnp.arange(NSA_CMP_BLOCK)[None, :]
    blocks = x[:, idx] + pe[:, None, :]
    flat = blocks.transpose(0, 1, 3, 2, 4).reshape(B, nc, G, NSA_CMP_BLOCK * dh)
    return jax.nn.gelu(flat @ w1) @ w2


def cmp_to_sel_weights(nc, ns):
    r = NSA_SEL_BLOCK // NSA_CMP_STRIDE
    m = NSA_CMP_BLOCK // NSA_CMP_STRIDE
    c = jnp.arange(nc)[:, None] - r * jnp.arange(ns)[None, :]
    w = jnp.minimum(jnp.minimum(c + 1, r + m - 1 - c), min(r, m))
    return jnp.clip(w, 0, None).astype(F32)


def nsa_attention(q, k_cmp, v_cmp, k_sel, v_sel, k_win, v_win, gate_logits,
                  pe_k, w1_k, w2_k, pe_v, w1_v, w2_v):
    B, S, H, dh = q.shape
    G = k_sel.shape[2]
    R = H // G
    scale = dh ** -0.5
    t = jnp.arange(S)
    qg = q.reshape(B, S, G, R, dh)
    kc = compress_tokens(k_cmp, pe_k, w1_k, w2_k)
    vc = compress_tokens(v_cmp, pe_v, w1_v, w2_v)
    nc = kc.shape[1]
    s_c = jnp.einsum('bsgrd,bngd->bgrsn', qg, kc, preferred_element_type=F32) * scale
    c_end = jnp.arange(nc) * NSA_CMP_STRIDE + NSA_CMP_BLOCK - 1
    c_mask = c_end[None, :] <= t[:, None]
    p_c = jax.nn.softmax(jnp.where(c_mask, s_c, NEG), axis=-1) * c_mask
    o_cmp = jnp.einsum('bgrsn,bngd->bsgrd', p_c.astype(vc.dtype), vc).reshape(B, S, H, dh)
    ns = S // NSA_SEL_BLOCK
    imp = jnp.einsum('bgrsn,nm->bgsm', p_c, cmp_to_sel_weights(nc, ns))
    q_blk = t // NSA_SEL_BLOCK
    j = jnp.arange(ns)[None, :]
    valid = j <= q_blk[:, None]
    forced = (j == 0) | (j >= q_blk[:, None] - 1)
    imp = jnp.where(valid, imp + jnp.where(forced, NSA_FORCE_BONUS, 0.0), NEG)
    kk = min(NSA_SEL_TOPK, ns)
    _, sel = lax.top_k(imp, kk)
    sel_ok = sel <= q_blk[:, None]
    sb = NSA_SEL_BLOCK
    k_blocks = k_sel.transpose(0, 2, 1, 3).reshape(B, G, ns, sb, dh)
    v_blocks = v_sel.transpose(0, 2, 1, 3).reshape(B, G, ns, sb, dh)
    wpad = ((0, 0), (0, 0), (NSA_WINDOW, 0), (0, 0))
    k_wp = jnp.pad(k_win.transpose(0, 2, 1, 3), wpad)
    v_wp = jnp.pad(v_win.transpose(0, 2, 1, 3), wpad)
    nqb = S // Q_BLOCK
    q_c = qg.transpose(0, 2, 3, 1, 4).reshape(B, G, R, nqb, Q_BLOCK, dh).transpose(3, 0, 1, 2, 4, 5)
    sel_c = sel.reshape(B, G, nqb, Q_BLOCK, kk).transpose(2, 0, 1, 3, 4)
    ok_c = sel_ok.reshape(B, G, nqb, Q_BLOCK, kk).transpose(2, 0, 1, 3, 4)
    bi = jnp.arange(B)[:, None, None, None]
    gi = jnp.arange(G)[None, :, None, None]

    def chunk(args):
        qc, selc, okc, ci = args
        tq = ci * Q_BLOCK + jnp.arange(Q_BLOCK)
        ks = k_blocks[bi, gi, selc].reshape(B, G, Q_BLOCK, kk * sb, dh)
        vs = v_blocks[bi, gi, selc].reshape(B, G, Q_BLOCK, kk * sb, dh)
        kpos = (selc[..., None] * sb + jnp.arange(sb)).reshape(B, G, Q_BLOCK, kk * sb)
        m = (kpos <= tq[:, None]) & jnp.repeat(okc, sb, axis=-1)
        s = jnp.einsum('bgrqd,bgqkd->bgrqk', qc, ks, preferred_element_type=F32) * scale
        p = jax.nn.softmax(jnp.where(m[:, :, None], s, NEG), axis=-1)
        o_s = jnp.einsum('bgrqk,bgqkd->bgrqd', p.astype(vs.dtype), vs)
        kw = lax.dynamic_slice_in_dim(k_wp, ci * Q_BLOCK, Q_BLOCK + NSA_WINDOW, axis=2)
        vw = lax.dynamic_slice_in_dim(v_wp, ci * Q_BLOCK, Q_BLOCK + NSA_WINDOW, axis=2)
        wpos = ci * Q_BLOCK - NSA_WINDOW + jnp.arange(Q_BLOCK + NSA_WINDOW)
        dist = tq[:, None] - wpos[None, :]
        wm = (dist >= 0) & (dist < NSA_WINDOW) & (wpos[None, :] >= 0)
        s_w = jnp.einsum('bgrqd,bgkd->bgrqk', qc, kw, preferred_element_type=F32) * scale
        p_w = jax.nn.softmax(jnp.where(wm, s_w, NEG), axis=-1)
        o_w = jnp.einsum('bgrqk,bgkd->bgrqd', p_w.astype(vw.dtype), vw)
        return o_s, o_w

    o_sel, o_win = lax.map(chunk, (q_c, sel_c, ok_c, jnp.arange(nqb)))
    o_sel = o_sel.transpose(1, 0, 4, 2, 3, 5).reshape(B, S, H, dh)
    o_win = o_win.transpose(1, 0, 4, 2, 3, 5).reshape(B, S, H, dh)
    g = jax.nn.sigmoid(gate_logits.astype(F32)).reshape(B, S, H, 3)
    out = g[..., 0:1] * o_cmp + g[..., 1:2] * o_sel + g[..., 2:3] * o_win
    return out.reshape(B, S, H * dh).astype(q.dtype)


def rg_lru_branch(x_in, gate_in, conv_w, conv_b, w_a, b_a, w_i, b_i, lam):
    B, S, C = x_in.shape
    xc = (causal_depthwise_conv(x_in, conv_w) + conv_b).astype(F32)
    xh = xc.reshape(B, S, LRU_HEADS, C // LRU_HEADS)
    r = jax.nn.sigmoid(jnp.einsum('bshi,hij->bshj', xh, w_a.astype(F32)) + b_a).reshape(B, S, C)
    i = jax.nn.sigmoid(jnp.einsum('bshi,hij->bshj', xh, w_i.astype(F32)) + b_i).reshape(B, S, C)
    log_a = -LRU_C * r * jax.nn.softplus(-lam.astype(F32))
    a = jnp.exp(log_a)
    b = jnp.sqrt(-jnp.expm1(2.0 * log_a)) * (i * xc)

    def combine(c1, c2):
        a1, b1 = c1
        a2, b2 = c2
        return a1 * a2, a2 * b1 + b2

    _, h = lax.associative_scan(combine, (a, b), axis=1)
    return (h * jax.nn.gelu(gate_in.astype(F32))).astype(x_in.dtype)


def short_conv_branch(x_in, b_gate, c_gate, conv_w):
    return b_gate * causal_depthwise_conv(c_gate * x_in, conv_w)


def cross_attention(h, mem_n, w_q, w_k, w_v, w_o):
    B, S, _ = h.shape
    M = mem_n.shape[1]
    q = dense(h, w_q).reshape(B, S, XA_HEADS, HEAD_DIM)
    k = dense(mem_n, w_k).reshape(B, M, XA_HEADS, HEAD_DIM)
    v = dense(mem_n, w_v).reshape(B, M, XA_HEADS, HEAD_DIM)
    s = jnp.einsum('bshd,bmhd->bhsm', q, k, preferred_element_type=F32) * HEAD_DIM ** -0.5
    p = jax.nn.softmax(s, axis=-1).astype(v.dtype)
    o = jnp.einsum('bhsm,bmhd->bshd', p, v).reshape(B, S, XA_WIDTH)
    return dense(o, w_o)


def hierarchical_moe(h, w_group, b_group, w_expert, b_expert, w_gate, w_up, w_down):
    B, S, D = h.shape
    t = h.reshape(B * S, D)
    g_logits = (t @ w_group + b_group).astype(F32)
    g_prob = jax.nn.softmax(g_logits, axis=-1)
    g_sel = jnp.argmax(g_logits, axis=-1)
    e_logits = (t @ w_expert + b_expert).astype(F32).reshape(-1, N_GROUPS, EXPERTS_PER_GROUP)
    e_in = jnp.take_along_axis(e_logits, g_sel[:, None, None], axis=1)[:, 0]
    top_v, top_i = lax.top_k(e_in, TOPK_IN_GROUP)
    p_g = jnp.take_along_axis(g_prob, g_sel[:, None], axis=1)
    w_top = jax.nn.softmax(top_v, axis=-1) * p_g
    eid = g_sel[:, None] * EXPERTS_PER_GROUP + top_i
    combine = jnp.sum(jax.nn.one_hot(eid, N_EXPERTS, dtype=F32) * w_top[..., None], axis=1)
    hid = jax.nn.silu(jnp.einsum('td,edf->tef', t, w_gate)) * jnp.einsum('td,edf->tef', t, w_up)
    out = jnp.einsum('tef,efd->td', hid * combine[:, :, None].astype(hid.dtype), w_down)
    return out.reshape(B, S, D)


def kernel(x, mem, positions, norm_mix, w_mix_in, lru_conv_w, lru_conv_b, lru_w_a, lru_b_a, lru_w_i, lru_b_i, lru_lambda, sc_conv_w, nsa_pe_k, nsa_w1_k, nsa_w2_k, nsa_pe_v, nsa_w1_v, nsa_w2_v, w_merge_gate, b_merge_gate, w_branch_out, w_mix_out, norm_xattn, norm_mem, xa_w_q, xa_w_k, xa_w_v, xa_w_o, norm_moe, moe_w_group, moe_b_group, moe_w_expert, moe_b_expert, moe_w_gate, moe_w_up, moe_w_down, norm_final):
    cos, sin = rope_tables(positions)
    for l in range(DEPTH):
        h = rms_norm(x, norm_mix[l])
        (a_q, a_k, a_v, n_q, n_kc, n_vc, n_ks, n_vs, n_kw, n_vw, n_g,
         r_x, r_g, c_b, c_c, c_x) = split_cols(dense(h, w_mix_in[l]))
        o_moba = moba_attention(rope(split_heads(a_q), cos, sin),
                                rope(split_heads(a_k), cos, sin), split_heads(a_v))
        o_lru = rg_lru_branch(r_x, r_g, lru_conv_w[l], lru_conv_b[l], lru_w_a[l], lru_b_a[l],
                              lru_w_i[l], lru_b_i[l], lru_lambda[l])
        o_conv = short_conv_branch(c_x, c_b, c_c, sc_conv_w[l])
        o_nsa = nsa_attention(rope(split_heads(n_q), cos, sin),
                              rope(split_heads(n_kc), cos, sin), split_heads(n_vc),
                              rope(split_heads(n_ks), cos, sin), split_heads(n_vs),
                              rope(split_heads(n_kw), cos, sin), split_heads(n_vw), n_g,
                              nsa_pe_k[l], nsa_w1_k[l], nsa_w2_k[l],
                              nsa_pe_v[l], nsa_w1_v[l], nsa_w2_v[l])
        merged = jnp.zeros_like(x)
        for n, o in enumerate((o_moba, o_lru, o_conv, o_nsa)):
            gate = jax.nn.sigmoid(dense(h, w_merge_gate[l, n]) + b_merge_gate[l, n])
            merged = merged + gate * dense(o, w_branch_out[l, n])
        x = x + dense(merged, w_mix_out[l]).astype(x.dtype)
        x = x + cross_attention(rms_norm(x, norm_xattn[l]), rms_norm(mem, norm_mem[l]),
                                xa_w_q[l], xa_w_k[l], xa_w_v[l], xa_w_o[l]).astype(x.dtype)
        x = x + hierarchical_moe(rms_norm(x, norm_moe[l]), moe_w_group[l], moe_b_group[l],
                                 moe_w_expert[l], moe_b_expert[l], moe_w_gate[l],
                                 moe_w_up[l], moe_w_down[l]).astype(x.dtype)
    return rms_norm(x, norm_final)
```

```python
import functools

import jax
import jax.numpy as jnp
import numpy as np
from jax import lax
from jax.experimental import pallas as pl
from jax.experimental.pallas import tpu as pltpu

F32 = jnp.float32
BF16 = jnp.bfloat16

D_MODEL = 4096
DEPTH = 2
HEAD_DIM = 128
ROPE_THETA = 10000.0
NORM_EPS = 1e-6
NEG = -1e30
N_BRANCH = 4
BRANCH_WIDTH = D_MODEL // 4
MOBA_HEADS = BRANCH_WIDTH // HEAD_DIM
MOBA_BLOCK = 256
MOBA_TOPK = 3
MOBA_Q_BLOCK = 64
LRU_WIDTH = BRANCH_WIDTH
LRU_HEADS = LRU_WIDTH // HEAD_DIM
LRU_C = 8.0
SC_WIDTH = BRANCH_WIDTH
NSA_HEADS = BRANCH_WIDTH // HEAD_DIM
NSA_KV_HEADS = NSA_HEADS // 4
NSA_CMP_BLOCK = 32
NSA_CMP_STRIDE = 16
NSA_SEL_BLOCK = 64
NSA_SEL_TOPK = 16
NSA_WINDOW = 512
NSA_FORCE_BONUS = 1e4
Q_BLOCK = 128
XA_HEADS = 4
XA_WIDTH = XA_HEADS * HEAD_DIM
N_GROUPS = 4
EXPERTS_PER_GROUP = 8
N_EXPERTS = N_GROUPS * EXPERTS_PER_GROUP
TOPK_IN_GROUP = 2
EXPERT_FF = D_MODEL // 8
MOBA_W = MOBA_HEADS * HEAD_DIM
NSA_Q_W = NSA_HEADS * HEAD_DIM
NSA_KV_W = NSA_KV_HEADS * HEAD_DIM
NSA_GATE_W = NSA_HEADS * 3
MIX_SPLITS = (MOBA_W, MOBA_W, MOBA_W,
              NSA_Q_W, NSA_KV_W, NSA_KV_W, NSA_KV_W, NSA_KV_W, NSA_KV_W, NSA_KV_W, NSA_GATE_W,
              LRU_WIDTH, LRU_WIDTH,
              SC_WIDTH, SC_WIDTH, SC_WIDTH)

VMEM_LIMIT_BYTES = 56 * 1024 * 1024


def _mm_kernel(*refs, rope, residual):
    a_ref, w_ref = refs[0], refs[1]
    o_ref = refs[-1]
    acc = jnp.dot(a_ref[...].astype(BF16), w_ref[...].astype(BF16), preferred_element_type=F32)
    if residual:
        acc = acc + refs[2][...]
    if rope:
        cos, sin = refs[2][...], refs[3][...]
        for c in range(acc.shape[1] // HEAD_DIM):
            blk = acc[:, c * HEAD_DIM:(c + 1) * HEAD_DIM]
            rot = pltpu.roll(blk, HEAD_DIM // 2, 1)
            o_ref[:, c * HEAD_DIM:(c + 1) * HEAD_DIM] = (blk * cos + rot * sin).astype(o_ref.dtype)
    else:
        o_ref[...] = acc.astype(o_ref.dtype)


def matmul(a, w, *, out_dtype=F32, rope=None, residual=None, tm=1024, tn=512):
    m, k = a.shape
    k2, n = w.shape
    assert k == k2 and not (rope is not None and residual is not None)
    tm = min(tm, m)
    tn = min(tn, n)
    in_specs = [pl.BlockSpec((tm, k), lambda i, j: (i, 0)),
                pl.BlockSpec((k, tn), lambda i, j: (0, j))]
    args = [a, w]
    if rope is not None:
        in_specs += [pl.BlockSpec((tm, HEAD_DIM), lambda i, j: (i, 0))] * 2
        args += list(rope)
    if residual is not None:
        in_specs.append(pl.BlockSpec((tm, tn), lambda i, j: (i, j)))
        args.append(residual)
    return pl.pallas_call(
        functools.partial(_mm_kernel, rope=rope is not None, residual=residual is not None),
        grid=(pl.cdiv(m, tm), pl.cdiv(n, tn)),
        in_specs=in_specs,
        out_specs=pl.BlockSpec((tm, tn), lambda i, j: (i, j)),
        out_shape=jax.ShapeDtypeStruct((m, n), out_dtype),
        compiler_params=pltpu.CompilerParams(
            dimension_semantics=("parallel", "parallel"),
            vmem_limit_bytes=VMEM_LIMIT_BYTES),
        name="matmul",
    )(*args)


def _merge_kernel(h_ref, o_ref, wg_ref, bg_ref, wu_ref, out_ref, acc_ref):
    n = pl.program_id(2)
    gate = jax.nn.sigmoid(jnp.dot(h_ref[...], wg_ref[...], preferred_element_type=F32) + bg_ref[...])
    term = gate * jnp.dot(o_ref[...], wu_ref[...], preferred_element_type=F32)

    @pl.when(n == 0)
    def _():
        acc_ref[...] = term

    @pl.when(n != 0)
    def _():
        acc_ref[...] += term

    @pl.when(n == pl.num_programs(2) - 1)
    def _():
        out_ref[...] = acc_ref[...].astype(out_ref.dtype)


def merge_branches_pallas(h, o_all, w_gate, b_gate, w_up, *, tm=1024, tn=512):
    T, D = h.shape
    N, _, W = o_all.shape
    return pl.pallas_call(
        _merge_kernel,
        grid=(T // tm, D // tn, N),
        in_specs=[pl.BlockSpec((tm, D), lambda i, j, n: (i, 0)),
                  pl.BlockSpec((None, tm, W), lambda i, j, n: (n, i, 0)),
                  pl.BlockSpec((None, D, tn), lambda i, j, n: (n, 0, j)),
                  pl.BlockSpec((None, 1, tn), lambda i, j, n: (n, 0, j)),
                  pl.BlockSpec((None, W, tn), lambda i, j, n: (n, 0, j))],
        out_specs=pl.BlockSpec((tm, tn), lambda i, j, n: (i, j)),
        out_shape=jax.ShapeDtypeStruct((T, D), BF16),
        scratch_shapes=[pltpu.VMEM((tm, tn), F32)],
        compiler_params=pltpu.CompilerParams(
            dimension_semantics=("parallel", "parallel", "arbitrary"),
            vmem_limit_bytes=VMEM_LIMIT_BYTES),
        name="merge_branches",
    )(h, o_all, w_gate, b_gate, w_up)


def dense(x, w):
    lead = x.shape[:-1]
    y = matmul(x.reshape(-1, x.shape[-1]).astype(BF16), w.astype(BF16))
    return y.reshape(*lead, w.shape[-1])


MASK_BIG = 1e30
LANES = 128


def _first_index_topk_mask(work, colf, k):
    sel = jnp.zeros(work.shape, F32)
    for _ in range(k):
        mx = jnp.max(work, axis=-1, keepdims=True)
        idx = jnp.min(jnp.where(work == mx, colf, float(LANES)), axis=-1, keepdims=True)
        pick = colf == idx
        sel = jnp.where(pick, 1.0, sel)
        work = jnp.where(pick, -jnp.inf, work)
    return sel


def _moba_kernel(q_ref, k_ref, v_ref, o_ref, kmean_ref, m_ref, l_ref, acc_ref, *, scale):
    i = pl.program_id(2)
    bs = MOBA_BLOCK
    seq = k_ref.shape[0]
    nt = (((1,), (1,)), ((), ()))

    @pl.when(i == 0)
    def _():
        row = lax.broadcasted_iota(jnp.int32, (LANES, seq), 0)
        col = lax.broadcasted_iota(jnp.int32, (LANES, seq), 1)
        ind = jnp.where(col // bs == row, 1.0, 0.0).astype(BF16)
        kmean_ref[...] = (jnp.dot(ind, k_ref[...], preferred_element_type=F32)
                          * (1.0 / bs)).astype(BF16)

    q = q_ref[...]
    gate = lax.dot_general(q, kmean_ref[...], nt, preferred_element_type=F32)
    coli = lax.broadcasted_iota(jnp.int32, (bs, LANES), 1)
    colf = coli.astype(F32)
    past = coli < i
    sel = _first_index_topk_mask(jnp.where(past, gate, NEG), colf, MOBA_TOPK)
    notsel = jnp.where((sel > 0.0) & past, 0.0, 1.0).astype(BF16)
    q_aug = jnp.concatenate([q, notsel], axis=-1)

    start = pl.multiple_of(i * bs, bs)
    s = lax.dot_general(q, k_ref[pl.ds(start, bs), :], nt, preferred_element_type=F32) * scale
    r_id = lax.broadcasted_iota(jnp.int32, (bs, bs), 0)
    c_id = lax.broadcasted_iota(jnp.int32, (bs, bs), 1)
    s = jnp.where(c_id <= r_id, s, NEG)
    m0 = jnp.max(s, axis=-1, keepdims=True)
    p = jnp.exp(s - m0)
    m_ref[...] = m0
    l_ref[...] = jnp.sum(p, axis=-1, keepdims=True)
    acc_ref[...] = jnp.dot(p.astype(BF16), v_ref[pl.ds(start, bs), :], preferred_element_type=F32)

    def body(j, carry):
        off = pl.multiple_of(j * bs, bs)
        bias = jnp.where(coli == j, -MASK_BIG, 0.0).astype(BF16)
        k_aug = jnp.concatenate([k_ref[pl.ds(off, bs), :], bias], axis=-1)
        sj = lax.dot_general(q_aug, k_aug, nt, preferred_element_type=F32) * scale
        m_prev = m_ref[...]
        m_new = jnp.maximum(m_prev, jnp.max(sj, axis=-1, keepdims=True))
        alpha = jnp.exp(m_prev - m_new)
        pj = jnp.exp(sj - m_new)
        l_ref[...] = alpha * l_ref[...] + jnp.sum(pj, axis=-1, keepdims=True)
        acc_ref[...] = alpha * acc_ref[...] + jnp.dot(pj.astype(BF16), v_ref[pl.ds(off, bs), :],
                                                      preferred_element_type=F32)
        m_ref[...] = m_new
        return carry

    lax.fori_loop(0, i, body, 0)
    o_ref[...] = (acc_ref[...] / l_ref[...]).astype(o_ref.dtype)


def moba_attention_pallas(zq, zk, zv, *, n_heads, q_off=0, k_off=0, v_off=0):
    B, S, _ = zq.shape
    bs = MOBA_BLOCK
    assert S % bs == 0 and S // bs <= LANES
    return pl.pallas_call(
        functools.partial(_moba_kernel, scale=HEAD_DIM ** -0.5),
        grid=(B, n_heads, S // bs),
        in_specs=[pl.BlockSpec((None, bs, HEAD_DIM), lambda b, h, i: (b, i, q_off + h)),
                  pl.BlockSpec((None, S, HEAD_DIM), lambda b, h, i: (b, 0, k_off + h)),
                  pl.BlockSpec((None, S, HEAD_DIM), lambda b, h, i: (b, 0, v_off + h))],
        out_specs=pl.BlockSpec((None, bs, HEAD_DIM), lambda b, h, i: (b, i, h)),
        out_shape=jax.ShapeDtypeStruct((B, S, n_heads * HEAD_DIM), BF16),
        scratch_shapes=[pltpu.VMEM((LANES, HEAD_DIM), BF16),
                        pltpu.VMEM((bs, 1), F32), pltpu.VMEM((bs, 1), F32),
                        pltpu.VMEM((bs, HEAD_DIM), F32)],
        compiler_params=pltpu.CompilerParams(
            dimension_semantics=("parallel", "parallel", "arbitrary"),
            vmem_limit_bytes=VMEM_LIMIT_BYTES),
        name="moba_attention",
    )(zq, zk, zv)


NSA_TQ = 256
NSA_NC_PAD = 256
CMP_ROW = NSA_CMP_STRIDE * HEAD_DIM


def _nsa_compress_kernel(x_ref, pe_ref, w1_ref, w2_ref, o_ref):
    x = x_ref[...].astype(F32)
    top = (x + pe_ref[0:1, :]).astype(BF16)
    bot = (x + pe_ref[1:2, :]).astype(BF16)
    a = jnp.dot(top, w1_ref[0:CMP_ROW, :], preferred_element_type=F32)
    b = jnp.dot(bot, w1_ref[CMP_ROW:2 * CMP_ROW, :], preferred_element_type=F32)
    pre = a + pltpu.roll(b, b.shape[0] - 1, 0)
    hid = jax.nn.gelu(pre)
    o_ref[...] = jnp.dot(hid.astype(BF16), w2_ref[...], preferred_element_type=F32).astype(o_ref.dtype)


def nsa_compress_pallas(x, pe, w1, w2):
    B, S, gw = x.shape
    G = gw // HEAD_DIM
    nrow = S // NSA_CMP_STRIDE
    xr = x.reshape(B, nrow, NSA_CMP_STRIDE, G, HEAD_DIM).transpose(0, 3, 1, 2, 4).reshape(B, G, nrow, CMP_ROW)
    pe2 = pe.astype(F32).reshape(2, CMP_ROW)
    return pl.pallas_call(
        _nsa_compress_kernel,
        grid=(B, G),
        in_specs=[pl.BlockSpec((None, None, nrow, CMP_ROW), lambda b, g: (b, g, 0, 0)),
                  pl.BlockSpec((2, CMP_ROW), lambda b, g: (0, 0)),
                  pl.BlockSpec((2 * CMP_ROW, HEAD_DIM), lambda b, g: (0, 0)),
                  pl.BlockSpec((HEAD_DIM, HEAD_DIM), lambda b, g: (0, 0))],
        out_specs=pl.BlockSpec((None, None, nrow, HEAD_DIM), lambda b, g: (b, g, 0, 0)),
        out_shape=jax.ShapeDtypeStruct((B, G, nrow, HEAD_DIM), BF16),
        compiler_params=pltpu.CompilerParams(
            dimension_semantics=("parallel", "parallel"), vmem_limit_bytes=VMEM_LIMIT_BYTES),
        name="nsa_compress",
    )(xr, pe2, w1.astype(BF16), w2.astype(BF16))


def _nsa_kernel(q_ref, kc_ref, vc_ref, ks_ref, vs_ref, kw_ref, vw_ref, g_ref, wmap_ref, o_ref,
                m_ref, l_ref, acc_ref, out_ref, *, scale):
    i = pl.program_id(2)
    tq = NSA_TQ
    R = NSA_HEADS // NSA_KV_HEADS
    nt = (((1,), (1,)), ((), ()))
    row = lax.broadcasted_iota(jnp.int32, (tq, tq), 0)
    col = lax.broadcasted_iota(jnp.int32, (tq, tq), 1)
    t_abs = i * tq + row
    gates = jax.nn.sigmoid(g_ref[...])

    def qh(r):
        return q_ref[:, r * HEAD_DIM:(r + 1) * HEAD_DIM]

    ncp = kc_ref.shape[0]
    cmask = (lax.broadcasted_iota(jnp.int32, (tq, ncp), 1) * NSA_CMP_STRIDE + (NSA_CMP_BLOCK - 1)
             <= i * tq + lax.broadcasted_iota(jnp.int32, (tq, ncp), 0))
    imp = jnp.zeros((tq, LANES), F32)
    for r in range(R):
        s = lax.dot_general(qh(r), kc_ref[...], nt, preferred_element_type=F32) * scale
        s = jnp.where(cmask, s, NEG)
        e = jnp.exp(s - jnp.max(s, axis=-1, keepdims=True))
        p = jnp.where(cmask, e / jnp.sum(e, axis=-1, keepdims=True), 0.0).astype(BF16)
        imp = imp + jnp.dot(p, wmap_ref[...], preferred_element_type=F32)
        o_c = jnp.dot(p, vc_ref[...], preferred_element_type=F32)
        out_ref[r] = gates[:, 3 * r:3 * r + 1] * o_c

    w_tiles = []
    for d in range(NSA_WINDOW // tq + 1):
        off = pl.multiple_of(jnp.maximum(i - d, 0) * tq, tq)
        gone = jnp.where(i >= d, 0, 2 * NSA_WINDOW + tq)
        dist = d * tq + row - col + gone
        w_tiles.append((off, (dist >= 0) & (dist < NSA_WINDOW)))
    for r in range(R):
        ss = []
        for off, mask in w_tiles:
            s = lax.dot_general(qh(r), kw_ref[pl.ds(off, tq), :], nt, preferred_element_type=F32) * scale
            ss.append(jnp.where(mask, s, NEG))
        m = jnp.max(ss[0], axis=-1, keepdims=True)
        for s in ss[1:]:
            m = jnp.maximum(m, jnp.max(s, axis=-1, keepdims=True))
        l = jnp.zeros((tq, 1), F32)
        o_w = jnp.zeros((tq, HEAD_DIM), F32)
        for (off, _), s in zip(w_tiles, ss):
            p = jnp.exp(s - m)
            l = l + jnp.sum(p, axis=-1, keepdims=True)
            o_w = o_w + jnp.dot(p.astype(BF16), vw_ref[pl.ds(off, tq), :], preferred_element_type=F32)
        out_ref[r] = out_ref[r] + gates[:, 3 * r + 2:3 * r + 3] * (o_w / l)

    rowl = lax.broadcasted_iota(jnp.int32, (tq, LANES), 0)
    coll = lax.broadcasted_iota(jnp.int32, (tq, LANES), 1)
    q_blk = (i * tq + rowl) // NSA_SEL_BLOCK
    valid = coll <= q_blk
    forced = (coll == 0) | (coll >= q_blk - 1)
    impm = jnp.where(valid, imp + jnp.where(forced, NSA_FORCE_BONUS, 0.0), NEG)
    sel = _first_index_topk_mask(impm, coll.astype(F32), NSA_SEL_TOPK)
    notsel = jnp.where((sel > 0.0) & valid, 0.0, 1.0).astype(BF16)

    blk_in_tile = tq // NSA_SEL_BLOCK
    key_blk = rowl // NSA_SEL_BLOCK

    def sel_tile(jt, causal):
        off = pl.multiple_of(jt * tq, tq)
        bias = jnp.where(coll == jt * blk_in_tile + key_blk, -MASK_BIG, 0.0).astype(BF16)
        k_aug = jnp.concatenate([ks_ref[pl.ds(off, tq), :], bias], axis=-1)
        v_t = vs_ref[pl.ds(off, tq), :]
        for r in range(R):
            q_aug = jnp.concatenate([qh(r), notsel], axis=-1)
            s = lax.dot_general(q_aug, k_aug, nt, preferred_element_type=F32) * scale
            if causal:
                s = jnp.where(col <= row, s, NEG)
                m_new = jnp.max(s, axis=-1, keepdims=True)
                p = jnp.exp(s - m_new)
                l_ref[r] = jnp.sum(p, axis=-1, keepdims=True)
                acc_ref[r] = jnp.dot(p.astype(BF16), v_t, preferred_element_type=F32)
            else:
                m_prev = m_ref[r]
                m_new = jnp.maximum(m_prev, jnp.max(s, axis=-1, keepdims=True))
                alpha = jnp.exp(m_prev - m_new)
                p = jnp.exp(s - m_new)
                l_ref[r] = alpha * l_ref[r] + jnp.sum(p, axis=-1, keepdims=True)
                acc_ref[r] = alpha * acc_ref[r] + jnp.dot(p.astype(BF16), v_t, preferred_element_type=F32)
            m_ref[r] = m_new

    sel_tile(i, True)

    def body(jt, carry):
        sel_tile(jt, False)
        return carry

    lax.fori_loop(0, i, body, 0)
    for r in range(R):
        o_s = acc_ref[r] / l_ref[r]
        o_ref[:, r * HEAD_DIM:(r + 1) * HEAD_DIM] = (
            out_ref[r] + gates[:, 3 * r + 1:3 * r + 2] * o_s).astype(o_ref.dtype)


def nsa_attention_pallas(zq, kc, vc, zks, zvs, zkw, zvw, gate_logits, *, q_off, ks_off, vs_off, kw_off, vw_off):
    B, S, _ = zq.shape
    G, tq = NSA_KV_HEADS, NSA_TQ
    R = NSA_HEADS // G
    assert S % tq == 0 and S // NSA_SEL_BLOCK <= LANES
    nc = (S - NSA_CMP_BLOCK) // NSA_CMP_STRIDE + 1
    ncp = kc.shape[2]
    ns = S // NSA_SEL_BLOCK
    wmap = jnp.pad(cmp_to_sel_weights(nc, ns), ((0, ncp - nc), (0, LANES - ns))).astype(BF16)
    rw = R * HEAD_DIM
    kv_spec = lambda off: pl.BlockSpec((None, S, HEAD_DIM), lambda b, g, i: (b, 0, off + g))
    c_spec = pl.BlockSpec((None, None, ncp, HEAD_DIM), lambda b, g, i: (b, g, 0, 0))
    return pl.pallas_call(
        functools.partial(_nsa_kernel, scale=HEAD_DIM ** -0.5),
        grid=(B, G, S // tq),
        in_specs=[pl.BlockSpec((None, tq, rw), lambda b, g, i: (b, i, q_off // R + g)),
                  c_spec, c_spec, kv_spec(ks_off), kv_spec(vs_off), kv_spec(kw_off), kv_spec(vw_off),
                  pl.BlockSpec((None, tq, LANES), lambda b, g, i: (b, i, g)),
                  pl.BlockSpec((ncp, LANES), lambda b, g, i: (0, 0))],
        out_specs=pl.BlockSpec((None, tq, rw), lambda b, g, i: (b, i, g)),
        out_shape=jax.ShapeDtypeStruct((B, S, NSA_HEADS * HEAD_DIM), BF16),
        scratch_shapes=[pltpu.VMEM((R, tq, 1), F32), pltpu.VMEM((R, tq, 1), F32),
                        pltpu.VMEM((R, tq, HEAD_DIM), F32), pltpu.VMEM((R, tq, HEAD_DIM), F32)],
        compiler_params=pltpu.CompilerParams(
            dimension_semantics=("parallel", "parallel", "arbitrary"),
            vmem_limit_bytes=VMEM_LIMIT_BYTES),
        name="nsa_attention",
    )(zq, kc, vc, zks, zvs, zkw, zvw, gate_logits, wmap)


MOE_TM = 512
MOE_TF = 256
MOE_TN = 512
ROUTER_TM = 256


def _router_kernel(x_ref, g_ref, w_ref, b_ref, t_ref, ids_ref, wts_ref):
    x = x_ref[...]
    t = x * lax.rsqrt(jnp.mean(x * x, axis=-1, keepdims=True) + NORM_EPS) * g_ref[...]
    t_ref[...] = t.astype(t_ref.dtype)
    logits = jnp.dot(t.astype(BF16), w_ref[...], preferred_element_type=F32) + b_ref[...]
    coli = lax.broadcasted_iota(jnp.int32, logits.shape, 1)
    colf = coli.astype(F32)
    first = lambda hit: jnp.min(jnp.where(hit, colf, float(LANES)), axis=-1, keepdims=True)
    is_g = coli < N_GROUPS
    gl = jnp.where(is_g, logits, -jnp.inf)
    gmax = jnp.max(gl, axis=-1, keepdims=True)
    g_sel = first(gl == gmax)
    p_g = 1.0 / jnp.sum(jnp.where(is_g, jnp.exp(logits - gmax), 0.0), axis=-1, keepdims=True)
    lo = N_GROUPS + EXPERTS_PER_GROUP * g_sel
    el = jnp.where((colf >= lo) & (colf < lo + EXPERTS_PER_GROUP), logits, -jnp.inf)
    v1 = jnp.max(el, axis=-1, keepdims=True)
    i1 = first(el == v1)
    el2 = jnp.where(colf == i1, -jnp.inf, el)
    v2 = jnp.max(el2, axis=-1, keepdims=True)
    i2 = first(el2 == v2)
    e = jnp.exp(v2 - v1)
    w1 = p_g / (1.0 + e)
    w2 = p_g * e / (1.0 + e)
    ids_ref[...] = jnp.where(coli == 0, i1 - N_GROUPS, jnp.where(coli == 1, i2 - N_GROUPS, 0.0)).astype(jnp.int32)
    wts_ref[...] = jnp.where(coli == 0, w1, jnp.where(coli == 1, w2, 0.0))


def moe_router_pallas(x2, gain, w_group, b_group, w_expert, b_expert):
    T, D = x2.shape
    tm = ROUTER_TM
    nr = N_GROUPS + N_EXPERTS
    w = jnp.pad(jnp.concatenate([w_group, w_expert], axis=1), ((0, 0), (0, LANES - nr))).astype(BF16)
    b = jnp.pad(jnp.concatenate([b_group, b_expert]), (0, LANES - nr)).astype(F32)[None, :]
    t, ids, wts = pl.pallas_call(
        _router_kernel,
        grid=(T // tm,),
        in_specs=[pl.BlockSpec((tm, D), lambda i: (i, 0)),
                  pl.BlockSpec((1, D), lambda i: (0, 0)),
                  pl.BlockSpec((D, LANES), lambda i: (0, 0)),
                  pl.BlockSpec((1, LANES), lambda i: (0, 0))],
        out_specs=[pl.BlockSpec((tm, D), lambda i: (i, 0)),
                   pl.BlockSpec((tm, LANES), lambda i: (i, 0)),
                   pl.BlockSpec((tm, LANES), lambda i: (i, 0))],
        out_shape=[jax.ShapeDtypeStruct((T, D), BF16),
                   jax.ShapeDtypeStruct((T, LANES), jnp.int32),
                   jax.ShapeDtypeStruct((T, LANES), F32)],
        compiler_params=pltpu.CompilerParams(
            dimension_semantics=("parallel",), vmem_limit_bytes=VMEM_LIMIT_BYTES),
        name="moe_router",
    )(x2, gain.astype(F32)[None, :], w, b)
    return t, ids[:, :TOPK_IN_GROUP], wts[:, :TOPK_IN_GROUP]


def _moe_expert_kernel(tile_e_ref, n_used_ref, x_ref, rw_ref, wg_ref, wu_ref, wd_ref, o_ref):
    k = pl.program_id(0)
    f = pl.program_id(1)

    @pl.when(f == 0)
    def _():
        o_ref[...] = jnp.zeros_like(o_ref)

    @pl.when(k < n_used_ref[0])
    def _():
        x = x_ref[...]
        hg = jnp.dot(x, wg_ref[...].astype(BF16), preferred_element_type=F32)
        hu = jnp.dot(x, wu_ref[...].astype(BF16), preferred_element_type=F32)
        hid = (jax.nn.silu(hg) * hu * rw_ref[...]).astype(BF16)
        for c in range(0, o_ref.shape[1], MOE_TN):
            o_ref[:, c:c + MOE_TN] += jnp.dot(hid, wd_ref[:, c:c + MOE_TN].astype(BF16),
                                              preferred_element_type=F32)


def moe_experts_pallas(x_sorted, row_w, tile_e, n_used, w_gate, w_up, w_down):
    P, D = x_sorted.shape
    E, _, F = w_gate.shape
    tm, tf = MOE_TM, MOE_TF
    grid_spec = pltpu.PrefetchScalarGridSpec(
        num_scalar_prefetch=2,
        grid=(P // tm, F // tf),
        in_specs=[pl.BlockSpec((tm, D), lambda k, f, te, nu: (k, 0)),
                  pl.BlockSpec((tm, 1), lambda k, f, te, nu: (k, 0)),
                  pl.BlockSpec((None, D, tf), lambda k, f, te, nu: (te[k], 0, f)),
                  pl.BlockSpec((None, D, tf), lambda k, f, te, nu: (te[k], 0, f)),
                  pl.BlockSpec((None, tf, D), lambda k, f, te, nu: (te[k], f, 0))],
        out_specs=pl.BlockSpec((tm, D), lambda k, f, te, nu: (k, 0)),
    )
    return pl.pallas_call(
        _moe_expert_kernel,
        grid_spec=grid_spec,
        out_shape=jax.ShapeDtypeStruct((P, D), F32),
        compiler_params=pltpu.CompilerParams(
            dimension_semantics=("parallel", "arbitrary"), vmem_limit_bytes=VMEM_LIMIT_BYTES),
        name="moe_experts",
    )(tile_e, n_used, x_sorted, row_w, w_gate, w_up, w_down)


def moe_dispatch_plan(ids):
    T, K = ids.shape
    n = T * K
    P = n + N_EXPERTS * MOE_TM
    flat = ids.reshape(n)
    order = jnp.argsort(flat, stable=True)
    sorted_e = flat[order]
    counts = jnp.zeros((N_EXPERTS,), jnp.int32).at[flat].add(1)
    padded = (counts + MOE_TM - 1) // MOE_TM * MOE_TM
    pend = jnp.cumsum(padded)
    pstart = pend - padded
    start = jnp.cumsum(counts) - counts
    dest = pstart[sorted_e] + (jnp.arange(n, dtype=jnp.int32) - start[sorted_e])
    pos = jnp.zeros((n,), jnp.int32).at[order].set(dest)
    row_tok = jnp.zeros((P,), jnp.int32).at[dest].set((order // K).astype(jnp.int32))
    tile_start = jnp.arange(P // MOE_TM, dtype=jnp.int32) * MOE_TM
    tile_e = jnp.minimum(jnp.searchsorted(pend, tile_start, side='right'), N_EXPERTS - 1).astype(jnp.int32)
    n_used = (pend[-1] // MOE_TM).astype(jnp.int32)[None]
    return pos.reshape(T, K), dest, order, row_tok, tile_e, n_used, P


def hierarchical_moe_pallas(x, gain, w_group, b_group, w_expert, b_expert, w_gate, w_up, w_down):
    B, S, D = x.shape
    x2 = x.reshape(B * S, D)
    t, ids, wts = moe_router_pallas(x2, gain, w_group, b_group, w_expert, b_expert)
    pos, dest, order, row_tok, tile_e, n_used, P = moe_dispatch_plan(ids)
    row_w = jnp.zeros((P,), F32).at[dest].set(wts.reshape(-1)[order])[:, None]
    x_sorted = jnp.take(t, row_tok, axis=0)
    y = moe_experts_pallas(x_sorted, row_w, tile_e, n_used, w_gate, w_up, w_down)
    out = x2 + jnp.take(y, pos[:, 0], axis=0) + jnp.take(y, pos[:, 1], axis=0)
    return out.reshape(B, S, D)


def rms_norm(x, g):
    xf = x.astype(F32)
    y = xf * lax.rsqrt(jnp.mean(xf * xf, axis=-1, keepdims=True) + NORM_EPS)
    return (y * g.astype(F32)).astype(x.dtype)


def rope_tables(positions):
    inv = ROPE_THETA ** (-jnp.arange(0, HEAD_DIM, 2, dtype=F32) / HEAD_DIM)
    ang = positions.astype(F32)[..., None] * inv
    return jnp.cos(ang)[:, :, None, :], jnp.sin(ang)[:, :, None, :]


def rope(x, cos, sin):
    x1, x2 = jnp.split(x.astype(F32), 2, axis=-1)
    return jnp.concatenate([x1 * cos - x2 * sin, x2 * cos + x1 * sin], axis=-1).astype(x.dtype)


def split_heads(t):
    return t.reshape(t.shape[0], t.shape[1], -1, HEAD_DIM)


def split_cols(z):
    return jnp.split(z, [int(v) for v in np.cumsum(MIX_SPLITS)[:-1]], axis=-1)


def causal_depthwise_conv(x, w):
    width, c = w.shape
    xp = jnp.pad(x, ((0, 0), (width - 1, 0), (0, 0)))
    return lax.conv_general_dilated(xp, w[:, None, :].astype(x.dtype), (1,), 'VALID',
                                    dimension_numbers=('NWC', 'WIO', 'NWC'),
                                    feature_group_count=c)


def moba_attention(q, k, v):
    B, S, H, dh = q.shape
    bs = MOBA_BLOCK
    nb = -(-S // bs)
    scale = dh ** -0.5
    qh, kh, vh = (t.transpose(0, 2, 1, 3) for t in (q, k, v))
    pad = nb * bs - S
    kp = jnp.pad(kh, ((0, 0), (0, 0), (0, pad), (0, 0)))
    vp = jnp.pad(vh, ((0, 0), (0, 0), (0, pad), (0, 0)))
    kb = kp.reshape(B, H, nb, bs, dh)
    vb = vp.reshape(B, H, nb, bs, dh)
    k_mean = jnp.mean(kb.astype(F32), axis=3)
    t = jnp.arange(S)
    q_blk = t // bs
    gate = jnp.einsum('bhsd,bhnd->bhsn', qh.astype(F32), k_mean)
    past = jnp.arange(nb)[None, :] < q_blk[:, None]
    gate = jnp.where(past, gate, NEG)
    kk = min(MOBA_TOPK, nb)
    _, sel = lax.top_k(gate, kk)
    sel_ok = sel < q_blk[:, None]
    qb = MOBA_Q_BLOCK
    nqb = S // qb
    q_c = qh.reshape(B, H, nqb, qb, dh).transpose(2, 0, 1, 3, 4)
    sel_c = sel.reshape(B, H, nqb, qb, kk).transpose(2, 0, 1, 3, 4)
    ok_c = sel_ok.reshape(B, H, nqb, qb, kk).transpose(2, 0, 1, 3, 4)
    bi = jnp.arange(B)[:, None, None, None]
    hi = jnp.arange(H)[None, :, None, None]

    def chunk(args):
        qc, selc, okc, ci = args
        tq = ci * qb + jnp.arange(qb)
        own = (ci * qb) // bs * bs
        k_own = lax.dynamic_slice_in_dim(kp, own, bs, axis=2)
        v_own = lax.dynamic_slice_in_dim(vp, own, bs, axis=2)
        own_mask = (own + jnp.arange(bs))[None, :] <= tq[:, None]
        k_s = kb[bi, hi, selc].reshape(B, H, qb, kk * bs, dh)
        v_s = vb[bi, hi, selc].reshape(B, H, qb, kk * bs, dh)
        s_sel = jnp.einsum('bhqd,bhqkd->bhqk', qc, k_s, preferred_element_type=F32) * scale
        s_sel = jnp.where(jnp.repeat(okc, bs, axis=-1), s_sel, NEG)
        s_own = jnp.einsum('bhqd,bhkd->bhqk', qc, k_own, preferred_element_type=F32) * scale
        s_own = jnp.where(own_mask, s_own, NEG)
        p = jax.nn.softmax(jnp.concatenate([s_sel, s_own], axis=-1), axis=-1).astype(v.dtype)
        return (jnp.einsum('bhqk,bhqkd->bhqd', p[..., :kk * bs], v_s)
                + jnp.einsum('bhqk,bhkd->bhqd', p[..., kk * bs:], v_own))

    o = lax.map(chunk, (q_c, sel_c, ok_c, jnp.arange(nqb)))
    return o.transpose(1, 0, 3, 2, 4).reshape(B, S, H * dh)


def compress_tokens(x, pe, w1, w2):
    B, S, G, dh = x.shape
    nc = (S - NSA_CMP_BLOCK) // NSA_CMP_STRIDE + 1
    idx = jnp.arange(nc)[:, None] * NSA_CMP_STRIDE + jnp.arange(NSA_CMP_BLOCK)[None, :]
    blocks = x[:, idx] + pe[:, None, :]
    flat = blocks.transpose(0, 1, 3, 2, 4).reshape(B, nc, G, NSA_CMP_BLOCK * dh)
    return jax.nn.gelu(flat @ w1) @ w2


def cmp_to_sel_weights(nc, ns):
    r = NSA_SEL_BLOCK // NSA_CMP_STRIDE
    m = NSA_CMP_BLOCK // NSA_CMP_STRIDE
    c = jnp.arange(nc)[:, None] - r * jnp.arange(ns)[None, :]
    w = jnp.minimum(jnp.minimum(c + 1, r + m - 1 - c), min(r, m))
    return jnp.clip(w, 0, None).astype(F32)


def nsa_attention(q, k_cmp, v_cmp, k_sel, v_sel, k_win, v_win, gate_logits,
                  pe_k, w1_k, w2_k, pe_v, w1_v, w2_v):
    B, S, H, dh = q.shape
    G = k_sel.shape[2]
    R = H // G
    scale = dh ** -0.5
    t = jnp.arange(S)
    qg = q.reshape(B, S, G, R, dh)
    kc = compress_tokens(k_cmp, pe_k, w1_k, w2_k)
    vc = compress_tokens(v_cmp, pe_v, w1_v, w2_v)
    nc = kc.shape[1]
    s_c = jnp.einsum('bsgrd,bngd->bgrsn', qg, kc, preferred_element_type=F32) * scale
    c_end = jnp.arange(nc) * NSA_CMP_STRIDE + NSA_CMP_BLOCK - 1
    c_mask = c_end[None, :] <= t[:, None]
    p_c = jax.nn.softmax(jnp.where(c_mask, s_c, NEG), axis=-1) * c_mask
    o_cmp = jnp.einsum('bgrsn,bngd->bsgrd', p_c.astype(vc.dtype), vc).reshape(B, S, H, dh)
    ns = S // NSA_SEL_BLOCK
    imp = jnp.einsum('bgrsn,nm->bgsm', p_c, cmp_to_sel_weights(nc, ns))
    q_blk = t // NSA_SEL_BLOCK
    j = jnp.arange(ns)[None, :]
    valid = j <= q_blk[:, None]
    forced = (j == 0) | (j >= q_blk[:, None] - 1)
    imp = jnp.where(valid, imp + jnp.where(forced, NSA_FORCE_BONUS, 0.0), NEG)
    kk = min(NSA_SEL_TOPK, ns)
    _, sel = lax.top_k(imp, kk)
    sel_ok = sel <= q_blk[:, None]
    sb = NSA_SEL_BLOCK
    k_blocks = k_sel.transpose(0, 2, 1, 3).reshape(B, G, ns, sb, dh)
    v_blocks = v_sel.transpose(0, 2, 1, 3).reshape(B, G, ns, sb, dh)
    wpad = ((0, 0), (0, 0), (NSA_WINDOW, 0), (0, 0))
    k_wp = jnp.pad(k_win.transpose(0, 2, 1, 3), wpad)
    v_wp = jnp.pad(v_win.transpose(0, 2, 1, 3), wpad)
    nqb = S // Q_BLOCK
    q_c = qg.transpose(0, 2, 3, 1, 4).reshape(B, G, R, nqb, Q_BLOCK, dh).transpose(3, 0, 1, 2, 4, 5)
    sel_c = sel.reshape(B, G, nqb, Q_BLOCK, kk).transpose(2, 0, 1, 3, 4)
    ok_c = sel_ok.reshape(B, G, nqb, Q_BLOCK, kk).transpose(2, 0, 1, 3, 4)
    bi = jnp.arange(B)[:, None, None, None]
    gi = jnp.arange(G)[None, :, None, None]

    def chunk(args):
        qc, selc, okc, ci = args
        tq = ci * Q_BLOCK + jnp.arange(Q_BLOCK)
        ks = k_blocks[bi, gi, selc].reshape(B, G, Q_BLOCK, kk * sb, dh)
        vs = v_blocks[bi, gi, selc].reshape(B, G, Q_BLOCK, kk * sb, dh)
        kpos = (selc[..., None] * sb + jnp.arange(sb)).reshape(B, G, Q_BLOCK, kk * sb)
        m = (kpos <= tq[:, None]) & jnp.repeat(okc, sb, axis=-1)
        s = jnp.einsum('bgrqd,bgqkd->bgrqk', qc, ks, preferred_element_type=F32) * scale
        p = jax.nn.softmax(jnp.where(m[:, :, None], s, NEG), axis=-1)
        o_s = jnp.einsum('bgrqk,bgqkd->bgrqd', p.astype(vs.dtype), vs)
        kw = lax.dynamic_slice_in_dim(k_wp, ci * Q_BLOCK, Q_BLOCK + NSA_WINDOW, axis=2)
        vw = lax.dynamic_slice_in_dim(v_wp, ci * Q_BLOCK, Q_BLOCK + NSA_WINDOW, axis=2)
        wpos = ci * Q_BLOCK - NSA_WINDOW + jnp.arange(Q_BLOCK + NSA_WINDOW)
        dist = tq[:, None] - wpos[None, :]
        wm = (dist >= 0) & (dist < NSA_WINDOW) & (wpos[None, :] >= 0)
        s_w = jnp.einsum('bgrqd,bgkd->bgrqk', qc, kw, preferred_element_type=F32) * scale
        p_w = jax.nn.softmax(jnp.where(wm, s_w, NEG), axis=-1)
        o_w = jnp.einsum('bgrqk,bgkd->bgrqd', p_w.astype(vw.dtype), vw)
        return o_s, o_w

    o_sel, o_win = lax.map(chunk, (q_c, sel_c, ok_c, jnp.arange(nqb)))
    o_sel = o_sel.transpose(1, 0, 4, 2, 3, 5).reshape(B, S, H, dh)
    o_win = o_win.transpose(1, 0, 4, 2, 3, 5).reshape(B, S, H, dh)
    g = jax.nn.sigmoid(gate_logits.astype(F32)).reshape(B, S, H, 3)
    out = g[..., 0:1] * o_cmp + g[..., 1:2] * o_sel + g[..., 2:3] * o_win
    return out.reshape(B, S, H * dh).astype(q.dtype)


def rg_lru_branch(x_in, gate_in, conv_w, conv_b, w_a, b_a, w_i, b_i, lam):
    B, S, C = x_in.shape
    xc = (causal_depthwise_conv(x_in, conv_w) + conv_b).astype(F32)
    xh = xc.reshape(B, S, LRU_HEADS, C // LRU_HEADS)
    r = jax.nn.sigmoid(jnp.einsum('bshi,hij->bshj', xh, w_a.astype(F32)) + b_a).reshape(B, S, C)
    i = jax.nn.sigmoid(jnp.einsum('bshi,hij->bshj', xh, w_i.astype(F32)) + b_i).reshape(B, S, C)
    log_a = -LRU_C * r * jax.nn.softplus(-lam.astype(F32))
    a = jnp.exp(log_a)
    b = jnp.sqrt(-jnp.expm1(2.0 * log_a)) * (i * xc)

    def combine(c1, c2):
        a1, b1 = c1
        a2, b2 = c2
        return a1 * a2, a2 * b1 + b2

    _, h = lax.associative_scan(combine, (a, b), axis=1)
    return (h * jax.nn.gelu(gate_in.astype(F32))).astype(x_in.dtype)


def short_conv_branch(x_in, b_gate, c_gate, conv_w):
    return b_gate * causal_depthwise_conv(c_gate * x_in, conv_w)


def cross_attention(h, mem_n, w_q, w_k, w_v, w_o):
    B, S, _ = h.shape
    M = mem_n.shape[1]
    q = dense(h, w_q).reshape(B, S, XA_HEADS, HEAD_DIM)
    k = dense(mem_n, w_k).reshape(B, M, XA_HEADS, HEAD_DIM)
    v = dense(mem_n, w_v).reshape(B, M, XA_HEADS, HEAD_DIM)
    s = jnp.einsum('bshd,bmhd->bhsm', q, k, preferred_element_type=F32) * HEAD_DIM ** -0.5
    p = jax.nn.softmax(s, axis=-1).astype(v.dtype)
    o = jnp.einsum('bhsm,bmhd->bshd', p, v).reshape(B, S, XA_WIDTH)
    return dense(o, w_o)


def hierarchical_moe(h, w_group, b_group, w_expert, b_expert, w_gate, w_up, w_down):
    B, S, D = h.shape
    t = h.reshape(B * S, D)
    g_logits = (t @ w_group + b_group).astype(F32)
    g_prob = jax.nn.softmax(g_logits, axis=-1)
    g_sel = jnp.argmax(g_logits, axis=-1)
    e_logits = (t @ w_expert + b_expert).astype(F32).reshape(-1, N_GROUPS, EXPERTS_PER_GROUP)
    e_in = jnp.take_along_axis(e_logits, g_sel[:, None, None], axis=1)[:, 0]
    top_v, top_i = lax.top_k(e_in, TOPK_IN_GROUP)
    p_g = jnp.take_along_axis(g_prob, g_sel[:, None], axis=1)
    w_top = jax.nn.softmax(top_v, axis=-1) * p_g
    eid = g_sel[:, None] * EXPERTS_PER_GROUP + top_i
    combine = jnp.sum(jax.nn.one_hot(eid, N_EXPERTS, dtype=F32) * w_top[..., None], axis=1)
    hid = jax.nn.silu(jnp.einsum('td,edf->tef', t, w_gate)) * jnp.einsum('td,edf->tef', t, w_up)
    out = jnp.einsum('tef,efd->td', hid * combine[:, :, None].astype(hid.dtype), w_down)
    return out.reshape(B, S, D)


def kernel(x, mem, positions, norm_mix, w_mix_in, lru_conv_w, lru_conv_b, lru_w_a, lru_b_a, lru_w_i, lru_b_i, lru_lambda, sc_conv_w, nsa_pe_k, nsa_w1_k, nsa_w2_k, nsa_pe_v, nsa_w1_v, nsa_w2_v, w_merge_gate, b_merge_gate, w_branch_out, w_mix_out, norm_xattn, norm_mem, xa_w_q, xa_w_k, xa_w_v, xa_w_o, norm_moe, moe_w_group, moe_b_group, moe_w_expert, moe_b_expert, moe_w_gate, moe_w_up, moe_w_down, norm_final):
    B, S, D = x.shape
    T = B * S
    cos, sin = rope_tables(positions)
    cos_t = jnp.concatenate([cos, cos], axis=-1).reshape(T, HEAD_DIM)
    sin_t = jnp.concatenate([-sin, sin], axis=-1).reshape(T, HEAD_DIM)
    offs = np.concatenate([[0], np.cumsum(MIX_SPLITS)])
    seg = lambda w, k: w[:, int(offs[k]):int(offs[k + 1])]
    G, R = NSA_KV_HEADS, NSA_HEADS // NSA_KV_HEADS
    for l in range(DEPTH):
        h = rms_norm(x, norm_mix[l]).reshape(T, D).astype(BF16)
        w_in = w_mix_in[l]
        w_rope = jnp.concatenate([seg(w_in, k) for k in (0, 1, 3, 4, 6, 8)], axis=1).astype(BF16)
        w_val = jnp.concatenate([seg(w_in, k) for k in (2, 5, 7, 9)], axis=1).astype(BF16)
        w_ng = jnp.pad(seg(w_in, 10).reshape(D, G, R * 3), ((0, 0), (0, 0), (0, LANES - R * 3)))
        w_rest = jnp.concatenate([w_ng.reshape(D, G * LANES)] + [seg(w_in, k) for k in range(11, 16)],
                                 axis=1).astype(BF16)
        z_rope = matmul(h, w_rope, out_dtype=BF16, rope=(cos_t, sin_t)).reshape(B, S, -1)
        z_val = matmul(h, w_val, out_dtype=BF16).reshape(B, S, -1)
        z_rest = matmul(h, w_rest).reshape(B, S, -1)
        hb = HEAD_DIM
        o_moba = moba_attention_pallas(z_rope, z_rope, z_val, n_heads=MOBA_HEADS,
                                       q_off=0, k_off=MOBA_W // hb, v_off=0)
        kc_off = (2 * MOBA_W + NSA_Q_W) // hb
        kc = nsa_compress_pallas(z_rope[:, :, kc_off * hb:kc_off * hb + NSA_KV_W],
                                 nsa_pe_k[l], nsa_w1_k[l], nsa_w2_k[l])
        vc = nsa_compress_pallas(z_val[:, :, MOBA_W:MOBA_W + NSA_KV_W],
                                 nsa_pe_v[l], nsa_w1_v[l], nsa_w2_v[l])
        o_nsa = nsa_attention_pallas(
            z_rope, kc, vc, z_rope, z_val, z_rope, z_val, z_rest,
            q_off=2 * MOBA_W // hb, ks_off=kc_off + G, vs_off=MOBA_W // hb + G,
            kw_off=kc_off + 2 * G, vw_off=MOBA_W // hb + 2 * G)
        c0 = G * LANES
        r_x, r_g, c_b, c_c, c_x = (z_rest[:, :, c0 + k * BRANCH_WIDTH:c0 + (k + 1) * BRANCH_WIDTH]
                                   for k in range(5))
        o_lru = rg_lru_branch(r_x, r_g, lru_conv_w[l], lru_conv_b[l], lru_w_a[l], lru_b_a[l],
                              lru_w_i[l], lru_b_i[l], lru_lambda[l])
        o_conv = short_conv_branch(c_x, c_b, c_c, sc_conv_w[l])
        o_all = jnp.stack([o_moba, o_lru.astype(BF16), o_conv.astype(BF16), o_nsa]).reshape(N_BRANCH, T, -1)
        merged = merge_branches_pallas(h, o_all, w_merge_gate[l].astype(BF16),
                                       b_merge_gate[l][:, None, :], w_branch_out[l].astype(BF16))
        x = matmul(merged, w_mix_out[l].astype(BF16), residual=x.reshape(T, D)).reshape(B, S, D)
        x = x + cross_attention(rms_norm(x, norm_xattn[l]), rms_norm(mem, norm_mem[l]),
                                xa_w_q[l], xa_w_k[l], xa_w_v[l], xa_w_o[l]).astype(x.dtype)
        x = hierarchical_moe_pallas(x, norm_moe[l], moe_w_group[l], moe_b_group[l],
                                    moe_w_expert[l], moe_b_expert[l], moe_w_gate[l],
                                    moe_w_up[l], moe_w_down[l])
    return rms_norm(x, norm_final)
```

```python
import functools

import jax
import jax.numpy as jnp
import numpy as np
from jax import lax
from jax.experimental import pallas as pl
from jax.experimental.pallas import tpu as pltpu

F32 = jnp.float32
BF16 = jnp.bfloat16

D_MODEL = 4096
DEPTH = 2
HEAD_DIM = 128
ROPE_THETA = 10000.0
NORM_EPS = 1e-6
NEG = -1e30
N_BRANCH = 4
BRANCH_WIDTH = D_MODEL // 4
MOBA_HEADS = BRANCH_WIDTH // HEAD_DIM
MOBA_BLOCK = 256
MOBA_TOPK = 3
MOBA_Q_BLOCK = 64
LRU_WIDTH = BRANCH_WIDTH
LRU_HEADS = LRU_WIDTH // HEAD_DIM
LRU_C = 8.0
SC_WIDTH = BRANCH_WIDTH
NSA_HEADS = BRANCH_WIDTH // HEAD_DIM
NSA_KV_HEADS = NSA_HEADS // 4
NSA_CMP_BLOCK = 32
NSA_CMP_STRIDE = 16
NSA_SEL_BLOCK = 64
NSA_SEL_TOPK = 16
NSA_WINDOW = 512
NSA_FORCE_BONUS = 1e4
Q_BLOCK = 128
XA_HEADS = 4
XA_WIDTH = XA_HEADS * HEAD_DIM
N_GROUPS = 4
EXPERTS_PER_GROUP = 8
N_EXPERTS = N_GROUPS * EXPERTS_PER_GROUP
TOPK_IN_GROUP = 2
EXPERT_FF = D_MODEL // 8
MOBA_W = MOBA_HEADS * HEAD_DIM
NSA_Q_W = NSA_HEADS * HEAD_DIM
NSA_KV_W = NSA_KV_HEADS * HEAD_DIM
NSA_GATE_W = NSA_HEADS * 3
MIX_SPLITS = (MOBA_W, MOBA_W, MOBA_W,
              NSA_Q_W, NSA_KV_W, NSA_KV_W, NSA_KV_W, NSA_KV_W, NSA_KV_W, NSA_KV_W, NSA_GATE_W,
              LRU_WIDTH, LRU_WIDTH,
              SC_WIDTH, SC_WIDTH, SC_WIDTH)

VMEM_LIMIT_BYTES = 56 * 1024 * 1024


def _mm_kernel(*refs, rope, residual):
    if rope:
        flags_ref, refs = refs[0], refs[1:]
    a_ref, w_ref = refs[0], refs[1]
    o_ref = refs[-1]
    acc = jnp.dot(a_ref[...].astype(BF16), w_ref[...].astype(BF16), preferred_element_type=F32)
    if residual:
        acc = acc + refs[2][...]
    if rope:
        cos, sin = refs[2][...], refs[3][...]
        nblk = acc.shape[1] // HEAD_DIM
        for c in range(nblk):
            blk = acc[:, c * HEAD_DIM:(c + 1) * HEAD_DIM]
            flag = flags_ref[pl.program_id(1) * nblk + c]

            @pl.when(flag != 0)
            def _():
                rot = pltpu.roll(blk, HEAD_DIM // 2, 1)
                o_ref[:, c * HEAD_DIM:(c + 1) * HEAD_DIM] = (blk * cos + rot * sin).astype(o_ref.dtype)

            @pl.when(flag == 0)
            def _():
                o_ref[:, c * HEAD_DIM:(c + 1) * HEAD_DIM] = blk.astype(o_ref.dtype)
    else:
        o_ref[...] = acc.astype(o_ref.dtype)


def matmul(a, w, *, w_lead=(), n_cols=None, out_dtype=F32, rope=None, residual=None, tm=1024, tn=512):
    m, k = a.shape
    n = w.shape[-1] if n_cols is None else n_cols
    assert w.shape[-2] == k and not (rope is not None and residual is not None)
    tm = min(tm, m)
    tn = min(tn, n)
    lead = tuple(int(v) for v in w_lead)
    in_specs = [pl.BlockSpec((tm, k), lambda i, j, *_: (i, 0)),
                pl.BlockSpec((None,) * len(lead) + (k, tn), lambda i, j, *_: lead + (0, j))]
    args = [a, w]
    prefetch = []
    if rope is not None:
        cos, sin, flags = rope
        assert n % tn == 0 and flags.shape == (n // HEAD_DIM,)
        in_specs += [pl.BlockSpec((tm, HEAD_DIM), lambda i, j, *_: (i, 0))] * 2
        args += [cos, sin]
        prefetch = [flags]
    if residual is not None:
        in_specs.append(pl.BlockSpec((tm, tn), lambda i, j, *_: (i, j)))
        args.append(residual)
    return pl.pallas_call(
        functools.partial(_mm_kernel, rope=rope is not None, residual=residual is not None),
        grid_spec=pltpu.PrefetchScalarGridSpec(
            num_scalar_prefetch=len(prefetch),
            grid=(pl.cdiv(m, tm), pl.cdiv(n, tn)),
            in_specs=in_specs,
            out_specs=pl.BlockSpec((tm, tn), lambda i, j, *_: (i, j))),
        out_shape=jax.ShapeDtypeStruct((m, n), out_dtype),
        compiler_params=pltpu.CompilerParams(
            dimension_semantics=("parallel", "parallel"),
            vmem_limit_bytes=VMEM_LIMIT_BYTES),
        name="matmul",
    )(*prefetch, *args)


def _rms_kernel(x_ref, g_ref, o_ref):
    x = x_ref[...]
    y = x * lax.rsqrt(jnp.mean(x * x, axis=-1, keepdims=True) + NORM_EPS)
    o_ref[...] = (y * g_ref[...]).astype(o_ref.dtype)


def rms_norm_pallas(x2, gain, out_dtype, *, tm=512):
    m, d = x2.shape
    tm = min(tm, m)
    return pl.pallas_call(
        _rms_kernel,
        grid=(m // tm,),
        in_specs=[pl.BlockSpec((tm, d), lambda i: (i, 0)), pl.BlockSpec((1, d), lambda i: (0, 0))],
        out_specs=pl.BlockSpec((tm, d), lambda i: (i, 0)),
        out_shape=jax.ShapeDtypeStruct((m, d), out_dtype),
        compiler_params=pltpu.CompilerParams(
            dimension_semantics=("parallel",), vmem_limit_bytes=VMEM_LIMIT_BYTES),
        name="rms_norm",
    )(x2, gain.astype(F32)[None, :])


def _merge_kernel(h_ref, o_ref, wg_ref, bg_ref, wu_ref, out_ref, acc_ref):
    n = pl.program_id(2)
    gate = jax.nn.sigmoid(
        jnp.dot(h_ref[...], wg_ref[...].astype(BF16), preferred_element_type=F32) + bg_ref[...])
    term = gate * jnp.dot(o_ref[...], wu_ref[...].astype(BF16), preferred_element_type=F32)

    @pl.when(n == 0)
    def _():
        acc_ref[...] = term

    @pl.when(n != 0)
    def _():
        acc_ref[...] += term

    @pl.when(n == pl.num_programs(2) - 1)
    def _():
        out_ref[...] = acc_ref[...].astype(out_ref.dtype)


def merge_branches_pallas(h, o_all, w_gate, b_gate, w_up, layer, *, tm=1024, tn=512):
    T, D = h.shape
    N, _, W = o_all.shape
    l = int(layer)
    return pl.pallas_call(
        _merge_kernel,
        grid=(T // tm, D // tn, N),
        in_specs=[pl.BlockSpec((tm, D), lambda i, j, n: (i, 0)),
                  pl.BlockSpec((None, tm, W), lambda i, j, n: (n, i, 0)),
                  pl.BlockSpec((None, None, D, tn), lambda i, j, n: (l, n, 0, j)),
                  pl.BlockSpec((None, None, 1, tn), lambda i, j, n: (l, n, 0, j)),
                  pl.BlockSpec((None, None, W, tn), lambda i, j, n: (l, n, 0, j))],
        out_specs=pl.BlockSpec((tm, tn), lambda i, j, n: (i, j)),
        out_shape=jax.ShapeDtypeStruct((T, D), BF16),
        scratch_shapes=[pltpu.VMEM((tm, tn), F32)],
        compiler_params=pltpu.CompilerParams(
            dimension_semantics=("parallel", "parallel", "arbitrary"),
            vmem_limit_bytes=VMEM_LIMIT_BYTES),
        name="merge_branches",
    )(h, o_all, w_gate, b_gate[:, :, None, :], w_up)


LRU_TS = 256
LRU_CW = 256


def _rows_back(x, prev, s, row):
    return jnp.where(row < s, pltpu.roll(prev, s, 0), pltpu.roll(x, s, 0))


def _lru_conv_kernel(rx_ref, rg_ref, cb_ref, cc_ref, cx_ref, lw_ref, lb_ref, wa_ref, ba_ref,
                     wi_ref, bi_ref, lam_ref, sw_ref, olru_ref, oconv_ref, px_ref, py_ref, h_ref):
    @pl.when(pl.program_id(2) == 0)
    def _():
        px_ref[...] = jnp.zeros_like(px_ref)
        py_ref[...] = jnp.zeros_like(py_ref)
        h_ref[...] = jnp.zeros_like(h_ref)

    ts, cw = rx_ref.shape
    row = lax.broadcasted_iota(jnp.int32, (ts, cw), 0)

    x = rx_ref[...]
    prev = px_ref[...]
    nk = lw_ref.shape[0]
    xc = lb_ref[...] + lw_ref[nk - 1:nk, :] * x
    for s in range(1, nk):
        xc = xc + lw_ref[nk - 1 - s:nk - s, :] * _rows_back(x, prev, s, row)
    px_ref[...] = x
    r_parts, i_parts = [], []
    for hh in range(cw // HEAD_DIM):
        cs = slice(hh * HEAD_DIM, (hh + 1) * HEAD_DIM)
        xh = xc[:, cs].astype(BF16)
        r_parts.append(jnp.dot(xh, wa_ref[hh].astype(BF16), preferred_element_type=F32))
        i_parts.append(jnp.dot(xh, wi_ref[hh].astype(BF16), preferred_element_type=F32))
    r = jax.nn.sigmoid(jnp.concatenate(r_parts, axis=-1) + ba_ref[...])
    gi = jax.nn.sigmoid(jnp.concatenate(i_parts, axis=-1) + bi_ref[...])
    log_a = -LRU_C * r * jax.nn.softplus(-lam_ref[...])
    a = jnp.exp(log_a)
    b = jnp.sqrt(1.0 - a * a) * (gi * xc)
    d = 1
    while d < ts:
        a_back = jnp.where(row < d, 1.0, pltpu.roll(a, d, 0))
        b_back = jnp.where(row < d, 0.0, pltpu.roll(b, d, 0))
        b = a * b_back + b
        a = a * a_back
        d *= 2
    h = b + a * h_ref[...]
    h_ref[...] = h[ts - 1:ts, :]
    olru_ref[...] = (h * jax.nn.gelu(rg_ref[...])).astype(olru_ref.dtype)

    y = cc_ref[...] * cx_ref[...]
    prev_y = py_ref[...]
    nk = sw_ref.shape[0]
    conv = sw_ref[nk - 1:nk, :] * y
    for s in range(1, nk):
        conv = conv + sw_ref[nk - 1 - s:nk - s, :] * _rows_back(y, prev_y, s, row)
    py_ref[...] = y
    oconv_ref[...] = (cb_ref[...] * conv).astype(oconv_ref.dtype)


def lru_conv_pallas(z, layer, lru_conv_w, lru_conv_b, lru_w_a, lru_b_a, lru_w_i, lru_b_i, lru_lambda,
                    sc_conv_w):
    B, S, zc = z.shape
    C = zc // 5
    ts, cw = LRU_TS, LRU_CW
    nblk = C // cw
    l = int(layer)
    col = lambda k: pl.BlockSpec((None, ts, cw), lambda b, c, t: (b, t, k * nblk + c))
    par = lambda rows: pl.BlockSpec((None, rows, cw), lambda b, c, t: (l, 0, c))
    hw = pl.BlockSpec((None, cw // HEAD_DIM, HEAD_DIM, HEAD_DIM), lambda b, c, t: (l, c, 0, 0))
    L = lru_conv_b.shape[0]
    flat = lambda p: p.reshape(L, 1, C)
    out = pl.BlockSpec((None, ts, cw), lambda b, c, t: (b, t, c))
    return pl.pallas_call(
        _lru_conv_kernel,
        grid=(B, nblk, S // ts),
        in_specs=[col(0), col(1), col(2), col(3), col(4),
                  par(lru_conv_w.shape[1]), par(1), hw, par(1), hw, par(1), par(1),
                  par(sc_conv_w.shape[1])],
        out_specs=[out, out],
        out_shape=[jax.ShapeDtypeStruct((B, S, C), BF16)] * 2,
        scratch_shapes=[pltpu.VMEM((ts, cw), F32), pltpu.VMEM((ts, cw), F32), pltpu.VMEM((1, cw), F32)],
        compiler_params=pltpu.CompilerParams(
            dimension_semantics=("parallel", "parallel", "arbitrary"),
            vmem_limit_bytes=VMEM_LIMIT_BYTES),
        name="lru_conv",
    )(z, z, z, z, z, lru_conv_w, flat(lru_conv_b), lru_w_a, flat(lru_b_a), lru_w_i, flat(lru_b_i),
      flat(lru_lambda), sc_conv_w)


XA_TM = 256


def _xattn_kernel(x_ref, g_ref, wq_ref, k_ref, v_ref, wo_ref, o_ref, *, scale):
    x = x_ref[...]
    xn = (x * lax.rsqrt(jnp.mean(x * x, axis=-1, keepdims=True) + NORM_EPS) * g_ref[...]).astype(BF16)
    q = jnp.dot(xn, wq_ref[...], preferred_element_type=F32).astype(BF16)
    heads = []
    for hh in range(XA_HEADS):
        cs = slice(hh * HEAD_DIM, (hh + 1) * HEAD_DIM)
        s = lax.dot_general(q[:, cs], k_ref[:, cs], (((1,), (1,)), ((), ())),
                            preferred_element_type=F32) * scale
        e = jnp.exp(s - jnp.max(s, axis=-1, keepdims=True))
        p = (e / jnp.sum(e, axis=-1, keepdims=True)).astype(BF16)
        heads.append(jnp.dot(p, v_ref[:, cs], preferred_element_type=F32))
    o = jnp.concatenate(heads, axis=-1).astype(BF16)
    o_ref[...] = x + jnp.dot(o, wo_ref[...], preferred_element_type=F32)


def cross_attention_pallas(x, gain, k, v, w_q, w_o):
    B, S, D = x.shape
    M = k.shape[1]
    tm = XA_TM
    per_b = S // tm
    kv = pl.BlockSpec((None, M, XA_WIDTH), lambda i: (i // per_b, 0, 0))
    return pl.pallas_call(
        functools.partial(_xattn_kernel, scale=HEAD_DIM ** -0.5),
        grid=(B * per_b,),
        in_specs=[pl.BlockSpec((tm, D), lambda i: (i, 0)),
                  pl.BlockSpec((1, D), lambda i: (0, 0)),
                  pl.BlockSpec((D, XA_WIDTH), lambda i: (0, 0)),
                  kv, kv,
                  pl.BlockSpec((XA_WIDTH, D), lambda i: (0, 0))],
        out_specs=pl.BlockSpec((tm, D), lambda i: (i, 0)),
        out_shape=jax.ShapeDtypeStruct((B * S, D), F32),
        compiler_params=pltpu.CompilerParams(
            dimension_semantics=("parallel",), vmem_limit_bytes=VMEM_LIMIT_BYTES),
        name="cross_attention",
    )(x.reshape(B * S, D), gain.astype(F32)[None, :], w_q, k, v, w_o).reshape(B, S, D)


MASK_BIG = 1e30
LANES = 128


def _first_index_topk_mask(work, colf, k):
    sel = jnp.zeros(work.shape, F32)
    for _ in range(k):
        mx = jnp.max(work, axis=-1, keepdims=True)
        idx = jnp.min(jnp.where(work == mx, colf, float(LANES)), axis=-1, keepdims=True)
        pick = colf == idx
        sel = jnp.where(pick, 1.0, sel)
        work = jnp.where(pick, -jnp.inf, work)
    return sel


def _moba_kernel(q_ref, k_ref, v_ref, o_ref, kmean_ref, m_ref, l_ref, acc_ref, *, scale):
    i = pl.program_id(2)
    bs = MOBA_BLOCK
    seq = k_ref.shape[0]
    nt = (((1,), (1,)), ((), ()))

    @pl.when(i == 0)
    def _():
        row = lax.broadcasted_iota(jnp.int32, (LANES, seq), 0)
        col = lax.broadcasted_iota(jnp.int32, (LANES, seq), 1)
        ind = jnp.where(col // bs == row, 1.0, 0.0).astype(BF16)
        kmean_ref[...] = (jnp.dot(ind, k_ref[...], preferred_element_type=F32)
                          * (1.0 / bs)).astype(BF16)

    q = q_ref[...]
    gate = lax.dot_general(q, kmean_ref[...], nt, preferred_element_type=F32)
    coli = lax.broadcasted_iota(jnp.int32, (bs, LANES), 1)
    colf = coli.astype(F32)
    past = coli < i
    sel = _first_index_topk_mask(jnp.where(past, gate, NEG), colf, MOBA_TOPK)
    notsel = jnp.where((sel > 0.0) & past, 0.0, 1.0).astype(BF16)
    q_aug = jnp.concatenate([q, notsel], axis=-1)

    start = pl.multiple_of(i * bs, bs)
    s = lax.dot_general(q, k_ref[pl.ds(start, bs), :], nt, preferred_element_type=F32) * scale
    r_id = lax.broadcasted_iota(jnp.int32, (bs, bs), 0)
    c_id = lax.broadcasted_iota(jnp.int32, (bs, bs), 1)
    s = jnp.where(c_id <= r_id, s, NEG)
    m0 = jnp.max(s, axis=-1, keepdims=True)
    p = jnp.exp(s - m0)
    m_ref[...] = m0
    l_ref[...] = jnp.sum(p, axis=-1, keepdims=True)
    acc_ref[...] = jnp.dot(p.astype(BF16), v_ref[pl.ds(start, bs), :], preferred_element_type=F32)

    def body(j, carry):
        off = pl.multiple_of(j * bs, bs)
        bias = jnp.where(coli == j, -MASK_BIG, 0.0).astype(BF16)
        k_aug = jnp.concatenate([k_ref[pl.ds(off, bs), :], bias], axis=-1)
        sj = lax.dot_general(q_aug, k_aug, nt, preferred_element_type=F32) * scale
        m_prev = m_ref[...]
        m_new = jnp.maximum(m_prev, jnp.max(sj, axis=-1, keepdims=True))
        alpha = jnp.exp(m_prev - m_new)
        pj = jnp.exp(sj - m_new)
        l_ref[...] = alpha * l_ref[...] + jnp.sum(pj, axis=-1, keepdims=True)
        acc_ref[...] = alpha * acc_ref[...] + jnp.dot(pj.astype(BF16), v_ref[pl.ds(off, bs), :],
                                                      preferred_element_type=F32)
        m_ref[...] = m_new
        return carry

    lax.fori_loop(0, i, body, 0)
    o_ref[...] = (acc_ref[...] / l_ref[...]).astype(o_ref.dtype)


def moba_attention_pallas(zq, zk, zv, *, n_heads, q_off=0, k_off=0, v_off=0):
    B, S, _ = zq.shape
    bs = MOBA_BLOCK
    assert S % bs == 0 and S // bs <= LANES
    return pl.pallas_call(
        functools.partial(_moba_kernel, scale=HEAD_DIM ** -0.5),
        grid=(B, n_heads, S // bs),
        in_specs=[pl.BlockSpec((None, bs, HEAD_DIM), lambda b, h, i: (b, i, q_off + h)),
                  pl.BlockSpec((None, S, HEAD_DIM), lambda b, h, i: (b, 0, k_off + h)),
                  pl.BlockSpec((None, S, HEAD_DIM), lambda b, h, i: (b, 0, v_off + h))],
        out_specs=pl.BlockSpec((None, bs, HEAD_DIM), lambda b, h, i: (b, i, h)),
        out_shape=jax.ShapeDtypeStruct((B, S, n_heads * HEAD_DIM), BF16),
        scratch_shapes=[pltpu.VMEM((LANES, HEAD_DIM), BF16),
                        pltpu.VMEM((bs, 1), F32), pltpu.VMEM((bs, 1), F32),
                        pltpu.VMEM((bs, HEAD_DIM), F32)],
        compiler_params=pltpu.CompilerParams(
            dimension_semantics=("parallel", "parallel", "arbitrary"),
            vmem_limit_bytes=VMEM_LIMIT_BYTES),
        name="moba_attention",
    )(zq, zk, zv)


NSA_TQ = 256
NSA_NC_PAD = 256
CMP_ROW = NSA_CMP_STRIDE * HEAD_DIM


def _nsa_compress_kernel(x_ref, pe_ref, w1_ref, w2_ref, o_ref):
    x = x_ref[...].astype(F32)
    top = (x + pe_ref[0:1, :]).astype(BF16)
    bot = (x + pe_ref[1:2, :]).astype(BF16)
    a = jnp.dot(top, w1_ref[0:CMP_ROW, :], preferred_element_type=F32)
    b = jnp.dot(bot, w1_ref[CMP_ROW:2 * CMP_ROW, :], preferred_element_type=F32)
    pre = a + pltpu.roll(b, b.shape[0] - 1, 0)
    hid = jax.nn.gelu(pre)
    o_ref[...] = jnp.dot(hid.astype(BF16), w2_ref[...], preferred_element_type=F32).astype(o_ref.dtype)


def nsa_compress_pallas(x, pe, w1, w2):
    B, S, gw = x.shape
    G = gw // HEAD_DIM
    nrow = S // NSA_CMP_STRIDE
    xr = x.reshape(B, nrow, NSA_CMP_STRIDE, G, HEAD_DIM).transpose(0, 3, 1, 2, 4).reshape(B, G, nrow, CMP_ROW)
    pe2 = pe.astype(F32).reshape(2, CMP_ROW)
    return pl.pallas_call(
        _nsa_compress_kernel,
        grid=(B, G),
        in_specs=[pl.BlockSpec((None, None, nrow, CMP_ROW), lambda b, g: (b, g, 0, 0)),
                  pl.BlockSpec((2, CMP_ROW), lambda b, g: (0, 0)),
                  pl.BlockSpec((2 * CMP_ROW, HEAD_DIM), lambda b, g: (0, 0)),
                  pl.BlockSpec((HEAD_DIM, HEAD_DIM), lambda b, g: (0, 0))],
        out_specs=pl.BlockSpec((None, None, nrow, HEAD_DIM), lambda b, g: (b, g, 0, 0)),
        out_shape=jax.ShapeDtypeStruct((B, G, nrow, HEAD_DIM), BF16),
        compiler_params=pltpu.CompilerParams(
            dimension_semantics=("parallel", "parallel"), vmem_limit_bytes=VMEM_LIMIT_BYTES),
        name="nsa_compress",
    )(xr, pe2, w1.astype(BF16), w2.astype(BF16))


def _nsa_kernel(q_ref, kc_ref, vc_ref, ks_ref, vs_ref, kw_ref, vw_ref, g_ref, wmap_ref, o_ref,
                m_ref, l_ref, acc_ref, out_ref, *, scale):
    i = pl.program_id(2)
    tq = NSA_TQ
    R = NSA_HEADS // NSA_KV_HEADS
    nt = (((1,), (1,)), ((), ()))
    row = lax.broadcasted_iota(jnp.int32, (tq, tq), 0)
    col = lax.broadcasted_iota(jnp.int32, (tq, tq), 1)
    t_abs = i * tq + row
    gates = jax.nn.sigmoid(g_ref[...])

    def qh(r):
        return q_ref[:, r * HEAD_DIM:(r + 1) * HEAD_DIM]

    ncp = kc_ref.shape[0]
    cmask = (lax.broadcasted_iota(jnp.int32, (tq, ncp), 1) * NSA_CMP_STRIDE + (NSA_CMP_BLOCK - 1)
             <= i * tq + lax.broadcasted_iota(jnp.int32, (tq, ncp), 0))
    imp = jnp.zeros((tq, LANES), F32)
    for r in range(R):
        s = lax.dot_general(qh(r), kc_ref[...], nt, preferred_element_type=F32) * scale
        s = jnp.where(cmask, s, NEG)
        e = jnp.exp(s - jnp.max(s, axis=-1, keepdims=True))
        p = jnp.where(cmask, e / jnp.sum(e, axis=-1, keepdims=True), 0.0).astype(BF16)
        imp = imp + jnp.dot(p, wmap_ref[...], preferred_element_type=F32)
        o_c = jnp.dot(p, vc_ref[...], preferred_element_type=F32)
        out_ref[r] = gates[:, 3 * r:3 * r + 1] * o_c

    w_tiles = []
    for d in range(NSA_WINDOW // tq + 1):
        off = pl.multiple_of(jnp.maximum(i - d, 0) * tq, tq)
        gone = jnp.where(i >= d, 0, 2 * NSA_WINDOW + tq)
        dist = d * tq + row - col + gone
        w_tiles.append((off, (dist >= 0) & (dist < NSA_WINDOW)))
    for r in range(R):
        ss = []
        for off, mask in w_tiles:
            s = lax.dot_general(qh(r), kw_ref[pl.ds(off, tq), :], nt, preferred_element_type=F32) * scale
            ss.append(jnp.where(mask, s, NEG))
        m = jnp.max(ss[0], axis=-1, keepdims=True)
        for s in ss[1:]:
            m = jnp.maximum(m, jnp.max(s, axis=-1, keepdims=True))
        l = jnp.zeros((tq, 1), F32)
        o_w = jnp.zeros((tq, HEAD_DIM), F32)
        for (off, _), s in zip(w_tiles, ss):
            p = jnp.exp(s - m)
            l = l + jnp.sum(p, axis=-1, keepdims=True)
            o_w = o_w + jnp.dot(p.astype(BF16), vw_ref[pl.ds(off, tq), :], preferred_element_type=F32)
        out_ref[r] = out_ref[r] + gates[:, 3 * r + 2:3 * r + 3] * (o_w / l)

    rowl = lax.broadcasted_iota(jnp.int32, (tq, LANES), 0)
    coll = lax.broadcasted_iota(jnp.int32, (tq, LANES), 1)
    q_blk = (i * tq + rowl) // NSA_SEL_BLOCK
    valid = coll <= q_blk
    forced = (coll == 0) | (coll >= q_blk - 1)
    impm = jnp.where(valid, imp + jnp.where(forced, NSA_FORCE_BONUS, 0.0), NEG)
    sel = _first_index_topk_mask(impm, coll.astype(F32), NSA_SEL_TOPK)
    notsel = jnp.where((sel > 0.0) & valid, 0.0, 1.0).astype(BF16)

    blk_in_tile = tq // NSA_SEL_BLOCK
    key_blk = rowl // NSA_SEL_BLOCK

    def sel_tile(jt, causal):
        off = pl.multiple_of(jt * tq, tq)
        bias = jnp.where(coll == jt * blk_in_tile + key_blk, -MASK_BIG, 0.0).astype(BF16)
        k_aug = jnp.concatenate([ks_ref[pl.ds(off, tq), :], bias], axis=-1)
        v_t = vs_ref[pl.ds(off, tq), :]
        for r in range(R):
            q_aug = jnp.concatenate([qh(r), notsel], axis=-1)
            s = lax.dot_general(q_aug, k_aug, nt, preferred_element_type=F32) * scale
            if causal:
                s = jnp.where(col <= row, s, NEG)
                m_new = jnp.max(s, axis=-1, keepdims=True)
                p = jnp.exp(s - m_new)
                l_ref[r] = jnp.sum(p, axis=-1, keepdims=True)
                acc_ref[r] = jnp.dot(p.astype(BF16), v_t, preferred_element_type=F32)
            else:
                m_prev = m_ref[r]
                m_new = jnp.maximum(m_prev, jnp.max(s, axis=-1, keepdims=True))
                alpha = jnp.exp(m_prev - m_new)
                p = jnp.exp(s - m_new)
                l_ref[r] = alpha * l_ref[r] + jnp.sum(p, axis=-1, keepdims=True)
                acc_ref[r] = alpha * acc_ref[r] + jnp.dot(p.astype(BF16), v_t, preferred_element_type=F32)
            m_ref[r] = m_new

    sel_tile(i, True)

    def body(jt, carry):
        sel_tile(jt, False)
        return carry

    lax.fori_loop(0, i, body, 0)
    for r in range(R):
        o_s = acc_ref[r] / l_ref[r]
        o_ref[:, r * HEAD_DIM:(r + 1) * HEAD_DIM] = (
            out_ref[r] + gates[:, 3 * r + 1:3 * r + 2] * o_s).astype(o_ref.dtype)


def nsa_attention_pallas(zq, kc, vc, zks, zvs, zkw, zvw, gate_logits, *, q_off, ks_off, vs_off, kw_off, vw_off):
    B, S, _ = zq.shape
    G, tq = NSA_KV_HEADS, NSA_TQ
    R = NSA_HEADS // G
    assert S % tq == 0 and S // NSA_SEL_BLOCK <= LANES
    nc = (S - NSA_CMP_BLOCK) // NSA_CMP_STRIDE + 1
    ncp = kc.shape[2]
    ns = S // NSA_SEL_BLOCK
    wmap = jnp.pad(cmp_to_sel_weights(nc, ns), ((0, ncp - nc), (0, LANES - ns))).astype(BF16)
    rw = R * HEAD_DIM
    kv_spec = lambda off: pl.BlockSpec((None, S, HEAD_DIM), lambda b, g, i: (b, 0, off + g))
    c_spec = pl.BlockSpec((None, None, ncp, HEAD_DIM), lambda b, g, i: (b, g, 0, 0))
    return pl.pallas_call(
        functools.partial(_nsa_kernel, scale=HEAD_DIM ** -0.5),
        grid=(B, G, S // tq),
        in_specs=[pl.BlockSpec((None, tq, rw), lambda b, g, i: (b, i, q_off // R + g)),
                  c_spec, c_spec, kv_spec(ks_off), kv_spec(vs_off), kv_spec(kw_off), kv_spec(vw_off),
                  pl.BlockSpec((None, tq, LANES), lambda b, g, i: (b, i, g)),
                  pl.BlockSpec((ncp, LANES), lambda b, g, i: (0, 0))],
        out_specs=pl.BlockSpec((None, tq, rw), lambda b, g, i: (b, i, g)),
        out_shape=jax.ShapeDtypeStruct((B, S, NSA_HEADS * HEAD_DIM), BF16),
        scratch_shapes=[pltpu.VMEM((R, tq, 1), F32), pltpu.VMEM((R, tq, 1), F32),
                        pltpu.VMEM((R, tq, HEAD_DIM), F32), pltpu.VMEM((R, tq, HEAD_DIM), F32)],
        compiler_params=pltpu.CompilerParams(
            dimension_semantics=("parallel", "parallel", "arbitrary"),
            vmem_limit_bytes=VMEM_LIMIT_BYTES),
        name="nsa_attention",
    )(zq, kc, vc, zks, zvs, zkw, zvw, gate_logits, wmap)


MOE_TM = 512
MOE_TF = 256
MOE_TN = 512
ROUTER_TM = 256


def _router_kernel(x_ref, g_ref, w_ref, b_ref, t_ref, ids_ref, wts_ref):
    x = x_ref[...]
    t = x * lax.rsqrt(jnp.mean(x * x, axis=-1, keepdims=True) + NORM_EPS) * g_ref[...]
    tb = t.astype(BF16)
    bits = lax.bitcast_convert_type(tb.astype(F32), jnp.uint32)
    half = bits.shape[1] // 2
    t_ref[...] = (bits[:, :half] & jnp.uint32(0xFFFF0000)) | (bits[:, half:] >> 16)
    logits = jnp.dot(tb, w_ref[...], preferred_element_type=F32) + b_ref[...]
    coli = lax.broadcasted_iota(jnp.int32, logits.shape, 1)
    colf = coli.astype(F32)
    first = lambda hit: jnp.min(jnp.where(hit, colf, float(LANES)), axis=-1, keepdims=True)
    is_g = coli < N_GROUPS
    gl = jnp.where(is_g, logits, -jnp.inf)
    gmax = jnp.max(gl, axis=-1, keepdims=True)
    g_sel = first(gl == gmax)
    p_g = 1.0 / jnp.sum(jnp.where(is_g, jnp.exp(logits - gmax), 0.0), axis=-1, keepdims=True)
    lo = N_GROUPS + EXPERTS_PER_GROUP * g_sel
    el = jnp.where((colf >= lo) & (colf < lo + EXPERTS_PER_GROUP), logits, -jnp.inf)
    v1 = jnp.max(el, axis=-1, keepdims=True)
    i1 = first(el == v1)
    el2 = jnp.where(colf == i1, -jnp.inf, el)
    v2 = jnp.max(el2, axis=-1, keepdims=True)
    i2 = first(el2 == v2)
    e = jnp.exp(v2 - v1)
    w1 = p_g / (1.0 + e)
    w2 = p_g * e / (1.0 + e)
    ids_ref[...] = jnp.where(coli == 0, i1 - N_GROUPS, jnp.where(coli == 1, i2 - N_GROUPS, 0.0)).astype(jnp.int32)
    wts_ref[...] = jnp.where(coli == 0, w1, jnp.where(coli == 1, w2, 0.0))


def moe_router_pallas(x2, gain, w_group, b_group, w_expert, b_expert):
    T, D = x2.shape
    tm = ROUTER_TM
    nr = N_GROUPS + N_EXPERTS
    w = jnp.pad(jnp.concatenate([w_group, w_expert], axis=1), ((0, 0), (0, LANES - nr))).astype(BF16)
    b = jnp.pad(jnp.concatenate([b_group, b_expert]), (0, LANES - nr)).astype(F32)[None, :]
    t, ids, wts = pl.pallas_call(
        _router_kernel,
        grid=(T // tm,),
        in_specs=[pl.BlockSpec((tm, D), lambda i: (i, 0)),
                  pl.BlockSpec((1, D), lambda i: (0, 0)),
                  pl.BlockSpec((D, LANES), lambda i: (0, 0)),
                  pl.BlockSpec((1, LANES), lambda i: (0, 0))],
        out_specs=[pl.BlockSpec((tm, D // 2), lambda i: (i, 0)),
                   pl.BlockSpec((tm, LANES), lambda i: (i, 0)),
                   pl.BlockSpec((tm, LANES), lambda i: (i, 0))],
        out_shape=[jax.ShapeDtypeStruct((T, D // 2), jnp.uint32),
                   jax.ShapeDtypeStruct((T, LANES), jnp.int32),
                   jax.ShapeDtypeStruct((T, LANES), F32)],
        compiler_params=pltpu.CompilerParams(
            dimension_semantics=("parallel",), vmem_limit_bytes=VMEM_LIMIT_BYTES),
        name="moe_router",
    )(x2, gain.astype(F32)[None, :], w, b)
    return t, ids[:, :TOPK_IN_GROUP], wts[:, :TOPK_IN_GROUP]


def _moe_expert_kernel(tile_e_ref, n_used_ref, row_tok_ref, t_hbm, rw_ref, wg_ref, wu_ref, wd_ref, o_ref,
                       xw_ref, xb_ref, sem):
    k = pl.program_id(0)
    f = pl.program_id(1)
    tm, half = xw_ref.shape
    n_used = n_used_ref[0]

    def row_copy(tok, r):
        return pltpu.make_async_copy(t_hbm.at[pl.ds(tok, 1), :], xw_ref.at[pl.ds(r, 1), :], sem.at[0])

    def start_gather(tile):
        def issue(r, carry):
            row_copy(row_tok_ref[tile * tm + r], r).start()
            return carry
        lax.fori_loop(0, tm, issue, 0, unroll=8)

    @pl.when(f == 0)
    def _():
        o_ref[...] = jnp.zeros_like(o_ref)

        @pl.when(k == 0)
        def _():
            start_gather(0)

        @pl.when(k < n_used)
        def _():
            def wait_row(r, carry):
                row_copy(0, r).wait()
                return carry
            lax.fori_loop(0, tm, wait_row, 0, unroll=8)
            u = xw_ref[...]
            xb_ref[:, :half] = lax.bitcast_convert_type(u & jnp.uint32(0xFFFF0000), F32).astype(BF16)
            xb_ref[:, half:] = lax.bitcast_convert_type(u << 16, F32).astype(BF16)

            @pl.when(k + 1 < n_used)
            def _():
                start_gather(k + 1)

    @pl.when(k < n_used)
    def _():
        x = xb_ref[...]
        hg = jnp.dot(x, wg_ref[...].astype(BF16), preferred_element_type=F32)
        hu = jnp.dot(x, wu_ref[...].astype(BF16), preferred_element_type=F32)
        hid = (jax.nn.silu(hg) * hu * rw_ref[...]).astype(BF16)
        for c in range(0, o_ref.shape[1], MOE_TN):
            o_ref[:, c:c + MOE_TN] += jnp.dot(hid, wd_ref[:, c:c + MOE_TN].astype(BF16),
                                              preferred_element_type=F32)


def moe_experts_pallas(t_packed, row_tok, row_w, tile_e, n_used, w_gate, w_up, w_down, layer):
    P = row_tok.shape[0]
    half = t_packed.shape[1]
    D = 2 * half
    F = w_gate.shape[-1]
    tm, tf = MOE_TM, min(MOE_TF, F)
    l = int(layer)
    grid_spec = pltpu.PrefetchScalarGridSpec(
        num_scalar_prefetch=3,
        grid=(P // tm, F // tf),
        in_specs=[pl.BlockSpec(memory_space=pl.ANY),
                  pl.BlockSpec((tm, 1), lambda k, f, *_: (k, 0)),
                  pl.BlockSpec((None, None, D, tf), lambda k, f, te, *_: (l, te[k], 0, f)),
                  pl.BlockSpec((None, None, D, tf), lambda k, f, te, *_: (l, te[k], 0, f)),
                  pl.BlockSpec((None, None, tf, D), lambda k, f, te, *_: (l, te[k], f, 0))],
        out_specs=pl.BlockSpec((tm, D), lambda k, f, *_: (k, 0)),
        scratch_shapes=[pltpu.VMEM((tm, half), jnp.uint32), pltpu.VMEM((tm, D), BF16),
                        pltpu.SemaphoreType.DMA((1,))],
    )
    return pl.pallas_call(
        _moe_expert_kernel,
        grid_spec=grid_spec,
        out_shape=jax.ShapeDtypeStruct((P, D), F32),
        compiler_params=pltpu.CompilerParams(
            dimension_semantics=("arbitrary", "arbitrary"), vmem_limit_bytes=VMEM_LIMIT_BYTES),
        name="moe_experts",
    )(tile_e, n_used, row_tok, t_packed, row_w, w_gate, w_up, w_down)


def moe_dispatch_plan(ids):
    T, K = ids.shape
    n = T * K
    P = n + N_EXPERTS * MOE_TM
    flat = ids.reshape(n)
    order = jnp.argsort(flat, stable=True)
    sorted_e = flat[order]
    counts = jnp.zeros((N_EXPERTS,), jnp.int32).at[flat].add(1)
    padded = (counts + MOE_TM - 1) // MOE_TM * MOE_TM
    pend = jnp.cumsum(padded)
    pstart = pend - padded
    start = jnp.cumsum(counts) - counts
    dest = pstart[sorted_e] + (jnp.arange(n, dtype=jnp.int32) - start[sorted_e])
    pos = jnp.zeros((n,), jnp.int32).at[order].set(dest)
    row_tok = jnp.zeros((P,), jnp.int32).at[dest].set((order // K).astype(jnp.int32))
    tile_start = jnp.arange(P // MOE_TM, dtype=jnp.int32) * MOE_TM
    tile_e = jnp.minimum(jnp.searchsorted(pend, tile_start, side='right'), N_EXPERTS - 1).astype(jnp.int32)
    n_used = (pend[-1] // MOE_TM).astype(jnp.int32)[None]
    return pos.reshape(T, K), dest, order, row_tok, tile_e, n_used, P


def hierarchical_moe_pallas(x, gain, w_group, b_group, w_expert, b_expert, w_gate, w_up, w_down, layer):
    B, S, D = x.shape
    x2 = x.reshape(B * S, D)
    t, ids, wts = moe_router_pallas(x2, gain, w_group, b_group, w_expert, b_expert)
    pos, dest, order, row_tok, tile_e, n_used, P = moe_dispatch_plan(ids)
    row_w = jnp.zeros((P,), F32).at[dest].set(wts.reshape(-1)[order])[:, None]
    y = moe_experts_pallas(t, row_tok, row_w, tile_e, n_used, w_gate, w_up, w_down, layer)
    out = x2 + jnp.take(y, pos[:, 0], axis=0) + jnp.take(y, pos[:, 1], axis=0)
    return out.reshape(B, S, D)


def rms_norm(x, g):
    xf = x.astype(F32)
    y = xf * lax.rsqrt(jnp.mean(xf * xf, axis=-1, keepdims=True) + NORM_EPS)
    return (y * g.astype(F32)).astype(x.dtype)


def rope_tables(positions):
    inv = ROPE_THETA ** (-jnp.arange(0, HEAD_DIM, 2, dtype=F32) / HEAD_DIM)
    ang = positions.astype(F32)[..., None] * inv
    return jnp.cos(ang)[:, :, None, :], jnp.sin(ang)[:, :, None, :]


def rope(x, cos, sin):
    x1, x2 = jnp.split(x.astype(F32), 2, axis=-1)
    return jnp.concatenate([x1 * cos - x2 * sin, x2 * cos + x1 * sin], axis=-1).astype(x.dtype)


def split_heads(t):
    return t.reshape(t.shape[0], t.shape[1], -1, HEAD_DIM)


def split_cols(z):
    return jnp.split(z, [int(v) for v in np.cumsum(MIX_SPLITS)[:-1]], axis=-1)


def causal_depthwise_conv(x, w):
    width, c = w.shape
    xp = jnp.pad(x, ((0, 0), (width - 1, 0), (0, 0)))
    return lax.conv_general_dilated(xp, w[:, None, :].astype(x.dtype), (1,), 'VALID',
                                    dimension_numbers=('NWC', 'WIO', 'NWC'),
                                    feature_group_count=c)


def moba_attention(q, k, v):
    B, S, H, dh = q.shape
    bs = MOBA_BLOCK
    nb = -(-S // bs)
    scale = dh ** -0.5
    qh, kh, vh = (t.transpose(0, 2, 1, 3) for t in (q, k, v))
    pad = nb * bs - S
    kp = jnp.pad(kh, ((0, 0), (0, 0), (0, pad), (0, 0)))
    vp = jnp.pad(vh, ((0, 0), (0, 0), (0, pad), (0, 0)))
    kb = kp.reshape(B, H, nb, bs, dh)
    vb = vp.reshape(B, H, nb, bs, dh)
    k_mean = jnp.mean(kb.astype(F32), axis=3)
    t = jnp.arange(S)
    q_blk = t // bs
    gate = jnp.einsum('bhsd,bhnd->bhsn', qh.astype(F32), k_mean)
    past = jnp.arange(nb)[None, :] < q_blk[:, None]
    gate = jnp.where(past, gate, NEG)
    kk = min(MOBA_TOPK, nb)
    _, sel = lax.top_k(gate, kk)
    sel_ok = sel < q_blk[:, None]
    qb = MOBA_Q_BLOCK
    nqb = S // qb
    q_c = qh.reshape(B, H, nqb, qb, dh).transpose(2, 0, 1, 3, 4)
    sel_c = sel.reshape(B, H, nqb, qb, kk).transpose(2, 0, 1, 3, 4)
    ok_c = sel_ok.reshape(B, H, nqb, qb, kk).transpose(2, 0, 1, 3, 4)
    bi = jnp.arange(B)[:, None, None, None]
    hi = jnp.arange(H)[None, :, None, None]

    def chunk(args):
        qc, selc, okc, ci = args
        tq = ci * qb + jnp.arange(qb)
        own = (ci * qb) // bs * bs
        k_own = lax.dynamic_slice_in_dim(kp, own, bs, axis=2)
        v_own = lax.dynamic_slice_in_dim(vp, own, bs, axis=2)
        own_mask = (own + jnp.arange(bs))[None, :] <= tq[:, None]
        k_s = kb[bi, hi, selc].reshape(B, H, qb, kk * bs, dh)
        v_s = vb[bi, hi, selc].reshape(B, H, qb, kk * bs, dh)
        s_sel = jnp.einsum('bhqd,bhqkd->bhqk', qc, k_s, preferred_element_type=F32) * scale
        s_sel = jnp.where(jnp.repeat(okc, bs, axis=-1), s_sel, NEG)
        s_own = jnp.einsum('bhqd,bhkd->bhqk', qc, k_own, preferred_element_type=F32) * scale
        s_own = jnp.where(own_mask, s_own, NEG)
        p = jax.nn.softmax(jnp.concatenate([s_sel, s_own], axis=-1), axis=-1).astype(v.dtype)
        return (jnp.einsum('bhqk,bhqkd->bhqd', p[..., :kk * bs], v_s)
                + jnp.einsum('bhqk,bhkd->bhqd', p[..., kk * bs:], v_own))

    o = lax.map(chunk, (q_c, sel_c, ok_c, jnp.arange(nqb)))
    return o.transpose(1, 0, 3, 2, 4).reshape(B, S, H * dh)


def compress_tokens(x, pe, w1, w2):
    B, S, G, dh = x.shape
    nc = (S - NSA_CMP_BLOCK) // NSA_CMP_STRIDE + 1
    idx = jnp.arange(nc)[:, None] * NSA_CMP_STRIDE + jnp.arange(NSA_CMP_BLOCK)[None, :]
    blocks = x[:, idx] + pe[:, None, :]
    flat = blocks.transpose(0, 1, 3, 2, 4).reshape(B, nc, G, NSA_CMP_BLOCK * dh)
    return jax.nn.gelu(flat @ w1) @ w2


def cmp_to_sel_weights(nc, ns):
    r = NSA_SEL_BLOCK // NSA_CMP_STRIDE
    m = NSA_CMP_BLOCK // NSA_CMP_STRIDE
    c = jnp.arange(nc)[:, None] - r * jnp.arange(ns)[None, :]
    w = jnp.minimum(jnp.minimum(c + 1, r + m - 1 - c), min(r, m))
    return jnp.clip(w, 0, None).astype(F32)


def nsa_attention(q, k_cmp, v_cmp, k_sel, v_sel, k_win, v_win, gate_logits,
                  pe_k, w1_k, w2_k, pe_v, w1_v, w2_v):
    B, S, H, dh = q.shape
    G = k_sel.shape[2]
    R = H // G
    scale = dh ** -0.5
    t = jnp.arange(S)
    qg = q.reshape(B, S, G, R, dh)
    kc = compress_tokens(k_cmp, pe_k, w1_k, w2_k)
    vc = compress_tokens(v_cmp, pe_v, w1_v, w2_v)
    nc = kc.shape[1]
    s_c = jnp.einsum('bsgrd,bngd->bgrsn', qg, kc, preferred_element_type=F32) * scale
    c_end = jnp.arange(nc) * NSA_CMP_STRIDE + NSA_CMP_BLOCK - 1
    c_mask = c_end[None, :] <= t[:, None]
    p_c = jax.nn.softmax(jnp.where(c_mask, s_c, NEG), axis=-1) * c_mask
    o_cmp = jnp.einsum('bgrsn,bngd->bsgrd', p_c.astype(vc.dtype), vc).reshape(B, S, H, dh)
    ns = S // NSA_SEL_BLOCK
    imp = jnp.einsum('bgrsn,nm->bgsm', p_c, cmp_to_sel_weights(nc, ns))
    q_blk = t // NSA_SEL_BLOCK
    j = jnp.arange(ns)[None, :]
    valid = j <= q_blk[:, None]
    forced = (j == 0) | (j >= q_blk[:, None] - 1)
    imp = jnp.where(valid, imp + jnp.where(forced, NSA_FORCE_BONUS, 0.0), NEG)
    kk = min(NSA_SEL_TOPK, ns)
    _, sel = lax.top_k(imp, kk)
    sel_ok = sel <= q_blk[:, None]
    sb = NSA_SEL_BLOCK
    k_blocks = k_sel.transpose(0, 2, 1, 3).reshape(B, G, ns, sb, dh)
    v_blocks = v_sel.transpose(0, 2, 1, 3).reshape(B, G, ns, sb, dh)
    wpad = ((0, 0), (0, 0), (NSA_WINDOW, 0), (0, 0))
    k_wp = jnp.pad(k_win.transpose(0, 2, 1, 3), wpad)
    v_wp = jnp.pad(v_win.transpose(0, 2, 1, 3), wpad)
    nqb = S // Q_BLOCK
    q_c = qg.transpose(0, 2, 3, 1, 4).reshape(B, G, R, nqb, Q_BLOCK, dh).transpose(3, 0, 1, 2, 4, 5)
    sel_c = sel.reshape(B, G, nqb, Q_BLOCK, kk).transpose(2, 0, 1, 3, 4)
    ok_c = sel_ok.reshape(B, G, nqb, Q_BLOCK, kk).transpose(2, 0, 1, 3, 4)
    bi = jnp.arange(B)[:, None, None, None]
    gi = jnp.arange(G)[None, :, None, None]

    def chunk(args):
        qc, selc, okc, ci = args
        tq = ci * Q_BLOCK + jnp.arange(Q_BLOCK)
        ks = k_blocks[bi, gi, selc].reshape(B, G, Q_BLOCK, kk * sb, dh)
        vs = v_blocks[bi, gi, selc].reshape(B, G, Q_BLOCK, kk * sb, dh)
        kpos = (selc[..., None] * sb + jnp.arange(sb)).reshape(B, G, Q_BLOCK, kk * sb)
        m = (kpos <= tq[:, None]) & jnp.repeat(okc, sb, axis=-1)
        s = jnp.einsum('bgrqd,bgqkd->bgrqk', qc, ks, preferred_element_type=F32) * scale
        p = jax.nn.softmax(jnp.where(m[:, :, None], s, NEG), axis=-1)
        o_s = jnp.einsum('bgrqk,bgqkd->bgrqd', p.astype(vs.dtype), vs)
        kw = lax.dynamic_slice_in_dim(k_wp, ci * Q_BLOCK, Q_BLOCK + NSA_WINDOW, axis=2)
        vw = lax.dynamic_slice_in_dim(v_wp, ci * Q_BLOCK, Q_BLOCK + NSA_WINDOW, axis=2)
        wpos = ci * Q_BLOCK - NSA_WINDOW + jnp.arange(Q_BLOCK + NSA_WINDOW)
        dist = tq[:, None] - wpos[None, :]
        wm = (dist >= 0) & (dist < NSA_WINDOW) & (wpos[None, :] >= 0)
        s_w = jnp.einsum('bgrqd,bgkd->bgrqk', qc, kw, preferred_element_type=F32) * scale
        p_w = jax.nn.softmax(jnp.where(wm, s_w, NEG), axis=-1)
        o_w = jnp.einsum('bgrqk,bgkd->bgrqd', p_w.astype(vw.dtype), vw)
        return o_s, o_w

    o_sel, o_win = lax.map(chunk, (q_c, sel_c, ok_c, jnp.arange(nqb)))
    o_sel = o_sel.transpose(1, 0, 4, 2, 3, 5).reshape(B, S, H, dh)
    o_win = o_win.transpose(1, 0, 4, 2, 3, 5).reshape(B, S, H, dh)
    g = jax.nn.sigmoid(gate_logits.astype(F32)).reshape(B, S, H, 3)
    out = g[..., 0:1] * o_cmp + g[..., 1:2] * o_sel + g[..., 2:3] * o_win
    return out.reshape(B, S, H * dh).astype(q.dtype)


def rg_lru_branch(x_in, gate_in, conv_w, conv_b, w_a, b_a, w_i, b_i, lam):
    B, S, C = x_in.shape
    xc = (causal_depthwise_conv(x_in, conv_w) + conv_b).astype(F32)
    xh = xc.reshape(B, S, LRU_HEADS, C // LRU_HEADS)
    r = jax.nn.sigmoid(jnp.einsum('bshi,hij->bshj', xh, w_a.astype(F32)) + b_a).reshape(B, S, C)
    i = jax.nn.sigmoid(jnp.einsum('bshi,hij->bshj', xh, w_i.astype(F32)) + b_i).reshape(B, S, C)
    log_a = -LRU_C * r * jax.nn.softplus(-lam.astype(F32))
    a = jnp.exp(log_a)
    b = jnp.sqrt(-jnp.expm1(2.0 * log_a)) * (i * xc)

    def combine(c1, c2):
        a1, b1 = c1
        a2, b2 = c2
        return a1 * a2, a2 * b1 + b2

    _, h = lax.associative_scan(combine, (a, b), axis=1)
    return (h * jax.nn.gelu(gate_in.astype(F32))).astype(x_in.dtype)


def short_conv_branch(x_in, b_gate, c_gate, conv_w):
    return b_gate * causal_depthwise_conv(c_gate * x_in, conv_w)


def cross_attention(h, mem_n, w_q, w_k, w_v, w_o):
    B, S, _ = h.shape
    M = mem_n.shape[1]
    q = dense(h, w_q).reshape(B, S, XA_HEADS, HEAD_DIM)
    k = dense(mem_n, w_k).reshape(B, M, XA_HEADS, HEAD_DIM)
    v = dense(mem_n, w_v).reshape(B, M, XA_HEADS, HEAD_DIM)
    s = jnp.einsum('bshd,bmhd->bhsm', q, k, preferred_element_type=F32) * HEAD_DIM ** -0.5
    p = jax.nn.softmax(s, axis=-1).astype(v.dtype)
    o = jnp.einsum('bhsm,bmhd->bshd', p, v).reshape(B, S, XA_WIDTH)
    return dense(o, w_o)


def hierarchical_moe(h, w_group, b_group, w_expert, b_expert, w_gate, w_up, w_down):
    B, S, D = h.shape
    t = h.reshape(B * S, D)
    g_logits = (t @ w_group + b_group).astype(F32)
    g_prob = jax.nn.softmax(g_logits, axis=-1)
    g_sel = jnp.argmax(g_logits, axis=-1)
    e_logits = (t @ w_expert + b_expert).astype(F32).reshape(-1, N_GROUPS, EXPERTS_PER_GROUP)
    e_in = jnp.take_along_axis(e_logits, g_sel[:, None, None], axis=1)[:, 0]
    top_v, top_i = lax.top_k(e_in, TOPK_IN_GROUP)
    p_g = jnp.take_along_axis(g_prob, g_sel[:, None], axis=1)
    w_top = jax.nn.softmax(top_v, axis=-1) * p_g
    eid = g_sel[:, None] * EXPERTS_PER_GROUP + top_i
    combine = jnp.sum(jax.nn.one_hot(eid, N_EXPERTS, dtype=F32) * w_top[..., None], axis=1)
    hid = jax.nn.silu(jnp.einsum('td,edf->tef', t, w_gate)) * jnp.einsum('td,edf->tef', t, w_up)
    out = jnp.einsum('tef,efd->td', hid * combine[:, :, None].astype(hid.dtype), w_down)
    return out.reshape(B, S, D)


def kernel(x, mem, positions, norm_mix, w_mix_in, lru_conv_w, lru_conv_b, lru_w_a, lru_b_a, lru_w_i, lru_b_i, lru_lambda, sc_conv_w, nsa_pe_k, nsa_w1_k, nsa_w2_k, nsa_pe_v, nsa_w1_v, nsa_w2_v, w_merge_gate, b_merge_gate, w_branch_out, w_mix_out, norm_xattn, norm_mem, xa_w_q, xa_w_k, xa_w_v, xa_w_o, norm_moe, moe_w_group, moe_b_group, moe_w_expert, moe_b_expert, moe_w_gate, moe_w_up, moe_w_down, norm_final):
    B, S, D = x.shape
    T = B * S
    cos, sin = rope_tables(positions)
    cos_t = jnp.concatenate([cos, cos], axis=-1).reshape(T, HEAD_DIM)
    sin_t = jnp.concatenate([-sin, sin], axis=-1).reshape(T, HEAD_DIM)
    offs = [int(v) for v in np.concatenate([[0], np.cumsum(MIX_SPLITS)])]
    blk = [v // HEAD_DIM for v in offs[:11]]
    n_head_cols = offs[10]
    rope_flags = np.zeros((n_head_cols // HEAD_DIM,), np.int32)
    for k in (0, 1, 3, 4, 6, 8):
        rope_flags[blk[k]:blk[k + 1]] = 1
    rope_flags = jnp.asarray(rope_flags)
    G, R = NSA_KV_HEADS, NSA_HEADS // NSA_KV_HEADS
    L = w_mix_in.shape[0]
    w_tail = w_mix_in[:, :, offs[11]:]
    w_ng = jnp.pad(w_mix_in[:, :, offs[10]:offs[11]].reshape(L, D, G, R * 3),
                   ((0, 0), (0, 0), (0, 0), (0, LANES - R * 3))).reshape(L, D, G * LANES)
    M = mem.shape[1]
    for l in range(DEPTH):
        x2 = x.reshape(T, D)
        h = rms_norm_pallas(x2, norm_mix[l], BF16)
        z_head = matmul(h, w_mix_in, w_lead=(l,), n_cols=n_head_cols, out_dtype=BF16,
                        rope=(cos_t, sin_t, rope_flags)).reshape(B, S, n_head_cols)
        z_tail = matmul(h, w_tail, w_lead=(l,)).reshape(B, S, -1)
        z_gate = matmul(h, w_ng, w_lead=(l,), tn=G * LANES).reshape(B, S, G * LANES)
        o_moba = moba_attention_pallas(z_head, z_head, z_head, n_heads=MOBA_HEADS,
                                       q_off=blk[0], k_off=blk[1], v_off=blk[2])
        kc = nsa_compress_pallas(z_head[:, :, offs[4]:offs[5]], nsa_pe_k[l], nsa_w1_k[l], nsa_w2_k[l])
        vc = nsa_compress_pallas(z_head[:, :, offs[5]:offs[6]], nsa_pe_v[l], nsa_w1_v[l], nsa_w2_v[l])
        o_nsa = nsa_attention_pallas(z_head, kc, vc, z_head, z_head, z_head, z_head, z_gate,
                                     q_off=blk[3], ks_off=blk[6], vs_off=blk[7], kw_off=blk[8], vw_off=blk[9])
        o_lru, o_conv = lru_conv_pallas(z_tail, l, lru_conv_w, lru_conv_b, lru_w_a, lru_b_a,
                                        lru_w_i, lru_b_i, lru_lambda, sc_conv_w)
        o_all = jnp.stack([o_moba, o_lru, o_conv, o_nsa]).reshape(N_BRANCH, T, -1)
        merged = merge_branches_pallas(h, o_all, w_merge_gate, b_merge_gate, w_branch_out, l)
        x2 = matmul(merged, w_mix_out, w_lead=(l,), residual=x2)
        mem_n = rms_norm_pallas(mem.reshape(B * M, D), norm_mem[l], BF16)
        xk = matmul(mem_n, xa_w_k, w_lead=(l,), out_dtype=BF16).reshape(B, M, XA_WIDTH)
        xv = matmul(mem_n, xa_w_v, w_lead=(l,), out_dtype=BF16).reshape(B, M, XA_WIDTH)
        x = cross_attention_pallas(x2.reshape(B, S, D), norm_xattn[l], xk, xv,
                                   xa_w_q[l].astype(BF16), xa_w_o[l].astype(BF16))
        x = hierarchical_moe_pallas(x, norm_moe[l], moe_w_group[l], moe_b_group[l],
                                    moe_w_expert[l], moe_b_expert[l], moe_w_gate,
                                    moe_w_up, moe_w_down, l)
    return rms_norm_pallas(x.reshape(T, D), norm_final, F32).reshape(B, S, D)
```

```python
import functools

import jax
import jax.numpy as jnp
import numpy as np
from jax import lax
from jax.experimental import pallas as pl
from jax.experimental.pallas import tpu as pltpu

F32 = jnp.float32
BF16 = jnp.bfloat16

D_MODEL = 4096
DEPTH = 2
HEAD_DIM = 128
ROPE_THETA = 10000.0
NORM_EPS = 1e-6
NEG = -1e30
N_BRANCH = 4
BRANCH_WIDTH = D_MODEL // 4
MOBA_HEADS = BRANCH_WIDTH // HEAD_DIM
MOBA_BLOCK = 256
MOBA_TOPK = 3
MOBA_Q_BLOCK = 64
LRU_WIDTH = BRANCH_WIDTH
LRU_HEADS = LRU_WIDTH // HEAD_DIM
LRU_C = 8.0
SC_WIDTH = BRANCH_WIDTH
NSA_HEADS = BRANCH_WIDTH // HEAD_DIM
NSA_KV_HEADS = NSA_HEADS // 4
NSA_CMP_BLOCK = 32
NSA_CMP_STRIDE = 16
NSA_SEL_BLOCK = 64
NSA_SEL_TOPK = 16
NSA_WINDOW = 512
NSA_FORCE_BONUS = 1e4
Q_BLOCK = 128
XA_HEADS = 4
XA_WIDTH = XA_HEADS * HEAD_DIM
N_GROUPS = 4
EXPERTS_PER_GROUP = 8
N_EXPERTS = N_GROUPS * EXPERTS_PER_GROUP
TOPK_IN_GROUP = 2
EXPERT_FF = D_MODEL // 8
MOBA_W = MOBA_HEADS * HEAD_DIM
NSA_Q_W = NSA_HEADS * HEAD_DIM
NSA_KV_W = NSA_KV_HEADS * HEAD_DIM
NSA_GATE_W = NSA_HEADS * 3
MIX_SPLITS = (MOBA_W, MOBA_W, MOBA_W,
              NSA_Q_W, NSA_KV_W, NSA_KV_W, NSA_KV_W, NSA_KV_W, NSA_KV_W, NSA_KV_W, NSA_GATE_W,
              LRU_WIDTH, LRU_WIDTH,
              SC_WIDTH, SC_WIDTH, SC_WIDTH)

VMEM_LIMIT_BYTES = 56 * 1024 * 1024


ROPE_PLAIN, ROPE_SCALED = 1, 2


def _mm_kernel(*refs, rope, residual):
    if rope:
        flags_ref, refs = refs[0], refs[1:]
    a_ref, w_ref = refs[0], refs[1]
    o_ref = refs[-1]
    acc = jnp.dot(a_ref[...].astype(BF16), w_ref[...].astype(BF16), preferred_element_type=F32)
    if residual:
        acc = acc + refs[2][...]
    if rope:
        cos, sin = refs[2][...], refs[3][...]
        nblk = acc.shape[1] // HEAD_DIM
        for c in range(nblk):
            blk = acc[:, c * HEAD_DIM:(c + 1) * HEAD_DIM]
            flag = flags_ref[pl.program_id(1) * nblk + c]

            @pl.when(flag != 0)
            def _():
                rot = pltpu.roll(blk, HEAD_DIM // 2, 1)
                scl = jnp.where(flag == ROPE_SCALED, HEAD_DIM ** -0.5, 1.0)
                o_ref[:, c * HEAD_DIM:(c + 1) * HEAD_DIM] = ((blk * cos + rot * sin) * scl).astype(o_ref.dtype)

            @pl.when(flag == 0)
            def _():
                o_ref[:, c * HEAD_DIM:(c + 1) * HEAD_DIM] = blk.astype(o_ref.dtype)
    else:
        o_ref[...] = acc.astype(o_ref.dtype)


def matmul(a, w, *, w_lead=(), n_cols=None, out_dtype=F32, rope=None, residual=None, tm=1024, tn=512):
    m, k = a.shape
    n = w.shape[-1] if n_cols is None else n_cols
    assert w.shape[-2] == k and not (rope is not None and residual is not None)
    tm = min(tm, m)
    tn = min(tn, n)
    lead = tuple(int(v) for v in w_lead)
    in_specs = [pl.BlockSpec((tm, k), lambda i, j, *_: (i, 0)),
                pl.BlockSpec((None,) * len(lead) + (k, tn), lambda i, j, *_: lead + (0, j))]
    args = [a, w]
    prefetch = []
    if rope is not None:
        cos, sin, flags = rope
        assert n % tn == 0 and flags.shape == (n // HEAD_DIM,)
        in_specs += [pl.BlockSpec((tm, HEAD_DIM), lambda i, j, *_: (i, 0))] * 2
        args += [cos, sin]
        prefetch = [flags]
    if residual is not None:
        in_specs.append(pl.BlockSpec((tm, tn), lambda i, j, *_: (i, j)))
        args.append(residual)
    return pl.pallas_call(
        functools.partial(_mm_kernel, rope=rope is not None, residual=residual is not None),
        grid_spec=pltpu.PrefetchScalarGridSpec(
            num_scalar_prefetch=len(prefetch),
            grid=(pl.cdiv(m, tm), pl.cdiv(n, tn)),
            in_specs=in_specs,
            out_specs=pl.BlockSpec((tm, tn), lambda i, j, *_: (i, j))),
        out_shape=jax.ShapeDtypeStruct((m, n), out_dtype),
        compiler_params=pltpu.CompilerParams(
            dimension_semantics=("parallel", "parallel"),
            vmem_limit_bytes=VMEM_LIMIT_BYTES),
        name="matmul",
    )(*prefetch, *args)


def _rms_kernel(x_ref, g_ref, o_ref):
    x = x_ref[...]
    y = x * lax.rsqrt(jnp.mean(x * x, axis=-1, keepdims=True) + NORM_EPS)
    o_ref[...] = (y * g_ref[...]).astype(o_ref.dtype)


def rms_norm_pallas(x2, gain, out_dtype, *, tm=512):
    m, d = x2.shape
    tm = min(tm, m)
    return pl.pallas_call(
        _rms_kernel,
        grid=(m // tm,),
        in_specs=[pl.BlockSpec((tm, d), lambda i: (i, 0)), pl.BlockSpec((1, d), lambda i: (0, 0))],
        out_specs=pl.BlockSpec((tm, d), lambda i: (i, 0)),
        out_shape=jax.ShapeDtypeStruct((m, d), out_dtype),
        compiler_params=pltpu.CompilerParams(
            dimension_semantics=("parallel",), vmem_limit_bytes=VMEM_LIMIT_BYTES),
        name="rms_norm",
    )(x2, gain.astype(F32)[None, :])


def _merge_kernel(h_ref, o_ref, wg_ref, bg_ref, wu_ref, out_ref, acc_ref):
    n = pl.program_id(2)
    gate = jax.nn.sigmoid(
        jnp.dot(h_ref[...], wg_ref[...].astype(BF16), preferred_element_type=F32) + bg_ref[...])
    term = gate * jnp.dot(o_ref[...], wu_ref[...].astype(BF16), preferred_element_type=F32)

    @pl.when(n == 0)
    def _():
        acc_ref[...] = term

    @pl.when(n != 0)
    def _():
        acc_ref[...] += term

    @pl.when(n == pl.num_programs(2) - 1)
    def _():
        out_ref[...] = acc_ref[...].astype(out_ref.dtype)


def merge_branches_pallas(h, o_all, w_gate, b_gate, w_up, layer, *, tm=1024, tn=512):
    T, D = h.shape
    N, _, W = o_all.shape
    l = int(layer)
    return pl.pallas_call(
        _merge_kernel,
        grid=(T // tm, D // tn, N),
        in_specs=[pl.BlockSpec((tm, D), lambda i, j, n: (i, 0)),
                  pl.BlockSpec((None, tm, W), lambda i, j, n: (n, i, 0)),
                  pl.BlockSpec((None, None, D, tn), lambda i, j, n: (l, n, 0, j)),
                  pl.BlockSpec((None, None, 1, tn), lambda i, j, n: (l, n, 0, j)),
                  pl.BlockSpec((None, None, W, tn), lambda i, j, n: (l, n, 0, j))],
        out_specs=pl.BlockSpec((tm, tn), lambda i, j, n: (i, j)),
        out_shape=jax.ShapeDtypeStruct((T, D), BF16),
        scratch_shapes=[pltpu.VMEM((tm, tn), F32)],
        compiler_params=pltpu.CompilerParams(
            dimension_semantics=("parallel", "parallel", "arbitrary"),
            vmem_limit_bytes=VMEM_LIMIT_BYTES),
        name="merge_branches",
    )(h, o_all, w_gate, b_gate[:, :, None, :], w_up)


LRU_TS = 256
LRU_CW = 256


def _rows_back(x, prev, s, row):
    return jnp.where(row < s, pltpu.roll(prev, s, 0), pltpu.roll(x, s, 0))


def _lru_conv_kernel(rx_ref, rg_ref, cb_ref, cc_ref, cx_ref, lw_ref, lb_ref, wa_ref, ba_ref,
                     wi_ref, bi_ref, lam_ref, sw_ref, olru_ref, oconv_ref, px_ref, py_ref, h_ref):
    @pl.when(pl.program_id(2) == 0)
    def _():
        px_ref[...] = jnp.zeros_like(px_ref)
        py_ref[...] = jnp.zeros_like(py_ref)
        h_ref[...] = jnp.zeros_like(h_ref)

    ts, cw = rx_ref.shape
    row = lax.broadcasted_iota(jnp.int32, (ts, cw), 0)

    x = rx_ref[...]
    prev = px_ref[...]
    nk = lw_ref.shape[0]
    xc = lb_ref[...] + lw_ref[nk - 1:nk, :] * x
    for s in range(1, nk):
        xc = xc + lw_ref[nk - 1 - s:nk - s, :] * _rows_back(x, prev, s, row)
    px_ref[...] = x
    r_parts, i_parts = [], []
    for hh in range(cw // HEAD_DIM):
        cs = slice(hh * HEAD_DIM, (hh + 1) * HEAD_DIM)
        xh = xc[:, cs].astype(BF16)
        r_parts.append(jnp.dot(xh, wa_ref[hh].astype(BF16), preferred_element_type=F32))
        i_parts.append(jnp.dot(xh, wi_ref[hh].astype(BF16), preferred_element_type=F32))
    r = jax.nn.sigmoid(jnp.concatenate(r_parts, axis=-1) + ba_ref[...])
    gi = jax.nn.sigmoid(jnp.concatenate(i_parts, axis=-1) + bi_ref[...])
    log_a = -LRU_C * r * jax.nn.softplus(-lam_ref[...])
    a = jnp.exp(log_a)
    b = jnp.sqrt(1.0 - a * a) * (gi * xc)
    d = 1
    while d < ts:
        a_back = jnp.where(row < d, 1.0, pltpu.roll(a, d, 0))
        b_back = jnp.where(row < d, 0.0, pltpu.roll(b, d, 0))
        b = a * b_back + b
        a = a * a_back
        d *= 2
    h = b + a * h_ref[...]
    h_ref[...] = h[ts - 1:ts, :]
    olru_ref[...] = (h * jax.nn.gelu(rg_ref[...])).astype(olru_ref.dtype)

    y = cc_ref[...] * cx_ref[...]
    prev_y = py_ref[...]
    nk = sw_ref.shape[0]
    conv = sw_ref[nk - 1:nk, :] * y
    for s in range(1, nk):
        conv = conv + sw_ref[nk - 1 - s:nk - s, :] * _rows_back(y, prev_y, s, row)
    py_ref[...] = y
    oconv_ref[...] = (cb_ref[...] * conv).astype(oconv_ref.dtype)


def lru_conv_pallas(z, layer, lru_conv_w, lru_conv_b, lru_w_a, lru_b_a, lru_w_i, lru_b_i, lru_lambda,
                    sc_conv_w):
    B, S, zc = z.shape
    C = zc // 5
    ts, cw = LRU_TS, LRU_CW
    nblk = C // cw
    l = int(layer)
    col = lambda k: pl.BlockSpec((None, ts, cw), lambda b, c, t: (b, t, k * nblk + c))
    par = lambda rows: pl.BlockSpec((None, rows, cw), lambda b, c, t: (l, 0, c))
    hw = pl.BlockSpec((None, cw // HEAD_DIM, HEAD_DIM, HEAD_DIM), lambda b, c, t: (l, c, 0, 0))
    L = lru_conv_b.shape[0]
    flat = lambda p: p.reshape(L, 1, C)
    out = pl.BlockSpec((None, ts, cw), lambda b, c, t: (b, t, c))
    return pl.pallas_call(
        _lru_conv_kernel,
        grid=(B, nblk, S // ts),
        in_specs=[col(0), col(1), col(2), col(3), col(4),
                  par(lru_conv_w.shape[1]), par(1), hw, par(1), hw, par(1), par(1),
                  par(sc_conv_w.shape[1])],
        out_specs=[out, out],
        out_shape=[jax.ShapeDtypeStruct((B, S, C), BF16)] * 2,
        scratch_shapes=[pltpu.VMEM((ts, cw), F32), pltpu.VMEM((ts, cw), F32), pltpu.VMEM((1, cw), F32)],
        compiler_params=pltpu.CompilerParams(
            dimension_semantics=("parallel", "parallel", "arbitrary"),
            vmem_limit_bytes=VMEM_LIMIT_BYTES),
        name="lru_conv",
    )(z, z, z, z, z, lru_conv_w, flat(lru_conv_b), lru_w_a, flat(lru_b_a), lru_w_i, flat(lru_b_i),
      flat(lru_lambda), sc_conv_w)


XA_TM = 256


def _xattn_kernel(x_ref, g_ref, wq_ref, k_ref, v_ref, wo_ref, o_ref, *, scale):
    x = x_ref[...]
    xn = (x * lax.rsqrt(jnp.mean(x * x, axis=-1, keepdims=True) + NORM_EPS) * g_ref[...]).astype(BF16)
    q = jnp.dot(xn, wq_ref[...], preferred_element_type=F32).astype(BF16)
    heads = []
    for hh in range(XA_HEADS):
        cs = slice(hh * HEAD_DIM, (hh + 1) * HEAD_DIM)
        s = lax.dot_general(q[:, cs], k_ref[:, cs], (((1,), (1,)), ((), ())),
                            preferred_element_type=F32) * scale
        e = jnp.exp(s - jnp.max(s, axis=-1, keepdims=True))
        p = (e / jnp.sum(e, axis=-1, keepdims=True)).astype(BF16)
        heads.append(jnp.dot(p, v_ref[:, cs], preferred_element_type=F32))
    o = jnp.concatenate(heads, axis=-1).astype(BF16)
    o_ref[...] = x + jnp.dot(o, wo_ref[...], preferred_element_type=F32)


def cross_attention_pallas(x, gain, k, v, w_q, w_o):
    B, S, D = x.shape
    M = k.shape[1]
    tm = XA_TM
    per_b = S // tm
    kv = pl.BlockSpec((None, M, XA_WIDTH), lambda i: (i // per_b, 0, 0))
    return pl.pallas_call(
        functools.partial(_xattn_kernel, scale=HEAD_DIM ** -0.5),
        grid=(B * per_b,),
        in_specs=[pl.BlockSpec((tm, D), lambda i: (i, 0)),
                  pl.BlockSpec((1, D), lambda i: (0, 0)),
                  pl.BlockSpec((D, XA_WIDTH), lambda i: (0, 0)),
                  kv, kv,
                  pl.BlockSpec((XA_WIDTH, D), lambda i: (0, 0))],
        out_specs=pl.BlockSpec((tm, D), lambda i: (i, 0)),
        out_shape=jax.ShapeDtypeStruct((B * S, D), F32),
        compiler_params=pltpu.CompilerParams(
            dimension_semantics=("parallel",), vmem_limit_bytes=VMEM_LIMIT_BYTES),
        name="cross_attention",
    )(x.reshape(B * S, D), gain.astype(F32)[None, :], w_q, k, v, w_o).reshape(B, S, D)


MASK_BIG = 1e30
LANES = 128


def _first_index_topk_mask(work, colf, k):
    sel = jnp.zeros(work.shape, F32)
    for _ in range(k):
        mx = jnp.max(work, axis=-1, keepdims=True)
        idx = jnp.min(jnp.where(work == mx, colf, float(LANES)), axis=-1, keepdims=True)
        pick = colf == idx
        sel = jnp.where(pick, 1.0, sel)
        work = jnp.where(pick, -jnp.inf, work)
    return sel


ATT_TK = 512
MOBA_HEADS_PER_STEP = 4
M_INIT = -1e38
NT_DIMS = (((1,), (1,)), ((), ()))


def _flash_update(q_aug, k_aug, v_aug, m_ref, acc_ref, causal):
    s = lax.dot_general(q_aug, k_aug, NT_DIMS, preferred_element_type=F32)
    if causal is not None:
        q0, k0 = causal
        row = lax.broadcasted_iota(jnp.int32, s.shape, 0)
        col = lax.broadcasted_iota(jnp.int32, s.shape, 1)
        s = jnp.where(k0 + col <= q0 + row, s, NEG)
    m_prev = m_ref[...]
    m_new = jnp.maximum(m_prev, jnp.max(s, axis=-1, keepdims=True))
    alpha = jnp.exp(m_prev - m_new)
    p = jnp.exp(s - m_new).astype(BF16)
    acc_ref[...] = alpha * acc_ref[...] + jnp.dot(p, v_aug, preferred_element_type=F32)
    m_ref[...] = m_new


def _moba_kernel(q_ref, k_ref, v_ref, o_ref, kmean_ref, qa_ref, m_ref, acc_ref):
    i = pl.program_id(2)
    bs = MOBA_BLOCK
    seq = k_ref.shape[0]
    hp = q_ref.shape[1] // HEAD_DIM
    nt = NT_DIMS
    hcol = lambda h: slice(h * HEAD_DIM, (h + 1) * HEAD_DIM)

    @pl.when(i == 0)
    def _():
        row = lax.broadcasted_iota(jnp.int32, (LANES, seq), 0)
        col = lax.broadcasted_iota(jnp.int32, (LANES, seq), 1)
        ind = jnp.where(col // bs == row, 1.0, 0.0).astype(BF16)
        kmean_ref[...] = (jnp.dot(ind, k_ref[...], preferred_element_type=F32)
                          * (1.0 / bs)).astype(BF16)

    coli = lax.broadcasted_iota(jnp.int32, (bs, LANES), 1)
    colf = coli.astype(F32)
    past = coli < i
    for h in range(hp):
        q = q_ref[:, hcol(h)]
        gate = lax.dot_general(q, kmean_ref[:, hcol(h)], nt, preferred_element_type=F32)
        sel = _first_index_topk_mask(jnp.where(past, gate, NEG), colf, MOBA_TOPK)
        notsel = jnp.where(((sel > 0.0) & past) | (coli == i), 0.0, 1.0).astype(BF16)
        qa_ref[h] = jnp.concatenate([q, notsel], axis=-1)

    tk = ATT_TK
    blk_per_tile = tk // bs
    n_tiles = (i + blk_per_tile) // blk_per_tile
    key_blk = lax.broadcasted_iota(jnp.int32, (tk, LANES), 0) // bs
    key_col = lax.broadcasted_iota(jnp.int32, (tk, LANES), 1)
    ones_v = jnp.ones((tk, LANES), BF16)
    m_ref[...] = jnp.full(m_ref.shape, M_INIT, F32)
    acc_ref[...] = jnp.zeros_like(acc_ref)

    def tile(jt, causal):
        off = pl.multiple_of(jt * tk, tk)
        bias = jnp.where(key_col == jt * blk_per_tile + key_blk, -MASK_BIG, 0.0).astype(BF16)
        for h in range(hp):
            k_aug = jnp.concatenate([k_ref[pl.ds(off, tk), hcol(h)], bias], axis=-1)
            v_aug = jnp.concatenate([v_ref[pl.ds(off, tk), hcol(h)], ones_v], axis=-1)
            _flash_update(qa_ref[h], k_aug, v_aug, m_ref.at[h], acc_ref.at[h],
                          (i * bs, off) if causal else None)

    def body(jt, carry):
        tile(jt, False)
        return carry

    lax.fori_loop(0, n_tiles - 1, body, 0)
    tile(n_tiles - 1, True)
    for h in range(hp):
        acc = acc_ref[h]
        o_ref[:, hcol(h)] = (acc[:, :HEAD_DIM] / acc[:, HEAD_DIM:]).astype(o_ref.dtype)


def moba_attention_pallas(zq, zk, zv, *, n_heads, q_off=0, k_off=0, v_off=0):
    B, S, _ = zq.shape
    bs = MOBA_BLOCK
    hp = min(MOBA_HEADS_PER_STEP, n_heads)
    hw = hp * HEAD_DIM
    assert S % ATT_TK == 0 and S // bs <= LANES and n_heads % hp == 0
    assert q_off % hp == 0 and k_off % hp == 0 and v_off % hp == 0
    return pl.pallas_call(
        _moba_kernel,
        grid=(B, n_heads // hp, S // bs),
        in_specs=[pl.BlockSpec((None, bs, hw), lambda b, h, i: (b, i, q_off // hp + h)),
                  pl.BlockSpec((None, S, hw), lambda b, h, i: (b, 0, k_off // hp + h)),
                  pl.BlockSpec((None, S, hw), lambda b, h, i: (b, 0, v_off // hp + h))],
        out_specs=pl.BlockSpec((None, bs, hw), lambda b, h, i: (b, i, h)),
        out_shape=jax.ShapeDtypeStruct((B, S, n_heads * HEAD_DIM), BF16),
        scratch_shapes=[pltpu.VMEM((LANES, hw), BF16), pltpu.VMEM((hp, bs, 2 * HEAD_DIM), BF16),
                        pltpu.VMEM((hp, bs, 1), F32), pltpu.VMEM((hp, bs, 2 * HEAD_DIM), F32)],
        compiler_params=pltpu.CompilerParams(
            dimension_semantics=("parallel", "parallel", "arbitrary"),
            vmem_limit_bytes=VMEM_LIMIT_BYTES),
        name="moba_attention",
    )(zq, zk, zv)


NSA_TQ = 256
NSA_NC_PAD = 256
CMP_ROW = NSA_CMP_STRIDE * HEAD_DIM


def _nsa_compress_kernel(x_ref, pe_ref, w1_ref, w2_ref, o_ref):
    x = x_ref[...].astype(F32)
    top = (x + pe_ref[0:1, :]).astype(BF16)
    bot = (x + pe_ref[1:2, :]).astype(BF16)
    a = jnp.dot(top, w1_ref[0:CMP_ROW, :], preferred_element_type=F32)
    b = jnp.dot(bot, w1_ref[CMP_ROW:2 * CMP_ROW, :], preferred_element_type=F32)
    pre = a + pltpu.roll(b, b.shape[0] - 1, 0)
    hid = jax.nn.gelu(pre)
    o_ref[...] = jnp.dot(hid.astype(BF16), w2_ref[...], preferred_element_type=F32).astype(o_ref.dtype)


def nsa_compress_pallas(x, pe, w1, w2):
    B, S, gw = x.shape
    G = gw // HEAD_DIM
    nrow = S // NSA_CMP_STRIDE
    xr = x.reshape(B, nrow, NSA_CMP_STRIDE, G, HEAD_DIM).transpose(0, 3, 1, 2, 4).reshape(B, G, nrow, CMP_ROW)
    pe2 = pe.astype(F32).reshape(2, CMP_ROW)
    return pl.pallas_call(
        _nsa_compress_kernel,
        grid=(B, G),
        in_specs=[pl.BlockSpec((None, None, nrow, CMP_ROW), lambda b, g: (b, g, 0, 0)),
                  pl.BlockSpec((2, CMP_ROW), lambda b, g: (0, 0)),
                  pl.BlockSpec((2 * CMP_ROW, HEAD_DIM), lambda b, g: (0, 0)),
                  pl.BlockSpec((HEAD_DIM, HEAD_DIM), lambda b, g: (0, 0))],
        out_specs=pl.BlockSpec((None, None, nrow, HEAD_DIM), lambda b, g: (b, g, 0, 0)),
        out_shape=jax.ShapeDtypeStruct((B, G, nrow, HEAD_DIM), BF16),
        compiler_params=pltpu.CompilerParams(
            dimension_semantics=("parallel", "parallel"), vmem_limit_bytes=VMEM_LIMIT_BYTES),
        name="nsa_compress",
    )(xr, pe2, w1.astype(BF16), w2.astype(BF16))


def _nsa_kernel(q_ref, kc_ref, vc_ref, ks_ref, vs_ref, kw_ref, vw_ref, g_ref, wmap_ref, o_ref,
                m_ref, acc_ref, out_ref):
    i = pl.program_id(2)
    tq = NSA_TQ
    R = NSA_HEADS // NSA_KV_HEADS
    nt = (((1,), (1,)), ((), ()))
    row = lax.broadcasted_iota(jnp.int32, (tq, tq), 0)
    col = lax.broadcasted_iota(jnp.int32, (tq, tq), 1)
    t_abs = i * tq + row
    gates = jax.nn.sigmoid(g_ref[...])

    def qh(r):
        return q_ref[:, r * HEAD_DIM:(r + 1) * HEAD_DIM]

    ncp = kc_ref.shape[0]
    cmask = (lax.broadcasted_iota(jnp.int32, (tq, ncp), 1) * NSA_CMP_STRIDE + (NSA_CMP_BLOCK - 1)
             <= i * tq + lax.broadcasted_iota(jnp.int32, (tq, ncp), 0))
    imp = jnp.zeros((tq, LANES), F32)
    for r in range(R):
        s = lax.dot_general(qh(r), kc_ref[...], nt, preferred_element_type=F32)
        s = jnp.where(cmask, s, NEG)
        e = jnp.exp(s - jnp.max(s, axis=-1, keepdims=True))
        p = jnp.where(cmask, e / jnp.sum(e, axis=-1, keepdims=True), 0.0).astype(BF16)
        imp = imp + jnp.dot(p, wmap_ref[...], preferred_element_type=F32)
        o_c = jnp.dot(p, vc_ref[...], preferred_element_type=F32)
        out_ref[r] = gates[:, 3 * r:3 * r + 1] * o_c

    w_tiles = []
    for d in range(NSA_WINDOW // tq + 1):
        off = pl.multiple_of(jnp.maximum(i - d, 0) * tq, tq)
        gone = jnp.where(i >= d, 0, 2 * NSA_WINDOW + tq)
        dist = d * tq + row - col + gone
        w_tiles.append((off, (dist >= 0) & (dist < NSA_WINDOW)))
    for r in range(R):
        ss = []
        for off, mask in w_tiles:
            s = lax.dot_general(qh(r), kw_ref[pl.ds(off, tq), :], nt, preferred_element_type=F32)
            ss.append(jnp.where(mask, s, NEG))
        m = jnp.max(ss[0], axis=-1, keepdims=True)
        for s in ss[1:]:
            m = jnp.maximum(m, jnp.max(s, axis=-1, keepdims=True))
        l = jnp.zeros((tq, 1), F32)
        o_w = jnp.zeros((tq, HEAD_DIM), F32)
        for (off, _), s in zip(w_tiles, ss):
            p = jnp.exp(s - m)
            l = l + jnp.sum(p, axis=-1, keepdims=True)
            o_w = o_w + jnp.dot(p.astype(BF16), vw_ref[pl.ds(off, tq), :], preferred_element_type=F32)
        out_ref[r] = out_ref[r] + gates[:, 3 * r + 2:3 * r + 3] * (o_w / l)

    rowl = lax.broadcasted_iota(jnp.int32, (tq, LANES), 0)
    coll = lax.broadcasted_iota(jnp.int32, (tq, LANES), 1)
    q_blk = (i * tq + rowl) // NSA_SEL_BLOCK
    valid = coll <= q_blk
    forced = (coll == 0) | (coll >= q_blk - 1)
    impm = jnp.where(valid, imp + jnp.where(forced, NSA_FORCE_BONUS, 0.0), NEG)
    sel = _first_index_topk_mask(impm, coll.astype(F32), NSA_SEL_TOPK)
    notsel = jnp.where((sel > 0.0) & valid, 0.0, 1.0).astype(BF16)

    tk = ATT_TK
    blk_per_tile = tk // NSA_SEL_BLOCK
    n_tiles = (i * tq) // tk + 1
    key_blk = lax.broadcasted_iota(jnp.int32, (tk, LANES), 0) // NSA_SEL_BLOCK
    key_col = lax.broadcasted_iota(jnp.int32, (tk, LANES), 1)
    ones_v = jnp.ones((tk, LANES), BF16)
    m_ref[...] = jnp.full(m_ref.shape, M_INIT, F32)
    acc_ref[...] = jnp.zeros_like(acc_ref)

    def sel_tile(jt, causal):
        off = pl.multiple_of(jt * tk, tk)
        bias = jnp.where(key_col == jt * blk_per_tile + key_blk, -MASK_BIG, 0.0).astype(BF16)
        k_aug = jnp.concatenate([ks_ref[pl.ds(off, tk), :], bias], axis=-1)
        v_aug = jnp.concatenate([vs_ref[pl.ds(off, tk), :], ones_v], axis=-1)
        for r in range(R):
            q_aug = jnp.concatenate([qh(r), notsel], axis=-1)
            _flash_update(q_aug, k_aug, v_aug, m_ref.at[r], acc_ref.at[r], (i * tq, off) if causal else None)

    def body(jt, carry):
        sel_tile(jt, False)
        return carry

    lax.fori_loop(0, n_tiles - 1, body, 0)
    sel_tile(n_tiles - 1, True)
    for r in range(R):
        acc = acc_ref[r]
        o_s = acc[:, :HEAD_DIM] / acc[:, HEAD_DIM:]
        o_ref[:, r * HEAD_DIM:(r + 1) * HEAD_DIM] = (
            out_ref[r] + gates[:, 3 * r + 1:3 * r + 2] * o_s).astype(o_ref.dtype)


def nsa_attention_pallas(zq, kc, vc, zks, zvs, zkw, zvw, gate_logits, *, q_off, ks_off, vs_off, kw_off, vw_off):
    B, S, _ = zq.shape
    G, tq = NSA_KV_HEADS, NSA_TQ
    R = NSA_HEADS // G
    assert S % ATT_TK == 0 and ATT_TK % tq == 0 and S // NSA_SEL_BLOCK <= LANES
    nc = (S - NSA_CMP_BLOCK) // NSA_CMP_STRIDE + 1
    ncp = kc.shape[2]
    ns = S // NSA_SEL_BLOCK
    wmap = jnp.pad(cmp_to_sel_weights(nc, ns), ((0, ncp - nc), (0, LANES - ns))).astype(BF16)
    rw = R * HEAD_DIM
    kv_spec = lambda off: pl.BlockSpec((None, S, HEAD_DIM), lambda b, g, i: (b, 0, off + g))
    c_spec = pl.BlockSpec((None, None, ncp, HEAD_DIM), lambda b, g, i: (b, g, 0, 0))
    return pl.pallas_call(
        _nsa_kernel,
        grid=(B, G, S // tq),
        in_specs=[pl.BlockSpec((None, tq, rw), lambda b, g, i: (b, i, q_off // R + g)),
                  c_spec, c_spec, kv_spec(ks_off), kv_spec(vs_off), kv_spec(kw_off), kv_spec(vw_off),
                  pl.BlockSpec((None, tq, LANES), lambda b, g, i: (b, i, g)),
                  pl.BlockSpec((ncp, LANES), lambda b, g, i: (0, 0))],
        out_specs=pl.BlockSpec((None, tq, rw), lambda b, g, i: (b, i, g)),
        out_shape=jax.ShapeDtypeStruct((B, S, NSA_HEADS * HEAD_DIM), BF16),
        scratch_shapes=[pltpu.VMEM((R, tq, 1), F32),
                        pltpu.VMEM((R, tq, 2 * HEAD_DIM), F32), pltpu.VMEM((R, tq, HEAD_DIM), F32)],
        compiler_params=pltpu.CompilerParams(
            dimension_semantics=("parallel", "parallel", "arbitrary"),
            vmem_limit_bytes=VMEM_LIMIT_BYTES),
        name="nsa_attention",
    )(zq, kc, vc, zks, zvs, zkw, zvw, gate_logits, wmap)


MOE_TM = 512
MOE_TF = 256
MOE_TN = 512
ROUTER_TM = 256


def _router_kernel(x_ref, g_ref, w_ref, b_ref, t_ref, ids_ref, wts_ref):
    x = x_ref[...]
    t = x * lax.rsqrt(jnp.mean(x * x, axis=-1, keepdims=True) + NORM_EPS) * g_ref[...]
    tb = t.astype(BF16)
    bits = lax.bitcast_convert_type(tb.astype(F32), jnp.uint32)
    tm, half = bits.shape[0], bits.shape[1] // 2
    packed = (bits[:, :half] & jnp.uint32(0xFFFF0000)) | (bits[:, half:] >> 16)
    n = half // LANES
    for c in range(n):
        t_ref[pl.ds(c, tm, stride=n), :] = packed[:, c * LANES:(c + 1) * LANES]
    logits = jnp.dot(tb, w_ref[...], preferred_element_type=F32) + b_ref[...]
    coli = lax.broadcasted_iota(jnp.int32, logits.shape, 1)
    colf = coli.astype(F32)
    first = lambda hit: jnp.min(jnp.where(hit, colf, float(LANES)), axis=-1, keepdims=True)
    is_g = coli < N_GROUPS
    gl = jnp.where(is_g, logits, -jnp.inf)
    gmax = jnp.max(gl, axis=-1, keepdims=True)
    g_sel = first(gl == gmax)
    p_g = 1.0 / jnp.sum(jnp.where(is_g, jnp.exp(logits - gmax), 0.0), axis=-1, keepdims=True)
    lo = N_GROUPS + EXPERTS_PER_GROUP * g_sel
    el = jnp.where((colf >= lo) & (colf < lo + EXPERTS_PER_GROUP), logits, -jnp.inf)
    v1 = jnp.max(el, axis=-1, keepdims=True)
    i1 = first(el == v1)
    el2 = jnp.where(colf == i1, -jnp.inf, el)
    v2 = jnp.max(el2, axis=-1, keepdims=True)
    i2 = first(el2 == v2)
    e = jnp.exp(v2 - v1)
    w1 = p_g / (1.0 + e)
    w2 = p_g * e / (1.0 + e)
    ids_ref[...] = jnp.where(coli == 0, i1 - N_GROUPS, jnp.where(coli == 1, i2 - N_GROUPS, 0.0)).astype(jnp.int32)
    wts_ref[...] = jnp.where(coli == 0, w1, jnp.where(coli == 1, w2, 0.0))


def moe_router_pallas(x2, gain, w_group, b_group, w_expert, b_expert):
    T, D = x2.shape
    tm = ROUTER_TM
    nr = N_GROUPS + N_EXPERTS
    w = jnp.pad(jnp.concatenate([w_group, w_expert], axis=1), ((0, 0), (0, LANES - nr))).astype(BF16)
    b = jnp.pad(jnp.concatenate([b_group, b_expert]), (0, LANES - nr)).astype(F32)[None, :]
    t, ids, wts = pl.pallas_call(
        _router_kernel,
        grid=(T // tm,),
        in_specs=[pl.BlockSpec((tm, D), lambda i: (i, 0)),
                  pl.BlockSpec((1, D), lambda i: (0, 0)),
                  pl.BlockSpec((D, LANES), lambda i: (0, 0)),
                  pl.BlockSpec((1, LANES), lambda i: (0, 0))],
        out_specs=[pl.BlockSpec((tm * (D // 2 // LANES), LANES), lambda i: (i, 0)),
                   pl.BlockSpec((tm, LANES), lambda i: (i, 0)),
                   pl.BlockSpec((tm, LANES), lambda i: (i, 0))],
        out_shape=[jax.ShapeDtypeStruct((T * (D // 2 // LANES), LANES), jnp.uint32),
                   jax.ShapeDtypeStruct((T, LANES), jnp.int32),
                   jax.ShapeDtypeStruct((T, LANES), F32)],
        compiler_params=pltpu.CompilerParams(
            dimension_semantics=("parallel",), vmem_limit_bytes=VMEM_LIMIT_BYTES),
        name="moe_router",
    )(x2, gain.astype(F32)[None, :], w, b)
    return t, ids[:, :TOPK_IN_GROUP], wts[:, :TOPK_IN_GROUP]


def _moe_expert_kernel(tile_e_ref, n_used_ref, row_tok_ref, t_hbm, rw_ref, wg_ref, wu_ref, wd_ref, o_ref,
                       xw_ref, xb_ref, sem):
    k = pl.program_id(0)
    f = pl.program_id(1)
    tm = xb_ref.shape[0]
    n = xw_ref.shape[0] // tm
    half = n * LANES
    n_used = n_used_ref[0]

    def row_copy(tok, r):
        return pltpu.make_async_copy(t_hbm.at[pl.ds(pl.multiple_of(tok * n, n), n), :],
                                     xw_ref.at[pl.ds(pl.multiple_of(r * n, n), n), :], sem.at[0])

    def start_gather(tile):
        def issue(r, carry):
            row_copy(row_tok_ref[tile * tm + r], r).start()
            return carry
        lax.fori_loop(0, tm, issue, 0, unroll=8)

    @pl.when(f == 0)
    def _():
        o_ref[...] = jnp.zeros_like(o_ref)

        @pl.when(k == 0)
        def _():
            start_gather(0)

        @pl.when(k < n_used)
        def _():
            def wait_row(r, carry):
                row_copy(0, r).wait()
                return carry
            lax.fori_loop(0, tm, wait_row, 0, unroll=8)
            for c in range(n):
                u = xw_ref[pl.ds(c, tm, stride=n), :]
                lo, hi = c * LANES, (c + 1) * LANES
                xb_ref[:, lo:hi] = lax.bitcast_convert_type(u & jnp.uint32(0xFFFF0000), F32).astype(BF16)
                xb_ref[:, half + lo:half + hi] = lax.bitcast_convert_type(u << 16, F32).astype(BF16)

            @pl.when(k + 1 < n_used)
            def _():
                start_gather(k + 1)

    @pl.when(k < n_used)
    def _():
        x = xb_ref[...]
        hg = jnp.dot(x, wg_ref[...].astype(BF16), preferred_element_type=F32)
        hu = jnp.dot(x, wu_ref[...].astype(BF16), preferred_element_type=F32)
        hid = (jax.nn.silu(hg) * hu * rw_ref[...]).astype(BF16)
        for c in range(0, o_ref.shape[1], MOE_TN):
            o_ref[:, c:c + MOE_TN] += jnp.dot(hid, wd_ref[:, c:c + MOE_TN].astype(BF16),
                                              preferred_element_type=F32)


def moe_experts_pallas(t_packed, row_tok, row_w, tile_e, n_used, w_gate, w_up, w_down, layer):
    P = row_tok.shape[0]
    D = w_gate.shape[-2]
    half = D // 2
    F = w_gate.shape[-1]
    tm, tf = MOE_TM, min(MOE_TF, F)
    l = int(layer)
    grid_spec = pltpu.PrefetchScalarGridSpec(
        num_scalar_prefetch=3,
        grid=(P // tm, F // tf),
        in_specs=[pl.BlockSpec(memory_space=pl.ANY),
                  pl.BlockSpec((tm, 1), lambda k, f, *_: (k, 0)),
                  pl.BlockSpec((None, None, D, tf), lambda k, f, te, *_: (l, te[k], 0, f)),
                  pl.BlockSpec((None, None, D, tf), lambda k, f, te, *_: (l, te[k], 0, f)),
                  pl.BlockSpec((None, None, tf, D), lambda k, f, te, *_: (l, te[k], f, 0))],
        out_specs=pl.BlockSpec((tm, D), lambda k, f, *_: (k, 0)),
        scratch_shapes=[pltpu.VMEM((tm * half // LANES, LANES), jnp.uint32), pltpu.VMEM((tm, D), BF16),
                        pltpu.SemaphoreType.DMA((1,))],
    )
    return pl.pallas_call(
        _moe_expert_kernel,
        grid_spec=grid_spec,
        out_shape=jax.ShapeDtypeStruct((P, D), F32),
        compiler_params=pltpu.CompilerParams(
            dimension_semantics=("arbitrary", "arbitrary"), vmem_limit_bytes=VMEM_LIMIT_BYTES),
        name="moe_experts",
    )(tile_e, n_used, row_tok, t_packed, row_w, w_gate, w_up, w_down)


def moe_dispatch_plan(ids):
    T, K = ids.shape
    n = T * K
    P = n + N_EXPERTS * MOE_TM
    flat = ids.reshape(n)
    order = jnp.argsort(flat, stable=True)
    sorted_e = flat[order]
    counts = jnp.zeros((N_EXPERTS,), jnp.int32).at[flat].add(1)
    padded = (counts + MOE_TM - 1) // MOE_TM * MOE_TM
    pend = jnp.cumsum(padded)
    pstart = pend - padded
    start = jnp.cumsum(counts) - counts
    dest = pstart[sorted_e] + (jnp.arange(n, dtype=jnp.int32) - start[sorted_e])
    pos = jnp.zeros((n,), jnp.int32).at[order].set(dest)
    row_tok = jnp.zeros((P,), jnp.int32).at[dest].set((order // K).astype(jnp.int32))
    tile_start = jnp.arange(P // MOE_TM, dtype=jnp.int32) * MOE_TM
    tile_e = jnp.minimum(jnp.searchsorted(pend, tile_start, side='right'), N_EXPERTS - 1).astype(jnp.int32)
    n_used = (pend[-1] // MOE_TM).astype(jnp.int32)[None]
    return pos.reshape(T, K), dest, order, row_tok, tile_e, n_used, P


def hierarchical_moe_pallas(x, gain, w_group, b_group, w_expert, b_expert, w_gate, w_up, w_down, layer):
    B, S, D = x.shape
    x2 = x.reshape(B * S, D)
    t, ids, wts = moe_router_pallas(x2, gain, w_group, b_group, w_expert, b_expert)
    pos, dest, order, row_tok, tile_e, n_used, P = moe_dispatch_plan(ids)
    row_w = jnp.zeros((P,), F32).at[dest].set(wts.reshape(-1)[order])[:, None]
    y = moe_experts_pallas(t, row_tok, row_w, tile_e, n_used, w_gate, w_up, w_down, layer)
    out = x2 + jnp.take(y, pos[:, 0], axis=0) + jnp.take(y, pos[:, 1], axis=0)
    return out.reshape(B, S, D)


def rms_norm(x, g):
    xf = x.astype(F32)
    y = xf * lax.rsqrt(jnp.mean(xf * xf, axis=-1, keepdims=True) + NORM_EPS)
    return (y * g.astype(F32)).astype(x.dtype)


def rope_tables(positions):
    inv = ROPE_THETA ** (-jnp.arange(0, HEAD_DIM, 2, dtype=F32) / HEAD_DIM)
    ang = positions.astype(F32)[..., None] * inv
    return jnp.cos(ang)[:, :, None, :], jnp.sin(ang)[:, :, None, :]


def rope(x, cos, sin):
    x1, x2 = jnp.split(x.astype(F32), 2, axis=-1)
    return jnp.concatenate([x1 * cos - x2 * sin, x2 * cos + x1 * sin], axis=-1).astype(x.dtype)


def split_heads(t):
    return t.reshape(t.shape[0], t.shape[1], -1, HEAD_DIM)


def split_cols(z):
    return jnp.split(z, [int(v) for v in np.cumsum(MIX_SPLITS)[:-1]], axis=-1)


def causal_depthwise_conv(x, w):
    width, c = w.shape
    xp = jnp.pad(x, ((0, 0), (width - 1, 0), (0, 0)))
    return lax.conv_general_dilated(xp, w[:, None, :].astype(x.dtype), (1,), 'VALID',
                                    dimension_numbers=('NWC', 'WIO', 'NWC'),
                                    feature_group_count=c)


def moba_attention(q, k, v):
    B, S, H, dh = q.shape
    bs = MOBA_BLOCK
    nb = -(-S // bs)
    scale = dh ** -0.5
    qh, kh, vh = (t.transpose(0, 2, 1, 3) for t in (q, k, v))
    pad = nb * bs - S
    kp = jnp.pad(kh, ((0, 0), (0, 0), (0, pad), (0, 0)))
    vp = jnp.pad(vh, ((0, 0), (0, 0), (0, pad), (0, 0)))
    kb = kp.reshape(B, H, nb, bs, dh)
    vb = vp.reshape(B, H, nb, bs, dh)
    k_mean = jnp.mean(kb.astype(F32), axis=3)
    t = jnp.arange(S)
    q_blk = t // bs
    gate = jnp.einsum('bhsd,bhnd->bhsn', qh.astype(F32), k_mean)
    past = jnp.arange(nb)[None, :] < q_blk[:, None]
    gate = jnp.where(past, gate, NEG)
    kk = min(MOBA_TOPK, nb)
    _, sel = lax.top_k(gate, kk)
    sel_ok = sel < q_blk[:, None]
    qb = MOBA_Q_BLOCK
    nqb = S // qb
    q_c = qh.reshape(B, H, nqb, qb, dh).transpose(2, 0, 1, 3, 4)
    sel_c = sel.reshape(B, H, nqb, qb, kk).transpose(2, 0, 1, 3, 4)
    ok_c = sel_ok.reshape(B, H, nqb, qb, kk).transpose(2, 0, 1, 3, 4)
    bi = jnp.arange(B)[:, None, None, None]
    hi = jnp.arange(H)[None, :, None, None]

    def chunk(args):
        qc, selc, okc, ci = args
        tq = ci * qb + jnp.arange(qb)
        own = (ci * qb) // bs * bs
        k_own = lax.dynamic_slice_in_dim(kp, own, bs, axis=2)
        v_own = lax.dynamic_slice_in_dim(vp, own, bs, axis=2)
        own_mask = (own + jnp.arange(bs))[None, :] <= tq[:, None]
        k_s = kb[bi, hi, selc].reshape(B, H, qb, kk * bs, dh)
        v_s = vb[bi, hi, selc].reshape(B, H, qb, kk * bs, dh)
        s_sel = jnp.einsum('bhqd,bhqkd->bhqk', qc, k_s, preferred_element_type=F32) * scale
        s_sel = jnp.where(jnp.repeat(okc, bs, axis=-1), s_sel, NEG)
        s_own = jnp.einsum('bhqd,bhkd->bhqk', qc, k_own, preferred_element_type=F32) * scale
        s_own = jnp.where(own_mask, s_own, NEG)
        p = jax.nn.softmax(jnp.concatenate([s_sel, s_own], axis=-1), axis=-1).astype(v.dtype)
        return (jnp.einsum('bhqk,bhqkd->bhqd', p[..., :kk * bs], v_s)
                + jnp.einsum('bhqk,bhkd->bhqd', p[..., kk * bs:], v_own))

    o = lax.map(chunk, (q_c, sel_c, ok_c, jnp.arange(nqb)))
    return o.transpose(1, 0, 3, 2, 4).reshape(B, S, H * dh)


def compress_tokens(x, pe, w1, w2):
    B, S, G, dh = x.shape
    nc = (S - NSA_CMP_BLOCK) // NSA_CMP_STRIDE + 1
    idx = jnp.arange(nc)[:, None] * NSA_CMP_STRIDE + jnp.arange(NSA_CMP_BLOCK)[None, :]
    blocks = x[:, idx] + pe[:, None, :]
    flat = blocks.transpose(0, 1, 3, 2, 4).reshape(B, nc, G, NSA_CMP_BLOCK * dh)
    return jax.nn.gelu(flat @ w1) @ w2


def cmp_to_sel_weights(nc, ns):
    r = NSA_SEL_BLOCK // NSA_CMP_STRIDE
    m = NSA_CMP_BLOCK // NSA_CMP_STRIDE
    c = jnp.arange(nc)[:, None] - r * jnp.arange(ns)[None, :]
    w = jnp.minimum(jnp.minimum(c + 1, r + m - 1 - c), min(r, m))
    return jnp.clip(w, 0, None).astype(F32)


def nsa_attention(q, k_cmp, v_cmp, k_sel, v_sel, k_win, v_win, gate_logits,
                  pe_k, w1_k, w2_k, pe_v, w1_v, w2_v):
    B, S, H, dh = q.shape
    G = k_sel.shape[2]
    R = H // G
    scale = dh ** -0.5
    t = jnp.arange(S)
    qg = q.reshape(B, S, G, R, dh)
    kc = compress_tokens(k_cmp, pe_k, w1_k, w2_k)
    vc = compress_tokens(v_cmp, pe_v, w1_v, w2_v)
    nc = kc.shape[1]
    s_c = jnp.einsum('bsgrd,bngd->bgrsn', qg, kc, preferred_element_type=F32) * scale
    c_end = jnp.arange(nc) * NSA_CMP_STRIDE + NSA_CMP_BLOCK - 1
    c_mask = c_end[None, :] <= t[:, None]
    p_c = jax.nn.softmax(jnp.where(c_mask, s_c, NEG), axis=-1) * c_mask
    o_cmp = jnp.einsum('bgrsn,bngd->bsgrd', p_c.astype(vc.dtype), vc).reshape(B, S, H, dh)
    ns = S // NSA_SEL_BLOCK
    imp = jnp.einsum('bgrsn,nm->bgsm', p_c, cmp_to_sel_weights(nc, ns))
    q_blk = t // NSA_SEL_BLOCK
    j = jnp.arange(ns)[None, :]
    valid = j <= q_blk[:, None]
    forced = (j == 0) | (j >= q_blk[:, None] - 1)
    imp = jnp.where(valid, imp + jnp.where(forced, NSA_FORCE_BONUS, 0.0), NEG)
    kk = min(NSA_SEL_TOPK, ns)
    _, sel = lax.top_k(imp, kk)
    sel_ok = sel <= q_blk[:, None]
    sb = NSA_SEL_BLOCK
    k_blocks = k_sel.transpose(0, 2, 1, 3).reshape(B, G, ns, sb, dh)
    v_blocks = v_sel.transpose(0, 2, 1, 3).reshape(B, G, ns, sb, dh)
    wpad = ((0, 0), (0, 0), (NSA_WINDOW, 0), (0, 0))
    k_wp = jnp.pad(k_win.transpose(0, 2, 1, 3), wpad)
    v_wp = jnp.pad(v_win.transpose(0, 2, 1, 3), wpad)
    nqb = S // Q_BLOCK
    q_c = qg.transpose(0, 2, 3, 1, 4).reshape(B, G, R, nqb, Q_BLOCK, dh).transpose(3, 0, 1, 2, 4, 5)
    sel_c = sel.reshape(B, G, nqb, Q_BLOCK, kk).transpose(2, 0, 1, 3, 4)
    ok_c = sel_ok.reshape(B, G, nqb, Q_BLOCK, kk).transpose(2, 0, 1, 3, 4)
    bi = jnp.arange(B)[:, None, None, None]
    gi = jnp.arange(G)[None, :, None, None]

    def chunk(args):
        qc, selc, okc, ci = args
        tq = ci * Q_BLOCK + jnp.arange(Q_BLOCK)
        ks = k_blocks[bi, gi, selc].reshape(B, G, Q_BLOCK, kk * sb, dh)
        vs = v_blocks[bi, gi, selc].reshape(B, G, Q_BLOCK, kk * sb, dh)
        kpos = (selc[..., None] * sb + jnp.arange(sb)).reshape(B, G, Q_BLOCK, kk * sb)
        m = (kpos <= tq[:, None]) & jnp.repeat(okc, sb, axis=-1)
        s = jnp.einsum('bgrqd,bgqkd->bgrqk', qc, ks, preferred_element_type=F32) * scale
        p = jax.nn.softmax(jnp.where(m[:, :, None], s, NEG), axis=-1)
        o_s = jnp.einsum('bgrqk,bgqkd->bgrqd', p.astype(vs.dtype), vs)
        kw = lax.dynamic_slice_in_dim(k_wp, ci * Q_BLOCK, Q_BLOCK + NSA_WINDOW, axis=2)
        vw = lax.dynamic_slice_in_dim(v_wp, ci * Q_BLOCK, Q_BLOCK + NSA_WINDOW, axis=2)
        wpos = ci * Q_BLOCK - NSA_WINDOW + jnp.arange(Q_BLOCK + NSA_WINDOW)
        dist = tq[:, None] - wpos[None, :]
        wm = (dist >= 0) & (dist < NSA_WINDOW) & (wpos[None, :] >= 0)
        s_w = jnp.einsum('bgrqd,bgkd->bgrqk', qc, kw, preferred_element_type=F32) * scale
        p_w = jax.nn.softmax(jnp.where(wm, s_w, NEG), axis=-1)
        o_w = jnp.einsum('bgrqk,bgkd->bgrqd', p_w.astype(vw.dtype), vw)
        return o_s, o_w

    o_sel, o_win = lax.map(chunk, (q_c, sel_c, ok_c, jnp.arange(nqb)))
    o_sel = o_sel.transpose(1, 0, 4, 2, 3, 5).reshape(B, S, H, dh)
    o_win = o_win.transpose(1, 0, 4, 2, 3, 5).reshape(B, S, H, dh)
    g = jax.nn.sigmoid(gate_logits.astype(F32)).reshape(B, S, H, 3)
    out = g[..., 0:1] * o_cmp + g[..., 1:2] * o_sel + g[..., 2:3] * o_win
    return out.reshape(B, S, H * dh).astype(q.dtype)


def rg_lru_branch(x_in, gate_in, conv_w, conv_b, w_a, b_a, w_i, b_i, lam):
    B, S, C = x_in.shape
    xc = (causal_depthwise_conv(x_in, conv_w) + conv_b).astype(F32)
    xh = xc.reshape(B, S, LRU_HEADS, C // LRU_HEADS)
    r = jax.nn.sigmoid(jnp.einsum('bshi,hij->bshj', xh, w_a.astype(F32)) + b_a).reshape(B, S, C)
    i = jax.nn.sigmoid(jnp.einsum('bshi,hij->bshj', xh, w_i.astype(F32)) + b_i).reshape(B, S, C)
    log_a = -LRU_C * r * jax.nn.softplus(-lam.astype(F32))
    a = jnp.exp(log_a)
    b = jnp.sqrt(-jnp.expm1(2.0 * log_a)) * (i * xc)

    def combine(c1, c2):
        a1, b1 = c1
        a2, b2 = c2
        return a1 * a2, a2 * b1 + b2

    _, h = lax.associative_scan(combine, (a, b), axis=1)
    return (h * jax.nn.gelu(gate_in.astype(F32))).astype(x_in.dtype)


def short_conv_branch(x_in, b_gate, c_gate, conv_w):
    return b_gate * causal_depthwise_conv(c_gate * x_in, conv_w)


def cross_attention(h, mem_n, w_q, w_k, w_v, w_o):
    B, S, _ = h.shape
    M = mem_n.shape[1]
    q = dense(h, w_q).reshape(B, S, XA_HEADS, HEAD_DIM)
    k = dense(mem_n, w_k).reshape(B, M, XA_HEADS, HEAD_DIM)
    v = dense(mem_n, w_v).reshape(B, M, XA_HEADS, HEAD_DIM)
    s = jnp.einsum('bshd,bmhd->bhsm', q, k, preferred_element_type=F32) * HEAD_DIM ** -0.5
    p = jax.nn.softmax(s, axis=-1).astype(v.dtype)
    o = jnp.einsum('bhsm,bmhd->bshd', p, v).reshape(B, S, XA_WIDTH)
    return dense(o, w_o)


def hierarchical_moe(h, w_group, b_group, w_expert, b_expert, w_gate, w_up, w_down):
    B, S, D = h.shape
    t = h.reshape(B * S, D)
    g_logits = (t @ w_group + b_group).astype(F32)
    g_prob = jax.nn.softmax(g_logits, axis=-1)
    g_sel = jnp.argmax(g_logits, axis=-1)
    e_logits = (t @ w_expert + b_expert).astype(F32).reshape(-1, N_GROUPS, EXPERTS_PER_GROUP)
    e_in = jnp.take_along_axis(e_logits, g_sel[:, None, None], axis=1)[:, 0]
    top_v, top_i = lax.top_k(e_in, TOPK_IN_GROUP)
    p_g = jnp.take_along_axis(g_prob, g_sel[:, None], axis=1)
    w_top = jax.nn.softmax(top_v, axis=-1) * p_g
    eid = g_sel[:, None] * EXPERTS_PER_GROUP + top_i
    combine = jnp.sum(jax.nn.one_hot(eid, N_EXPERTS, dtype=F32) * w_top[..., None], axis=1)
    hid = jax.nn.silu(jnp.einsum('td,edf->tef', t, w_gate)) * jnp.einsum('td,edf->tef', t, w_up)
    out = jnp.einsum('tef,efd->td', hid * combine[:, :, None].astype(hid.dtype), w_down)
    return out.reshape(B, S, D)


def kernel(x, mem, positions, norm_mix, w_mix_in, lru_conv_w, lru_conv_b, lru_w_a, lru_b_a, lru_w_i, lru_b_i, lru_lambda, sc_conv_w, nsa_pe_k, nsa_w1_k, nsa_w2_k, nsa_pe_v, nsa_w1_v, nsa_w2_v, w_merge_gate, b_merge_gate, w_branch_out, w_mix_out, norm_xattn, norm_mem, xa_w_q, xa_w_k, xa_w_v, xa_w_o, norm_moe, moe_w_group, moe_b_group, moe_w_expert, moe_b_expert, moe_w_gate, moe_w_up, moe_w_down, norm_final):
    B, S, D = x.shape
    T = B * S
    cos, sin = rope_tables(positions)
    cos_t = jnp.concatenate([cos, cos], axis=-1).reshape(T, HEAD_DIM)
    sin_t = jnp.concatenate([-sin, sin], axis=-1).reshape(T, HEAD_DIM)
    offs = [int(v) for v in np.concatenate([[0], np.cumsum(MIX_SPLITS)])]
    blk = [v // HEAD_DIM for v in offs[:11]]
    n_head_cols = offs[10]
    rope_flags = np.zeros((n_head_cols // HEAD_DIM,), np.int32)
    for k in (1, 4, 6, 8):
        rope_flags[blk[k]:blk[k + 1]] = ROPE_PLAIN
    for k in (0, 3):
        rope_flags[blk[k]:blk[k + 1]] = ROPE_SCALED
    rope_flags = jnp.asarray(rope_flags)
    G, R = NSA_KV_HEADS, NSA_HEADS // NSA_KV_HEADS
    L = w_mix_in.shape[0]
    w_tail = w_mix_in[:, :, offs[11]:]
    w_ng = jnp.pad(w_mix_in[:, :, offs[10]:offs[11]].reshape(L, D, G, R * 3),
                   ((0, 0), (0, 0), (0, 0), (0, LANES - R * 3))).reshape(L, D, G * LANES)
    M = mem.shape[1]
    for l in range(DEPTH):
        x2 = x.reshape(T, D)
        h = rms_norm_pallas(x2, norm_mix[l], BF16)
        z_head = matmul(h, w_mix_in, w_lead=(l,), n_cols=n_head_cols, out_dtype=BF16,
                        rope=(cos_t, sin_t, rope_flags)).reshape(B, S, n_head_cols)
        z_tail = matmul(h, w_tail, w_lead=(l,)).reshape(B, S, -1)
        z_gate = matmul(h, w_ng, w_lead=(l,), tn=G * LANES).reshape(B, S, G * LANES)
        o_moba = moba_attention_pallas(z_head, z_head, z_head, n_heads=MOBA_HEADS,
                                       q_off=blk[0], k_off=blk[1], v_off=blk[2])
        kc = nsa_compress_pallas(z_head[:, :, offs[4]:offs[5]], nsa_pe_k[l], nsa_w1_k[l], nsa_w2_k[l])
        vc = nsa_compress_pallas(z_head[:, :, offs[5]:offs[6]], nsa_pe_v[l], nsa_w1_v[l], nsa_w2_v[l])
        o_nsa = nsa_attention_pallas(z_head, kc, vc, z_head, z_head, z_head, z_head, z_gate,
                                     q_off=blk[3], ks_off=blk[6], vs_off=blk[7], kw_off=blk[8], vw_off=blk[9])
        o_lru, o_conv = lru_conv_pallas(z_tail, l, lru_conv_w, lru_conv_b, lru_w_a, lru_b_a,
                                        lru_w_i, lru_b_i, lru_lambda, sc_conv_w)
        o_all = jnp.stack([o_moba, o_lru, o_conv, o_nsa]).reshape(N_BRANCH, T, -1)
        merged = merge_branches_pallas(h, o_all, w_merge_gate, b_merge_gate, w_branch_out, l)
        x2 = matmul(merged, w_mix_out, w_lead=(l,), residual=x2)
        mem_n = rms_norm_pallas(mem.reshape(B * M, D), norm_mem[l], BF16)
        xk = matmul(mem_n, xa_w_k, w_lead=(l,), out_dtype=BF16).reshape(B, M, XA_WIDTH)
        xv = matmul(mem_n, xa_w_v, w_lead=(l,), out_dtype=BF16).reshape(B, M, XA_WIDTH)
        x = cross_attention_pallas(x2.reshape(B, S, D), norm_xattn[l], xk, xv,
                                   xa_w_q[l].astype(BF16), xa_w_o[l].astype(BF16))
        x = hierarchical_moe_pallas(x, norm_moe[l], moe_w_group[l], moe_b_group[l],
                                    moe_w_expert[l], moe_b_expert[l], moe_w_gate,
                                    moe_w_up, moe_w_down, l)
    return rms_norm_pallas(x.reshape(T, D), norm_final, F32).reshape(B, S, D)
```

```python
import functools

import jax
import jax.numpy as jnp
import numpy as np
from jax import lax
from jax.experimental import pallas as pl
from jax.experimental.pallas import tpu as pltpu

F32 = jnp.float32
BF16 = jnp.bfloat16

D_MODEL = 4096
DEPTH = 2
HEAD_DIM = 128
ROPE_THETA = 10000.0
NORM_EPS = 1e-6
NEG = -1e30
N_BRANCH = 4
BRANCH_WIDTH = D_MODEL // 4
MOBA_HEADS = BRANCH_WIDTH // HEAD_DIM
MOBA_BLOCK = 256
MOBA_TOPK = 3
MOBA_Q_BLOCK = 64
LRU_WIDTH = BRANCH_WIDTH
LRU_HEADS = LRU_WIDTH // HEAD_DIM
LRU_C = 8.0
SC_WIDTH = BRANCH_WIDTH
NSA_HEADS = BRANCH_WIDTH // HEAD_DIM
NSA_KV_HEADS = NSA_HEADS // 4
NSA_CMP_BLOCK = 32
NSA_CMP_STRIDE = 16
NSA_SEL_BLOCK = 64
NSA_SEL_TOPK = 16
NSA_WINDOW = 512
NSA_FORCE_BONUS = 1e4
Q_BLOCK = 128
XA_HEADS = 4
XA_WIDTH = XA_HEADS * HEAD_DIM
N_GROUPS = 4
EXPERTS_PER_GROUP = 8
N_EXPERTS = N_GROUPS * EXPERTS_PER_GROUP
TOPK_IN_GROUP = 2
EXPERT_FF = D_MODEL // 8
MOBA_W = MOBA_HEADS * HEAD_DIM
NSA_Q_W = NSA_HEADS * HEAD_DIM
NSA_KV_W = NSA_KV_HEADS * HEAD_DIM
NSA_GATE_W = NSA_HEADS * 3
MIX_SPLITS = (MOBA_W, MOBA_W, MOBA_W,
              NSA_Q_W, NSA_KV_W, NSA_KV_W, NSA_KV_W, NSA_KV_W, NSA_KV_W, NSA_KV_W, NSA_GATE_W,
              LRU_WIDTH, LRU_WIDTH,
              SC_WIDTH, SC_WIDTH, SC_WIDTH)

VMEM_LIMIT_BYTES = 56 * 1024 * 1024


ROPE_PLAIN, ROPE_SCALED = 1, 2


def _mm_kernel(*refs, rope, residual):
    if rope:
        flags_ref, refs = refs[0], refs[1:]
    a_ref, w_ref = refs[0], refs[1]
    o_ref = refs[-1]
    acc = jnp.dot(a_ref[...].astype(BF16), w_ref[...].astype(BF16), preferred_element_type=F32)
    if residual:
        acc = acc + refs[2][...]
    if rope:
        cos, sin = refs[2][...], refs[3][...]
        nblk = acc.shape[1] // HEAD_DIM
        for c in range(nblk):
            blk = acc[:, c * HEAD_DIM:(c + 1) * HEAD_DIM]
            flag = flags_ref[pl.program_id(1) * nblk + c]

            @pl.when(flag != 0)
            def _():
                rot = pltpu.roll(blk, HEAD_DIM // 2, 1)
                scl = jnp.where(flag == ROPE_SCALED, HEAD_DIM ** -0.5, 1.0)
                o_ref[:, c * HEAD_DIM:(c + 1) * HEAD_DIM] = ((blk * cos + rot * sin) * scl).astype(o_ref.dtype)

            @pl.when(flag == 0)
            def _():
                o_ref[:, c * HEAD_DIM:(c + 1) * HEAD_DIM] = blk.astype(o_ref.dtype)
    else:
        o_ref[...] = acc.astype(o_ref.dtype)


def matmul(a, w, *, w_lead=(), n_cols=None, out_dtype=F32, rope=None, residual=None, tm=1024, tn=512):
    m, k = a.shape
    n = w.shape[-1] if n_cols is None else n_cols
    assert w.shape[-2] == k and not (rope is not None and residual is not None)
    tm = min(tm, m)
    tn = min(tn, n)
    lead = tuple(int(v) for v in w_lead)
    in_specs = [pl.BlockSpec((tm, k), lambda i, j, *_: (i, 0)),
                pl.BlockSpec((None,) * len(lead) + (k, tn), lambda i, j, *_: lead + (0, j))]
    args = [a, w]
    prefetch = []
    if rope is not None:
        cos, sin, flags = rope
        assert n % tn == 0 and flags.shape == (n // HEAD_DIM,)
        in_specs += [pl.BlockSpec((tm, HEAD_DIM), lambda i, j, *_: (i, 0))] * 2
        args += [cos, sin]
        prefetch = [flags]
    if residual is not None:
        in_specs.append(pl.BlockSpec((tm, tn), lambda i, j, *_: (i, j)))
        args.append(residual)
    return pl.pallas_call(
        functools.partial(_mm_kernel, rope=rope is not None, residual=residual is not None),
        grid_spec=pltpu.PrefetchScalarGridSpec(
            num_scalar_prefetch=len(prefetch),
            grid=(pl.cdiv(m, tm), pl.cdiv(n, tn)),
            in_specs=in_specs,
            out_specs=pl.BlockSpec((tm, tn), lambda i, j, *_: (i, j))),
        out_shape=jax.ShapeDtypeStruct((m, n), out_dtype),
        compiler_params=pltpu.CompilerParams(
            dimension_semantics=("parallel", "parallel"),
            vmem_limit_bytes=VMEM_LIMIT_BYTES),
        name="matmul",
    )(*prefetch, *args)


def _rms_kernel(x_ref, g_ref, o_ref):
    x = x_ref[...]
    y = x * lax.rsqrt(jnp.mean(x * x, axis=-1, keepdims=True) + NORM_EPS)
    o_ref[...] = (y * g_ref[...]).astype(o_ref.dtype)


def rms_norm_pallas(x2, gain, out_dtype, *, tm=512):
    m, d = x2.shape
    tm = min(tm, m)
    return pl.pallas_call(
        _rms_kernel,
        grid=(m // tm,),
        in_specs=[pl.BlockSpec((tm, d), lambda i: (i, 0)), pl.BlockSpec((1, d), lambda i: (0, 0))],
        out_specs=pl.BlockSpec((tm, d), lambda i: (i, 0)),
        out_shape=jax.ShapeDtypeStruct((m, d), out_dtype),
        compiler_params=pltpu.CompilerParams(
            dimension_semantics=("parallel",), vmem_limit_bytes=VMEM_LIMIT_BYTES),
        name="rms_norm",
    )(x2, gain.astype(F32)[None, :])


def _merge_kernel(h_ref, o_ref, wg_ref, bg_ref, wu_ref, out_ref, acc_ref):
    n = pl.program_id(2)
    gate = jax.nn.sigmoid(
        jnp.dot(h_ref[...], wg_ref[...].astype(BF16), preferred_element_type=F32) + bg_ref[...])
    term = gate * jnp.dot(o_ref[...], wu_ref[...].astype(BF16), preferred_element_type=F32)

    @pl.when(n == 0)
    def _():
        acc_ref[...] = term

    @pl.when(n != 0)
    def _():
        acc_ref[...] += term

    @pl.when(n == pl.num_programs(2) - 1)
    def _():
        out_ref[...] = acc_ref[...].astype(out_ref.dtype)


def merge_branches_pallas(h, o_all, w_gate, b_gate, w_up, layer, *, tm=1024, tn=512):
    T, D = h.shape
    N, _, W = o_all.shape
    l = int(layer)
    return pl.pallas_call(
        _merge_kernel,
        grid=(T // tm, D // tn, N),
        in_specs=[pl.BlockSpec((tm, D), lambda i, j, n: (i, 0)),
                  pl.BlockSpec((None, tm, W), lambda i, j, n: (n, i, 0)),
                  pl.BlockSpec((None, None, D, tn), lambda i, j, n: (l, n, 0, j)),
                  pl.BlockSpec((None, None, 1, tn), lambda i, j, n: (l, n, 0, j)),
                  pl.BlockSpec((None, None, W, tn), lambda i, j, n: (l, n, 0, j))],
        out_specs=pl.BlockSpec((tm, tn), lambda i, j, n: (i, j)),
        out_shape=jax.ShapeDtypeStruct((T, D), BF16),
        scratch_shapes=[pltpu.VMEM((tm, tn), F32)],
        compiler_params=pltpu.CompilerParams(
            dimension_semantics=("parallel", "parallel", "arbitrary"),
            vmem_limit_bytes=VMEM_LIMIT_BYTES),
        name="merge_branches",
    )(h, o_all, w_gate, b_gate[:, :, None, :], w_up)


LRU_TS = 256
LRU_CW = 256


def _rows_back(x, prev, s, row):
    return jnp.where(row < s, pltpu.roll(prev, s, 0), pltpu.roll(x, s, 0))


def _lru_conv_kernel(rx_ref, rg_ref, cb_ref, cc_ref, cx_ref, lw_ref, lb_ref, wa_ref, ba_ref,
                     wi_ref, bi_ref, lam_ref, sw_ref, olru_ref, oconv_ref, px_ref, py_ref, h_ref):
    @pl.when(pl.program_id(2) == 0)
    def _():
        px_ref[...] = jnp.zeros_like(px_ref)
        py_ref[...] = jnp.zeros_like(py_ref)
        h_ref[...] = jnp.zeros_like(h_ref)

    ts, cw = rx_ref.shape
    row = lax.broadcasted_iota(jnp.int32, (ts, cw), 0)

    x = rx_ref[...]
    prev = px_ref[...]
    nk = lw_ref.shape[0]
    xc = lb_ref[...] + lw_ref[nk - 1:nk, :] * x
    for s in range(1, nk):
        xc = xc + lw_ref[nk - 1 - s:nk - s, :] * _rows_back(x, prev, s, row)
    px_ref[...] = x
    r_parts, i_parts = [], []
    for hh in range(cw // HEAD_DIM):
        cs = slice(hh * HEAD_DIM, (hh + 1) * HEAD_DIM)
        xh = xc[:, cs].astype(BF16)
        r_parts.append(jnp.dot(xh, wa_ref[hh].astype(BF16), preferred_element_type=F32))
        i_parts.append(jnp.dot(xh, wi_ref[hh].astype(BF16), preferred_element_type=F32))
    r = jax.nn.sigmoid(jnp.concatenate(r_parts, axis=-1) + ba_ref[...])
    gi = jax.nn.sigmoid(jnp.concatenate(i_parts, axis=-1) + bi_ref[...])
    log_a = -LRU_C * r * jax.nn.softplus(-lam_ref[...])
    a = jnp.exp(log_a)
    b = jnp.sqrt(1.0 - a * a) * (gi * xc)
    d = 1
    while d < ts:
        a_back = jnp.where(row < d, 1.0, pltpu.roll(a, d, 0))
        b_back = jnp.where(row < d, 0.0, pltpu.roll(b, d, 0))
        b = a * b_back + b
        a = a * a_back
        d *= 2
    h = b + a * h_ref[...]
    h_ref[...] = h[ts - 1:ts, :]
    olru_ref[...] = (h * jax.nn.gelu(rg_ref[...])).astype(olru_ref.dtype)

    y = cc_ref[...] * cx_ref[...]
    prev_y = py_ref[...]
    nk = sw_ref.shape[0]
    conv = sw_ref[nk - 1:nk, :] * y
    for s in range(1, nk):
        conv = conv + sw_ref[nk - 1 - s:nk - s, :] * _rows_back(y, prev_y, s, row)
    py_ref[...] = y
    oconv_ref[...] = (cb_ref[...] * conv).astype(oconv_ref.dtype)


def lru_conv_pallas(z, layer, lru_conv_w, lru_conv_b, lru_w_a, lru_b_a, lru_w_i, lru_b_i, lru_lambda,
                    sc_conv_w):
    B, S, zc = z.shape
    C = zc // 5
    ts, cw = LRU_TS, LRU_CW
    nblk = C // cw
    l = int(layer)
    col = lambda k: pl.BlockSpec((None, ts, cw), lambda b, c, t: (b, t, k * nblk + c))
    par = lambda rows: pl.BlockSpec((None, rows, cw), lambda b, c, t: (l, 0, c))
    hw = pl.BlockSpec((None, cw // HEAD_DIM, HEAD_DIM, HEAD_DIM), lambda b, c, t: (l, c, 0, 0))
    L = lru_conv_b.shape[0]
    flat = lambda p: p.reshape(L, 1, C)
    out = pl.BlockSpec((None, ts, cw), lambda b, c, t: (b, t, c))
    return pl.pallas_call(
        _lru_conv_kernel,
        grid=(B, nblk, S // ts),
        in_specs=[col(0), col(1), col(2), col(3), col(4),
                  par(lru_conv_w.shape[1]), par(1), hw, par(1), hw, par(1), par(1),
                  par(sc_conv_w.shape[1])],
        out_specs=[out, out],
        out_shape=[jax.ShapeDtypeStruct((B, S, C), BF16)] * 2,
        scratch_shapes=[pltpu.VMEM((ts, cw), F32), pltpu.VMEM((ts, cw), F32), pltpu.VMEM((1, cw), F32)],
        compiler_params=pltpu.CompilerParams(
            dimension_semantics=("parallel", "parallel", "arbitrary"),
            vmem_limit_bytes=VMEM_LIMIT_BYTES),
        name="lru_conv",
    )(z, z, z, z, z, lru_conv_w, flat(lru_conv_b), lru_w_a, flat(lru_b_a), lru_w_i, flat(lru_b_i),
      flat(lru_lambda), sc_conv_w)


XA_TM = 256


def _xattn_kernel(x_ref, g_ref, wq_ref, k_ref, v_ref, wo_ref, o_ref, *, scale):
    x = x_ref[...]
    xn = (x * lax.rsqrt(jnp.mean(x * x, axis=-1, keepdims=True) + NORM_EPS) * g_ref[...]).astype(BF16)
    q = jnp.dot(xn, wq_ref[...], preferred_element_type=F32).astype(BF16)
    heads = []
    for hh in range(XA_HEADS):
        cs = slice(hh * HEAD_DIM, (hh + 1) * HEAD_DIM)
        s = lax.dot_general(q[:, cs], k_ref[:, cs], (((1,), (1,)), ((), ())),
                            preferred_element_type=F32) * scale
        e = jnp.exp(s - jnp.max(s, axis=-1, keepdims=True))
        p = (e / jnp.sum(e, axis=-1, keepdims=True)).astype(BF16)
        heads.append(jnp.dot(p, v_ref[:, cs], preferred_element_type=F32))
    o = jnp.concatenate(heads, axis=-1).astype(BF16)
    o_ref[...] = x + jnp.dot(o, wo_ref[...], preferred_element_type=F32)


def cross_attention_pallas(x, gain, k, v, w_q, w_o):
    B, S, D = x.shape
    M = k.shape[1]
    tm = XA_TM
    per_b = S // tm
    kv = pl.BlockSpec((None, M, XA_WIDTH), lambda i: (i // per_b, 0, 0))
    return pl.pallas_call(
        functools.partial(_xattn_kernel, scale=HEAD_DIM ** -0.5),
        grid=(B * per_b,),
        in_specs=[pl.BlockSpec((tm, D), lambda i: (i, 0)),
                  pl.BlockSpec((1, D), lambda i: (0, 0)),
                  pl.BlockSpec((D, XA_WIDTH), lambda i: (0, 0)),
                  kv, kv,
                  pl.BlockSpec((XA_WIDTH, D), lambda i: (0, 0))],
        out_specs=pl.BlockSpec((tm, D), lambda i: (i, 0)),
        out_shape=jax.ShapeDtypeStruct((B * S, D), F32),
        compiler_params=pltpu.CompilerParams(
            dimension_semantics=("parallel",), vmem_limit_bytes=VMEM_LIMIT_BYTES),
        name="cross_attention",
    )(x.reshape(B * S, D), gain.astype(F32)[None, :], w_q, k, v, w_o).reshape(B, S, D)


MASK_BIG = 1e30
LANES = 128


def _first_index_topk_mask(work, colf, k):
    sel = jnp.zeros(work.shape, F32)
    for _ in range(k):
        mx = jnp.max(work, axis=-1, keepdims=True)
        idx = jnp.min(jnp.where(work == mx, colf, float(LANES)), axis=-1, keepdims=True)
        pick = colf == idx
        sel = jnp.where(pick, 1.0, sel)
        work = jnp.where(pick, -jnp.inf, work)
    return sel


ATT_TK = 512
MOBA_HEADS_PER_STEP = 4
M_INIT = -1e38
NT_DIMS = (((1,), (1,)), ((), ()))


def _flash_update(q_aug, k_aug, v_aug, m_ref, acc_ref, causal):
    s = lax.dot_general(q_aug, k_aug, NT_DIMS, preferred_element_type=F32)
    if causal is not None:
        q0, k0 = causal
        row = lax.broadcasted_iota(jnp.int32, s.shape, 0)
        col = lax.broadcasted_iota(jnp.int32, s.shape, 1)
        s = jnp.where(k0 + col <= q0 + row, s, NEG)
    m_prev = m_ref[...]
    m_new = jnp.maximum(m_prev, jnp.max(s, axis=-1, keepdims=True))
    alpha = jnp.exp(m_prev - m_new)
    p = jnp.exp(s - m_new).astype(BF16)
    acc_ref[...] = alpha * acc_ref[...] + jnp.dot(p, v_aug, preferred_element_type=F32)
    m_ref[...] = m_new


def _moba_kernel(q_ref, k_ref, v_ref, o_ref, kmean_ref, qa_ref, m_ref, acc_ref):
    i = pl.program_id(2)
    bs = MOBA_BLOCK
    seq = k_ref.shape[0]
    hp = q_ref.shape[1] // HEAD_DIM
    nt = NT_DIMS
    hcol = lambda h: slice(h * HEAD_DIM, (h + 1) * HEAD_DIM)

    @pl.when(i == 0)
    def _():
        row = lax.broadcasted_iota(jnp.int32, (LANES, seq), 0)
        col = lax.broadcasted_iota(jnp.int32, (LANES, seq), 1)
        ind = jnp.where(col // bs == row, 1.0, 0.0).astype(BF16)
        kmean_ref[...] = (jnp.dot(ind, k_ref[...], preferred_element_type=F32)
                          * (1.0 / bs)).astype(BF16)

    coli = lax.broadcasted_iota(jnp.int32, (bs, LANES), 1)
    colf = coli.astype(F32)
    past = coli < i
    for h in range(hp):
        q = q_ref[:, hcol(h)]
        gate = lax.dot_general(q, kmean_ref[:, hcol(h)], nt, preferred_element_type=F32)
        sel = _first_index_topk_mask(jnp.where(past, gate, NEG), colf, MOBA_TOPK)
        notsel = jnp.where(((sel > 0.0) & past) | (coli == i), 0.0, 1.0).astype(BF16)
        qa_ref[h] = jnp.concatenate([q, notsel], axis=-1)

    tk = ATT_TK
    blk_per_tile = tk // bs
    n_tiles = (i + blk_per_tile) // blk_per_tile
    key_blk = lax.broadcasted_iota(jnp.int32, (tk, LANES), 0) // bs
    key_col = lax.broadcasted_iota(jnp.int32, (tk, LANES), 1)
    ones_v = jnp.ones((tk, LANES), BF16)
    m_ref[...] = jnp.full(m_ref.shape, M_INIT, F32)
    acc_ref[...] = jnp.zeros_like(acc_ref)

    def tile(jt, causal):
        off = pl.multiple_of(jt * tk, tk)
        bias = jnp.where(key_col == jt * blk_per_tile + key_blk, -MASK_BIG, 0.0).astype(BF16)
        for h in range(hp):
            k_aug = jnp.concatenate([k_ref[pl.ds(off, tk), hcol(h)], bias], axis=-1)
            v_aug = jnp.concatenate([v_ref[pl.ds(off, tk), hcol(h)], ones_v], axis=-1)
            _flash_update(qa_ref[h], k_aug, v_aug, m_ref.at[h], acc_ref.at[h],
                          (i * bs, off) if causal else None)

    def body(jt, carry):
        tile(jt, False)
        return carry

    lax.fori_loop(0, n_tiles - 1, body, 0)
    tile(n_tiles - 1, True)
    for h in range(hp):
        acc = acc_ref[h]
        o_ref[:, hcol(h)] = (acc[:, :HEAD_DIM] / acc[:, HEAD_DIM:]).astype(o_ref.dtype)


def moba_attention_pallas(zq, zk, zv, *, n_heads, q_off=0, k_off=0, v_off=0):
    B, S, _ = zq.shape
    bs = MOBA_BLOCK
    hp = min(MOBA_HEADS_PER_STEP, n_heads)
    hw = hp * HEAD_DIM
    assert S % ATT_TK == 0 and S // bs <= LANES and n_heads % hp == 0
    assert q_off % hp == 0 and k_off % hp == 0 and v_off % hp == 0
    return pl.pallas_call(
        _moba_kernel,
        grid=(B, n_heads // hp, S // bs),
        in_specs=[pl.BlockSpec((None, bs, hw), lambda b, h, i: (b, i, q_off // hp + h)),
                  pl.BlockSpec((None, S, hw), lambda b, h, i: (b, 0, k_off // hp + h)),
                  pl.BlockSpec((None, S, hw), lambda b, h, i: (b, 0, v_off // hp + h))],
        out_specs=pl.BlockSpec((None, bs, hw), lambda b, h, i: (b, i, h)),
        out_shape=jax.ShapeDtypeStruct((B, S, n_heads * HEAD_DIM), BF16),
        scratch_shapes=[pltpu.VMEM((LANES, hw), BF16), pltpu.VMEM((hp, bs, 2 * HEAD_DIM), BF16),
                        pltpu.VMEM((hp, bs, 1), F32), pltpu.VMEM((hp, bs, 2 * HEAD_DIM), F32)],
        compiler_params=pltpu.CompilerParams(
            dimension_semantics=("parallel", "parallel", "arbitrary"),
            vmem_limit_bytes=VMEM_LIMIT_BYTES),
        name="moba_attention",
    )(zq, zk, zv)


NSA_TQ = 256
NSA_NC_PAD = 256
CMP_ROW = NSA_CMP_STRIDE * HEAD_DIM


def _nsa_compress_kernel(x_ref, pe_ref, w1_ref, w2_ref, o_ref):
    x = x_ref[...].astype(F32)
    top = (x + pe_ref[0:1, :]).astype(BF16)
    bot = (x + pe_ref[1:2, :]).astype(BF16)
    a = jnp.dot(top, w1_ref[0:CMP_ROW, :], preferred_element_type=F32)
    b = jnp.dot(bot, w1_ref[CMP_ROW:2 * CMP_ROW, :], preferred_element_type=F32)
    pre = a + pltpu.roll(b, b.shape[0] - 1, 0)
    hid = jax.nn.gelu(pre)
    o_ref[...] = jnp.dot(hid.astype(BF16), w2_ref[...], preferred_element_type=F32).astype(o_ref.dtype)


def nsa_compress_pallas(x, pe, w1, w2):
    B, S, gw = x.shape
    G = gw // HEAD_DIM
    nrow = S // NSA_CMP_STRIDE
    xr = x.reshape(B, nrow, NSA_CMP_STRIDE, G, HEAD_DIM).transpose(0, 3, 1, 2, 4).reshape(B, G, nrow, CMP_ROW)
    pe2 = pe.astype(F32).reshape(2, CMP_ROW)
    return pl.pallas_call(
        _nsa_compress_kernel,
        grid=(B, G),
        in_specs=[pl.BlockSpec((None, None, nrow, CMP_ROW), lambda b, g: (b, g, 0, 0)),
                  pl.BlockSpec((2, CMP_ROW), lambda b, g: (0, 0)),
                  pl.BlockSpec((2 * CMP_ROW, HEAD_DIM), lambda b, g: (0, 0)),
                  pl.BlockSpec((HEAD_DIM, HEAD_DIM), lambda b, g: (0, 0))],
        out_specs=pl.BlockSpec((None, None, nrow, HEAD_DIM), lambda b, g: (b, g, 0, 0)),
        out_shape=jax.ShapeDtypeStruct((B, G, nrow, HEAD_DIM), BF16),
        compiler_params=pltpu.CompilerParams(
            dimension_semantics=("parallel", "parallel"), vmem_limit_bytes=VMEM_LIMIT_BYTES),
        name="nsa_compress",
    )(xr, pe2, w1.astype(BF16), w2.astype(BF16))


def _nsa_kernel(q_ref, kc_ref, vc_ref, ks_ref, vs_ref, kw_ref, vw_ref, g_ref, wmap_ref, o_ref,
                m_ref, acc_ref, out_ref):
    i = pl.program_id(2)
    tq = NSA_TQ
    R = NSA_HEADS // NSA_KV_HEADS
    nt = (((1,), (1,)), ((), ()))
    row = lax.broadcasted_iota(jnp.int32, (tq, tq), 0)
    col = lax.broadcasted_iota(jnp.int32, (tq, tq), 1)
    t_abs = i * tq + row
    gates = jax.nn.sigmoid(g_ref[...])

    def qh(r):
        return q_ref[:, r * HEAD_DIM:(r + 1) * HEAD_DIM]

    ncp = kc_ref.shape[0]
    cmask = (lax.broadcasted_iota(jnp.int32, (tq, ncp), 1) * NSA_CMP_STRIDE + (NSA_CMP_BLOCK - 1)
             <= i * tq + lax.broadcasted_iota(jnp.int32, (tq, ncp), 0))
    imp = jnp.zeros((tq, LANES), F32)
    vc_aug = jnp.concatenate([vc_ref[...], jnp.ones((ncp, LANES), BF16), wmap_ref[...]], axis=-1)
    for r in range(R):
        s = lax.dot_general(qh(r), kc_ref[...], nt, preferred_element_type=F32)
        s = jnp.where(cmask, s, NEG)
        e = jnp.where(cmask, jnp.exp(s - jnp.max(s, axis=-1, keepdims=True)), 0.0).astype(BF16)
        acc = jnp.dot(e, vc_aug, preferred_element_type=F32)
        l = acc[:, HEAD_DIM:2 * HEAD_DIM]
        inv = jnp.where(l > 0.0, 1.0 / l, 0.0)
        imp = imp + acc[:, 2 * HEAD_DIM:] * inv
        out_ref[r] = gates[:, 3 * r:3 * r + 1] * (acc[:, :HEAD_DIM] * inv)

    ones_w = jnp.ones((tq, LANES), BF16)
    w_tiles = []
    for d in range(NSA_WINDOW // tq + 1):
        off = pl.multiple_of(jnp.maximum(i - d, 0) * tq, tq)
        gone = jnp.where(i >= d, 0, 2 * NSA_WINDOW + tq)
        dist = d * tq + row - col + gone
        w_tiles.append((off, (dist >= 0) & (dist < NSA_WINDOW)))
    for r in range(R):
        ss = []
        for off, mask in w_tiles:
            s = lax.dot_general(qh(r), kw_ref[pl.ds(off, tq), :], nt, preferred_element_type=F32)
            ss.append(jnp.where(mask, s, NEG))
        m = jnp.max(ss[0], axis=-1, keepdims=True)
        for s in ss[1:]:
            m = jnp.maximum(m, jnp.max(s, axis=-1, keepdims=True))
        acc = jnp.zeros((tq, 2 * HEAD_DIM), F32)
        for (off, _), s in zip(w_tiles, ss):
            v_aug = jnp.concatenate([vw_ref[pl.ds(off, tq), :], ones_w], axis=-1)
            acc = acc + jnp.dot(jnp.exp(s - m).astype(BF16), v_aug, preferred_element_type=F32)
        out_ref[r] = out_ref[r] + gates[:, 3 * r + 2:3 * r + 3] * (acc[:, :HEAD_DIM] / acc[:, HEAD_DIM:])

    rowl = lax.broadcasted_iota(jnp.int32, (tq, LANES), 0)
    coll = lax.broadcasted_iota(jnp.int32, (tq, LANES), 1)
    q_blk = (i * tq + rowl) // NSA_SEL_BLOCK
    valid = coll <= q_blk
    forced = (coll == 0) | (coll >= q_blk - 1)
    impm = jnp.where(valid, imp + jnp.where(forced, NSA_FORCE_BONUS, 0.0), NEG)
    sel = _first_index_topk_mask(impm, coll.astype(F32), NSA_SEL_TOPK)
    notsel = jnp.where((sel > 0.0) & valid, 0.0, 1.0).astype(BF16)

    tk = ATT_TK
    blk_per_tile = tk // NSA_SEL_BLOCK
    n_tiles = (i * tq) // tk + 1
    key_blk = lax.broadcasted_iota(jnp.int32, (tk, LANES), 0) // NSA_SEL_BLOCK
    key_col = lax.broadcasted_iota(jnp.int32, (tk, LANES), 1)
    ones_v = jnp.ones((tk, LANES), BF16)
    m_ref[...] = jnp.full(m_ref.shape, M_INIT, F32)
    acc_ref[...] = jnp.zeros_like(acc_ref)

    def sel_tile(jt, causal):
        off = pl.multiple_of(jt * tk, tk)
        bias = jnp.where(key_col == jt * blk_per_tile + key_blk, -MASK_BIG, 0.0).astype(BF16)
        k_aug = jnp.concatenate([ks_ref[pl.ds(off, tk), :], bias], axis=-1)
        v_aug = jnp.concatenate([vs_ref[pl.ds(off, tk), :], ones_v], axis=-1)
        for r in range(R):
            q_aug = jnp.concatenate([qh(r), notsel], axis=-1)
            _flash_update(q_aug, k_aug, v_aug, m_ref.at[r], acc_ref.at[r], (i * tq, off) if causal else None)

    def body(jt, carry):
        sel_tile(jt, False)
        return carry

    lax.fori_loop(0, n_tiles - 1, body, 0)
    sel_tile(n_tiles - 1, True)
    for r in range(R):
        acc = acc_ref[r]
        o_s = acc[:, :HEAD_DIM] / acc[:, HEAD_DIM:]
        o_ref[:, r * HEAD_DIM:(r + 1) * HEAD_DIM] = (
            out_ref[r] + gates[:, 3 * r + 1:3 * r + 2] * o_s).astype(o_ref.dtype)


def nsa_attention_pallas(zq, kc, vc, zks, zvs, zkw, zvw, gate_logits, *, q_off, ks_off, vs_off, kw_off, vw_off):
    B, S, _ = zq.shape
    G, tq = NSA_KV_HEADS, NSA_TQ
    R = NSA_HEADS // G
    assert S % ATT_TK == 0 and ATT_TK % tq == 0 and S // NSA_SEL_BLOCK <= LANES
    nc = (S - NSA_CMP_BLOCK) // NSA_CMP_STRIDE + 1
    ncp = kc.shape[2]
    ns = S // NSA_SEL_BLOCK
    wmap = jnp.pad(cmp_to_sel_weights(nc, ns), ((0, ncp - nc), (0, LANES - ns))).astype(BF16)
    rw = R * HEAD_DIM
    kv_spec = lambda off: pl.BlockSpec((None, S, HEAD_DIM), lambda b, g, i: (b, 0, off + g))
    c_spec = pl.BlockSpec((None, None, ncp, HEAD_DIM), lambda b, g, i: (b, g, 0, 0))
    return pl.pallas_call(
        _nsa_kernel,
        grid=(B, G, S // tq),
        in_specs=[pl.BlockSpec((None, tq, rw), lambda b, g, i: (b, i, q_off // R + g)),
                  c_spec, c_spec, kv_spec(ks_off), kv_spec(vs_off), kv_spec(kw_off), kv_spec(vw_off),
                  pl.BlockSpec((None, tq, LANES), lambda b, g, i: (b, i, g)),
                  pl.BlockSpec((ncp, LANES), lambda b, g, i: (0, 0))],
        out_specs=pl.BlockSpec((None, tq, rw), lambda b, g, i: (b, i, g)),
        out_shape=jax.ShapeDtypeStruct((B, S, NSA_HEADS * HEAD_DIM), BF16),
        scratch_shapes=[pltpu.VMEM((R, tq, 1), F32),
                        pltpu.VMEM((R, tq, 2 * HEAD_DIM), F32), pltpu.VMEM((R, tq, HEAD_DIM), F32)],
        compiler_params=pltpu.CompilerParams(
            dimension_semantics=("parallel", "parallel", "arbitrary"),
            vmem_limit_bytes=VMEM_LIMIT_BYTES),
        name="nsa_attention",
    )(zq, kc, vc, zks, zvs, zkw, zvw, gate_logits, wmap)


MOE_TM = 256
MOE_TN = 512
ROUTER_TM = 256


def _router_kernel(x_ref, g_ref, w_ref, b_ref, t_ref, ids_ref, wts_ref):
    x = x_ref[...]
    t = x * lax.rsqrt(jnp.mean(x * x, axis=-1, keepdims=True) + NORM_EPS) * g_ref[...]
    tb = t.astype(BF16)
    bits = lax.bitcast_convert_type(tb.astype(F32), jnp.uint32)
    tm, half = bits.shape[0], bits.shape[1] // 2
    packed = (bits[:, :half] & jnp.uint32(0xFFFF0000)) | (bits[:, half:] >> 16)
    n = half // LANES
    for c in range(n):
        t_ref[pl.ds(c, tm, stride=n), :] = packed[:, c * LANES:(c + 1) * LANES]
    logits = jnp.dot(tb, w_ref[...], preferred_element_type=F32) + b_ref[...]
    coli = lax.broadcasted_iota(jnp.int32, logits.shape, 1)
    colf = coli.astype(F32)
    first = lambda hit: jnp.min(jnp.where(hit, colf, float(LANES)), axis=-1, keepdims=True)
    is_g = coli < N_GROUPS
    gl = jnp.where(is_g, logits, -jnp.inf)
    gmax = jnp.max(gl, axis=-1, keepdims=True)
    g_sel = first(gl == gmax)
    p_g = 1.0 / jnp.sum(jnp.where(is_g, jnp.exp(logits - gmax), 0.0), axis=-1, keepdims=True)
    lo = N_GROUPS + EXPERTS_PER_GROUP * g_sel
    el = jnp.where((colf >= lo) & (colf < lo + EXPERTS_PER_GROUP), logits, -jnp.inf)
    v1 = jnp.max(el, axis=-1, keepdims=True)
    i1 = first(el == v1)
    el2 = jnp.where(colf == i1, -jnp.inf, el)
    v2 = jnp.max(el2, axis=-1, keepdims=True)
    i2 = first(el2 == v2)
    e = jnp.exp(v2 - v1)
    w1 = p_g / (1.0 + e)
    w2 = p_g * e / (1.0 + e)
    ids_ref[...] = jnp.where(coli == 0, i1 - N_GROUPS, jnp.where(coli == 1, i2 - N_GROUPS, 0.0)).astype(jnp.int32)
    wts_ref[...] = jnp.where(coli == 0, w1, jnp.where(coli == 1, w2, 0.0))


def moe_router_pallas(x2, gain, w_group, b_group, w_expert, b_expert):
    T, D = x2.shape
    tm = ROUTER_TM
    nr = N_GROUPS + N_EXPERTS
    w = jnp.pad(jnp.concatenate([w_group, w_expert], axis=1), ((0, 0), (0, LANES - nr))).astype(BF16)
    b = jnp.pad(jnp.concatenate([b_group, b_expert]), (0, LANES - nr)).astype(F32)[None, :]
    t, ids, wts = pl.pallas_call(
        _router_kernel,
        grid=(T // tm,),
        in_specs=[pl.BlockSpec((tm, D), lambda i: (i, 0)),
                  pl.BlockSpec((1, D), lambda i: (0, 0)),
                  pl.BlockSpec((D, LANES), lambda i: (0, 0)),
                  pl.BlockSpec((1, LANES), lambda i: (0, 0))],
        out_specs=[pl.BlockSpec((tm * (D // 2 // LANES), LANES), lambda i: (i, 0)),
                   pl.BlockSpec((tm, LANES), lambda i: (i, 0)),
                   pl.BlockSpec((tm, LANES), lambda i: (i, 0))],
        out_shape=[jax.ShapeDtypeStruct((T * (D // 2 // LANES), LANES), jnp.uint32),
                   jax.ShapeDtypeStruct((T, LANES), jnp.int32),
                   jax.ShapeDtypeStruct((T, LANES), F32)],
        compiler_params=pltpu.CompilerParams(
            dimension_semantics=("parallel",), vmem_limit_bytes=VMEM_LIMIT_BYTES),
        name="moe_router",
    )(x2, gain.astype(F32)[None, :], w, b)
    return t, ids[:, :TOPK_IN_GROUP], wts[:, :TOPK_IN_GROUP]


def _moe_up_kernel(tile_e_ref, n_used_ref, row_tok_ref, t_hbm, rw_ref, wg_ref, wu_ref, hid_ref,
                   xw_ref, xb_ref, sem):
    k = pl.program_id(0)
    tm = xb_ref.shape[0]
    n = xw_ref.shape[0] // tm
    half = n * LANES
    n_used = n_used_ref[0]

    def row_copy(tok, r):
        return pltpu.make_async_copy(t_hbm.at[pl.ds(pl.multiple_of(tok * n, n), n), :],
                                     xw_ref.at[pl.ds(pl.multiple_of(r * n, n), n), :], sem.at[0])

    def start_gather(tile):
        def issue(r, carry):
            row_copy(row_tok_ref[tile * tm + r], r).start()
            return carry
        lax.fori_loop(0, tm, issue, 0, unroll=8)

    @pl.when(k == 0)
    def _():
        start_gather(0)

    @pl.when(k < n_used)
    def _():
        def wait_row(r, carry):
            row_copy(0, r).wait()
            return carry
        lax.fori_loop(0, tm, wait_row, 0, unroll=8)
        for c in range(n):
            u = xw_ref[pl.ds(c, tm, stride=n), :]
            lo, hi = c * LANES, (c + 1) * LANES
            xb_ref[:, lo:hi] = lax.bitcast_convert_type(u & jnp.uint32(0xFFFF0000), F32).astype(BF16)
            xb_ref[:, half + lo:half + hi] = lax.bitcast_convert_type(u << 16, F32).astype(BF16)

        @pl.when(k + 1 < n_used)
        def _():
            start_gather(k + 1)

        x = xb_ref[...]
        hg = jnp.dot(x, wg_ref[...].astype(BF16), preferred_element_type=F32)
        hu = jnp.dot(x, wu_ref[...].astype(BF16), preferred_element_type=F32)
        hid_ref[...] = (jax.nn.silu(hg) * hu * rw_ref[...]).astype(hid_ref.dtype)

    @pl.when(k >= n_used)
    def _():
        hid_ref[...] = jnp.zeros_like(hid_ref)


def _moe_down_kernel(tile_e_ref, n_used_ref, hid_ref, wd_ref, o_ref):
    k = pl.program_id(0)

    @pl.when(k < n_used_ref[0])
    def _():
        hid = hid_ref[...]
        for c in range(0, o_ref.shape[1], MOE_TN):
            o_ref[:, c:c + MOE_TN] = jnp.dot(hid, wd_ref[:, c:c + MOE_TN].astype(BF16),
                                             preferred_element_type=F32)

    @pl.when(k >= n_used_ref[0])
    def _():
        o_ref[...] = jnp.zeros_like(o_ref)


def moe_experts_pallas(t_packed, row_tok, row_w, tile_e, n_used, w_gate, w_up, w_down, layer):
    P = row_tok.shape[0]
    D = w_gate.shape[-2]
    half = D // 2
    F = w_gate.shape[-1]
    tm = MOE_TM
    l = int(layer)
    w_in = pl.BlockSpec((None, None, D, F), lambda k, te, *_: (l, te[k], 0, 0))
    hid = pl.pallas_call(
        _moe_up_kernel,
        grid_spec=pltpu.PrefetchScalarGridSpec(
            num_scalar_prefetch=3,
            grid=(P // tm,),
            in_specs=[pl.BlockSpec(memory_space=pl.ANY),
                      pl.BlockSpec((tm, 1), lambda k, *_: (k, 0)),
                      w_in, w_in],
            out_specs=pl.BlockSpec((tm, F), lambda k, *_: (k, 0)),
            scratch_shapes=[pltpu.VMEM((tm * half // LANES, LANES), jnp.uint32), pltpu.VMEM((tm, D), BF16),
                            pltpu.SemaphoreType.DMA((1,))]),
        out_shape=jax.ShapeDtypeStruct((P, F), BF16),
        compiler_params=pltpu.CompilerParams(
            dimension_semantics=("arbitrary",), vmem_limit_bytes=VMEM_LIMIT_BYTES),
        name="moe_up",
    )(tile_e, n_used, row_tok, t_packed, row_w, w_gate, w_up)
    return pl.pallas_call(
        _moe_down_kernel,
        grid_spec=pltpu.PrefetchScalarGridSpec(
            num_scalar_prefetch=2,
            grid=(P // tm,),
            in_specs=[pl.BlockSpec((tm, F), lambda k, *_: (k, 0)),
                      pl.BlockSpec((None, None, F, D), lambda k, te, *_: (l, te[k], 0, 0))],
            out_specs=pl.BlockSpec((tm, D), lambda k, *_: (k, 0))),
        out_shape=jax.ShapeDtypeStruct((P, D), F32),
        compiler_params=pltpu.CompilerParams(
            dimension_semantics=("parallel",), vmem_limit_bytes=VMEM_LIMIT_BYTES),
        name="moe_down",
    )(tile_e, n_used, hid, w_down)


def moe_dispatch_plan(ids, wts):
    T, K = ids.shape
    n = T * K
    P = n + N_EXPERTS * MOE_TM
    flat = ids.reshape(n)
    onehot = (flat[:, None] == jnp.arange(N_EXPERTS, dtype=jnp.int32)[None, :]).astype(jnp.int32)
    csum = jnp.cumsum(onehot, axis=0)
    counts = csum[-1]
    padded = (counts + MOE_TM - 1) // MOE_TM * MOE_TM
    pend = jnp.cumsum(padded)
    pstart = pend - padded
    pos = jnp.sum(onehot * (csum - 1 + pstart[None, :]), axis=1)
    tok = (jnp.arange(n, dtype=jnp.int32) // K).astype(F32)
    info = jnp.zeros((P, 2), F32).at[pos].set(jnp.stack([tok, wts.reshape(n)], axis=1))
    row_tok = info[:, 0].astype(jnp.int32)
    row_w = info[:, 1:2]
    tile_start = jnp.arange(P // MOE_TM, dtype=jnp.int32) * MOE_TM
    tile_e = jnp.minimum(jnp.sum((pend[None, :] <= tile_start[:, None]).astype(jnp.int32), axis=1),
                         N_EXPERTS - 1).astype(jnp.int32)
    n_used = (pend[-1] // MOE_TM).astype(jnp.int32)[None]
    return pos.reshape(T, K), row_tok, row_w, tile_e, n_used


def hierarchical_moe_pallas(x, gain, w_group, b_group, w_expert, b_expert, w_gate, w_up, w_down, layer):
    B, S, D = x.shape
    x2 = x.reshape(B * S, D)
    t, ids, wts = moe_router_pallas(x2, gain, w_group, b_group, w_expert, b_expert)
    pos, row_tok, row_w, tile_e, n_used = moe_dispatch_plan(ids, wts)
    y = moe_experts_pallas(t, row_tok, row_w, tile_e, n_used, w_gate, w_up, w_down, layer)
    out = x2 + jnp.take(y, pos[:, 0], axis=0) + jnp.take(y, pos[:, 1], axis=0)
    return out.reshape(B, S, D)


def rms_norm(x, g):
    xf = x.astype(F32)
    y = xf * lax.rsqrt(jnp.mean(xf * xf, axis=-1, keepdims=True) + NORM_EPS)
    return (y * g.astype(F32)).astype(x.dtype)


def rope_tables(positions):
    inv = ROPE_THETA ** (-jnp.arange(0, HEAD_DIM, 2, dtype=F32) / HEAD_DIM)
    ang = positions.astype(F32)[..., None] * inv
    return jnp.cos(ang)[:, :, None, :], jnp.sin(ang)[:, :, None, :]


def rope(x, cos, sin):
    x1, x2 = jnp.split(x.astype(F32), 2, axis=-1)
    return jnp.concatenate([x1 * cos - x2 * sin, x2 * cos + x1 * sin], axis=-1).astype(x.dtype)


def split_heads(t):
    return t.reshape(t.shape[0], t.shape[1], -1, HEAD_DIM)


def split_cols(z):
    return jnp.split(z, [int(v) for v in np.cumsum(MIX_SPLITS)[:-1]], axis=-1)


def causal_depthwise_conv(x, w):
    width, c = w.shape
    xp = jnp.pad(x, ((0, 0), (width - 1, 0), (0, 0)))
    return lax.conv_general_dilated(xp, w[:, None, :].astype(x.dtype), (1,), 'VALID',
                                    dimension_numbers=('NWC', 'WIO', 'NWC'),
                                    feature_group_count=c)


def moba_attention(q, k, v):
    B, S, H, dh = q.shape
    bs = MOBA_BLOCK
    nb = -(-S // bs)
    scale = dh ** -0.5
    qh, kh, vh = (t.transpose(0, 2, 1, 3) for t in (q, k, v))
    pad = nb * bs - S
    kp = jnp.pad(kh, ((0, 0), (0, 0), (0, pad), (0, 0)))
    vp = jnp.pad(vh, ((0, 0), (0, 0), (0, pad), (0, 0)))
    kb = kp.reshape(B, H, nb, bs, dh)
    vb = vp.reshape(B, H, nb, bs, dh)
    k_mean = jnp.mean(kb.astype(F32), axis=3)
    t = jnp.arange(S)
    q_blk = t // bs
    gate = jnp.einsum('bhsd,bhnd->bhsn', qh.astype(F32), k_mean)
    past = jnp.arange(nb)[None, :] < q_blk[:, None]
    gate = jnp.where(past, gate, NEG)
    kk = min(MOBA_TOPK, nb)
    _, sel = lax.top_k(gate, kk)
    sel_ok = sel < q_blk[:, None]
    qb = MOBA_Q_BLOCK
    nqb = S // qb
    q_c = qh.reshape(B, H, nqb, qb, dh).transpose(2, 0, 1, 3, 4)
    sel_c = sel.reshape(B, H, nqb, qb, kk).transpose(2, 0, 1, 3, 4)
    ok_c = sel_ok.reshape(B, H, nqb, qb, kk).transpose(2, 0, 1, 3, 4)
    bi = jnp.arange(B)[:, None, None, None]
    hi = jnp.arange(H)[None, :, None, None]

    def chunk(args):
        qc, selc, okc, ci = args
        tq = ci * qb + jnp.arange(qb)
        own = (ci * qb) // bs * bs
        k_own = lax.dynamic_slice_in_dim(kp, own, bs, axis=2)
        v_own = lax.dynamic_slice_in_dim(vp, own, bs, axis=2)
        own_mask = (own + jnp.arange(bs))[None, :] <= tq[:, None]
        k_s = kb[bi, hi, selc].reshape(B, H, qb, kk * bs, dh)
        v_s = vb[bi, hi, selc].reshape(B, H, qb, kk * bs, dh)
        s_sel = jnp.einsum('bhqd,bhqkd->bhqk', qc, k_s, preferred_element_type=F32) * scale
        s_sel = jnp.where(jnp.repeat(okc, bs, axis=-1), s_sel, NEG)
        s_own = jnp.einsum('bhqd,bhkd->bhqk', qc, k_own, preferred_element_type=F32) * scale
        s_own = jnp.where(own_mask, s_own, NEG)
        p = jax.nn.softmax(jnp.concatenate([s_sel, s_own], axis=-1), axis=-1).astype(v.dtype)
        return (jnp.einsum('bhqk,bhqkd->bhqd', p[..., :kk * bs], v_s)
                + jnp.einsum('bhqk,bhkd->bhqd', p[..., kk * bs:], v_own))

    o = lax.map(chunk, (q_c, sel_c, ok_c, jnp.arange(nqb)))
    return o.transpose(1, 0, 3, 2, 4).reshape(B, S, H * dh)


def compress_tokens(x, pe, w1, w2):
    B, S, G, dh = x.shape
    nc = (S - NSA_CMP_BLOCK) // NSA_CMP_STRIDE + 1
    idx = jnp.arange(nc)[:, None] * NSA_CMP_STRIDE + jnp.arange(NSA_CMP_BLOCK)[None, :]
    blocks = x[:, idx] + pe[:, None, :]
    flat = blocks.transpose(0, 1, 3, 2, 4).reshape(B, nc, G, NSA_CMP_BLOCK * dh)
    return jax.nn.gelu(flat @ w1) @ w2


def cmp_to_sel_weights(nc, ns):
    r = NSA_SEL_BLOCK // NSA_CMP_STRIDE
    m = NSA_CMP_BLOCK // NSA_CMP_STRIDE
    c = jnp.arange(nc)[:, None] - r * jnp.arange(ns)[None, :]
    w = jnp.minimum(jnp.minimum(c + 1, r + m - 1 - c), min(r, m))
    return jnp.clip(w, 0, None).astype(F32)


def nsa_attention(q, k_cmp, v_cmp, k_sel, v_sel, k_win, v_win, gate_logits,
                  pe_k, w1_k, w2_k, pe_v, w1_v, w2_v):
    B, S, H, dh = q.shape
    G = k_sel.shape[2]
    R = H // G
    scale = dh ** -0.5
    t = jnp.arange(S)
    qg = q.reshape(B, S, G, R, dh)
    kc = compress_tokens(k_cmp, pe_k, w1_k, w2_k)
    vc = compress_tokens(v_cmp, pe_v, w1_v, w2_v)
    nc = kc.shape[1]
    s_c = jnp.einsum('bsgrd,bngd->bgrsn', qg, kc, preferred_element_type=F32) * scale
    c_end = jnp.arange(nc) * NSA_CMP_STRIDE + NSA_CMP_BLOCK - 1
    c_mask = c_end[None, :] <= t[:, None]
    p_c = jax.nn.softmax(jnp.where(c_mask, s_c, NEG), axis=-1) * c_mask
    o_cmp = jnp.einsum('bgrsn,bngd->bsgrd', p_c.astype(vc.dtype), vc).reshape(B, S, H, dh)
    ns = S // NSA_SEL_BLOCK
    imp = jnp.einsum('bgrsn,nm->bgsm', p_c, cmp_to_sel_weights(nc, ns))
    q_blk = t // NSA_SEL_BLOCK
    j = jnp.arange(ns)[None, :]
    valid = j <= q_blk[:, None]
    forced = (j == 0) | (j >= q_blk[:, None] - 1)
    imp = jnp.where(valid, imp + jnp.where(forced, NSA_FORCE_BONUS, 0.0), NEG)
    kk = min(NSA_SEL_TOPK, ns)
    _, sel = lax.top_k(imp, kk)
    sel_ok = sel <= q_blk[:, None]
    sb = NSA_SEL_BLOCK
    k_blocks = k_sel.transpose(0, 2, 1, 3).reshape(B, G, ns, sb, dh)
    v_blocks = v_sel.transpose(0, 2, 1, 3).reshape(B, G, ns, sb, dh)
    wpad = ((0, 0), (0, 0), (NSA_WINDOW, 0), (0, 0))
    k_wp = jnp.pad(k_win.transpose(0, 2, 1, 3), wpad)
    v_wp = jnp.pad(v_win.transpose(0, 2, 1, 3), wpad)
    nqb = S // Q_BLOCK
    q_c = qg.transpose(0, 2, 3, 1, 4).reshape(B, G, R, nqb, Q_BLOCK, dh).transpose(3, 0, 1, 2, 4, 5)
    sel_c = sel.reshape(B, G, nqb, Q_BLOCK, kk).transpose(2, 0, 1, 3, 4)
    ok_c = sel_ok.reshape(B, G, nqb, Q_BLOCK, kk).transpose(2, 0, 1, 3, 4)
    bi = jnp.arange(B)[:, None, None, None]
    gi = jnp.arange(G)[None, :, None, None]

    def chunk(args):
        qc, selc, okc, ci = args
        tq = ci * Q_BLOCK + jnp.arange(Q_BLOCK)
        ks = k_blocks[bi, gi, selc].reshape(B, G, Q_BLOCK, kk * sb, dh)
        vs = v_blocks[bi, gi, selc].reshape(B, G, Q_BLOCK, kk * sb, dh)
        kpos = (selc[..., None] * sb + jnp.arange(sb)).reshape(B, G, Q_BLOCK, kk * sb)
        m = (kpos <= tq[:, None]) & jnp.repeat(okc, sb, axis=-1)
        s = jnp.einsum('bgrqd,bgqkd->bgrqk', qc, ks, preferred_element_type=F32) * scale
        p = jax.nn.softmax(jnp.where(m[:, :, None], s, NEG), axis=-1)
        o_s = jnp.einsum('bgrqk,bgqkd->bgrqd', p.astype(vs.dtype), vs)
        kw = lax.dynamic_slice_in_dim(k_wp, ci * Q_BLOCK, Q_BLOCK + NSA_WINDOW, axis=2)
        vw = lax.dynamic_slice_in_dim(v_wp, ci * Q_BLOCK, Q_BLOCK + NSA_WINDOW, axis=2)
        wpos = ci * Q_BLOCK - NSA_WINDOW + jnp.arange(Q_BLOCK + NSA_WINDOW)
        dist = tq[:, None] - wpos[None, :]
        wm = (dist >= 0) & (dist < NSA_WINDOW) & (wpos[None, :] >= 0)
        s_w = jnp.einsum('bgrqd,bgkd->bgrqk', qc, kw, preferred_element_type=F32) * scale
        p_w = jax.nn.softmax(jnp.where(wm, s_w, NEG), axis=-1)
        o_w = jnp.einsum('bgrqk,bgkd->bgrqd', p_w.astype(vw.dtype), vw)
        return o_s, o_w

    o_sel, o_win = lax.map(chunk, (q_c, sel_c, ok_c, jnp.arange(nqb)))
    o_sel = o_sel.transpose(1, 0, 4, 2, 3, 5).reshape(B, S, H, dh)
    o_win = o_win.transpose(1, 0, 4, 2, 3, 5).reshape(B, S, H, dh)
    g = jax.nn.sigmoid(gate_logits.astype(F32)).reshape(B, S, H, 3)
    out = g[..., 0:1] * o_cmp + g[..., 1:2] * o_sel + g[..., 2:3] * o_win
    return out.reshape(B, S, H * dh).astype(q.dtype)


def rg_lru_branch(x_in, gate_in, conv_w, conv_b, w_a, b_a, w_i, b_i, lam):
    B, S, C = x_in.shape
    xc = (causal_depthwise_conv(x_in, conv_w) + conv_b).astype(F32)
    xh = xc.reshape(B, S, LRU_HEADS, C // LRU_HEADS)
    r = jax.nn.sigmoid(jnp.einsum('bshi,hij->bshj', xh, w_a.astype(F32)) + b_a).reshape(B, S, C)
    i = jax.nn.sigmoid(jnp.einsum('bshi,hij->bshj', xh, w_i.astype(F32)) + b_i).reshape(B, S, C)
    log_a = -LRU_C * r * jax.nn.softplus(-lam.astype(F32))
    a = jnp.exp(log_a)
    b = jnp.sqrt(-jnp.expm1(2.0 * log_a)) * (i * xc)

    def combine(c1, c2):
        a1, b1 = c1
        a2, b2 = c2
        return a1 * a2, a2 * b1 + b2

    _, h = lax.associative_scan(combine, (a, b), axis=1)
    return (h * jax.nn.gelu(gate_in.astype(F32))).astype(x_in.dtype)


def short_conv_branch(x_in, b_gate, c_gate, conv_w):
    return b_gate * causal_depthwise_conv(c_gate * x_in, conv_w)


def cross_attention(h, mem_n, w_q, w_k, w_v, w_o):
    B, S, _ = h.shape
    M = mem_n.shape[1]
    q = dense(h, w_q).reshape(B, S, XA_HEADS, HEAD_DIM)
    k = dense(mem_n, w_k).reshape(B, M, XA_HEADS, HEAD_DIM)
    v = dense(mem_n, w_v).reshape(B, M, XA_HEADS, HEAD_DIM)
    s = jnp.einsum('bshd,bmhd->bhsm', q, k, preferred_element_type=F32) * HEAD_DIM ** -0.5
    p = jax.nn.softmax(s, axis=-1).astype(v.dtype)
    o = jnp.einsum('bhsm,bmhd->bshd', p, v).reshape(B, S, XA_WIDTH)
    return dense(o, w_o)


def hierarchical_moe(h, w_group, b_group, w_expert, b_expert, w_gate, w_up, w_down):
    B, S, D = h.shape
    t = h.reshape(B * S, D)
    g_logits = (t @ w_group + b_group).astype(F32)
    g_prob = jax.nn.softmax(g_logits, axis=-1)
    g_sel = jnp.argmax(g_logits, axis=-1)
    e_logits = (t @ w_expert + b_expert).astype(F32).reshape(-1, N_GROUPS, EXPERTS_PER_GROUP)
    e_in = jnp.take_along_axis(e_logits, g_sel[:, None, None], axis=1)[:, 0]
    top_v, top_i = lax.top_k(e_in, TOPK_IN_GROUP)
    p_g = jnp.take_along_axis(g_prob, g_sel[:, None], axis=1)
    w_top = jax.nn.softmax(top_v, axis=-1) * p_g
    eid = g_sel[:, None] * EXPERTS_PER_GROUP + top_i
    combine = jnp.sum(jax.nn.one_hot(eid, N_EXPERTS, dtype=F32) * w_top[..., None], axis=1)
    hid = jax.nn.silu(jnp.einsum('td,edf->tef', t, w_gate)) * jnp.einsum('td,edf->tef', t, w_up)
    out = jnp.einsum('tef,efd->td', hid * combine[:, :, None].astype(hid.dtype), w_down)
    return out.reshape(B, S, D)


def kernel(x, mem, positions, norm_mix, w_mix_in, lru_conv_w, lru_conv_b, lru_w_a, lru_b_a, lru_w_i, lru_b_i, lru_lambda, sc_conv_w, nsa_pe_k, nsa_w1_k, nsa_w2_k, nsa_pe_v, nsa_w1_v, nsa_w2_v, w_merge_gate, b_merge_gate, w_branch_out, w_mix_out, norm_xattn, norm_mem, xa_w_q, xa_w_k, xa_w_v, xa_w_o, norm_moe, moe_w_group, moe_b_group, moe_w_expert, moe_b_expert, moe_w_gate, moe_w_up, moe_w_down, norm_final):
    B, S, D = x.shape
    T = B * S
    cos, sin = rope_tables(positions)
    cos_t = jnp.concatenate([cos, cos], axis=-1).reshape(T, HEAD_DIM)
    sin_t = jnp.concatenate([-sin, sin], axis=-1).reshape(T, HEAD_DIM)
    offs = [int(v) for v in np.concatenate([[0], np.cumsum(MIX_SPLITS)])]
    blk = [v // HEAD_DIM for v in offs[:11]]
    n_head_cols = offs[10]
    rope_flags = np.zeros((n_head_cols // HEAD_DIM,), np.int32)
    for k in (1, 4, 6, 8):
        rope_flags[blk[k]:blk[k + 1]] = ROPE_PLAIN
    for k in (0, 3):
        rope_flags[blk[k]:blk[k + 1]] = ROPE_SCALED
    rope_flags = jnp.asarray(rope_flags)
    G, R = NSA_KV_HEADS, NSA_HEADS // NSA_KV_HEADS
    L = w_mix_in.shape[0]
    w_tail = w_mix_in[:, :, offs[11]:].astype(BF16)
    w_ng = jnp.pad(w_mix_in[:, :, offs[10]:offs[11]].reshape(L, D, G, R * 3),
                   ((0, 0), (0, 0), (0, 0), (0, LANES - R * 3))).reshape(L, D, G * LANES)
    M = mem.shape[1]
    for l in range(DEPTH):
        x2 = x.reshape(T, D)
        h = rms_norm_pallas(x2, norm_mix[l], BF16)
        z_head = matmul(h, w_mix_in, w_lead=(l,), n_cols=n_head_cols, out_dtype=BF16,
                        rope=(cos_t, sin_t, rope_flags)).reshape(B, S, n_head_cols)
        z_tail = matmul(h, w_tail, w_lead=(l,)).reshape(B, S, -1)
        z_gate = matmul(h, w_ng, w_lead=(l,), tn=G * LANES).reshape(B, S, G * LANES)
        o_moba = moba_attention_pallas(z_head, z_head, z_head, n_heads=MOBA_HEADS,
                                       q_off=blk[0], k_off=blk[1], v_off=blk[2])
        kc = nsa_compress_pallas(z_head[:, :, offs[4]:offs[5]], nsa_pe_k[l], nsa_w1_k[l], nsa_w2_k[l])
        vc = nsa_compress_pallas(z_head[:, :, offs[5]:offs[6]], nsa_pe_v[l], nsa_w1_v[l], nsa_w2_v[l])
        o_nsa = nsa_attention_pallas(z_head, kc, vc, z_head, z_head, z_head, z_head, z_gate,
                                     q_off=blk[3], ks_off=blk[6], vs_off=blk[7], kw_off=blk[8], vw_off=blk[9])
        o_lru, o_conv = lru_conv_pallas(z_tail, l, lru_conv_w, lru_conv_b, lru_w_a, lru_b_a,
                                        lru_w_i, lru_b_i, lru_lambda, sc_conv_w)
        o_all = jnp.stack([o_moba, o_lru, o_conv, o_nsa]).reshape(N_BRANCH, T, -1)
        merged = merge_branches_pallas(h, o_all, w_merge_gate, b_merge_gate, w_branch_out, l)
        x2 = matmul(merged, w_mix_out, w_lead=(l,), residual=x2)
        mem_n = rms_norm_pallas(mem.reshape(B * M, D), norm_mem[l], BF16)
        xk = matmul(mem_n, xa_w_k, w_lead=(l,), out_dtype=BF16).reshape(B, M, XA_WIDTH)
        xv = matmul(mem_n, xa_w_v, w_lead=(l,), out_dtype=BF16).reshape(B, M, XA_WIDTH)
        x = cross_attention_pallas(x2.reshape(B, S, D), norm_xattn[l], xk, xv,
                                   xa_w_q[l].astype(BF16), xa_w_o[l].astype(BF16))
        x = hierarchical_moe_pallas(x, norm_moe[l], moe_w_group[l], moe_b_group[l],
                                    moe_w_expert[l], moe_b_expert[l], moe_w_gate,
                                    moe_w_up, moe_w_down, l)
    return rms_norm_pallas(x.reshape(T, D), norm_final, F32).reshape(B, S, D)
```

```python
import functools

import jax
import jax.numpy as jnp
import numpy as np
from jax import lax
from jax.experimental import pallas as pl
from jax.experimental.pallas import tpu as pltpu

F32 = jnp.float32
BF16 = jnp.bfloat16

D_MODEL = 4096
DEPTH = 2
HEAD_DIM = 128
ROPE_THETA = 10000.0
NORM_EPS = 1e-6
NEG = -1e30
N_BRANCH = 4
BRANCH_WIDTH = D_MODEL // 4
MOBA_HEADS = BRANCH_WIDTH // HEAD_DIM
MOBA_BLOCK = 256
MOBA_TOPK = 3
MOBA_Q_BLOCK = 64
LRU_WIDTH = BRANCH_WIDTH
LRU_HEADS = LRU_WIDTH // HEAD_DIM
LRU_C = 8.0
SC_WIDTH = BRANCH_WIDTH
NSA_HEADS = BRANCH_WIDTH // HEAD_DIM
NSA_KV_HEADS = NSA_HEADS // 4
NSA_CMP_BLOCK = 32
NSA_CMP_STRIDE = 16
NSA_SEL_BLOCK = 64
NSA_SEL_TOPK = 16
NSA_WINDOW = 512
NSA_FORCE_BONUS = 1e4
Q_BLOCK = 128
XA_HEADS = 4
XA_WIDTH = XA_HEADS * HEAD_DIM
N_GROUPS = 4
EXPERTS_PER_GROUP = 8
N_EXPERTS = N_GROUPS * EXPERTS_PER_GROUP
TOPK_IN_GROUP = 2
EXPERT_FF = D_MODEL // 8
MOBA_W = MOBA_HEADS * HEAD_DIM
NSA_Q_W = NSA_HEADS * HEAD_DIM
NSA_KV_W = NSA_KV_HEADS * HEAD_DIM
NSA_GATE_W = NSA_HEADS * 3
MIX_SPLITS = (MOBA_W, MOBA_W, MOBA_W,
              NSA_Q_W, NSA_KV_W, NSA_KV_W, NSA_KV_W, NSA_KV_W, NSA_KV_W, NSA_KV_W, NSA_GATE_W,
              LRU_WIDTH, LRU_WIDTH,
              SC_WIDTH, SC_WIDTH, SC_WIDTH)

VMEM_LIMIT_BYTES = 56 * 1024 * 1024


ROPE_PLAIN, ROPE_SCALED = 1, 2


def _mm_kernel(*refs, rope, residual):
    if rope:
        flags_ref, refs = refs[0], refs[1:]
    a_ref, w_ref = refs[0], refs[1]
    o_ref = refs[-1]
    acc = jnp.dot(a_ref[...].astype(BF16), w_ref[...].astype(BF16), preferred_element_type=F32)
    if residual:
        acc = acc + refs[2][...]
    if rope:
        cos, sin = refs[2][...], refs[3][...]
        nblk = acc.shape[1] // HEAD_DIM
        for c in range(nblk):
            blk = acc[:, c * HEAD_DIM:(c + 1) * HEAD_DIM]
            flag = flags_ref[pl.program_id(1) * nblk + c]

            @pl.when(flag != 0)
            def _():
                rot = pltpu.roll(blk, HEAD_DIM // 2, 1)
                scl = jnp.where(flag == ROPE_SCALED, HEAD_DIM ** -0.5, 1.0)
                o_ref[:, c * HEAD_DIM:(c + 1) * HEAD_DIM] = ((blk * cos + rot * sin) * scl).astype(o_ref.dtype)

            @pl.when(flag == 0)
            def _():
                o_ref[:, c * HEAD_DIM:(c + 1) * HEAD_DIM] = blk.astype(o_ref.dtype)
    else:
        o_ref[...] = acc.astype(o_ref.dtype)


def matmul(a, w, *, w_lead=(), col0=0, n_cols=None, out_dtype=F32, rope=None, residual=None, tm=1024, tn=512):
    m, k = a.shape
    n = w.shape[-1] if n_cols is None else n_cols
    assert w.shape[-2] == k and not (rope is not None and residual is not None)
    tm = min(tm, m)
    tn = min(tn, n)
    lead = tuple(int(v) for v in w_lead)
    in_specs = [pl.BlockSpec((tm, k), lambda i, j, *_: (i, 0)),
                pl.BlockSpec((None,) * len(lead) + (k, tn), lambda i, j, *_: lead + (0, j + col0))]
    args = [a, w]
    prefetch = []
    if rope is not None:
        cos, sin, flags = rope
        assert n % tn == 0 and flags.shape == (n // HEAD_DIM,)
        in_specs += [pl.BlockSpec((tm, HEAD_DIM), lambda i, j, *_: (i, 0))] * 2
        args += [cos, sin]
        prefetch = [flags]
    if residual is not None:
        in_specs.append(pl.BlockSpec((tm, tn), lambda i, j, *_: (i, j)))
        args.append(residual)
    return pl.pallas_call(
        functools.partial(_mm_kernel, rope=rope is not None, residual=residual is not None),
        grid_spec=pltpu.PrefetchScalarGridSpec(
            num_scalar_prefetch=len(prefetch),
            grid=(pl.cdiv(m, tm), pl.cdiv(n, tn)),
            in_specs=in_specs,
            out_specs=pl.BlockSpec((tm, tn), lambda i, j, *_: (i, j))),
        out_shape=jax.ShapeDtypeStruct((m, n), out_dtype),
        compiler_params=pltpu.CompilerParams(
            dimension_semantics=("parallel", "parallel"),
            vmem_limit_bytes=VMEM_LIMIT_BYTES),
        name="matmul",
    )(*prefetch, *args)


def _rms_kernel(x_ref, g_ref, o_ref):
    x = x_ref[...]
    y = x * lax.rsqrt(jnp.mean(x * x, axis=-1, keepdims=True) + NORM_EPS)
    o_ref[...] = (y * g_ref[...]).astype(o_ref.dtype)


def rms_norm_pallas(x2, gain, out_dtype, *, tm=512):
    m, d = x2.shape
    tm = min(tm, m)
    return pl.pallas_call(
        _rms_kernel,
        grid=(m // tm,),
        in_specs=[pl.BlockSpec((tm, d), lambda i: (i, 0)), pl.BlockSpec((1, d), lambda i: (0, 0))],
        out_specs=pl.BlockSpec((tm, d), lambda i: (i, 0)),
        out_shape=jax.ShapeDtypeStruct((m, d), out_dtype),
        compiler_params=pltpu.CompilerParams(
            dimension_semantics=("parallel",), vmem_limit_bytes=VMEM_LIMIT_BYTES),
        name="rms_norm",
    )(x2, gain.astype(F32)[None, :])


def _merge_kernel(h_ref, o_ref, wg_ref, bg_ref, wu_ref, out_ref, acc_ref):
    n = pl.program_id(2)
    gate = jax.nn.sigmoid(
        jnp.dot(h_ref[...], wg_ref[...].astype(BF16), preferred_element_type=F32) + bg_ref[...])
    term = gate * jnp.dot(o_ref[...], wu_ref[...].astype(BF16), preferred_element_type=F32)

    @pl.when(n == 0)
    def _():
        acc_ref[...] = term

    @pl.when(n != 0)
    def _():
        acc_ref[...] += term

    @pl.when(n == pl.num_programs(2) - 1)
    def _():
        out_ref[...] = acc_ref[...].astype(out_ref.dtype)


def merge_branches_pallas(h, o_all, w_gate, b_gate, w_up, layer, *, tm=1024, tn=512):
    T, D = h.shape
    N, _, W = o_all.shape
    l = int(layer)
    return pl.pallas_call(
        _merge_kernel,
        grid=(T // tm, D // tn, N),
        in_specs=[pl.BlockSpec((tm, D), lambda i, j, n: (i, 0)),
                  pl.BlockSpec((None, tm, W), lambda i, j, n: (n, i, 0)),
                  pl.BlockSpec((None, None, D, tn), lambda i, j, n: (l, n, 0, j)),
                  pl.BlockSpec((None, None, 1, tn), lambda i, j, n: (l, n, 0, j)),
                  pl.BlockSpec((None, None, W, tn), lambda i, j, n: (l, n, 0, j))],
        out_specs=pl.BlockSpec((tm, tn), lambda i, j, n: (i, j)),
        out_shape=jax.ShapeDtypeStruct((T, D), BF16),
        scratch_shapes=[pltpu.VMEM((tm, tn), F32)],
        compiler_params=pltpu.CompilerParams(
            dimension_semantics=("parallel", "parallel", "arbitrary"),
            vmem_limit_bytes=VMEM_LIMIT_BYTES),
        name="merge_branches",
    )(h, o_all, w_gate, b_gate[:, :, None, :], w_up)


LRU_TS = 256
LRU_CW = 256


def _rows_back(x, prev, s, row):
    return jnp.where(row < s, pltpu.roll(prev, s, 0), pltpu.roll(x, s, 0))


def _lane_window(blocks, shift):
    if shift == 0:
        return jnp.concatenate([b[...] for b in blocks], axis=-1)
    back = LANES - shift
    rolled = [pltpu.roll(b[...], back, 1) for b in blocks]
    lane = lax.broadcasted_iota(jnp.int32, rolled[0].shape, 1)
    return jnp.concatenate([jnp.where(lane < back, rolled[j], rolled[j + 1])
                            for j in range(len(blocks) - 1)], axis=-1)


def _lru_conv_kernel(*refs, n_lane_blocks, shift):
    nb = n_lane_blocks
    rx, rg, cb, cc, cx = (_lane_window(refs[k * nb:(k + 1) * nb], shift) for k in range(5))
    (lw_ref, lb_ref, wa_ref, ba_ref, wi_ref, bi_ref, lam_ref, sw_ref,
     olru_ref, oconv_ref, px_ref, py_ref, h_ref) = refs[5 * nb:]

    @pl.when(pl.program_id(2) == 0)
    def _():
        px_ref[...] = jnp.zeros_like(px_ref)
        py_ref[...] = jnp.zeros_like(py_ref)
        h_ref[...] = jnp.zeros_like(h_ref)

    ts, cw = px_ref.shape
    row = lax.broadcasted_iota(jnp.int32, (ts, cw), 0)

    x = rx
    prev = px_ref[...]
    nk = lw_ref.shape[0]
    xc = lb_ref[...] + lw_ref[nk - 1:nk, :] * x
    for s in range(1, nk):
        xc = xc + lw_ref[nk - 1 - s:nk - s, :] * _rows_back(x, prev, s, row)
    px_ref[...] = x
    r_parts, i_parts = [], []
    for hh in range(cw // HEAD_DIM):
        cs = slice(hh * HEAD_DIM, (hh + 1) * HEAD_DIM)
        xh = xc[:, cs].astype(BF16)
        r_parts.append(jnp.dot(xh, wa_ref[hh].astype(BF16), preferred_element_type=F32))
        i_parts.append(jnp.dot(xh, wi_ref[hh].astype(BF16), preferred_element_type=F32))
    r = jax.nn.sigmoid(jnp.concatenate(r_parts, axis=-1) + ba_ref[...])
    gi = jax.nn.sigmoid(jnp.concatenate(i_parts, axis=-1) + bi_ref[...])
    log_a = -LRU_C * r * jax.nn.softplus(-lam_ref[...])
    a = jnp.exp(log_a)
    b = jnp.sqrt(1.0 - a * a) * (gi * xc)
    d = 1
    while d < ts:
        a_back = jnp.where(row < d, 1.0, pltpu.roll(a, d, 0))
        b_back = jnp.where(row < d, 0.0, pltpu.roll(b, d, 0))
        b = a * b_back + b
        a = a * a_back
        d *= 2
    h = b + a * h_ref[...]
    h_ref[...] = h[ts - 1:ts, :]
    olru_ref[...] = (h * jax.nn.gelu(rg)).astype(olru_ref.dtype)

    y = cc * cx
    prev_y = py_ref[...]
    nk = sw_ref.shape[0]
    conv = sw_ref[nk - 1:nk, :] * y
    for s in range(1, nk):
        conv = conv + sw_ref[nk - 1 - s:nk - s, :] * _rows_back(y, prev_y, s, row)
    py_ref[...] = y
    oconv_ref[...] = (cb * conv).astype(oconv_ref.dtype)


def lru_conv_pallas(z, layer, lru_conv_w, lru_conv_b, lru_w_a, lru_b_a, lru_w_i, lru_b_i, lru_lambda,
                    sc_conv_w, *, shift=0):
    B, S, _ = z.shape
    C = lru_conv_b.shape[-1]
    ts, cw = LRU_TS, LRU_CW
    nblk = C // cw
    l = int(layer)
    per_blk = cw // LANES
    nb = per_blk + (1 if shift else 0)
    lane_blk = lambda k, j: pl.BlockSpec(
        (None, ts, LANES), lambda b, c, t: (b, t, (k * nblk + c) * per_blk + j))
    cols = [lane_blk(k, j) for k in range(5) for j in range(nb)]
    par = lambda rows: pl.BlockSpec((None, rows, cw), lambda b, c, t: (l, 0, c))
    hw = pl.BlockSpec((None, cw // HEAD_DIM, HEAD_DIM, HEAD_DIM), lambda b, c, t: (l, c, 0, 0))
    L = lru_conv_b.shape[0]
    flat = lambda p: p.reshape(L, 1, C)
    out = pl.BlockSpec((None, ts, cw), lambda b, c, t: (b, t, c))
    return pl.pallas_call(
        functools.partial(_lru_conv_kernel, n_lane_blocks=nb, shift=shift),
        grid=(B, nblk, S // ts),
        in_specs=cols + [par(lru_conv_w.shape[1]), par(1), hw, par(1), hw, par(1), par(1),
                         par(sc_conv_w.shape[1])],
        out_specs=[out, out],
        out_shape=[jax.ShapeDtypeStruct((B, S, C), BF16)] * 2,
        scratch_shapes=[pltpu.VMEM((ts, cw), F32), pltpu.VMEM((ts, cw), F32), pltpu.VMEM((1, cw), F32)],
        compiler_params=pltpu.CompilerParams(
            dimension_semantics=("parallel", "parallel", "arbitrary"),
            vmem_limit_bytes=VMEM_LIMIT_BYTES),
        name="lru_conv",
    )(*([z] * len(cols)), lru_conv_w, flat(lru_conv_b), lru_w_a, flat(lru_b_a), lru_w_i, flat(lru_b_i),
      flat(lru_lambda), sc_conv_w)


XA_TM = 256


def _xattn_kernel(x_ref, g_ref, wq_ref, k_ref, v_ref, wo_ref, o_ref, *, scale):
    x = x_ref[...]
    xn = (x * lax.rsqrt(jnp.mean(x * x, axis=-1, keepdims=True) + NORM_EPS) * g_ref[...]).astype(BF16)
    q = jnp.dot(xn, wq_ref[...], preferred_element_type=F32).astype(BF16)
    heads = []
    for hh in range(XA_HEADS):
        cs = slice(hh * HEAD_DIM, (hh + 1) * HEAD_DIM)
        s = lax.dot_general(q[:, cs], k_ref[:, cs], (((1,), (1,)), ((), ())),
                            preferred_element_type=F32) * scale
        e = jnp.exp(s - jnp.max(s, axis=-1, keepdims=True))
        p = (e / jnp.sum(e, axis=-1, keepdims=True)).astype(BF16)
        heads.append(jnp.dot(p, v_ref[:, cs], preferred_element_type=F32))
    o = jnp.concatenate(heads, axis=-1).astype(BF16)
    o_ref[...] = x + jnp.dot(o, wo_ref[...], preferred_element_type=F32)


def cross_attention_pallas(x, gain, k, v, w_q, w_o):
    B, S, D = x.shape
    M = k.shape[1]
    tm = XA_TM
    per_b = S // tm
    kv = pl.BlockSpec((None, M, XA_WIDTH), lambda i: (i // per_b, 0, 0))
    return pl.pallas_call(
        functools.partial(_xattn_kernel, scale=HEAD_DIM ** -0.5),
        grid=(B * per_b,),
        in_specs=[pl.BlockSpec((tm, D), lambda i: (i, 0)),
                  pl.BlockSpec((1, D), lambda i: (0, 0)),
                  pl.BlockSpec((D, XA_WIDTH), lambda i: (0, 0)),
                  kv, kv,
                  pl.BlockSpec((XA_WIDTH, D), lambda i: (0, 0))],
        out_specs=pl.BlockSpec((tm, D), lambda i: (i, 0)),
        out_shape=jax.ShapeDtypeStruct((B * S, D), F32),
        compiler_params=pltpu.CompilerParams(
            dimension_semantics=("parallel",), vmem_limit_bytes=VMEM_LIMIT_BYTES),
        name="cross_attention",
    )(x.reshape(B * S, D), gain.astype(F32)[None, :], w_q, k, v, w_o).reshape(B, S, D)


MASK_BIG = 1e30
LANES = 128


def _first_index_topk_mask(work, colf, k):
    sel = jnp.zeros(work.shape, F32)
    for _ in range(k):
        mx = jnp.max(work, axis=-1, keepdims=True)
        idx = jnp.min(jnp.where(work == mx, colf, float(LANES)), axis=-1, keepdims=True)
        pick = colf == idx
        sel = jnp.where(pick, 1.0, sel)
        work = jnp.where(pick, -jnp.inf, work)
    return sel


ATT_TK = 512
MOBA_HEADS_PER_STEP = 4
M_INIT = -1e38
NT_DIMS = (((1,), (1,)), ((), ()))


def _flash_update(q_aug, k_aug, v_aug, m_ref, acc_ref, causal):
    s = lax.dot_general(q_aug, k_aug, NT_DIMS, preferred_element_type=F32)
    if causal is not None:
        q0, k0 = causal
        row = lax.broadcasted_iota(jnp.int32, s.shape, 0)
        col = lax.broadcasted_iota(jnp.int32, s.shape, 1)
        s = jnp.where(k0 + col <= q0 + row, s, NEG)
    m_prev = m_ref[...]
    m_new = jnp.maximum(m_prev, jnp.max(s, axis=-1, keepdims=True))
    alpha = jnp.exp(m_prev - m_new)
    p = jnp.exp(s - m_new).astype(BF16)
    acc_ref[...] = alpha * acc_ref[...] + jnp.dot(p, v_aug, preferred_element_type=F32)
    m_ref[...] = m_new


def _moba_kernel(q_ref, k_ref, v_ref, o_ref, kmean_ref, qa_ref, m_ref, acc_ref):
    i = pl.program_id(2)
    bs = MOBA_BLOCK
    seq = k_ref.shape[0]
    hp = q_ref.shape[1] // HEAD_DIM
    nt = NT_DIMS
    hcol = lambda h: slice(h * HEAD_DIM, (h + 1) * HEAD_DIM)

    @pl.when(i == 0)
    def _():
        row = lax.broadcasted_iota(jnp.int32, (LANES, seq), 0)
        col = lax.broadcasted_iota(jnp.int32, (LANES, seq), 1)
        ind = jnp.where(col // bs == row, 1.0, 0.0).astype(BF16)
        kmean_ref[...] = (jnp.dot(ind, k_ref[...], preferred_element_type=F32)
                          * (1.0 / bs)).astype(BF16)

    coli = lax.broadcasted_iota(jnp.int32, (bs, LANES), 1)
    colf = coli.astype(F32)
    past = coli < i
    for h in range(hp):
        q = q_ref[:, hcol(h)]
        gate = lax.dot_general(q, kmean_ref[:, hcol(h)], nt, preferred_element_type=F32)
        sel = _first_index_topk_mask(jnp.where(past, gate, NEG), colf, MOBA_TOPK)
        notsel = jnp.where(((sel > 0.0) & past) | (coli == i), 0.0, 1.0).astype(BF16)
        qa_ref[h] = jnp.concatenate([q, notsel], axis=-1)

    tk = ATT_TK
    blk_per_tile = tk // bs
    n_tiles = (i + blk_per_tile) // blk_per_tile
    key_blk = lax.broadcasted_iota(jnp.int32, (tk, LANES), 0) // bs
    key_col = lax.broadcasted_iota(jnp.int32, (tk, LANES), 1)
    ones_v = jnp.ones((tk, LANES), BF16)
    m_ref[...] = jnp.full(m_ref.shape, M_INIT, F32)
    acc_ref[...] = jnp.zeros_like(acc_ref)

    def tile(jt, causal):
        off = pl.multiple_of(jt * tk, tk)
        bias = jnp.where(key_col == jt * blk_per_tile + key_blk, -MASK_BIG, 0.0).astype(BF16)
        for h in range(hp):
            k_aug = jnp.concatenate([k_ref[pl.ds(off, tk), hcol(h)], bias], axis=-1)
            v_aug = jnp.concatenate([v_ref[pl.ds(off, tk), hcol(h)], ones_v], axis=-1)
            _flash_update(qa_ref[h], k_aug, v_aug, m_ref.at[h], acc_ref.at[h],
                          (i * bs, off) if causal else None)

    def body(jt, carry):
        tile(jt, False)
        return carry

    lax.fori_loop(0, n_tiles - 1, body, 0)
    tile(n_tiles - 1, True)
    for h in range(hp):
        acc = acc_ref[h]
        o_ref[:, hcol(h)] = (acc[:, :HEAD_DIM] / acc[:, HEAD_DIM:]).astype(o_ref.dtype)


def moba_attention_pallas(zq, zk, zv, *, n_heads, q_off=0, k_off=0, v_off=0):
    B, S, _ = zq.shape
    bs = MOBA_BLOCK
    hp = min(MOBA_HEADS_PER_STEP, n_heads)
    hw = hp * HEAD_DIM
    assert S % ATT_TK == 0 and S // bs <= LANES and n_heads % hp == 0
    assert q_off % hp == 0 and k_off % hp == 0 and v_off % hp == 0
    return pl.pallas_call(
        _moba_kernel,
        grid=(B, n_heads // hp, S // bs),
        in_specs=[pl.BlockSpec((None, bs, hw), lambda b, h, i: (b, i, q_off // hp + h)),
                  pl.BlockSpec((None, S, hw), lambda b, h, i: (b, 0, k_off // hp + h)),
                  pl.BlockSpec((None, S, hw), lambda b, h, i: (b, 0, v_off // hp + h))],
        out_specs=pl.BlockSpec((None, bs, hw), lambda b, h, i: (b, i, h)),
        out_shape=jax.ShapeDtypeStruct((B, S, n_heads * HEAD_DIM), BF16),
        scratch_shapes=[pltpu.VMEM((LANES, hw), BF16), pltpu.VMEM((hp, bs, 2 * HEAD_DIM), BF16),
                        pltpu.VMEM((hp, bs, 1), F32), pltpu.VMEM((hp, bs, 2 * HEAD_DIM), F32)],
        compiler_params=pltpu.CompilerParams(
            dimension_semantics=("parallel", "parallel", "arbitrary"),
            vmem_limit_bytes=VMEM_LIMIT_BYTES),
        name="moba_attention",
    )(zq, zk, zv)


NSA_TQ = 256
NSA_NC_PAD = 256
CMP_ROW = NSA_CMP_STRIDE * HEAD_DIM


def _nsa_compress_kernel(x_ref, pe_ref, w1_ref, w2_ref, o_ref):
    x = x_ref[...].astype(F32)
    top = (x + pe_ref[0:1, :]).astype(BF16)
    bot = (x + pe_ref[1:2, :]).astype(BF16)
    a = jnp.dot(top, w1_ref[0:CMP_ROW, :], preferred_element_type=F32)
    b = jnp.dot(bot, w1_ref[CMP_ROW:2 * CMP_ROW, :], preferred_element_type=F32)
    pre = a + pltpu.roll(b, b.shape[0] - 1, 0)
    hid = jax.nn.gelu(pre)
    o_ref[...] = jnp.dot(hid.astype(BF16), w2_ref[...], preferred_element_type=F32).astype(o_ref.dtype)


def nsa_compress_pallas(x, pe, w1, w2):
    B, S, gw = x.shape
    G = gw // HEAD_DIM
    nrow = S // NSA_CMP_STRIDE
    xr = x.reshape(B, nrow, NSA_CMP_STRIDE, G, HEAD_DIM).transpose(0, 3, 1, 2, 4).reshape(B, G, nrow, CMP_ROW)
    pe2 = pe.astype(F32).reshape(2, CMP_ROW)
    return pl.pallas_call(
        _nsa_compress_kernel,
        grid=(B, G),
        in_specs=[pl.BlockSpec((None, None, nrow, CMP_ROW), lambda b, g: (b, g, 0, 0)),
                  pl.BlockSpec((2, CMP_ROW), lambda b, g: (0, 0)),
                  pl.BlockSpec((2 * CMP_ROW, HEAD_DIM), lambda b, g: (0, 0)),
                  pl.BlockSpec((HEAD_DIM, HEAD_DIM), lambda b, g: (0, 0))],
        out_specs=pl.BlockSpec((None, None, nrow, HEAD_DIM), lambda b, g: (b, g, 0, 0)),
        out_shape=jax.ShapeDtypeStruct((B, G, nrow, HEAD_DIM), BF16),
        compiler_params=pltpu.CompilerParams(
            dimension_semantics=("parallel", "parallel"), vmem_limit_bytes=VMEM_LIMIT_BYTES),
        name="nsa_compress",
    )(xr, pe2, w1.astype(BF16), w2.astype(BF16))


def _nsa_kernel(q_ref, kc_ref, vc_ref, ks_ref, vs_ref, kw_ref, vw_ref, g_ref, wmap_ref, o_ref,
                m_ref, acc_ref, out_ref):
    i = pl.program_id(2)
    tq = NSA_TQ
    R = NSA_HEADS // NSA_KV_HEADS
    nt = (((1,), (1,)), ((), ()))
    row = lax.broadcasted_iota(jnp.int32, (tq, tq), 0)
    col = lax.broadcasted_iota(jnp.int32, (tq, tq), 1)
    t_abs = i * tq + row
    gates = jax.nn.sigmoid(g_ref[...])

    def qh(r):
        return q_ref[:, r * HEAD_DIM:(r + 1) * HEAD_DIM]

    ncp = kc_ref.shape[0]
    cmask = (lax.broadcasted_iota(jnp.int32, (tq, ncp), 1) * NSA_CMP_STRIDE + (NSA_CMP_BLOCK - 1)
             <= i * tq + lax.broadcasted_iota(jnp.int32, (tq, ncp), 0))
    imp = jnp.zeros((tq, LANES), F32)
    vc_aug = jnp.concatenate([vc_ref[...], jnp.ones((ncp, LANES), BF16), wmap_ref[...]], axis=-1)
    for r in range(R):
        s = lax.dot_general(qh(r), kc_ref[...], nt, preferred_element_type=F32)
        s = jnp.where(cmask, s, NEG)
        e = jnp.where(cmask, jnp.exp(s - jnp.max(s, axis=-1, keepdims=True)), 0.0).astype(BF16)
        acc = jnp.dot(e, vc_aug, preferred_element_type=F32)
        l = acc[:, HEAD_DIM:2 * HEAD_DIM]
        inv = jnp.where(l > 0.0, 1.0 / l, 0.0)
        imp = imp + acc[:, 2 * HEAD_DIM:] * inv
        out_ref[r] = gates[:, 3 * r:3 * r + 1] * (acc[:, :HEAD_DIM] * inv)

    ones_w = jnp.ones((tq, LANES), BF16)
    w_tiles = []
    for d in range(NSA_WINDOW // tq + 1):
        off = pl.multiple_of(jnp.maximum(i - d, 0) * tq, tq)
        gone = jnp.where(i >= d, 0, 2 * NSA_WINDOW + tq)
        dist = d * tq + row - col + gone
        w_tiles.append((off, (dist >= 0) & (dist < NSA_WINDOW)))
    for r in range(R):
        ss = []
        for off, mask in w_tiles:
            s = lax.dot_general(qh(r), kw_ref[pl.ds(off, tq), :], nt, preferred_element_type=F32)
            ss.append(jnp.where(mask, s, NEG))
        m = jnp.max(ss[0], axis=-1, keepdims=True)
        for s in ss[1:]:
            m = jnp.maximum(m, jnp.max(s, axis=-1, keepdims=True))
        acc = jnp.zeros((tq, 2 * HEAD_DIM), F32)
        for (off, _), s in zip(w_tiles, ss):
            v_aug = jnp.concatenate([vw_ref[pl.ds(off, tq), :], ones_w], axis=-1)
            acc = acc + jnp.dot(jnp.exp(s - m).astype(BF16), v_aug, preferred_element_type=F32)
        out_ref[r] = out_ref[r] + gates[:, 3 * r + 2:3 * r + 3] * (acc[:, :HEAD_DIM] / acc[:, HEAD_DIM:])

    rowl = lax.broadcasted_iota(jnp.int32, (tq, LANES), 0)
    coll = lax.broadcasted_iota(jnp.int32, (tq, LANES), 1)
    q_blk = (i * tq + rowl) // NSA_SEL_BLOCK
    valid = coll <= q_blk
    forced = (coll == 0) | (coll >= q_blk - 1)
    impm = jnp.where(valid, imp + jnp.where(forced, NSA_FORCE_BONUS, 0.0), NEG)
    sel = _first_index_topk_mask(impm, coll.astype(F32), NSA_SEL_TOPK)
    notsel = jnp.where((sel > 0.0) & valid, 0.0, 1.0).astype(BF16)

    tk = ATT_TK
    blk_per_tile = tk // NSA_SEL_BLOCK
    n_tiles = (i * tq) // tk + 1
    key_blk = lax.broadcasted_iota(jnp.int32, (tk, LANES), 0) // NSA_SEL_BLOCK
    key_col = lax.broadcasted_iota(jnp.int32, (tk, LANES), 1)
    ones_v = jnp.ones((tk, LANES), BF16)
    m_ref[...] = jnp.full(m_ref.shape, M_INIT, F32)
    acc_ref[...] = jnp.zeros_like(acc_ref)

    def sel_tile(jt, causal):
        off = pl.multiple_of(jt * tk, tk)
        bias = jnp.where(key_col == jt * blk_per_tile + key_blk, -MASK_BIG, 0.0).astype(BF16)
        k_aug = jnp.concatenate([ks_ref[pl.ds(off, tk), :], bias], axis=-1)
        v_aug = jnp.concatenate([vs_ref[pl.ds(off, tk), :], ones_v], axis=-1)
        for r in range(R):
            q_aug = jnp.concatenate([qh(r), notsel], axis=-1)
            _flash_update(q_aug, k_aug, v_aug, m_ref.at[r], acc_ref.at[r], (i * tq, off) if causal else None)

    def body(jt, carry):
        sel_tile(jt, False)
        return carry

    lax.fori_loop(0, n_tiles - 1, body, 0)
    sel_tile(n_tiles - 1, True)
    for r in range(R):
        acc = acc_ref[r]
        o_s = acc[:, :HEAD_DIM] / acc[:, HEAD_DIM:]
        o_ref[:, r * HEAD_DIM:(r + 1) * HEAD_DIM] = (
            out_ref[r] + gates[:, 3 * r + 1:3 * r + 2] * o_s).astype(o_ref.dtype)


def nsa_attention_pallas(zq, kc, vc, zks, zvs, zkw, zvw, gate_logits, *, q_off, ks_off, vs_off, kw_off, vw_off):
    B, S, _ = zq.shape
    G, tq = NSA_KV_HEADS, NSA_TQ
    R = NSA_HEADS // G
    assert S % ATT_TK == 0 and ATT_TK % tq == 0 and S // NSA_SEL_BLOCK <= LANES
    nc = (S - NSA_CMP_BLOCK) // NSA_CMP_STRIDE + 1
    ncp = kc.shape[2]
    ns = S // NSA_SEL_BLOCK
    wmap = jnp.pad(cmp_to_sel_weights(nc, ns), ((0, ncp - nc), (0, LANES - ns))).astype(BF16)
    rw = R * HEAD_DIM
    kv_spec = lambda off: pl.BlockSpec((None, S, HEAD_DIM), lambda b, g, i: (b, 0, off + g))
    c_spec = pl.BlockSpec((None, None, ncp, HEAD_DIM), lambda b, g, i: (b, g, 0, 0))
    return pl.pallas_call(
        _nsa_kernel,
        grid=(B, G, S // tq),
        in_specs=[pl.BlockSpec((None, tq, rw), lambda b, g, i: (b, i, q_off // R + g)),
                  c_spec, c_spec, kv_spec(ks_off), kv_spec(vs_off), kv_spec(kw_off), kv_spec(vw_off),
                  pl.BlockSpec((None, tq, LANES), lambda b, g, i: (b, i, g)),
                  pl.BlockSpec((ncp, LANES), lambda b, g, i: (0, 0))],
        out_specs=pl.BlockSpec((None, tq, rw), lambda b, g, i: (b, i, g)),
        out_shape=jax.ShapeDtypeStruct((B, S, NSA_HEADS * HEAD_DIM), BF16),
        scratch_shapes=[pltpu.VMEM((R, tq, 1), F32),
                        pltpu.VMEM((R, tq, 2 * HEAD_DIM), F32), pltpu.VMEM((R, tq, HEAD_DIM), F32)],
        compiler_params=pltpu.CompilerParams(
            dimension_semantics=("parallel", "parallel", "arbitrary"),
            vmem_limit_bytes=VMEM_LIMIT_BYTES),
        name="nsa_attention",
    )(zq, kc, vc, zks, zvs, zkw, zvw, gate_logits, wmap)


MOE_TM = 256
MOE_TN = 512
ROUTER_TM = 256


def _router_kernel(x_ref, g_ref, w_ref, b_ref, t_ref, ids_ref, wts_ref):
    x = x_ref[...]
    t = x * lax.rsqrt(jnp.mean(x * x, axis=-1, keepdims=True) + NORM_EPS) * g_ref[...]
    tb = t.astype(BF16)
    bits = lax.bitcast_convert_type(tb.astype(F32), jnp.uint32)
    tm, half = bits.shape[0], bits.shape[1] // 2
    packed = (bits[:, :half] & jnp.uint32(0xFFFF0000)) | (bits[:, half:] >> 16)
    n = half // LANES
    for c in range(n):
        t_ref[pl.ds(c, tm, stride=n), :] = packed[:, c * LANES:(c + 1) * LANES]
    logits = jnp.dot(tb, w_ref[...], preferred_element_type=F32) + b_ref[...]
    coli = lax.broadcasted_iota(jnp.int32, logits.shape, 1)
    colf = coli.astype(F32)
    first = lambda hit: jnp.min(jnp.where(hit, colf, float(LANES)), axis=-1, keepdims=True)
    is_g = coli < N_GROUPS
    gl = jnp.where(is_g, logits, -jnp.inf)
    gmax = jnp.max(gl, axis=-1, keepdims=True)
    g_sel = first(gl == gmax)
    p_g = 1.0 / jnp.sum(jnp.where(is_g, jnp.exp(logits - gmax), 0.0), axis=-1, keepdims=True)
    lo = N_GROUPS + EXPERTS_PER_GROUP * g_sel
    el = jnp.where((colf >= lo) & (colf < lo + EXPERTS_PER_GROUP), logits, -jnp.inf)
    v1 = jnp.max(el, axis=-1, keepdims=True)
    i1 = first(el == v1)
    el2 = jnp.where(colf == i1, -jnp.inf, el)
    v2 = jnp.max(el2, axis=-1, keepdims=True)
    i2 = first(el2 == v2)
    e = jnp.exp(v2 - v1)
    w1 = p_g / (1.0 + e)
    w2 = p_g * e / (1.0 + e)
    ids_ref[...] = jnp.where(coli == 0, i1 - N_GROUPS, jnp.where(coli == 1, i2 - N_GROUPS, 0.0)).astype(jnp.int32)
    wts_ref[...] = jnp.where(coli == 0, w1, jnp.where(coli == 1, w2, 0.0))


def moe_router_pallas(x2, gain, w_group, b_group, w_expert, b_expert):
    T, D = x2.shape
    tm = ROUTER_TM
    nr = N_GROUPS + N_EXPERTS
    w = jnp.pad(jnp.concatenate([w_group, w_expert], axis=1), ((0, 0), (0, LANES - nr))).astype(BF16)
    b = jnp.pad(jnp.concatenate([b_group, b_expert]), (0, LANES - nr)).astype(F32)[None, :]
    t, ids, wts = pl.pallas_call(
        _router_kernel,
        grid=(T // tm,),
        in_specs=[pl.BlockSpec((tm, D), lambda i: (i, 0)),
                  pl.BlockSpec((1, D), lambda i: (0, 0)),
                  pl.BlockSpec((D, LANES), lambda i: (0, 0)),
                  pl.BlockSpec((1, LANES), lambda i: (0, 0))],
        out_specs=[pl.BlockSpec((tm * (D // 2 // LANES), LANES), lambda i: (i, 0)),
                   pl.BlockSpec((tm, LANES), lambda i: (i, 0)),
                   pl.BlockSpec((tm, LANES), lambda i: (i, 0))],
        out_shape=[jax.ShapeDtypeStruct((T * (D // 2 // LANES), LANES), jnp.uint32),
                   jax.ShapeDtypeStruct((T, LANES), jnp.int32),
                   jax.ShapeDtypeStruct((T, LANES), F32)],
        compiler_params=pltpu.CompilerParams(
            dimension_semantics=("parallel",), vmem_limit_bytes=VMEM_LIMIT_BYTES),
        name="moe_router",
    )(x2, gain.astype(F32)[None, :], w, b)
    return t, ids[:, :TOPK_IN_GROUP], wts[:, :TOPK_IN_GROUP]


def _moe_up_kernel(tile_e_ref, n_used_ref, row_tok_ref, t_hbm, rw_ref, wg_ref, wu_ref, hid_ref,
                   xw_ref, xb_ref, sem):
    k = pl.program_id(0)
    tm = xb_ref.shape[0]
    n = xw_ref.shape[0] // tm
    half = n * LANES
    n_used = n_used_ref[0]

    def row_copy(tok, r):
        return pltpu.make_async_copy(t_hbm.at[pl.ds(pl.multiple_of(tok * n, n), n), :],
                                     xw_ref.at[pl.ds(pl.multiple_of(r * n, n), n), :], sem.at[0])

    def start_gather(tile):
        def issue(r, carry):
            row_copy(row_tok_ref[tile * tm + r], r).start()
            return carry
        lax.fori_loop(0, tm, issue, 0, unroll=8)

    @pl.when(k == 0)
    def _():
        start_gather(0)

    @pl.when(k < n_used)
    def _():
        def wait_row(r, carry):
            row_copy(0, r).wait()
            return carry
        lax.fori_loop(0, tm, wait_row, 0, unroll=8)
        for c in range(n):
            u = xw_ref[pl.ds(c, tm, stride=n), :]
            lo, hi = c * LANES, (c + 1) * LANES
            xb_ref[:, lo:hi] = lax.bitcast_convert_type(u & jnp.uint32(0xFFFF0000), F32).astype(BF16)
            xb_ref[:, half + lo:half + hi] = lax.bitcast_convert_type(u << 16, F32).astype(BF16)

        @pl.when(k + 1 < n_used)
        def _():
            start_gather(k + 1)

        x = xb_ref[...]
        hg = jnp.dot(x, wg_ref[...].astype(BF16), preferred_element_type=F32)
        hu = jnp.dot(x, wu_ref[...].astype(BF16), preferred_element_type=F32)
        hid_ref[...] = (jax.nn.silu(hg) * hu * rw_ref[...]).astype(hid_ref.dtype)

    @pl.when(k >= n_used)
    def _():
        hid_ref[...] = jnp.zeros_like(hid_ref)


def _moe_down_kernel(tile_e_ref, n_used_ref, hid_ref, wd_ref, o_ref):
    k = pl.program_id(0)

    tm = hid_ref.shape[0]
    rows = o_ref.shape[0] // tm

    @pl.when(k < n_used_ref[0])
    def _():
        hid = hid_ref[...]
        for c in range(0, wd_ref.shape[1], MOE_TN):
            y = jnp.dot(hid, wd_ref[:, c:c + MOE_TN].astype(BF16), preferred_element_type=F32)
            for j in range(MOE_TN // LANES):
                o_ref[pl.ds(c // LANES + j, tm, stride=rows), :] = y[:, j * LANES:(j + 1) * LANES]

    @pl.when(k >= n_used_ref[0])
    def _():
        o_ref[...] = jnp.zeros_like(o_ref)


COMBINE_TC = 128


def _moe_combine_kernel(pos_ref, x_ref, y_hbm, o_ref, buf, sem):
    i = pl.program_id(0)
    tc, d = x_ref.shape
    rows = d // LANES
    nk = TOPK_IN_GROUP

    def row_copy(slot, a, p):
        return pltpu.make_async_copy(y_hbm.at[pl.ds(pl.multiple_of(p * rows, rows), rows), :],
                                     buf.at[slot, pl.ds(pl.multiple_of(a * rows, rows), rows), :],
                                     sem.at[slot])

    def start_gather(tile, slot):
        def issue(r, carry):
            for kk in range(nk):
                row_copy(slot, kk * tc + r, pos_ref[(tile * tc + r) * nk + kk]).start()
            return carry
        lax.fori_loop(0, tc, issue, 0, unroll=4)

    slot = i % 2

    @pl.when(i == 0)
    def _():
        start_gather(0, 0)

    def wait_row(a, carry):
        row_copy(slot, a, 0).wait()
        return carry
    lax.fori_loop(0, nk * tc, wait_row, 0, unroll=8)

    @pl.when(i + 1 < pl.num_programs(0))
    def _():
        start_gather(i + 1, 1 - slot)

    for c in range(rows):
        v = x_ref[:, c * LANES:(c + 1) * LANES]
        for kk in range(nk):
            v = v + buf[slot, pl.ds(kk * tc * rows + c, tc, stride=rows), :]
        o_ref[:, c * LANES:(c + 1) * LANES] = v


def moe_combine_pallas(x2, y_rows, pos):
    T, D = x2.shape
    tc = COMBINE_TC
    rows = D // LANES
    return pl.pallas_call(
        _moe_combine_kernel,
        grid_spec=pltpu.PrefetchScalarGridSpec(
            num_scalar_prefetch=1,
            grid=(T // tc,),
            in_specs=[pl.BlockSpec((tc, D), lambda i, *_: (i, 0)),
                      pl.BlockSpec(memory_space=pl.ANY)],
            out_specs=pl.BlockSpec((tc, D), lambda i, *_: (i, 0)),
            scratch_shapes=[pltpu.VMEM((2, TOPK_IN_GROUP * tc * rows, LANES), F32),
                            pltpu.SemaphoreType.DMA((2,))]),
        out_shape=jax.ShapeDtypeStruct((T, D), F32),
        compiler_params=pltpu.CompilerParams(
            dimension_semantics=("arbitrary",), vmem_limit_bytes=VMEM_LIMIT_BYTES),
        name="moe_combine",
    )(pos.reshape(-1), x2, y_rows)


def moe_experts_pallas(t_packed, row_tok, row_w, tile_e, n_used, w_gate, w_up, w_down, layer):
    P = row_tok.shape[0]
    D = w_gate.shape[-2]
    half = D // 2
    F = w_gate.shape[-1]
    tm = MOE_TM
    l = int(layer)
    w_in = pl.BlockSpec((None, None, D, F), lambda k, te, *_: (l, te[k], 0, 0))
    hid = pl.pallas_call(
        _moe_up_kernel,
        grid_spec=pltpu.PrefetchScalarGridSpec(
            num_scalar_prefetch=3,
            grid=(P // tm,),
            in_specs=[pl.BlockSpec(memory_space=pl.ANY),
                      pl.BlockSpec((tm, 1), lambda k, *_: (k, 0)),
                      w_in, w_in],
            out_specs=pl.BlockSpec((tm, F), lambda k, *_: (k, 0)),
            scratch_shapes=[pltpu.VMEM((tm * half // LANES, LANES), jnp.uint32), pltpu.VMEM((tm, D), BF16),
                            pltpu.SemaphoreType.DMA((1,))]),
        out_shape=jax.ShapeDtypeStruct((P, F), BF16),
        compiler_params=pltpu.CompilerParams(
            dimension_semantics=("arbitrary",), vmem_limit_bytes=VMEM_LIMIT_BYTES),
        name="moe_up",
    )(tile_e, n_used, row_tok, t_packed, row_w, w_gate, w_up)
    return pl.pallas_call(
        _moe_down_kernel,
        grid_spec=pltpu.PrefetchScalarGridSpec(
            num_scalar_prefetch=2,
            grid=(P // tm,),
            in_specs=[pl.BlockSpec((tm, F), lambda k, *_: (k, 0)),
                      pl.BlockSpec((None, None, F, D), lambda k, te, *_: (l, te[k], 0, 0))],
            out_specs=pl.BlockSpec((tm * (D // LANES), LANES), lambda k, *_: (k, 0))),
        out_shape=jax.ShapeDtypeStruct((P * (D // LANES), LANES), F32),
        compiler_params=pltpu.CompilerParams(
            dimension_semantics=("parallel",), vmem_limit_bytes=VMEM_LIMIT_BYTES),
        name="moe_down",
    )(tile_e, n_used, hid, w_down)


def moe_dispatch_plan(ids, wts):
    T, K = ids.shape
    n = T * K
    P = n + N_EXPERTS * MOE_TM
    flat = ids.reshape(n)
    onehot = (flat[:, None] == jnp.arange(N_EXPERTS, dtype=jnp.int32)[None, :]).astype(jnp.int32)
    csum = jnp.cumsum(onehot, axis=0)
    counts = csum[-1]
    padded = (counts + MOE_TM - 1) // MOE_TM * MOE_TM
    pend = jnp.cumsum(padded)
    pstart = pend - padded
    pos = jnp.sum(onehot * (csum - 1 + pstart[None, :]), axis=1)
    tok = (jnp.arange(n, dtype=jnp.int32) // K).astype(F32)
    info = jnp.zeros((P, 2), F32).at[pos].set(jnp.stack([tok, wts.reshape(n)], axis=1))
    row_tok = info[:, 0].astype(jnp.int32)
    row_w = info[:, 1:2]
    tile_start = jnp.arange(P // MOE_TM, dtype=jnp.int32) * MOE_TM
    tile_e = jnp.minimum(jnp.sum((pend[None, :] <= tile_start[:, None]).astype(jnp.int32), axis=1),
                         N_EXPERTS - 1).astype(jnp.int32)
    n_used = (pend[-1] // MOE_TM).astype(jnp.int32)[None]
    return pos.reshape(T, K), row_tok, row_w, tile_e, n_used


def hierarchical_moe_pallas(x, gain, w_group, b_group, w_expert, b_expert, w_gate, w_up, w_down, layer):
    B, S, D = x.shape
    x2 = x.reshape(B * S, D)
    t, ids, wts = moe_router_pallas(x2, gain, w_group, b_group, w_expert, b_expert)
    pos, row_tok, row_w, tile_e, n_used = moe_dispatch_plan(ids, wts)
    y = moe_experts_pallas(t, row_tok, row_w, tile_e, n_used, w_gate, w_up, w_down, layer)
    return moe_combine_pallas(x2, y, pos).reshape(B, S, D)


def rms_norm(x, g):
    xf = x.astype(F32)
    y = xf * lax.rsqrt(jnp.mean(xf * xf, axis=-1, keepdims=True) + NORM_EPS)
    return (y * g.astype(F32)).astype(x.dtype)


def rope_tables(positions):
    inv = ROPE_THETA ** (-jnp.arange(0, HEAD_DIM, 2, dtype=F32) / HEAD_DIM)
    ang = positions.astype(F32)[..., None] * inv
    return jnp.cos(ang)[:, :, None, :], jnp.sin(ang)[:, :, None, :]


def rope(x, cos, sin):
    x1, x2 = jnp.split(x.astype(F32), 2, axis=-1)
    return jnp.concatenate([x1 * cos - x2 * sin, x2 * cos + x1 * sin], axis=-1).astype(x.dtype)


def split_heads(t):
    return t.reshape(t.shape[0], t.shape[1], -1, HEAD_DIM)


def split_cols(z):
    return jnp.split(z, [int(v) for v in np.cumsum(MIX_SPLITS)[:-1]], axis=-1)


def causal_depthwise_conv(x, w):
    width, c = w.shape
    xp = jnp.pad(x, ((0, 0), (width - 1, 0), (0, 0)))
    return lax.conv_general_dilated(xp, w[:, None, :].astype(x.dtype), (1,), 'VALID',
                                    dimension_numbers=('NWC', 'WIO', 'NWC'),
                                    feature_group_count=c)


def moba_attention(q, k, v):
    B, S, H, dh = q.shape
    bs = MOBA_BLOCK
    nb = -(-S // bs)
    scale = dh ** -0.5
    qh, kh, vh = (t.transpose(0, 2, 1, 3) for t in (q, k, v))
    pad = nb * bs - S
    kp = jnp.pad(kh, ((0, 0), (0, 0), (0, pad), (0, 0)))
    vp = jnp.pad(vh, ((0, 0), (0, 0), (0, pad), (0, 0)))
    kb = kp.reshape(B, H, nb, bs, dh)
    vb = vp.reshape(B, H, nb, bs, dh)
    k_mean = jnp.mean(kb.astype(F32), axis=3)
    t = jnp.arange(S)
    q_blk = t // bs
    gate = jnp.einsum('bhsd,bhnd->bhsn', qh.astype(F32), k_mean)
    past = jnp.arange(nb)[None, :] < q_blk[:, None]
    gate = jnp.where(past, gate, NEG)
    kk = min(MOBA_TOPK, nb)
    _, sel = lax.top_k(gate, kk)
    sel_ok = sel < q_blk[:, None]
    qb = MOBA_Q_BLOCK
    nqb = S // qb
    q_c = qh.reshape(B, H, nqb, qb, dh).transpose(2, 0, 1, 3, 4)
    sel_c = sel.reshape(B, H, nqb, qb, kk).transpose(2, 0, 1, 3, 4)
    ok_c = sel_ok.reshape(B, H, nqb, qb, kk).transpose(2, 0, 1, 3, 4)
    bi = jnp.arange(B)[:, None, None, None]
    hi = jnp.arange(H)[None, :, None, None]

    def chunk(args):
        qc, selc, okc, ci = args
        tq = ci * qb + jnp.arange(qb)
        own = (ci * qb) // bs * bs
        k_own = lax.dynamic_slice_in_dim(kp, own, bs, axis=2)
        v_own = lax.dynamic_slice_in_dim(vp, own, bs, axis=2)
        own_mask = (own + jnp.arange(bs))[None, :] <= tq[:, None]
        k_s = kb[bi, hi, selc].reshape(B, H, qb, kk * bs, dh)
        v_s = vb[bi, hi, selc].reshape(B, H, qb, kk * bs, dh)
        s_sel = jnp.einsum('bhqd,bhqkd->bhqk', qc, k_s, preferred_element_type=F32) * scale
        s_sel = jnp.where(jnp.repeat(okc, bs, axis=-1), s_sel, NEG)
        s_own = jnp.einsum('bhqd,bhkd->bhqk', qc, k_own, preferred_element_type=F32) * scale
        s_own = jnp.where(own_mask, s_own, NEG)
        p = jax.nn.softmax(jnp.concatenate([s_sel, s_own], axis=-1), axis=-1).astype(v.dtype)
        return (jnp.einsum('bhqk,bhqkd->bhqd', p[..., :kk * bs], v_s)
                + jnp.einsum('bhqk,bhkd->bhqd', p[..., kk * bs:], v_own))

    o = lax.map(chunk, (q_c, sel_c, ok_c, jnp.arange(nqb)))
    return o.transpose(1, 0, 3, 2, 4).reshape(B, S, H * dh)


def compress_tokens(x, pe, w1, w2):
    B, S, G, dh = x.shape
    nc = (S - NSA_CMP_BLOCK) // NSA_CMP_STRIDE + 1
    idx = jnp.arange(nc)[:, None] * NSA_CMP_STRIDE + jnp.arange(NSA_CMP_BLOCK)[None, :]
    blocks = x[:, idx] + pe[:, None, :]
    flat = blocks.transpose(0, 1, 3, 2, 4).reshape(B, nc, G, NSA_CMP_BLOCK * dh)
    return jax.nn.gelu(flat @ w1) @ w2


def cmp_to_sel_weights(nc, ns):
    r = NSA_SEL_BLOCK // NSA_CMP_STRIDE
    m = NSA_CMP_BLOCK // NSA_CMP_STRIDE
    c = jnp.arange(nc)[:, None] - r * jnp.arange(ns)[None, :]
    w = jnp.minimum(jnp.minimum(c + 1, r + m - 1 - c), min(r, m))
    return jnp.clip(w, 0, None).astype(F32)


def nsa_attention(q, k_cmp, v_cmp, k_sel, v_sel, k_win, v_win, gate_logits,
                  pe_k, w1_k, w2_k, pe_v, w1_v, w2_v):
    B, S, H, dh = q.shape
    G = k_sel.shape[2]
    R = H // G
    scale = dh ** -0.5
    t = jnp.arange(S)
    qg = q.reshape(B, S, G, R, dh)
    kc = compress_tokens(k_cmp, pe_k, w1_k, w2_k)
    vc = compress_tokens(v_cmp, pe_v, w1_v, w2_v)
    nc = kc.shape[1]
    s_c = jnp.einsum('bsgrd,bngd->bgrsn', qg, kc, preferred_element_type=F32) * scale
    c_end = jnp.arange(nc) * NSA_CMP_STRIDE + NSA_CMP_BLOCK - 1
    c_mask = c_end[None, :] <= t[:, None]
    p_c = jax.nn.softmax(jnp.where(c_mask, s_c, NEG), axis=-1) * c_mask
    o_cmp = jnp.einsum('bgrsn,bngd->bsgrd', p_c.astype(vc.dtype), vc).reshape(B, S, H, dh)
    ns = S // NSA_SEL_BLOCK
    imp = jnp.einsum('bgrsn,nm->bgsm', p_c, cmp_to_sel_weights(nc, ns))
    q_blk = t // NSA_SEL_BLOCK
    j = jnp.arange(ns)[None, :]
    valid = j <= q_blk[:, None]
    forced = (j == 0) | (j >= q_blk[:, None] - 1)
    imp = jnp.where(valid, imp + jnp.where(forced, NSA_FORCE_BONUS, 0.0), NEG)
    kk = min(NSA_SEL_TOPK, ns)
    _, sel = lax.top_k(imp, kk)
    sel_ok = sel <= q_blk[:, None]
    sb = NSA_SEL_BLOCK
    k_blocks = k_sel.transpose(0, 2, 1, 3).reshape(B, G, ns, sb, dh)
    v_blocks = v_sel.transpose(0, 2, 1, 3).reshape(B, G, ns, sb, dh)
    wpad = ((0, 0), (0, 0), (NSA_WINDOW, 0), (0, 0))
    k_wp = jnp.pad(k_win.transpose(0, 2, 1, 3), wpad)
    v_wp = jnp.pad(v_win.transpose(0, 2, 1, 3), wpad)
    nqb = S // Q_BLOCK
    q_c = qg.transpose(0, 2, 3, 1, 4).reshape(B, G, R, nqb, Q_BLOCK, dh).transpose(3, 0, 1, 2, 4, 5)
    sel_c = sel.reshape(B, G, nqb, Q_BLOCK, kk).transpose(2, 0, 1, 3, 4)
    ok_c = sel_ok.reshape(B, G, nqb, Q_BLOCK, kk).transpose(2, 0, 1, 3, 4)
    bi = jnp.arange(B)[:, None, None, None]
    gi = jnp.arange(G)[None, :, None, None]

    def chunk(args):
        qc, selc, okc, ci = args
        tq = ci * Q_BLOCK + jnp.arange(Q_BLOCK)
        ks = k_blocks[bi, gi, selc].reshape(B, G, Q_BLOCK, kk * sb, dh)
        vs = v_blocks[bi, gi, selc].reshape(B, G, Q_BLOCK, kk * sb, dh)
        kpos = (selc[..., None] * sb + jnp.arange(sb)).reshape(B, G, Q_BLOCK, kk * sb)
        m = (kpos <= tq[:, None]) & jnp.repeat(okc, sb, axis=-1)
        s = jnp.einsum('bgrqd,bgqkd->bgrqk', qc, ks, preferred_element_type=F32) * scale
        p = jax.nn.softmax(jnp.where(m[:, :, None], s, NEG), axis=-1)
        o_s = jnp.einsum('bgrqk,bgqkd->bgrqd', p.astype(vs.dtype), vs)
        kw = lax.dynamic_slice_in_dim(k_wp, ci * Q_BLOCK, Q_BLOCK + NSA_WINDOW, axis=2)
        vw = lax.dynamic_slice_in_dim(v_wp, ci * Q_BLOCK, Q_BLOCK + NSA_WINDOW, axis=2)
        wpos = ci * Q_BLOCK - NSA_WINDOW + jnp.arange(Q_BLOCK + NSA_WINDOW)
        dist = tq[:, None] - wpos[None, :]
        wm = (dist >= 0) & (dist < NSA_WINDOW) & (wpos[None, :] >= 0)
        s_w = jnp.einsum('bgrqd,bgkd->bgrqk', qc, kw, preferred_element_type=F32) * scale
        p_w = jax.nn.softmax(jnp.where(wm, s_w, NEG), axis=-1)
        o_w = jnp.einsum('bgrqk,bgkd->bgrqd', p_w.astype(vw.dtype), vw)
        return o_s, o_w

    o_sel, o_win = lax.map(chunk, (q_c, sel_c, ok_c, jnp.arange(nqb)))
    o_sel = o_sel.transpose(1, 0, 4, 2, 3, 5).reshape(B, S, H, dh)
    o_win = o_win.transpose(1, 0, 4, 2, 3, 5).reshape(B, S, H, dh)
    g = jax.nn.sigmoid(gate_logits.astype(F32)).reshape(B, S, H, 3)
    out = g[..., 0:1] * o_cmp + g[..., 1:2] * o_sel + g[..., 2:3] * o_win
    return out.reshape(B, S, H * dh).astype(q.dtype)


def rg_lru_branch(x_in, gate_in, conv_w, conv_b, w_a, b_a, w_i, b_i, lam):
    B, S, C = x_in.shape
    xc = (causal_depthwise_conv(x_in, conv_w) + conv_b).astype(F32)
    xh = xc.reshape(B, S, LRU_HEADS, C // LRU_HEADS)
    r = jax.nn.sigmoid(jnp.einsum('bshi,hij->bshj', xh, w_a.astype(F32)) + b_a).reshape(B, S, C)
    i = jax.nn.sigmoid(jnp.einsum('bshi,hij->bshj', xh, w_i.astype(F32)) + b_i).reshape(B, S, C)
    log_a = -LRU_C * r * jax.nn.softplus(-lam.astype(F32))
    a = jnp.exp(log_a)
    b = jnp.sqrt(-jnp.expm1(2.0 * log_a)) * (i * xc)

    def combine(c1, c2):
        a1, b1 = c1
        a2, b2 = c2
        return a1 * a2, a2 * b1 + b2

    _, h = lax.associative_scan(combine, (a, b), axis=1)
    return (h * jax.nn.gelu(gate_in.astype(F32))).astype(x_in.dtype)


def short_conv_branch(x_in, b_gate, c_gate, conv_w):
    return b_gate * causal_depthwise_conv(c_gate * x_in, conv_w)


def cross_attention(h, mem_n, w_q, w_k, w_v, w_o):
    B, S, _ = h.shape
    M = mem_n.shape[1]
    q = dense(h, w_q).reshape(B, S, XA_HEADS, HEAD_DIM)
    k = dense(mem_n, w_k).reshape(B, M, XA_HEADS, HEAD_DIM)
    v = dense(mem_n, w_v).reshape(B, M, XA_HEADS, HEAD_DIM)
    s = jnp.einsum('bshd,bmhd->bhsm', q, k, preferred_element_type=F32) * HEAD_DIM ** -0.5
    p = jax.nn.softmax(s, axis=-1).astype(v.dtype)
    o = jnp.einsum('bhsm,bmhd->bshd', p, v).reshape(B, S, XA_WIDTH)
    return dense(o, w_o)


def hierarchical_moe(h, w_group, b_group, w_expert, b_expert, w_gate, w_up, w_down):
    B, S, D = h.shape
    t = h.reshape(B * S, D)
    g_logits = (t @ w_group + b_group).astype(F32)
    g_prob = jax.nn.softmax(g_logits, axis=-1)
    g_sel = jnp.argmax(g_logits, axis=-1)
    e_logits = (t @ w_expert + b_expert).astype(F32).reshape(-1, N_GROUPS, EXPERTS_PER_GROUP)
    e_in = jnp.take_along_axis(e_logits, g_sel[:, None, None], axis=1)[:, 0]
    top_v, top_i = lax.top_k(e_in, TOPK_IN_GROUP)
    p_g = jnp.take_along_axis(g_prob, g_sel[:, None], axis=1)
    w_top = jax.nn.softmax(top_v, axis=-1) * p_g
    eid = g_sel[:, None] * EXPERTS_PER_GROUP + top_i
    combine = jnp.sum(jax.nn.one_hot(eid, N_EXPERTS, dtype=F32) * w_top[..., None], axis=1)
    hid = jax.nn.silu(jnp.einsum('td,edf->tef', t, w_gate)) * jnp.einsum('td,edf->tef', t, w_up)
    out = jnp.einsum('tef,efd->td', hid * combine[:, :, None].astype(hid.dtype), w_down)
    return out.reshape(B, S, D)


def kernel(x, mem, positions, norm_mix, w_mix_in, lru_conv_w, lru_conv_b, lru_w_a, lru_b_a, lru_w_i, lru_b_i, lru_lambda, sc_conv_w, nsa_pe_k, nsa_w1_k, nsa_w2_k, nsa_pe_v, nsa_w1_v, nsa_w2_v, w_merge_gate, b_merge_gate, w_branch_out, w_mix_out, norm_xattn, norm_mem, xa_w_q, xa_w_k, xa_w_v, xa_w_o, norm_moe, moe_w_group, moe_b_group, moe_w_expert, moe_b_expert, moe_w_gate, moe_w_up, moe_w_down, norm_final):
    B, S, D = x.shape
    T = B * S
    cos, sin = rope_tables(positions)
    cos_t = jnp.concatenate([cos, cos], axis=-1).reshape(T, HEAD_DIM)
    sin_t = jnp.concatenate([-sin, sin], axis=-1).reshape(T, HEAD_DIM)
    offs = [int(v) for v in np.concatenate([[0], np.cumsum(MIX_SPLITS)])]
    blk = [v // HEAD_DIM for v in offs[:11]]
    n_head_cols = offs[10]
    rope_flags = np.zeros((n_head_cols // HEAD_DIM,), np.int32)
    for k in (1, 4, 6, 8):
        rope_flags[blk[k]:blk[k + 1]] = ROPE_PLAIN
    for k in (0, 3):
        rope_flags[blk[k]:blk[k + 1]] = ROPE_SCALED
    rope_flags = jnp.asarray(rope_flags)
    G, R = NSA_KV_HEADS, NSA_HEADS // NSA_KV_HEADS
    tail_tn = 512
    assert n_head_cols % tail_tn == 0
    n_tail_cols = offs[16] - offs[10]
    M = mem.shape[1]
    for l in range(DEPTH):
        x2 = x.reshape(T, D)
        h = rms_norm_pallas(x2, norm_mix[l], BF16)
        z_head = matmul(h, w_mix_in, w_lead=(l,), n_cols=n_head_cols, out_dtype=BF16,
                        rope=(cos_t, sin_t, rope_flags)).reshape(B, S, n_head_cols)
        z_tail = matmul(h, w_mix_in, w_lead=(l,), col0=n_head_cols // tail_tn, n_cols=n_tail_cols,
                        tn=tail_tn).reshape(B, S, n_tail_cols)
        z_gate = jnp.pad(z_tail[:, :, :NSA_GATE_W].reshape(B, S, G, R * 3),
                         ((0, 0), (0, 0), (0, 0), (0, LANES - R * 3))).reshape(B, S, G * LANES)
        o_moba = moba_attention_pallas(z_head, z_head, z_head, n_heads=MOBA_HEADS,
                                       q_off=blk[0], k_off=blk[1], v_off=blk[2])
        kc = nsa_compress_pallas(z_head[:, :, offs[4]:offs[5]], nsa_pe_k[l], nsa_w1_k[l], nsa_w2_k[l])
        vc = nsa_compress_pallas(z_head[:, :, offs[5]:offs[6]], nsa_pe_v[l], nsa_w1_v[l], nsa_w2_v[l])
        o_nsa = nsa_attention_pallas(z_head, kc, vc, z_head, z_head, z_head, z_head, z_gate,
                                     q_off=blk[3], ks_off=blk[6], vs_off=blk[7], kw_off=blk[8], vw_off=blk[9])
        o_lru, o_conv = lru_conv_pallas(z_tail, l, lru_conv_w, lru_conv_b, lru_w_a, lru_b_a,
                                        lru_w_i, lru_b_i, lru_lambda, sc_conv_w, shift=NSA_GATE_W)
        o_all = jnp.stack([o_moba, o_lru, o_conv, o_nsa]).reshape(N_BRANCH, T, -1)
        merged = merge_branches_pallas(h, o_all, w_merge_gate, b_merge_gate, w_branch_out, l)
        x2 = matmul(merged, w_mix_out, w_lead=(l,), residual=x2)
        mem_n = rms_norm_pallas(mem.reshape(B * M, D), norm_mem[l], BF16)
        xk = matmul(mem_n, xa_w_k, w_lead=(l,), out_dtype=BF16).reshape(B, M, XA_WIDTH)
        xv = matmul(mem_n, xa_w_v, w_lead=(l,), out_dtype=BF16).reshape(B, M, XA_WIDTH)
        x = cross_attention_pallas(x2.reshape(B, S, D), norm_xattn[l], xk, xv,
                                   xa_w_q[l].astype(BF16), xa_w_o[l].astype(BF16))
        x = hierarchical_moe_pallas(x, norm_moe[l], moe_w_group[l], moe_b_group[l],
                                    moe_w_expert[l], moe_b_expert[l], moe_w_gate,
                                    moe_w_up, moe_w_down, l)
    return rms_norm_pallas(x.reshape(T, D), norm_final, F32).reshape(B, S, D)
```

```python
import functools

import jax
import jax.numpy as jnp
import numpy as np
from jax import lax
from jax.experimental import pallas as pl
from jax.experimental.pallas import tpu as pltpu

F32 = jnp.float32
BF16 = jnp.bfloat16

D_MODEL = 4096
DEPTH = 2
HEAD_DIM = 128
ROPE_THETA = 10000.0
NORM_EPS = 1e-6
NEG = -1e30
N_BRANCH = 4
BRANCH_WIDTH = D_MODEL // 4
MOBA_HEADS = BRANCH_WIDTH // HEAD_DIM
MOBA_BLOCK = 256
MOBA_TOPK = 3
MOBA_Q_BLOCK = 64
LRU_WIDTH = BRANCH_WIDTH
LRU_HEADS = LRU_WIDTH // HEAD_DIM
LRU_C = 8.0
SC_WIDTH = BRANCH_WIDTH
NSA_HEADS = BRANCH_WIDTH // HEAD_DIM
NSA_KV_HEADS = NSA_HEADS // 4
NSA_CMP_BLOCK = 32
NSA_CMP_STRIDE = 16
NSA_SEL_BLOCK = 64
NSA_SEL_TOPK = 16
NSA_WINDOW = 512
NSA_FORCE_BONUS = 1e4
Q_BLOCK = 128
XA_HEADS = 4
XA_WIDTH = XA_HEADS * HEAD_DIM
N_GROUPS = 4
EXPERTS_PER_GROUP = 8
N_EXPERTS = N_GROUPS * EXPERTS_PER_GROUP
TOPK_IN_GROUP = 2
EXPERT_FF = D_MODEL // 8
MOBA_W = MOBA_HEADS * HEAD_DIM
NSA_Q_W = NSA_HEADS * HEAD_DIM
NSA_KV_W = NSA_KV_HEADS * HEAD_DIM
NSA_GATE_W = NSA_HEADS * 3
MIX_SPLITS = (MOBA_W, MOBA_W, MOBA_W,
              NSA_Q_W, NSA_KV_W, NSA_KV_W, NSA_KV_W, NSA_KV_W, NSA_KV_W, NSA_KV_W, NSA_GATE_W,
              LRU_WIDTH, LRU_WIDTH,
              SC_WIDTH, SC_WIDTH, SC_WIDTH)

VMEM_LIMIT_BYTES = 56 * 1024 * 1024


ROPE_PLAIN, ROPE_SCALED = 1, 2


def _mm_kernel(*refs, rope, residual, w_is_nk):
    if rope:
        flags_ref, refs = refs[0], refs[1:]
    a_ref, w_ref = refs[0], refs[1]
    o_ref = refs[-1]
    dims = (((1,), (1,)), ((), ())) if w_is_nk else (((1,), (0,)), ((), ()))
    a = a_ref[...].astype(BF16)
    if rope:
        cos, sin = refs[2][...], refs[3][...]
        tn = o_ref.shape[1]
        step = 2 * HEAD_DIM
        for c0 in range(0, tn, step):
            w_blk = w_ref[c0:c0 + step, :] if w_is_nk else w_ref[:, c0:c0 + step]
            acc = lax.dot_general(a, w_blk.astype(BF16), dims, preferred_element_type=F32)
            for c in range(c0, c0 + step, HEAD_DIM):
                blk = acc[:, c - c0:c - c0 + HEAD_DIM]
                flag = flags_ref[(pl.program_id(1) * tn + c) // HEAD_DIM]
                scl = jnp.where(flag == ROPE_SCALED, HEAD_DIM ** -0.5, 1.0)
                cos_c = jnp.where(flag != 0, cos * scl, 1.0)
                sin_c = jnp.where(flag != 0, sin * scl, 0.0)
                rot = pltpu.roll(blk, HEAD_DIM // 2, 1)
                o_ref[:, c:c + HEAD_DIM] = (blk * cos_c + rot * sin_c).astype(o_ref.dtype)
        return
    acc = lax.dot_general(a, w_ref[...].astype(BF16), dims, preferred_element_type=F32)
    if residual:
        acc = acc + refs[2][...]
    o_ref[...] = acc.astype(o_ref.dtype)


def matmul(a, w, *, w_lead=(), w_is_nk=False, col0=0, row0=None, n_cols=None, out_dtype=F32, rope=None,
           residual=None, tm=1024, tn=512):
    m, k = a.shape
    n = w.shape[-2 if w_is_nk else -1] if n_cols is None else n_cols
    assert w.shape[-1 if w_is_nk else -2] == k and not (rope is not None and residual is not None)
    tm = min(tm, m)
    tn = min(tn, n)
    lead = tuple(int(v) for v in w_lead)
    if w_is_nk and row0 is not None:
        rows_per_slab = w.shape[-2]
        slab = int(np.ravel_multi_index(lead, w.shape[:-2])) if lead else 0
        w = w.reshape(-1, k)
        w_spec = pl.BlockSpec((pl.Element(tn), pl.Element(k)),
                              lambda i, j, *_: (pl.multiple_of(slab * rows_per_slab + row0 + j * tn, 8), 0))
    elif w_is_nk:
        w_spec = pl.BlockSpec((None,) * len(lead) + (tn, k), lambda i, j, *_: lead + (j + col0, 0))
    else:
        w_spec = pl.BlockSpec((None,) * len(lead) + (k, tn), lambda i, j, *_: lead + (0, j + col0))
    in_specs = [pl.BlockSpec((tm, k), lambda i, j, *_: (i, 0)), w_spec]
    args = [a, w]
    prefetch = []
    if rope is not None:
        cos, sin, flags = rope
        assert n % tn == 0 and flags.shape == (n // HEAD_DIM,)
        in_specs += [pl.BlockSpec((tm, HEAD_DIM), lambda i, j, *_: (i, 0))] * 2
        args += [cos, sin]
        prefetch = [flags]
    if residual is not None:
        in_specs.append(pl.BlockSpec((tm, tn), lambda i, j, *_: (i, j)))
        args.append(residual)
    return pl.pallas_call(
        functools.partial(_mm_kernel, rope=rope is not None, residual=residual is not None,
                          w_is_nk=w_is_nk),
        grid_spec=pltpu.PrefetchScalarGridSpec(
            num_scalar_prefetch=len(prefetch),
            grid=(pl.cdiv(m, tm), pl.cdiv(n, tn)),
            in_specs=in_specs,
            out_specs=pl.BlockSpec((tm, tn), lambda i, j, *_: (i, j))),
        out_shape=jax.ShapeDtypeStruct((m, n), out_dtype),
        compiler_params=pltpu.CompilerParams(
            dimension_semantics=("parallel", "parallel"),
            vmem_limit_bytes=VMEM_LIMIT_BYTES),
        name="matmul",
    )(*prefetch, *args)


def _rms_kernel(x_ref, g_ref, o_ref):
    x = x_ref[...]
    y = x * lax.rsqrt(jnp.mean(x * x, axis=-1, keepdims=True) + NORM_EPS)
    o_ref[...] = (y * g_ref[...]).astype(o_ref.dtype)


def rms_norm_pallas(x2, gain, out_dtype, *, tm=512):
    m, d = x2.shape
    tm = min(tm, m)
    return pl.pallas_call(
        _rms_kernel,
        grid=(m // tm,),
        in_specs=[pl.BlockSpec((tm, d), lambda i: (i, 0)), pl.BlockSpec((1, d), lambda i: (0, 0))],
        out_specs=pl.BlockSpec((tm, d), lambda i: (i, 0)),
        out_shape=jax.ShapeDtypeStruct((m, d), out_dtype),
        compiler_params=pltpu.CompilerParams(
            dimension_semantics=("parallel",), vmem_limit_bytes=VMEM_LIMIT_BYTES),
        name="rms_norm",
    )(x2, gain.astype(F32)[None, :])


def _merge_kernel(h_ref, o_ref, wg_ref, bg_ref, wu_ref, out_ref, acc_ref):
    n = pl.program_id(2)
    gate = jax.nn.sigmoid(
        jnp.dot(h_ref[...], wg_ref[...].astype(BF16), preferred_element_type=F32) + bg_ref[...])
    term = gate * jnp.dot(o_ref[...], wu_ref[...].astype(BF16), preferred_element_type=F32)

    @pl.when(n == 0)
    def _():
        acc_ref[...] = term

    @pl.when(n != 0)
    def _():
        acc_ref[...] += term

    @pl.when(n == pl.num_programs(2) - 1)
    def _():
        out_ref[...] = acc_ref[...].astype(out_ref.dtype)


def merge_branches_pallas(h, o_all, w_gate, b_gate, w_up, layer, *, tm=1024, tn=512):
    T, D = h.shape
    N, _, W = o_all.shape
    l = int(layer)
    return pl.pallas_call(
        _merge_kernel,
        grid=(T // tm, D // tn, N),
        in_specs=[pl.BlockSpec((tm, D), lambda i, j, n: (i, 0)),
                  pl.BlockSpec((None, tm, W), lambda i, j, n: (n, i, 0)),
                  pl.BlockSpec((None, None, D, tn), lambda i, j, n: (l, n, 0, j)),
                  pl.BlockSpec((None, None, 1, tn), lambda i, j, n: (l, n, 0, j)),
                  pl.BlockSpec((None, None, W, tn), lambda i, j, n: (l, n, 0, j))],
        out_specs=pl.BlockSpec((tm, tn), lambda i, j, n: (i, j)),
        out_shape=jax.ShapeDtypeStruct((T, D), BF16),
        scratch_shapes=[pltpu.VMEM((tm, tn), F32)],
        compiler_params=pltpu.CompilerParams(
            dimension_semantics=("parallel", "parallel", "arbitrary"),
            vmem_limit_bytes=VMEM_LIMIT_BYTES),
        name="merge_branches",
    )(h, o_all, w_gate, b_gate[:, :, None, :], w_up)


LRU_TS = 256
LRU_CW = 256


def _rows_back(x, prev, s, row):
    return jnp.where(row < s, pltpu.roll(prev, s, 0), pltpu.roll(x, s, 0))


def _lane_window(blocks, shift):
    if shift == 0:
        return jnp.concatenate([b[...] for b in blocks], axis=-1)
    back = LANES - shift
    rolled = [pltpu.roll(b[...], back, 1) for b in blocks]
    lane = lax.broadcasted_iota(jnp.int32, rolled[0].shape, 1)
    return jnp.concatenate([jnp.where(lane < back, rolled[j], rolled[j + 1])
                            for j in range(len(blocks) - 1)], axis=-1)


def _lru_conv_kernel(*refs, n_lane_blocks, shift):
    nb = n_lane_blocks
    rx, rg, cb, cc, cx = (_lane_window(refs[k * nb:(k + 1) * nb], shift) for k in range(5))
    (lw_ref, lb_ref, wa_ref, ba_ref, wi_ref, bi_ref, lam_ref, sw_ref,
     olru_ref, oconv_ref, px_ref, py_ref, h_ref) = refs[5 * nb:]

    @pl.when(pl.program_id(2) == 0)
    def _():
        px_ref[...] = jnp.zeros_like(px_ref)
        py_ref[...] = jnp.zeros_like(py_ref)
        h_ref[...] = jnp.zeros_like(h_ref)

    ts, cw = px_ref.shape
    row = lax.broadcasted_iota(jnp.int32, (ts, cw), 0)

    x = rx
    prev = px_ref[...]
    nk = lw_ref.shape[0]
    xc = lb_ref[...] + lw_ref[nk - 1:nk, :] * x
    for s in range(1, nk):
        xc = xc + lw_ref[nk - 1 - s:nk - s, :] * _rows_back(x, prev, s, row)
    px_ref[...] = x
    r_parts, i_parts = [], []
    for hh in range(cw // HEAD_DIM):
        cs = slice(hh * HEAD_DIM, (hh + 1) * HEAD_DIM)
        xh = xc[:, cs].astype(BF16)
        r_parts.append(jnp.dot(xh, wa_ref[hh].astype(BF16), preferred_element_type=F32))
        i_parts.append(jnp.dot(xh, wi_ref[hh].astype(BF16), preferred_element_type=F32))
    r = jax.nn.sigmoid(jnp.concatenate(r_parts, axis=-1) + ba_ref[...])
    gi = jax.nn.sigmoid(jnp.concatenate(i_parts, axis=-1) + bi_ref[...])
    log_a = -LRU_C * r * jax.nn.softplus(-lam_ref[...])
    a = jnp.exp(log_a)
    b = jnp.sqrt(1.0 - a * a) * (gi * xc)
    d = 1
    while d < ts:
        a_back = jnp.where(row < d, 1.0, pltpu.roll(a, d, 0))
        b_back = jnp.where(row < d, 0.0, pltpu.roll(b, d, 0))
        b = a * b_back + b
        a = a * a_back
        d *= 2
    h = b + a * h_ref[...]
    h_ref[...] = h[ts - 1:ts, :]
    olru_ref[...] = (h * jax.nn.gelu(rg)).astype(olru_ref.dtype)

    y = cc * cx
    prev_y = py_ref[...]
    nk = sw_ref.shape[0]
    conv = sw_ref[nk - 1:nk, :] * y
    for s in range(1, nk):
        conv = conv + sw_ref[nk - 1 - s:nk - s, :] * _rows_back(y, prev_y, s, row)
    py_ref[...] = y
    oconv_ref[...] = (cb * conv).astype(oconv_ref.dtype)


def lru_conv_pallas(z, layer, lru_conv_w, lru_conv_b, lru_w_a, lru_b_a, lru_w_i, lru_b_i, lru_lambda,
                    sc_conv_w, *, shift=0):
    B, S, _ = z.shape
    C = lru_conv_b.shape[-1]
    ts, cw = LRU_TS, LRU_CW
    nblk = C // cw
    l = int(layer)
    per_blk = cw // LANES
    nb = per_blk + (1 if shift else 0)
    lane_blk = lambda k, j: pl.BlockSpec(
        (None, ts, LANES), lambda b, c, t: (b, t, (k * nblk + c) * per_blk + j))
    cols = [lane_blk(k, j) for k in range(5) for j in range(nb)]
    par = lambda rows: pl.BlockSpec((None, rows, cw), lambda b, c, t: (l, 0, c))
    hw = pl.BlockSpec((None, cw // HEAD_DIM, HEAD_DIM, HEAD_DIM), lambda b, c, t: (l, c, 0, 0))
    L = lru_conv_b.shape[0]
    flat = lambda p: p.reshape(L, 1, C)
    out = pl.BlockSpec((None, ts, cw), lambda b, c, t: (b, t, c))
    return pl.pallas_call(
        functools.partial(_lru_conv_kernel, n_lane_blocks=nb, shift=shift),
        grid=(B, nblk, S // ts),
        in_specs=cols + [par(lru_conv_w.shape[1]), par(1), hw, par(1), hw, par(1), par(1),
                         par(sc_conv_w.shape[1])],
        out_specs=[out, out],
        out_shape=[jax.ShapeDtypeStruct((B, S, C), BF16)] * 2,
        scratch_shapes=[pltpu.VMEM((ts, cw), F32), pltpu.VMEM((ts, cw), F32), pltpu.VMEM((1, cw), F32)],
        compiler_params=pltpu.CompilerParams(
            dimension_semantics=("parallel", "parallel", "arbitrary"),
            vmem_limit_bytes=VMEM_LIMIT_BYTES),
        name="lru_conv",
    )(*([z] * len(cols)), lru_conv_w, flat(lru_conv_b), lru_w_a, flat(lru_b_a), lru_w_i, flat(lru_b_i),
      flat(lru_lambda), sc_conv_w)


XA_TM = 256


def _xattn_kernel(x_ref, g_ref, wq_ref, k_ref, v_ref, wo_ref, o_ref, *, scale):
    x = x_ref[...]
    xn = (x * lax.rsqrt(jnp.mean(x * x, axis=-1, keepdims=True) + NORM_EPS) * g_ref[...]).astype(BF16)
    q = jnp.dot(xn, wq_ref[...], preferred_element_type=F32).astype(BF16)
    heads = []
    for hh in range(XA_HEADS):
        cs = slice(hh * HEAD_DIM, (hh + 1) * HEAD_DIM)
        s = lax.dot_general(q[:, cs], k_ref[:, cs], (((1,), (1,)), ((), ())),
                            preferred_element_type=F32) * scale
        e = jnp.exp(s - jnp.max(s, axis=-1, keepdims=True))
        p = (e / jnp.sum(e, axis=-1, keepdims=True)).astype(BF16)
        heads.append(jnp.dot(p, v_ref[:, cs], preferred_element_type=F32))
    o = jnp.concatenate(heads, axis=-1).astype(BF16)
    o_ref[...] = x + jnp.dot(o, wo_ref[...], preferred_element_type=F32)


def cross_attention_pallas(x, gain, k, v, w_q, w_o):
    B, S, D = x.shape
    M = k.shape[1]
    tm = XA_TM
    per_b = S // tm
    kv = pl.BlockSpec((None, M, XA_WIDTH), lambda i: (i // per_b, 0, 0))
    return pl.pallas_call(
        functools.partial(_xattn_kernel, scale=HEAD_DIM ** -0.5),
        grid=(B * per_b,),
        in_specs=[pl.BlockSpec((tm, D), lambda i: (i, 0)),
                  pl.BlockSpec((1, D), lambda i: (0, 0)),
                  pl.BlockSpec((D, XA_WIDTH), lambda i: (0, 0)),
                  kv, kv,
                  pl.BlockSpec((XA_WIDTH, D), lambda i: (0, 0))],
        out_specs=pl.BlockSpec((tm, D), lambda i: (i, 0)),
        out_shape=jax.ShapeDtypeStruct((B * S, D), F32),
        compiler_params=pltpu.CompilerParams(
            dimension_semantics=("parallel",), vmem_limit_bytes=VMEM_LIMIT_BYTES),
        name="cross_attention",
    )(x.reshape(B * S, D), gain.astype(F32)[None, :], w_q, k, v, w_o).reshape(B, S, D)


MASK_BIG = 1e30
LANES = 128


def _first_index_topk_mask(work, colf, k):
    sel = jnp.zeros(work.shape, F32)
    for _ in range(k):
        mx = jnp.max(work, axis=-1, keepdims=True)
        idx = jnp.min(jnp.where(work == mx, colf, float(LANES)), axis=-1, keepdims=True)
        pick = colf == idx
        sel = jnp.where(pick, 1.0, sel)
        work = jnp.where(pick, -jnp.inf, work)
    return sel


ATT_TK = 512
MOBA_HEADS_PER_STEP = 4
M_INIT = -1e38
NT_DIMS = (((1,), (1,)), ((), ()))


def _flash_update(q_aug, k_aug, v_aug, m_ref, acc_ref, causal):
    s = lax.dot_general(q_aug, k_aug, NT_DIMS, preferred_element_type=F32)
    if causal is not None:
        q0, k0 = causal
        row = lax.broadcasted_iota(jnp.int32, s.shape, 0)
        col = lax.broadcasted_iota(jnp.int32, s.shape, 1)
        s = jnp.where(k0 + col <= q0 + row, s, NEG)
    m_prev = m_ref[...]
    m_new = jnp.maximum(m_prev, jnp.max(s, axis=-1, keepdims=True))
    alpha = jnp.exp(m_prev - m_new)
    p = jnp.exp(s - m_new).astype(BF16)
    acc_ref[...] = alpha * acc_ref[...] + jnp.dot(p, v_aug, preferred_element_type=F32)
    m_ref[...] = m_new


def _moba_kernel(q_ref, k_ref, v_ref, o_ref, kmean_ref, qa_ref, m_ref, acc_ref):
    i = pl.program_id(2)
    bs = MOBA_BLOCK
    seq = k_ref.shape[0]
    hp = q_ref.shape[1] // HEAD_DIM
    nt = NT_DIMS
    hcol = lambda h: slice(h * HEAD_DIM, (h + 1) * HEAD_DIM)

    @pl.when(i == 0)
    def _():
        row = lax.broadcasted_iota(jnp.int32, (LANES, seq), 0)
        col = lax.broadcasted_iota(jnp.int32, (LANES, seq), 1)
        ind = jnp.where(col // bs == row, 1.0, 0.0).astype(BF16)
        kmean_ref[...] = (jnp.dot(ind, k_ref[...], preferred_element_type=F32)
                          * (1.0 / bs)).astype(BF16)

    coli = lax.broadcasted_iota(jnp.int32, (bs, LANES), 1)
    colf = coli.astype(F32)
    past = coli < i
    for h in range(hp):
        q = q_ref[:, hcol(h)]
        gate = lax.dot_general(q, kmean_ref[:, hcol(h)], nt, preferred_element_type=F32)
        sel = _first_index_topk_mask(jnp.where(past, gate, NEG), colf, MOBA_TOPK)
        notsel = jnp.where(((sel > 0.0) & past) | (coli == i), 0.0, 1.0).astype(BF16)
        qa_ref[h] = jnp.concatenate([q, notsel], axis=-1)

    tk = ATT_TK
    blk_per_tile = tk // bs
    n_tiles = (i + blk_per_tile) // blk_per_tile
    key_blk = lax.broadcasted_iota(jnp.int32, (tk, LANES), 0) // bs
    key_col = lax.broadcasted_iota(jnp.int32, (tk, LANES), 1)
    ones_v = jnp.ones((tk, LANES), BF16)
    m_ref[...] = jnp.full(m_ref.shape, M_INIT, F32)
    acc_ref[...] = jnp.zeros_like(acc_ref)

    def tile(jt, causal):
        off = pl.multiple_of(jt * tk, tk)
        bias = jnp.where(key_col == jt * blk_per_tile + key_blk, -MASK_BIG, 0.0).astype(BF16)
        for h in range(hp):
            k_aug = jnp.concatenate([k_ref[pl.ds(off, tk), hcol(h)], bias], axis=-1)
            v_aug = jnp.concatenate([v_ref[pl.ds(off, tk), hcol(h)], ones_v], axis=-1)
            _flash_update(qa_ref[h], k_aug, v_aug, m_ref.at[h], acc_ref.at[h],
                          (i * bs, off) if causal else None)

    def body(jt, carry):
        tile(jt, False)
        return carry

    lax.fori_loop(0, n_tiles - 1, body, 0)
    tile(n_tiles - 1, True)
    for h in range(hp):
        acc = acc_ref[h]
        o_ref[:, hcol(h)] = (acc[:, :HEAD_DIM] / acc[:, HEAD_DIM:]).astype(o_ref.dtype)


def moba_attention_pallas(zq, zk, zv, *, n_heads, q_off=0, k_off=0, v_off=0):
    B, S, _ = zq.shape
    bs = MOBA_BLOCK
    hp = min(MOBA_HEADS_PER_STEP, n_heads)
    hw = hp * HEAD_DIM
    assert S % ATT_TK == 0 and S // bs <= LANES and n_heads % hp == 0
    assert q_off % hp == 0 and k_off % hp == 0 and v_off % hp == 0
    return pl.pallas_call(
        _moba_kernel,
        grid=(B, n_heads // hp, S // bs),
        in_specs=[pl.BlockSpec((None, bs, hw), lambda b, h, i: (b, i, q_off // hp + h)),
                  pl.BlockSpec((None, S, hw), lambda b, h, i: (b, 0, k_off // hp + h)),
                  pl.BlockSpec((None, S, hw), lambda b, h, i: (b, 0, v_off // hp + h))],
        out_specs=pl.BlockSpec((None, bs, hw), lambda b, h, i: (b, i, h)),
        out_shape=jax.ShapeDtypeStruct((B, S, n_heads * HEAD_DIM), BF16),
        scratch_shapes=[pltpu.VMEM((LANES, hw), BF16), pltpu.VMEM((hp, bs, 2 * HEAD_DIM), BF16),
                        pltpu.VMEM((hp, bs, 1), F32), pltpu.VMEM((hp, bs, 2 * HEAD_DIM), F32)],
        compiler_params=pltpu.CompilerParams(
            dimension_semantics=("parallel", "parallel", "arbitrary"),
            vmem_limit_bytes=VMEM_LIMIT_BYTES),
        name="moba_attention",
    )(zq, zk, zv)


NSA_TQ = 256
NSA_NC_PAD = 256
CMP_ROW = NSA_CMP_STRIDE * HEAD_DIM


def _nsa_compress_kernel(x_ref, pe_ref, w1_ref, w2_ref, o_ref):
    x = x_ref[...].astype(F32)
    top = (x + pe_ref[0:1, :]).astype(BF16)
    bot = (x + pe_ref[1:2, :]).astype(BF16)
    a = jnp.dot(top, w1_ref[0:CMP_ROW, :], preferred_element_type=F32)
    b = jnp.dot(bot, w1_ref[CMP_ROW:2 * CMP_ROW, :], preferred_element_type=F32)
    pre = a + pltpu.roll(b, b.shape[0] - 1, 0)
    hid = jax.nn.gelu(pre)
    o_ref[...] = jnp.dot(hid.astype(BF16), w2_ref[...], preferred_element_type=F32).astype(o_ref.dtype)


def nsa_compress_pallas(x, pe, w1, w2):
    B, S, gw = x.shape
    G = gw // HEAD_DIM
    nrow = S // NSA_CMP_STRIDE
    xr = x.reshape(B, nrow, NSA_CMP_STRIDE, G, HEAD_DIM).transpose(0, 3, 1, 2, 4).reshape(B, G, nrow, CMP_ROW)
    pe2 = pe.astype(F32).reshape(2, CMP_ROW)
    return pl.pallas_call(
        _nsa_compress_kernel,
        grid=(B, G),
        in_specs=[pl.BlockSpec((None, None, nrow, CMP_ROW), lambda b, g: (b, g, 0, 0)),
                  pl.BlockSpec((2, CMP_ROW), lambda b, g: (0, 0)),
                  pl.BlockSpec((2 * CMP_ROW, HEAD_DIM), lambda b, g: (0, 0)),
                  pl.BlockSpec((HEAD_DIM, HEAD_DIM), lambda b, g: (0, 0))],
        out_specs=pl.BlockSpec((None, None, nrow, HEAD_DIM), lambda b, g: (b, g, 0, 0)),
        out_shape=jax.ShapeDtypeStruct((B, G, nrow, HEAD_DIM), BF16),
        compiler_params=pltpu.CompilerParams(
            dimension_semantics=("parallel", "parallel"), vmem_limit_bytes=VMEM_LIMIT_BYTES),
        name="nsa_compress",
    )(xr, pe2, w1.astype(BF16), w2.astype(BF16))


def _nsa_kernel(q_ref, kc_ref, vc_ref, ks_ref, vs_ref, kw_ref, vw_ref, g_ref, wmap_ref, o_ref,
                m_ref, acc_ref, out_ref):
    i = pl.program_id(2)
    tq = NSA_TQ
    R = NSA_HEADS // NSA_KV_HEADS
    nt = (((1,), (1,)), ((), ()))
    row = lax.broadcasted_iota(jnp.int32, (tq, tq), 0)
    col = lax.broadcasted_iota(jnp.int32, (tq, tq), 1)
    t_abs = i * tq + row
    gates = jax.nn.sigmoid(g_ref[...])

    def qh(r):
        return q_ref[:, r * HEAD_DIM:(r + 1) * HEAD_DIM]

    ncp = kc_ref.shape[0]
    cmask = (lax.broadcasted_iota(jnp.int32, (tq, ncp), 1) * NSA_CMP_STRIDE + (NSA_CMP_BLOCK - 1)
             <= i * tq + lax.broadcasted_iota(jnp.int32, (tq, ncp), 0))
    imp = jnp.zeros((tq, LANES), F32)
    vc_aug = jnp.concatenate([vc_ref[...], jnp.ones((ncp, LANES), BF16), wmap_ref[...]], axis=-1)
    for r in range(R):
        s = lax.dot_general(qh(r), kc_ref[...], nt, preferred_element_type=F32)
        s = jnp.where(cmask, s, NEG)
        e = jnp.where(cmask, jnp.exp(s - jnp.max(s, axis=-1, keepdims=True)), 0.0).astype(BF16)
        acc = jnp.dot(e, vc_aug, preferred_element_type=F32)
        l = acc[:, HEAD_DIM:2 * HEAD_DIM]
        inv = jnp.where(l > 0.0, 1.0 / l, 0.0)
        imp = imp + acc[:, 2 * HEAD_DIM:] * inv
        out_ref[r] = gates[:, 3 * r:3 * r + 1] * (acc[:, :HEAD_DIM] * inv)

    ones_w = jnp.ones((tq, LANES), BF16)
    w_tiles = []
    for d in range(NSA_WINDOW // tq + 1):
        off = pl.multiple_of(jnp.maximum(i - d, 0) * tq, tq)
        gone = jnp.where(i >= d, 0, 2 * NSA_WINDOW + tq)
        dist = d * tq + row - col + gone
        w_tiles.append((off, (dist >= 0) & (dist < NSA_WINDOW)))
    for r in range(R):
        ss = []
        for off, mask in w_tiles:
            s = lax.dot_general(qh(r), kw_ref[pl.ds(off, tq), :], nt, preferred_element_type=F32)
            ss.append(jnp.where(mask, s, NEG))
        m = jnp.max(ss[0], axis=-1, keepdims=True)
        for s in ss[1:]:
            m = jnp.maximum(m, jnp.max(s, axis=-1, keepdims=True))
        acc = jnp.zeros((tq, 2 * HEAD_DIM), F32)
        for (off, _), s in zip(w_tiles, ss):
            v_aug = jnp.concatenate([vw_ref[pl.ds(off, tq), :], ones_w], axis=-1)
            acc = acc + jnp.dot(jnp.exp(s - m).astype(BF16), v_aug, preferred_element_type=F32)
        out_ref[r] = out_ref[r] + gates[:, 3 * r + 2:3 * r + 3] * (acc[:, :HEAD_DIM] / acc[:, HEAD_DIM:])

    rowl = lax.broadcasted_iota(jnp.int32, (tq, LANES), 0)
    coll = lax.broadcasted_iota(jnp.int32, (tq, LANES), 1)
    q_blk = (i * tq + rowl) // NSA_SEL_BLOCK
    valid = coll <= q_blk
    forced = (coll == 0) | (coll >= q_blk - 1)
    impm = jnp.where(valid, imp + jnp.where(forced, NSA_FORCE_BONUS, 0.0), NEG)
    sel = _first_index_topk_mask(impm, coll.astype(F32), NSA_SEL_TOPK)
    notsel = jnp.where((sel > 0.0) & valid, 0.0, 1.0).astype(BF16)

    tk = ATT_TK
    blk_per_tile = tk // NSA_SEL_BLOCK
    n_tiles = (i * tq) // tk + 1
    key_blk = lax.broadcasted_iota(jnp.int32, (tk, LANES), 0) // NSA_SEL_BLOCK
    key_col = lax.broadcasted_iota(jnp.int32, (tk, LANES), 1)
    ones_v = jnp.ones((tk, LANES), BF16)
    m_ref[...] = jnp.full(m_ref.shape, M_INIT, F32)
    acc_ref[...] = jnp.zeros_like(acc_ref)

    def sel_tile(jt, causal):
        off = pl.multiple_of(jt * tk, tk)
        bias = jnp.where(key_col == jt * blk_per_tile + key_blk, -MASK_BIG, 0.0).astype(BF16)
        k_aug = jnp.concatenate([ks_ref[pl.ds(off, tk), :], bias], axis=-1)
        v_aug = jnp.concatenate([vs_ref[pl.ds(off, tk), :], ones_v], axis=-1)
        for r in range(R):
            q_aug = jnp.concatenate([qh(r), notsel], axis=-1)
            _flash_update(q_aug, k_aug, v_aug, m_ref.at[r], acc_ref.at[r], (i * tq, off) if causal else None)

    def body(jt, carry):
        sel_tile(jt, False)
        return carry

    lax.fori_loop(0, n_tiles - 1, body, 0)
    sel_tile(n_tiles - 1, True)
    for r in range(R):
        acc = acc_ref[r]
        o_s = acc[:, :HEAD_DIM] / acc[:, HEAD_DIM:]
        o_ref[:, r * HEAD_DIM:(r + 1) * HEAD_DIM] = (
            out_ref[r] + gates[:, 3 * r + 1:3 * r + 2] * o_s).astype(o_ref.dtype)


def nsa_attention_pallas(zq, kc, vc, zks, zvs, zkw, zvw, gate_logits, *, q_off, ks_off, vs_off, kw_off, vw_off):
    B, S, _ = zq.shape
    G, tq = NSA_KV_HEADS, NSA_TQ
    R = NSA_HEADS // G
    assert S % ATT_TK == 0 and ATT_TK % tq == 0 and S // NSA_SEL_BLOCK <= LANES
    nc = (S - NSA_CMP_BLOCK) // NSA_CMP_STRIDE + 1
    ncp = kc.shape[2]
    ns = S // NSA_SEL_BLOCK
    wmap = jnp.pad(cmp_to_sel_weights(nc, ns), ((0, ncp - nc), (0, LANES - ns))).astype(BF16)
    rw = R * HEAD_DIM
    kv_spec = lambda off: pl.BlockSpec((None, S, HEAD_DIM), lambda b, g, i: (b, 0, off + g))
    c_spec = pl.BlockSpec((None, None, ncp, HEAD_DIM), lambda b, g, i: (b, g, 0, 0))
    return pl.pallas_call(
        _nsa_kernel,
        grid=(B, G, S // tq),
        in_specs=[pl.BlockSpec((None, tq, rw), lambda b, g, i: (b, i, q_off // R + g)),
                  c_spec, c_spec, kv_spec(ks_off), kv_spec(vs_off), kv_spec(kw_off), kv_spec(vw_off),
                  pl.BlockSpec((None, tq, LANES), lambda b, g, i: (b, i, g)),
                  pl.BlockSpec((ncp, LANES), lambda b, g, i: (0, 0))],
        out_specs=pl.BlockSpec((None, tq, rw), lambda b, g, i: (b, i, g)),
        out_shape=jax.ShapeDtypeStruct((B, S, NSA_HEADS * HEAD_DIM), BF16),
        scratch_shapes=[pltpu.VMEM((R, tq, 1), F32),
                        pltpu.VMEM((R, tq, 2 * HEAD_DIM), F32), pltpu.VMEM((R, tq, HEAD_DIM), F32)],
        compiler_params=pltpu.CompilerParams(
            dimension_semantics=("parallel", "parallel", "arbitrary"),
            vmem_limit_bytes=VMEM_LIMIT_BYTES),
        name="nsa_attention",
    )(zq, kc, vc, zks, zvs, zkw, zvw, gate_logits, wmap)


MOE_TM = 256
MOE_TN = 512
ROUTER_TM = 256


def _router_kernel(x_ref, g_ref, w_ref, b_ref, t_ref, ids_ref, wts_ref):
    x = x_ref[...]
    t = x * lax.rsqrt(jnp.mean(x * x, axis=-1, keepdims=True) + NORM_EPS) * g_ref[...]
    tb = t.astype(BF16)
    bits = lax.bitcast_convert_type(tb.astype(F32), jnp.uint32)
    tm, half = bits.shape[0], bits.shape[1] // 2
    packed = (bits[:, :half] & jnp.uint32(0xFFFF0000)) | (bits[:, half:] >> 16)
    n = half // LANES
    for c in range(n):
        t_ref[pl.ds(c, tm, stride=n), :] = packed[:, c * LANES:(c + 1) * LANES]
    logits = jnp.dot(tb, w_ref[...], preferred_element_type=F32) + b_ref[...]
    coli = lax.broadcasted_iota(jnp.int32, logits.shape, 1)
    colf = coli.astype(F32)
    first = lambda hit: jnp.min(jnp.where(hit, colf, float(LANES)), axis=-1, keepdims=True)
    is_g = coli < N_GROUPS
    gl = jnp.where(is_g, logits, -jnp.inf)
    gmax = jnp.max(gl, axis=-1, keepdims=True)
    g_sel = first(gl == gmax)
    p_g = 1.0 / jnp.sum(jnp.where(is_g, jnp.exp(logits - gmax), 0.0), axis=-1, keepdims=True)
    lo = N_GROUPS + EXPERTS_PER_GROUP * g_sel
    el = jnp.where((colf >= lo) & (colf < lo + EXPERTS_PER_GROUP), logits, -jnp.inf)
    v1 = jnp.max(el, axis=-1, keepdims=True)
    i1 = first(el == v1)
    el2 = jnp.where(colf == i1, -jnp.inf, el)
    v2 = jnp.max(el2, axis=-1, keepdims=True)
    i2 = first(el2 == v2)
    e = jnp.exp(v2 - v1)
    w1 = p_g / (1.0 + e)
    w2 = p_g * e / (1.0 + e)
    ids_ref[...] = jnp.where(coli == 0, i1 - N_GROUPS, jnp.where(coli == 1, i2 - N_GROUPS, 0.0)).astype(jnp.int32)
    wts_ref[...] = jnp.where(coli == 0, w1, jnp.where(coli == 1, w2, 0.0))


def moe_router_pallas(x2, gain, w_group, b_group, w_expert, b_expert):
    T, D = x2.shape
    tm = ROUTER_TM
    nr = N_GROUPS + N_EXPERTS
    w = jnp.pad(jnp.concatenate([w_group, w_expert], axis=1), ((0, 0), (0, LANES - nr))).astype(BF16)
    b = jnp.pad(jnp.concatenate([b_group, b_expert]), (0, LANES - nr)).astype(F32)[None, :]
    t, ids, wts = pl.pallas_call(
        _router_kernel,
        grid=(T // tm,),
        in_specs=[pl.BlockSpec((tm, D), lambda i: (i, 0)),
                  pl.BlockSpec((1, D), lambda i: (0, 0)),
                  pl.BlockSpec((D, LANES), lambda i: (0, 0)),
                  pl.BlockSpec((1, LANES), lambda i: (0, 0))],
        out_specs=[pl.BlockSpec((tm * (D // 2 // LANES), LANES), lambda i: (i, 0)),
                   pl.BlockSpec((tm, LANES), lambda i: (i, 0)),
                   pl.BlockSpec((tm, LANES), lambda i: (i, 0))],
        out_shape=[jax.ShapeDtypeStruct((T * (D // 2 // LANES), LANES), jnp.uint32),
                   jax.ShapeDtypeStruct((T, LANES), jnp.int32),
                   jax.ShapeDtypeStruct((T, LANES), F32)],
        compiler_params=pltpu.CompilerParams(
            dimension_semantics=("parallel",), vmem_limit_bytes=VMEM_LIMIT_BYTES),
        name="moe_router",
    )(x2, gain.astype(F32)[None, :], w, b)
    return t, ids[:, :TOPK_IN_GROUP], wts[:, :TOPK_IN_GROUP]


def _moe_up_kernel(tile_e_ref, n_used_ref, row_tok_ref, t_hbm, rw_ref, wg_ref, wu_ref, hid_ref,
                   xw_ref, xb_ref, sem):
    k = pl.program_id(0)
    tm = xb_ref.shape[0]
    n = xw_ref.shape[0] // tm
    half = n * LANES
    n_used = n_used_ref[0]

    def row_copy(tok, r):
        return pltpu.make_async_copy(t_hbm.at[pl.ds(pl.multiple_of(tok * n, n), n), :],
                                     xw_ref.at[pl.ds(pl.multiple_of(r * n, n), n), :], sem.at[0])

    def start_gather(tile):
        def issue(r, carry):
            row_copy(row_tok_ref[tile * tm + r], r).start()
            return carry
        lax.fori_loop(0, tm, issue, 0, unroll=8)

    @pl.when(k == 0)
    def _():
        start_gather(0)

    @pl.when(k < n_used)
    def _():
        def wait_row(r, carry):
            row_copy(0, r).wait()
            return carry
        lax.fori_loop(0, tm, wait_row, 0, unroll=8)
        for c in range(n):
            u = xw_ref[pl.ds(c, tm, stride=n), :]
            lo, hi = c * LANES, (c + 1) * LANES
            xb_ref[:, lo:hi] = lax.bitcast_convert_type(u & jnp.uint32(0xFFFF0000), F32).astype(BF16)
            xb_ref[:, half + lo:half + hi] = lax.bitcast_convert_type(u << 16, F32).astype(BF16)

        @pl.when(k + 1 < n_used)
        def _():
            start_gather(k + 1)

        x = xb_ref[...]
        hg = jnp.dot(x, wg_ref[...].astype(BF16), preferred_element_type=F32)
        hu = jnp.dot(x, wu_ref[...].astype(BF16), preferred_element_type=F32)
        hid_ref[...] = (jax.nn.silu(hg) * hu * rw_ref[...]).astype(hid_ref.dtype)

    @pl.when(k >= n_used)
    def _():
        hid_ref[...] = jnp.zeros_like(hid_ref)


def _moe_down_kernel(tile_e_ref, n_used_ref, hid_ref, wd_ref, o_ref):
    k = pl.program_id(0)

    @pl.when(k < n_used_ref[0])
    def _():
        hid = hid_ref[...]
        for c in range(0, o_ref.shape[1], MOE_TN):
            o_ref[:, c:c + MOE_TN] = jnp.dot(hid, wd_ref[:, c:c + MOE_TN].astype(BF16),
                                             preferred_element_type=F32)

    @pl.when(k >= n_used_ref[0])
    def _():
        o_ref[...] = jnp.zeros_like(o_ref)


COMBINE_TC = 128


def _moe_combine_kernel(pos_ref, x_ref, y_hbm, o_ref, buf, sem):
    i = pl.program_id(0)
    tc = x_ref.shape[0]
    nk = TOPK_IN_GROUP

    def row_copy(slot, a, p):
        return pltpu.make_async_copy(y_hbm.at[pl.ds(p, 1), :], buf.at[slot, pl.ds(a, 1), :], sem.at[slot])

    def start_gather(tile, slot):
        def issue(r, carry):
            for kk in range(nk):
                row_copy(slot, kk * tc + r, pos_ref[(tile * tc + r) * nk + kk]).start()
            return carry
        lax.fori_loop(0, tc, issue, 0, unroll=4)

    slot = i % 2

    @pl.when(i == 0)
    def _():
        start_gather(0, 0)

    def wait_row(a, carry):
        row_copy(slot, a, 0).wait()
        return carry
    lax.fori_loop(0, nk * tc, wait_row, 0, unroll=8)

    @pl.when(i + 1 < pl.num_programs(0))
    def _():
        start_gather(i + 1, 1 - slot)

    v = x_ref[...]
    for kk in range(nk):
        v = v + buf[slot, kk * tc:(kk + 1) * tc, :]
    o_ref[...] = v


def moe_combine_pallas(x2, y_rows, pos):
    T, D = x2.shape
    tc = COMBINE_TC
    return pl.pallas_call(
        _moe_combine_kernel,
        grid_spec=pltpu.PrefetchScalarGridSpec(
            num_scalar_prefetch=1,
            grid=(T // tc,),
            in_specs=[pl.BlockSpec((tc, D), lambda i, *_: (i, 0)),
                      pl.BlockSpec(memory_space=pl.ANY)],
            out_specs=pl.BlockSpec((tc, D), lambda i, *_: (i, 0)),
            scratch_shapes=[pltpu.VMEM((2, TOPK_IN_GROUP * tc, D), F32),
                            pltpu.SemaphoreType.DMA((2,))]),
        out_shape=jax.ShapeDtypeStruct((T, D), F32),
        compiler_params=pltpu.CompilerParams(
            dimension_semantics=("arbitrary",), vmem_limit_bytes=VMEM_LIMIT_BYTES),
        name="moe_combine",
    )(pos.reshape(-1), x2, y_rows)


def moe_experts_pallas(t_packed, row_tok, row_w, tile_e, n_used, w_gate, w_up, w_down, layer):
    P = row_tok.shape[0]
    D = w_gate.shape[-2]
    half = D // 2
    F = w_gate.shape[-1]
    tm = MOE_TM
    l = int(layer)
    w_in = pl.BlockSpec((None, None, D, F), lambda k, te, *_: (l, te[k], 0, 0))
    hid = pl.pallas_call(
        _moe_up_kernel,
        grid_spec=pltpu.PrefetchScalarGridSpec(
            num_scalar_prefetch=3,
            grid=(P // tm,),
            in_specs=[pl.BlockSpec(memory_space=pl.ANY),
                      pl.BlockSpec((tm, 1), lambda k, *_: (k, 0)),
                      w_in, w_in],
            out_specs=pl.BlockSpec((tm, F), lambda k, *_: (k, 0)),
            scratch_shapes=[pltpu.VMEM((tm * half // LANES, LANES), jnp.uint32), pltpu.VMEM((tm, D), BF16),
                            pltpu.SemaphoreType.DMA((1,))]),
        out_shape=jax.ShapeDtypeStruct((P, F), BF16),
        compiler_params=pltpu.CompilerParams(
            dimension_semantics=("arbitrary",), vmem_limit_bytes=VMEM_LIMIT_BYTES),
        name="moe_up",
    )(tile_e, n_used, row_tok, t_packed, row_w, w_gate, w_up)
    return pl.pallas_call(
        _moe_down_kernel,
        grid_spec=pltpu.PrefetchScalarGridSpec(
            num_scalar_prefetch=2,
            grid=(P // tm,),
            in_specs=[pl.BlockSpec((tm, F), lambda k, *_: (k, 0)),
                      pl.BlockSpec((None, None, F, D), lambda k, te, *_: (l, te[k], 0, 0))],
            out_specs=pl.BlockSpec((tm, D), lambda k, *_: (k, 0))),
        out_shape=jax.ShapeDtypeStruct((P, D), F32),
        compiler_params=pltpu.CompilerParams(
            dimension_semantics=("parallel",), vmem_limit_bytes=VMEM_LIMIT_BYTES),
        name="moe_down",
    )(tile_e, n_used, hid, w_down)


def moe_dispatch_plan(ids, wts):
    T, K = ids.shape
    n = T * K
    P = n + N_EXPERTS * MOE_TM
    flat = ids.reshape(n)
    onehot = (flat[:, None] == jnp.arange(N_EXPERTS, dtype=jnp.int32)[None, :]).astype(jnp.int32)
    csum = jnp.cumsum(onehot, axis=0)
    counts = csum[-1]
    padded = (counts + MOE_TM - 1) // MOE_TM * MOE_TM
    pend = jnp.cumsum(padded)
    pstart = pend - padded
    pos = jnp.sum(onehot * (csum - 1 + pstart[None, :]), axis=1)
    tok = (jnp.arange(n, dtype=jnp.int32) // K).astype(F32)
    info = jnp.zeros((P, 2), F32).at[pos].set(jnp.stack([tok, wts.reshape(n)], axis=1))
    row_tok = info[:, 0].astype(jnp.int32)
    row_w = info[:, 1:2]
    tile_start = jnp.arange(P // MOE_TM, dtype=jnp.int32) * MOE_TM
    tile_e = jnp.minimum(jnp.sum((pend[None, :] <= tile_start[:, None]).astype(jnp.int32), axis=1),
                         N_EXPERTS - 1).astype(jnp.int32)
    n_used = (pend[-1] // MOE_TM).astype(jnp.int32)[None]
    return pos.reshape(T, K), row_tok, row_w, tile_e, n_used


def hierarchical_moe_pallas(x, gain, w_group, b_group, w_expert, b_expert, w_gate, w_up, w_down, layer):
    B, S, D = x.shape
    x2 = x.reshape(B * S, D)
    t, ids, wts = moe_router_pallas(x2, gain, w_group, b_group, w_expert, b_expert)
    pos, row_tok, row_w, tile_e, n_used = moe_dispatch_plan(ids, wts)
    y = moe_experts_pallas(t, row_tok, row_w, tile_e, n_used, w_gate, w_up, w_down, layer)
    return moe_combine_pallas(x2, y, pos).reshape(B, S, D)


def rms_norm(x, g):
    xf = x.astype(F32)
    y = xf * lax.rsqrt(jnp.mean(xf * xf, axis=-1, keepdims=True) + NORM_EPS)
    return (y * g.astype(F32)).astype(x.dtype)


def rope_tables(positions):
    inv = ROPE_THETA ** (-jnp.arange(0, HEAD_DIM, 2, dtype=F32) / HEAD_DIM)
    ang = positions.astype(F32)[..., None] * inv
    return jnp.cos(ang)[:, :, None, :], jnp.sin(ang)[:, :, None, :]


def rope(x, cos, sin):
    x1, x2 = jnp.split(x.astype(F32), 2, axis=-1)
    return jnp.concatenate([x1 * cos - x2 * sin, x2 * cos + x1 * sin], axis=-1).astype(x.dtype)


def split_heads(t):
    return t.reshape(t.shape[0], t.shape[1], -1, HEAD_DIM)


def split_cols(z):
    return jnp.split(z, [int(v) for v in np.cumsum(MIX_SPLITS)[:-1]], axis=-1)


def causal_depthwise_conv(x, w):
    width, c = w.shape
    xp = jnp.pad(x, ((0, 0), (width - 1, 0), (0, 0)))
    return lax.conv_general_dilated(xp, w[:, None, :].astype(x.dtype), (1,), 'VALID',
                                    dimension_numbers=('NWC', 'WIO', 'NWC'),
                                    feature_group_count=c)


def moba_attention(q, k, v):
    B, S, H, dh = q.shape
    bs = MOBA_BLOCK
    nb = -(-S // bs)
    scale = dh ** -0.5
    qh, kh, vh = (t.transpose(0, 2, 1, 3) for t in (q, k, v))
    pad = nb * bs - S
    kp = jnp.pad(kh, ((0, 0), (0, 0), (0, pad), (0, 0)))
    vp = jnp.pad(vh, ((0, 0), (0, 0), (0, pad), (0, 0)))
    kb = kp.reshape(B, H, nb, bs, dh)
    vb = vp.reshape(B, H, nb, bs, dh)
    k_mean = jnp.mean(kb.astype(F32), axis=3)
    t = jnp.arange(S)
    q_blk = t // bs
    gate = jnp.einsum('bhsd,bhnd->bhsn', qh.astype(F32), k_mean)
    past = jnp.arange(nb)[None, :] < q_blk[:, None]
    gate = jnp.where(past, gate, NEG)
    kk = min(MOBA_TOPK, nb)
    _, sel = lax.top_k(gate, kk)
    sel_ok = sel < q_blk[:, None]
    qb = MOBA_Q_BLOCK
    nqb = S // qb
    q_c = qh.reshape(B, H, nqb, qb, dh).transpose(2, 0, 1, 3, 4)
    sel_c = sel.reshape(B, H, nqb, qb, kk).transpose(2, 0, 1, 3, 4)
    ok_c = sel_ok.reshape(B, H, nqb, qb, kk).transpose(2, 0, 1, 3, 4)
    bi = jnp.arange(B)[:, None, None, None]
    hi = jnp.arange(H)[None, :, None, None]

    def chunk(args):
        qc, selc, okc, ci = args
        tq = ci * qb + jnp.arange(qb)
        own = (ci * qb) // bs * bs
        k_own = lax.dynamic_slice_in_dim(kp, own, bs, axis=2)
        v_own = lax.dynamic_slice_in_dim(vp, own, bs, axis=2)
        own_mask = (own + jnp.arange(bs))[None, :] <= tq[:, None]
        k_s = kb[bi, hi, selc].reshape(B, H, qb, kk * bs, dh)
        v_s = vb[bi, hi, selc].reshape(B, H, qb, kk * bs, dh)
        s_sel = jnp.einsum('bhqd,bhqkd->bhqk', qc, k_s, preferred_element_type=F32) * scale
        s_sel = jnp.where(jnp.repeat(okc, bs, axis=-1), s_sel, NEG)
        s_own = jnp.einsum('bhqd,bhkd->bhqk', qc, k_own, preferred_element_type=F32) * scale
        s_own = jnp.where(own_mask, s_own, NEG)
        p = jax.nn.softmax(jnp.concatenate([s_sel, s_own], axis=-1), axis=-1).astype(v.dtype)
        return (jnp.einsum('bhqk,bhqkd->bhqd', p[..., :kk * bs], v_s)
                + jnp.einsum('bhqk,bhkd->bhqd', p[..., kk * bs:], v_own))

    o = lax.map(chunk, (q_c, sel_c, ok_c, jnp.arange(nqb)))
    return o.transpose(1, 0, 3, 2, 4).reshape(B, S, H * dh)


def compress_tokens(x, pe, w1, w2):
    B, S, G, dh = x.shape
    nc = (S - NSA_CMP_BLOCK) // NSA_CMP_STRIDE + 1
    idx = jnp.arange(nc)[:, None] * NSA_CMP_STRIDE + jnp.arange(NSA_CMP_BLOCK)[None, :]
    blocks = x[:, idx] + pe[:, None, :]
    flat = blocks.transpose(0, 1, 3, 2, 4).reshape(B, nc, G, NSA_CMP_BLOCK * dh)
    return jax.nn.gelu(flat @ w1) @ w2


def cmp_to_sel_weights(nc, ns):
    r = NSA_SEL_BLOCK // NSA_CMP_STRIDE
    m = NSA_CMP_BLOCK // NSA_CMP_STRIDE
    c = jnp.arange(nc)[:, None] - r * jnp.arange(ns)[None, :]
    w = jnp.minimum(jnp.minimum(c + 1, r + m - 1 - c), min(r, m))
    return jnp.clip(w, 0, None).astype(F32)


def nsa_attention(q, k_cmp, v_cmp, k_sel, v_sel, k_win, v_win, gate_logits,
                  pe_k, w1_k, w2_k, pe_v, w1_v, w2_v):
    B, S, H, dh = q.shape
    G = k_sel.shape[2]
    R = H // G
    scale = dh ** -0.5
    t = jnp.arange(S)
    qg = q.reshape(B, S, G, R, dh)
    kc = compress_tokens(k_cmp, pe_k, w1_k, w2_k)
    vc = compress_tokens(v_cmp, pe_v, w1_v, w2_v)
    nc = kc.shape[1]
    s_c = jnp.einsum('bsgrd,bngd->bgrsn', qg, kc, preferred_element_type=F32) * scale
    c_end = jnp.arange(nc) * NSA_CMP_STRIDE + NSA_CMP_BLOCK - 1
    c_mask = c_end[None, :] <= t[:, None]
    p_c = jax.nn.softmax(jnp.where(c_mask, s_c, NEG), axis=-1) * c_mask
    o_cmp = jnp.einsum('bgrsn,bngd->bsgrd', p_c.astype(vc.dtype), vc).reshape(B, S, H, dh)
    ns = S // NSA_SEL_BLOCK
    imp = jnp.einsum('bgrsn,nm->bgsm', p_c, cmp_to_sel_weights(nc, ns))
    q_blk = t // NSA_SEL_BLOCK
    j = jnp.arange(ns)[None, :]
    valid = j <= q_blk[:, None]
    forced = (j == 0) | (j >= q_blk[:, None] - 1)
    imp = jnp.where(valid, imp + jnp.where(forced, NSA_FORCE_BONUS, 0.0), NEG)
    kk = min(NSA_SEL_TOPK, ns)
    _, sel = lax.top_k(imp, kk)
    sel_ok = sel <= q_blk[:, None]
    sb = NSA_SEL_BLOCK
    k_blocks = k_sel.transpose(0, 2, 1, 3).reshape(B, G, ns, sb, dh)
    v_blocks = v_sel.transpose(0, 2, 1, 3).reshape(B, G, ns, sb, dh)
    wpad = ((0, 0), (0, 0), (NSA_WINDOW, 0), (0, 0))
    k_wp = jnp.pad(k_win.transpose(0, 2, 1, 3), wpad)
    v_wp = jnp.pad(v_win.transpose(0, 2, 1, 3), wpad)
    nqb = S // Q_BLOCK
    q_c = qg.transpose(0, 2, 3, 1, 4).reshape(B, G, R, nqb, Q_BLOCK, dh).transpose(3, 0, 1, 2, 4, 5)
    sel_c = sel.reshape(B, G, nqb, Q_BLOCK, kk).transpose(2, 0, 1, 3, 4)
    ok_c = sel_ok.reshape(B, G, nqb, Q_BLOCK, kk).transpose(2, 0, 1, 3, 4)
    bi = jnp.arange(B)[:, None, None, None]
    gi = jnp.arange(G)[None, :, None, None]

    def chunk(args):
        qc, selc, okc, ci = args
        tq = ci * Q_BLOCK + jnp.arange(Q_BLOCK)
        ks = k_blocks[bi, gi, selc].reshape(B, G, Q_BLOCK, kk * sb, dh)
        vs = v_blocks[bi, gi, selc].reshape(B, G, Q_BLOCK, kk * sb, dh)
        kpos = (selc[..., None] * sb + jnp.arange(sb)).reshape(B, G, Q_BLOCK, kk * sb)
        m = (kpos <= tq[:, None]) & jnp.repeat(okc, sb, axis=-1)
        s = jnp.einsum('bgrqd,bgqkd->bgrqk', qc, ks, preferred_element_type=F32) * scale
        p = jax.nn.softmax(jnp.where(m[:, :, None], s, NEG), axis=-1)
        o_s = jnp.einsum('bgrqk,bgqkd->bgrqd', p.astype(vs.dtype), vs)
        kw = lax.dynamic_slice_in_dim(k_wp, ci * Q_BLOCK, Q_BLOCK + NSA_WINDOW, axis=2)
        vw = lax.dynamic_slice_in_dim(v_wp, ci * Q_BLOCK, Q_BLOCK + NSA_WINDOW, axis=2)
        wpos = ci * Q_BLOCK - NSA_WINDOW + jnp.arange(Q_BLOCK + NSA_WINDOW)
        dist = tq[:, None] - wpos[None, :]
        wm = (dist >= 0) & (dist < NSA_WINDOW) & (wpos[None, :] >= 0)
        s_w = jnp.einsum('bgrqd,bgkd->bgrqk', qc, kw, preferred_element_type=F32) * scale
        p_w = jax.nn.softmax(jnp.where(wm, s_w, NEG), axis=-1)
        o_w = jnp.einsum('bgrqk,bgkd->bgrqd', p_w.astype(vw.dtype), vw)
        return o_s, o_w

    o_sel, o_win = lax.map(chunk, (q_c, sel_c, ok_c, jnp.arange(nqb)))
    o_sel = o_sel.transpose(1, 0, 4, 2, 3, 5).reshape(B, S, H, dh)
    o_win = o_win.transpose(1, 0, 4, 2, 3, 5).reshape(B, S, H, dh)
    g = jax.nn.sigmoid(gate_logits.astype(F32)).reshape(B, S, H, 3)
    out = g[..., 0:1] * o_cmp + g[..., 1:2] * o_sel + g[..., 2:3] * o_win
    return out.reshape(B, S, H * dh).astype(q.dtype)


def rg_lru_branch(x_in, gate_in, conv_w, conv_b, w_a, b_a, w_i, b_i, lam):
    B, S, C = x_in.shape
    xc = (causal_depthwise_conv(x_in, conv_w) + conv_b).astype(F32)
    xh = xc.reshape(B, S, LRU_HEADS, C // LRU_HEADS)
    r = jax.nn.sigmoid(jnp.einsum('bshi,hij->bshj', xh, w_a.astype(F32)) + b_a).reshape(B, S, C)
    i = jax.nn.sigmoid(jnp.einsum('bshi,hij->bshj', xh, w_i.astype(F32)) + b_i).reshape(B, S, C)
    log_a = -LRU_C * r * jax.nn.softplus(-lam.astype(F32))
    a = jnp.exp(log_a)
    b = jnp.sqrt(-jnp.expm1(2.0 * log_a)) * (i * xc)

    def combine(c1, c2):
        a1, b1 = c1
        a2, b2 = c2
        return a1 * a2, a2 * b1 + b2

    _, h = lax.associative_scan(combine, (a, b), axis=1)
    return (h * jax.nn.gelu(gate_in.astype(F32))).astype(x_in.dtype)


def short_conv_branch(x_in, b_gate, c_gate, conv_w):
    return b_gate * causal_depthwise_conv(c_gate * x_in, conv_w)


def cross_attention(h, mem_n, w_q, w_k, w_v, w_o):
    B, S, _ = h.shape
    M = mem_n.shape[1]
    q = dense(h, w_q).reshape(B, S, XA_HEADS, HEAD_DIM)
    k = dense(mem_n, w_k).reshape(B, M, XA_HEADS, HEAD_DIM)
    v = dense(mem_n, w_v).reshape(B, M, XA_HEADS, HEAD_DIM)
    s = jnp.einsum('bshd,bmhd->bhsm', q, k, preferred_element_type=F32) * HEAD_DIM ** -0.5
    p = jax.nn.softmax(s, axis=-1).astype(v.dtype)
    o = jnp.einsum('bhsm,bmhd->bshd', p, v).reshape(B, S, XA_WIDTH)
    return dense(o, w_o)


def hierarchical_moe(h, w_group, b_group, w_expert, b_expert, w_gate, w_up, w_down):
    B, S, D = h.shape
    t = h.reshape(B * S, D)
    g_logits = (t @ w_group + b_group).astype(F32)
    g_prob = jax.nn.softmax(g_logits, axis=-1)
    g_sel = jnp.argmax(g_logits, axis=-1)
    e_logits = (t @ w_expert + b_expert).astype(F32).reshape(-1, N_GROUPS, EXPERTS_PER_GROUP)
    e_in = jnp.take_along_axis(e_logits, g_sel[:, None, None], axis=1)[:, 0]
    top_v, top_i = lax.top_k(e_in, TOPK_IN_GROUP)
    p_g = jnp.take_along_axis(g_prob, g_sel[:, None], axis=1)
    w_top = jax.nn.softmax(top_v, axis=-1) * p_g
    eid = g_sel[:, None] * EXPERTS_PER_GROUP + top_i
    combine = jnp.sum(jax.nn.one_hot(eid, N_EXPERTS, dtype=F32) * w_top[..., None], axis=1)
    hid = jax.nn.silu(jnp.einsum('td,edf->tef', t, w_gate)) * jnp.einsum('td,edf->tef', t, w_up)
    out = jnp.einsum('tef,efd->td', hid * combine[:, :, None].astype(hid.dtype), w_down)
    return out.reshape(B, S, D)


def kernel(x, mem, positions, norm_mix, w_mix_in, lru_conv_w, lru_conv_b, lru_w_a, lru_b_a, lru_w_i, lru_b_i, lru_lambda, sc_conv_w, nsa_pe_k, nsa_w1_k, nsa_w2_k, nsa_pe_v, nsa_w1_v, nsa_w2_v, w_merge_gate, b_merge_gate, w_branch_out, w_mix_out, norm_xattn, norm_mem, xa_w_q, xa_w_k, xa_w_v, xa_w_o, norm_moe, moe_w_group, moe_b_group, moe_w_expert, moe_b_expert, moe_w_gate, moe_w_up, moe_w_down, norm_final):
    B, S, D = x.shape
    T = B * S
    cos, sin = rope_tables(positions)
    cos_t = jnp.concatenate([cos, cos], axis=-1).reshape(T, HEAD_DIM)
    sin_t = jnp.concatenate([-sin, sin], axis=-1).reshape(T, HEAD_DIM)
    offs = [int(v) for v in np.concatenate([[0], np.cumsum(MIX_SPLITS)])]
    blk = [v // HEAD_DIM for v in offs[:11]]
    n_head_cols = offs[10]
    rope_flags = np.zeros((n_head_cols // HEAD_DIM,), np.int32)
    for k in (1, 4, 6, 8):
        rope_flags[blk[k]:blk[k + 1]] = ROPE_PLAIN
    for k in (0, 3):
        rope_flags[blk[k]:blk[k + 1]] = ROPE_SCALED
    rope_flags = jnp.asarray(rope_flags)
    G, R = NSA_KV_HEADS, NSA_HEADS // NSA_KV_HEADS
    w_in_nk = jnp.swapaxes(w_mix_in, 1, 2)
    assert offs[10] % 8 == 0 and offs[11] % 8 == 0
    M = mem.shape[1]
    for l in range(DEPTH):
        x2 = x.reshape(T, D)
        h = rms_norm_pallas(x2, norm_mix[l], BF16)
        z_head = matmul(h, w_in_nk, w_lead=(l,), w_is_nk=True, n_cols=n_head_cols, out_dtype=BF16,
                        rope=(cos_t, sin_t, rope_flags)).reshape(B, S, n_head_cols)
        z_tail = matmul(h, w_in_nk, w_lead=(l,), w_is_nk=True, row0=offs[11],
                        n_cols=offs[16] - offs[11]).reshape(B, S, -1)
        z_g = matmul(h, w_in_nk, w_lead=(l,), w_is_nk=True, row0=offs[10], n_cols=LANES)
        z_gate = jnp.pad(z_g[:, :NSA_GATE_W].reshape(B, S, G, R * 3),
                         ((0, 0), (0, 0), (0, 0), (0, LANES - R * 3))).reshape(B, S, G * LANES)
        o_moba = moba_attention_pallas(z_head, z_head, z_head, n_heads=MOBA_HEADS,
                                       q_off=blk[0], k_off=blk[1], v_off=blk[2])
        kc = nsa_compress_pallas(z_head[:, :, offs[4]:offs[5]], nsa_pe_k[l], nsa_w1_k[l], nsa_w2_k[l])
        vc = nsa_compress_pallas(z_head[:, :, offs[5]:offs[6]], nsa_pe_v[l], nsa_w1_v[l], nsa_w2_v[l])
        o_nsa = nsa_attention_pallas(z_head, kc, vc, z_head, z_head, z_head, z_head, z_gate,
                                     q_off=blk[3], ks_off=blk[6], vs_off=blk[7], kw_off=blk[8], vw_off=blk[9])
        o_lru, o_conv = lru_conv_pallas(z_tail, l, lru_conv_w, lru_conv_b, lru_w_a, lru_b_a,
                                        lru_w_i, lru_b_i, lru_lambda, sc_conv_w)
        o_all = jnp.stack([o_moba, o_lru, o_conv, o_nsa]).reshape(N_BRANCH, T, -1)
        merged = merge_branches_pallas(h, o_all, w_merge_gate, b_merge_gate, w_branch_out, l)
        x2 = matmul(merged, w_mix_out, w_lead=(l,), residual=x2)
        mem_n = rms_norm_pallas(mem.reshape(B * M, D), norm_mem[l], BF16)
        xk = matmul(mem_n, xa_w_k, w_lead=(l,), out_dtype=BF16).reshape(B, M, XA_WIDTH)
        xv = matmul(mem_n, xa_w_v, w_lead=(l,), out_dtype=BF16).reshape(B, M, XA_WIDTH)
        x = cross_attention_pallas(x2.reshape(B, S, D), norm_xattn[l], xk, xv,
                                   xa_w_q[l].astype(BF16), xa_w_o[l].astype(BF16))
        x = hierarchical_moe_pallas(x, norm_moe[l], moe_w_group[l], moe_b_group[l],
                                    moe_w_expert[l], moe_b_expert[l], moe_w_gate,
                                    moe_w_up, moe_w_down, l)
    return rms_norm_pallas(x.reshape(T, D), norm_final, F32).reshape(B, S, D)
```

```python
import functools

import jax
import jax.numpy as jnp
import numpy as np
from jax import lax
from jax.experimental import pallas as pl
from jax.experimental.pallas import tpu as pltpu

F32 = jnp.float32
BF16 = jnp.bfloat16

D_MODEL = 4096
DEPTH = 2
HEAD_DIM = 128
ROPE_THETA = 10000.0
NORM_EPS = 1e-6
NEG = -1e30
N_BRANCH = 4
BRANCH_WIDTH = D_MODEL // 4
MOBA_HEADS = BRANCH_WIDTH // HEAD_DIM
MOBA_BLOCK = 256
MOBA_TOPK = 3
MOBA_Q_BLOCK = 64
LRU_WIDTH = BRANCH_WIDTH
LRU_HEADS = LRU_WIDTH // HEAD_DIM
LRU_C = 8.0
SC_WIDTH = BRANCH_WIDTH
NSA_HEADS = BRANCH_WIDTH // HEAD_DIM
NSA_KV_HEADS = NSA_HEADS // 4
NSA_CMP_BLOCK = 32
NSA_CMP_STRIDE = 16
NSA_SEL_BLOCK = 64
NSA_SEL_TOPK = 16
NSA_WINDOW = 512
NSA_FORCE_BONUS = 1e4
Q_BLOCK = 128
XA_HEADS = 4
XA_WIDTH = XA_HEADS * HEAD_DIM
N_GROUPS = 4
EXPERTS_PER_GROUP = 8
N_EXPERTS = N_GROUPS * EXPERTS_PER_GROUP
TOPK_IN_GROUP = 2
EXPERT_FF = D_MODEL // 8
MOBA_W = MOBA_HEADS * HEAD_DIM
NSA_Q_W = NSA_HEADS * HEAD_DIM
NSA_KV_W = NSA_KV_HEADS * HEAD_DIM
NSA_GATE_W = NSA_HEADS * 3
MIX_SPLITS = (MOBA_W, MOBA_W, MOBA_W,
              NSA_Q_W, NSA_KV_W, NSA_KV_W, NSA_KV_W, NSA_KV_W, NSA_KV_W, NSA_KV_W, NSA_GATE_W,
              LRU_WIDTH, LRU_WIDTH,
              SC_WIDTH, SC_WIDTH, SC_WIDTH)

VMEM_LIMIT_BYTES = 56 * 1024 * 1024


ROPE_PLAIN, ROPE_SCALED = 1, 2


def _mm_kernel(*refs, rope, residual, w_is_nk):
    if rope:
        flags_ref, refs = refs[0], refs[1:]
    a_ref, w_ref = refs[0], refs[1]
    o_ref = refs[-1]
    dims = (((1,), (1,)), ((), ())) if w_is_nk else (((1,), (0,)), ((), ()))
    a = a_ref[...].astype(BF16)
    if rope:
        cos, sin = refs[2][...], refs[3][...]
        tn = o_ref.shape[1]
        step = 2 * HEAD_DIM
        for c0 in range(0, tn, step):
            w_blk = w_ref[c0:c0 + step, :] if w_is_nk else w_ref[:, c0:c0 + step]
            acc = lax.dot_general(a, w_blk.astype(BF16), dims, preferred_element_type=F32)
            for c in range(c0, c0 + step, HEAD_DIM):
                blk = acc[:, c - c0:c - c0 + HEAD_DIM]
                flag = flags_ref[(pl.program_id(1) * tn + c) // HEAD_DIM]
                scl = jnp.where(flag == ROPE_SCALED, HEAD_DIM ** -0.5, 1.0)
                cos_c = jnp.where(flag != 0, cos * scl, 1.0)
                sin_c = jnp.where(flag != 0, sin * scl, 0.0)
                rot = pltpu.roll(blk, HEAD_DIM // 2, 1)
                o_ref[:, c:c + HEAD_DIM] = (blk * cos_c + rot * sin_c).astype(o_ref.dtype)
        return
    acc = lax.dot_general(a, w_ref[...].astype(BF16), dims, preferred_element_type=F32)
    if residual:
        acc = acc + refs[2][...]
    o_ref[...] = acc.astype(o_ref.dtype)


def matmul(a, w, *, w_lead=(), w_is_nk=False, col0=0, row0=None, n_cols=None, out_dtype=F32, rope=None,
           residual=None, tm=1024, tn=512):
    m, k = a.shape
    n = w.shape[-2 if w_is_nk else -1] if n_cols is None else n_cols
    assert w.shape[-1 if w_is_nk else -2] == k and not (rope is not None and residual is not None)
    tm = min(tm, m)
    tn = min(tn, n)
    lead = tuple(int(v) for v in w_lead)
    if w_is_nk and row0 is not None:
        rows_per_slab = w.shape[-2]
        slab = int(np.ravel_multi_index(lead, w.shape[:-2])) if lead else 0
        w = w.reshape(-1, k)
        w_spec = pl.BlockSpec((pl.Element(tn), pl.Element(k)),
                              lambda i, j, *_: (pl.multiple_of(slab * rows_per_slab + row0 + j * tn, 8), 0))
    elif w_is_nk:
        w_spec = pl.BlockSpec((None,) * len(lead) + (tn, k), lambda i, j, *_: lead + (j + col0, 0))
    else:
        w_spec = pl.BlockSpec((None,) * len(lead) + (k, tn), lambda i, j, *_: lead + (0, j + col0))
    in_specs = [pl.BlockSpec((tm, k), lambda i, j, *_: (i, 0)), w_spec]
    args = [a, w]
    prefetch = []
    if rope is not None:
        cos, sin, flags = rope
        assert n % tn == 0 and flags.shape == (n // HEAD_DIM,)
        in_specs += [pl.BlockSpec((tm, HEAD_DIM), lambda i, j, *_: (i, 0))] * 2
        args += [cos, sin]
        prefetch = [flags]
    if residual is not None:
        in_specs.append(pl.BlockSpec((tm, tn), lambda i, j, *_: (i, j)))
        args.append(residual)
    return pl.pallas_call(
        functools.partial(_mm_kernel, rope=rope is not None, residual=residual is not None,
                          w_is_nk=w_is_nk),
        grid_spec=pltpu.PrefetchScalarGridSpec(
            num_scalar_prefetch=len(prefetch),
            grid=(pl.cdiv(m, tm), pl.cdiv(n, tn)),
            in_specs=in_specs,
            out_specs=pl.BlockSpec((tm, tn), lambda i, j, *_: (i, j))),
        out_shape=jax.ShapeDtypeStruct((m, n), out_dtype),
        compiler_params=pltpu.CompilerParams(
            dimension_semantics=("parallel", "parallel"),
            vmem_limit_bytes=VMEM_LIMIT_BYTES),
        name="matmul",
    )(*prefetch, *args)


def _rms_kernel(x_ref, g_ref, o_ref):
    x = x_ref[...]
    y = x * lax.rsqrt(jnp.mean(x * x, axis=-1, keepdims=True) + NORM_EPS)
    o_ref[...] = (y * g_ref[...]).astype(o_ref.dtype)


def rms_norm_pallas(x2, gain, out_dtype, *, tm=512):
    m, d = x2.shape
    tm = min(tm, m)
    return pl.pallas_call(
        _rms_kernel,
        grid=(m // tm,),
        in_specs=[pl.BlockSpec((tm, d), lambda i: (i, 0)), pl.BlockSpec((1, d), lambda i: (0, 0))],
        out_specs=pl.BlockSpec((tm, d), lambda i: (i, 0)),
        out_shape=jax.ShapeDtypeStruct((m, d), out_dtype),
        compiler_params=pltpu.CompilerParams(
            dimension_semantics=("parallel",), vmem_limit_bytes=VMEM_LIMIT_BYTES),
        name="rms_norm",
    )(x2, gain.astype(F32)[None, :])


def _merge_kernel(h_ref, o_ref, wg_ref, bg_ref, wu_ref, out_ref, acc_ref):
    n = pl.program_id(2)
    gate = jax.nn.sigmoid(
        jnp.dot(h_ref[...], wg_ref[...].astype(BF16), preferred_element_type=F32) + bg_ref[...])
    term = gate * jnp.dot(o_ref[...], wu_ref[...].astype(BF16), preferred_element_type=F32)

    @pl.when(n == 0)
    def _():
        acc_ref[...] = term

    @pl.when(n != 0)
    def _():
        acc_ref[...] += term

    @pl.when(n == pl.num_programs(2) - 1)
    def _():
        out_ref[...] = acc_ref[...].astype(out_ref.dtype)


def merge_branches_pallas(h, o_all, w_gate, b_gate, w_up, layer, *, tm=1024, tn=512):
    T, D = h.shape
    N, _, W = o_all.shape
    l = int(layer)
    return pl.pallas_call(
        _merge_kernel,
        grid=(T // tm, D // tn, N),
        in_specs=[pl.BlockSpec((tm, D), lambda i, j, n: (i, 0)),
                  pl.BlockSpec((None, tm, W), lambda i, j, n: (n, i, 0)),
                  pl.BlockSpec((None, None, D, tn), lambda i, j, n: (l, n, 0, j)),
                  pl.BlockSpec((None, None, 1, tn), lambda i, j, n: (l, n, 0, j)),
                  pl.BlockSpec((None, None, W, tn), lambda i, j, n: (l, n, 0, j))],
        out_specs=pl.BlockSpec((tm, tn), lambda i, j, n: (i, j)),
        out_shape=jax.ShapeDtypeStruct((T, D), BF16),
        scratch_shapes=[pltpu.VMEM((tm, tn), F32)],
        compiler_params=pltpu.CompilerParams(
            dimension_semantics=("parallel", "parallel", "arbitrary"),
            vmem_limit_bytes=VMEM_LIMIT_BYTES),
        name="merge_branches",
    )(h, o_all, w_gate, b_gate[:, :, None, :], w_up)


LRU_TS = 256
LRU_CW = 256


def _rows_back(x, prev, s, row):
    return jnp.where(row < s, pltpu.roll(prev, s, 0), pltpu.roll(x, s, 0))


def _lane_window(blocks, shift):
    if shift == 0:
        return jnp.concatenate([b[...] for b in blocks], axis=-1)
    back = LANES - shift
    rolled = [pltpu.roll(b[...], back, 1) for b in blocks]
    lane = lax.broadcasted_iota(jnp.int32, rolled[0].shape, 1)
    return jnp.concatenate([jnp.where(lane < back, rolled[j], rolled[j + 1])
                            for j in range(len(blocks) - 1)], axis=-1)


def _lru_conv_kernel(*refs, n_lane_blocks, shift):
    nb = n_lane_blocks
    rx, rg, cb, cc, cx = (_lane_window(refs[k * nb:(k + 1) * nb], shift) for k in range(5))
    (lw_ref, lb_ref, wa_ref, ba_ref, wi_ref, bi_ref, lam_ref, sw_ref,
     olru_ref, oconv_ref, px_ref, py_ref, h_ref) = refs[5 * nb:]

    @pl.when(pl.program_id(2) == 0)
    def _():
        px_ref[...] = jnp.zeros_like(px_ref)
        py_ref[...] = jnp.zeros_like(py_ref)
        h_ref[...] = jnp.zeros_like(h_ref)

    ts, cw = px_ref.shape
    row = lax.broadcasted_iota(jnp.int32, (ts, cw), 0)

    x = rx
    prev = px_ref[...]
    nk = lw_ref.shape[0]
    xc = lb_ref[...] + lw_ref[nk - 1:nk, :] * x
    for s in range(1, nk):
        xc = xc + lw_ref[nk - 1 - s:nk - s, :] * _rows_back(x, prev, s, row)
    px_ref[...] = x
    r_parts, i_parts = [], []
    for hh in range(cw // HEAD_DIM):
        cs = slice(hh * HEAD_DIM, (hh + 1) * HEAD_DIM)
        xh = xc[:, cs].astype(BF16)
        r_parts.append(jnp.dot(xh, wa_ref[hh].astype(BF16), preferred_element_type=F32))
        i_parts.append(jnp.dot(xh, wi_ref[hh].astype(BF16), preferred_element_type=F32))
    r = jax.nn.sigmoid(jnp.concatenate(r_parts, axis=-1) + ba_ref[...])
    gi = jax.nn.sigmoid(jnp.concatenate(i_parts, axis=-1) + bi_ref[...])
    log_a = -LRU_C * r * jax.nn.softplus(-lam_ref[...])
    a = jnp.exp(log_a)
    b = jnp.sqrt(1.0 - a * a) * (gi * xc)
    d = 1
    while d < ts:
        a_back = jnp.where(row < d, 1.0, pltpu.roll(a, d, 0))
        b_back = jnp.where(row < d, 0.0, pltpu.roll(b, d, 0))
        b = a * b_back + b
        a = a * a_back
        d *= 2
    h = b + a * h_ref[...]
    h_ref[...] = h[ts - 1:ts, :]
    olru_ref[...] = (h * jax.nn.gelu(rg)).astype(olru_ref.dtype)

    y = cc * cx
    prev_y = py_ref[...]
    nk = sw_ref.shape[0]
    conv = sw_ref[nk - 1:nk, :] * y
    for s in range(1, nk):
        conv = conv + sw_ref[nk - 1 - s:nk - s, :] * _rows_back(y, prev_y, s, row)
    py_ref[...] = y
    oconv_ref[...] = (cb * conv).astype(oconv_ref.dtype)


def lru_conv_pallas(z, layer, lru_conv_w, lru_conv_b, lru_w_a, lru_b_a, lru_w_i, lru_b_i, lru_lambda,
                    sc_conv_w, *, shift=0):
    B, S, _ = z.shape
    C = lru_conv_b.shape[-1]
    ts, cw = LRU_TS, LRU_CW
    nblk = C // cw
    l = int(layer)
    per_blk = cw // LANES
    nb = per_blk + (1 if shift else 0)
    lane_blk = lambda k, j: pl.BlockSpec(
        (None, ts, LANES), lambda b, c, t: (b, t, (k * nblk + c) * per_blk + j))
    cols = [lane_blk(k, j) for k in range(5) for j in range(nb)]
    par = lambda rows: pl.BlockSpec((None, rows, cw), lambda b, c, t: (l, 0, c))
    hw = pl.BlockSpec((None, cw // HEAD_DIM, HEAD_DIM, HEAD_DIM), lambda b, c, t: (l, c, 0, 0))
    L = lru_conv_b.shape[0]
    flat = lambda p: p.reshape(L, 1, C)
    out = pl.BlockSpec((None, ts, cw), lambda b, c, t: (b, t, c))
    return pl.pallas_call(
        functools.partial(_lru_conv_kernel, n_lane_blocks=nb, shift=shift),
        grid=(B, nblk, S // ts),
        in_specs=cols + [par(lru_conv_w.shape[1]), par(1), hw, par(1), hw, par(1), par(1),
                         par(sc_conv_w.shape[1])],
        out_specs=[out, out],
        out_shape=[jax.ShapeDtypeStruct((B, S, C), BF16)] * 2,
        scratch_shapes=[pltpu.VMEM((ts, cw), F32), pltpu.VMEM((ts, cw), F32), pltpu.VMEM((1, cw), F32)],
        compiler_params=pltpu.CompilerParams(
            dimension_semantics=("parallel", "parallel", "arbitrary"),
            vmem_limit_bytes=VMEM_LIMIT_BYTES),
        name="lru_conv",
    )(*([z] * len(cols)), lru_conv_w, flat(lru_conv_b), lru_w_a, flat(lru_b_a), lru_w_i, flat(lru_b_i),
      flat(lru_lambda), sc_conv_w)


XA_TM = 256


def _xattn_kernel(x_ref, g_ref, wq_ref, k_ref, v_ref, wo_ref, o_ref, *, scale):
    x = x_ref[...]
    xn = (x * lax.rsqrt(jnp.mean(x * x, axis=-1, keepdims=True) + NORM_EPS) * g_ref[...]).astype(BF16)
    q = jnp.dot(xn, wq_ref[...], preferred_element_type=F32).astype(BF16)
    heads = []
    for hh in range(XA_HEADS):
        cs = slice(hh * HEAD_DIM, (hh + 1) * HEAD_DIM)
        s = lax.dot_general(q[:, cs], k_ref[:, cs], (((1,), (1,)), ((), ())),
                            preferred_element_type=F32) * scale
        e = jnp.exp(s - jnp.max(s, axis=-1, keepdims=True))
        p = (e / jnp.sum(e, axis=-1, keepdims=True)).astype(BF16)
        heads.append(jnp.dot(p, v_ref[:, cs], preferred_element_type=F32))
    o = jnp.concatenate(heads, axis=-1).astype(BF16)
    o_ref[...] = x + jnp.dot(o, wo_ref[...], preferred_element_type=F32)


def cross_attention_pallas(x, gain, k, v, w_q, w_o):
    B, S, D = x.shape
    M = k.shape[1]
    tm = XA_TM
    per_b = S // tm
    kv = pl.BlockSpec((None, M, XA_WIDTH), lambda i: (i // per_b, 0, 0))
    return pl.pallas_call(
        functools.partial(_xattn_kernel, scale=HEAD_DIM ** -0.5),
        grid=(B * per_b,),
        in_specs=[pl.BlockSpec((tm, D), lambda i: (i, 0)),
                  pl.BlockSpec((1, D), lambda i: (0, 0)),
                  pl.BlockSpec((D, XA_WIDTH), lambda i: (0, 0)),
                  kv, kv,
                  pl.BlockSpec((XA_WIDTH, D), lambda i: (0, 0))],
        out_specs=pl.BlockSpec((tm, D), lambda i: (i, 0)),
        out_shape=jax.ShapeDtypeStruct((B * S, D), F32),
        compiler_params=pltpu.CompilerParams(
            dimension_semantics=("parallel",), vmem_limit_bytes=VMEM_LIMIT_BYTES),
        name="cross_attention",
    )(x.reshape(B * S, D), gain.astype(F32)[None, :], w_q, k, v, w_o).reshape(B, S, D)


MASK_BIG = 1e30
LANES = 128


def _first_index_topk_mask(work, colf, k):
    sel = jnp.zeros(work.shape, F32)
    for _ in range(k):
        mx = jnp.max(work, axis=-1, keepdims=True)
        idx = jnp.min(jnp.where(work == mx, colf, float(LANES)), axis=-1, keepdims=True)
        pick = colf == idx
        sel = jnp.where(pick, 1.0, sel)
        work = jnp.where(pick, -jnp.inf, work)
    return sel


def _first_index_topk_mask_t(work, rowf, k):
    n = work.shape[0]
    sel = jnp.zeros(work.shape, F32)
    for _ in range(k):
        mx = jnp.max(work, axis=0, keepdims=True)
        idx = jnp.min(jnp.where(work == mx, rowf, float(n)), axis=0, keepdims=True)
        pick = rowf == idx
        sel = jnp.where(pick, 1.0, sel)
        work = jnp.where(pick, -jnp.inf, work)
    return sel


ATT_TK = 512
MOBA_HEADS_PER_STEP = 4
M_INIT = -1e38
NT_DIMS = (((1,), (1,)), ((), ()))


def _flash_update(q_aug, k_aug, v_aug, m_ref, acc_ref, causal):
    s = lax.dot_general(q_aug, k_aug, NT_DIMS, preferred_element_type=F32)
    if causal is not None:
        q0, k0 = causal
        row = lax.broadcasted_iota(jnp.int32, s.shape, 0)
        col = lax.broadcasted_iota(jnp.int32, s.shape, 1)
        s = jnp.where(k0 + col <= q0 + row, s, NEG)
    m_prev = m_ref[...]
    m_new = jnp.maximum(m_prev, jnp.max(s, axis=-1, keepdims=True))
    alpha = jnp.exp(m_prev - m_new)
    p = jnp.exp(s - m_new).astype(BF16)
    acc_ref[...] = alpha * acc_ref[...] + jnp.dot(p, v_aug, preferred_element_type=F32)
    m_ref[...] = m_new


def _moba_kernel(q_ref, k_ref, v_ref, o_ref, kmean_ref, qa_ref, m_ref, acc_ref):
    i = pl.program_id(2)
    bs = MOBA_BLOCK
    seq = k_ref.shape[0]
    hp = q_ref.shape[1] // HEAD_DIM
    nt = NT_DIMS
    hcol = lambda h: slice(h * HEAD_DIM, (h + 1) * HEAD_DIM)

    @pl.when(i == 0)
    def _():
        row = lax.broadcasted_iota(jnp.int32, (LANES, seq), 0)
        col = lax.broadcasted_iota(jnp.int32, (LANES, seq), 1)
        ind = jnp.where(col // bs == row, 1.0, 0.0).astype(BF16)
        kmean_ref[...] = (jnp.dot(ind, k_ref[...], preferred_element_type=F32)
                          * (1.0 / bs)).astype(BF16)

    nb = seq // bs
    nbp = -(-nb // 8) * 8
    blk_t = lax.broadcasted_iota(jnp.int32, (nbp, bs), 0)
    past_t = blk_t < i
    for h in range(hp):
        q = q_ref[:, hcol(h)]
        gate_t = lax.dot_general(kmean_ref[0:nbp, hcol(h)], q, nt, preferred_element_type=F32)
        sel_t = _first_index_topk_mask_t(jnp.where(past_t, gate_t, NEG), blk_t.astype(F32), MOBA_TOPK)
        notsel_t = jnp.where(((sel_t > 0.0) & past_t) | (blk_t == i), 0.0, 1.0)
        if nbp < LANES:
            notsel_t = jnp.concatenate([notsel_t, jnp.ones((LANES - nbp, bs), F32)], axis=0)
        qa_ref[h] = jnp.concatenate([q, notsel_t.T.astype(BF16)], axis=-1)

    tk = ATT_TK
    blk_per_tile = tk // bs
    n_tiles = (i + blk_per_tile) // blk_per_tile
    key_blk = lax.broadcasted_iota(jnp.int32, (tk, LANES), 0) // bs
    key_col = lax.broadcasted_iota(jnp.int32, (tk, LANES), 1)
    ones_v = jnp.ones((tk, LANES), BF16)
    m_ref[...] = jnp.full(m_ref.shape, M_INIT, F32)
    acc_ref[...] = jnp.zeros_like(acc_ref)

    def tile(jt, causal):
        off = pl.multiple_of(jt * tk, tk)
        bias = jnp.where(key_col == jt * blk_per_tile + key_blk, -MASK_BIG, 0.0).astype(BF16)
        for h in range(hp):
            k_aug = jnp.concatenate([k_ref[pl.ds(off, tk), hcol(h)], bias], axis=-1)
            v_aug = jnp.concatenate([v_ref[pl.ds(off, tk), hcol(h)], ones_v], axis=-1)
            _flash_update(qa_ref[h], k_aug, v_aug, m_ref.at[h], acc_ref.at[h],
                          (i * bs, off) if causal else None)

    def body(jt, carry):
        tile(jt, False)
        return carry

    lax.fori_loop(0, n_tiles - 1, body, 0)
    tile(n_tiles - 1, True)
    for h in range(hp):
        acc = acc_ref[h]
        o_ref[:, hcol(h)] = (acc[:, :HEAD_DIM] / acc[:, HEAD_DIM:]).astype(o_ref.dtype)


def moba_attention_pallas(zq, zk, zv, *, n_heads, q_off=0, k_off=0, v_off=0):
    B, S, _ = zq.shape
    bs = MOBA_BLOCK
    hp = min(MOBA_HEADS_PER_STEP, n_heads)
    hw = hp * HEAD_DIM
    assert S % ATT_TK == 0 and S // bs <= LANES and n_heads % hp == 0
    assert q_off % hp == 0 and k_off % hp == 0 and v_off % hp == 0
    return pl.pallas_call(
        _moba_kernel,
        grid=(B, n_heads // hp, S // bs),
        in_specs=[pl.BlockSpec((None, bs, hw), lambda b, h, i: (b, i, q_off // hp + h)),
                  pl.BlockSpec((None, S, hw), lambda b, h, i: (b, 0, k_off // hp + h)),
                  pl.BlockSpec((None, S, hw), lambda b, h, i: (b, 0, v_off // hp + h))],
        out_specs=pl.BlockSpec((None, bs, hw), lambda b, h, i: (b, i, h)),
        out_shape=jax.ShapeDtypeStruct((B, S, n_heads * HEAD_DIM), BF16),
        scratch_shapes=[pltpu.VMEM((LANES, hw), BF16), pltpu.VMEM((hp, bs, 2 * HEAD_DIM), BF16),
                        pltpu.VMEM((hp, bs, 1), F32), pltpu.VMEM((hp, bs, 2 * HEAD_DIM), F32)],
        compiler_params=pltpu.CompilerParams(
            dimension_semantics=("parallel", "parallel", "arbitrary"),
            vmem_limit_bytes=VMEM_LIMIT_BYTES),
        name="moba_attention",
    )(zq, zk, zv)


NSA_TQ = 256
NSA_NC_PAD = 256
CMP_ROW = NSA_CMP_STRIDE * HEAD_DIM


def _nsa_compress_kernel(x_ref, pe_ref, w1_ref, w2_ref, o_ref):
    x = x_ref[...].astype(F32)
    top = (x + pe_ref[0:1, :]).astype(BF16)
    bot = (x + pe_ref[1:2, :]).astype(BF16)
    a = jnp.dot(top, w1_ref[0:CMP_ROW, :], preferred_element_type=F32)
    b = jnp.dot(bot, w1_ref[CMP_ROW:2 * CMP_ROW, :], preferred_element_type=F32)
    pre = a + pltpu.roll(b, b.shape[0] - 1, 0)
    hid = jax.nn.gelu(pre)
    o_ref[...] = jnp.dot(hid.astype(BF16), w2_ref[...], preferred_element_type=F32).astype(o_ref.dtype)


def nsa_compress_pallas(x, pe, w1, w2):
    B, S, gw = x.shape
    G = gw // HEAD_DIM
    nrow = S // NSA_CMP_STRIDE
    xr = x.reshape(B, nrow, NSA_CMP_STRIDE, G, HEAD_DIM).transpose(0, 3, 1, 2, 4).reshape(B, G, nrow, CMP_ROW)
    pe2 = pe.astype(F32).reshape(2, CMP_ROW)
    return pl.pallas_call(
        _nsa_compress_kernel,
        grid=(B, G),
        in_specs=[pl.BlockSpec((None, None, nrow, CMP_ROW), lambda b, g: (b, g, 0, 0)),
                  pl.BlockSpec((2, CMP_ROW), lambda b, g: (0, 0)),
                  pl.BlockSpec((2 * CMP_ROW, HEAD_DIM), lambda b, g: (0, 0)),
                  pl.BlockSpec((HEAD_DIM, HEAD_DIM), lambda b, g: (0, 0))],
        out_specs=pl.BlockSpec((None, None, nrow, HEAD_DIM), lambda b, g: (b, g, 0, 0)),
        out_shape=jax.ShapeDtypeStruct((B, G, nrow, HEAD_DIM), BF16),
        compiler_params=pltpu.CompilerParams(
            dimension_semantics=("parallel", "parallel"), vmem_limit_bytes=VMEM_LIMIT_BYTES),
        name="nsa_compress",
    )(xr, pe2, w1.astype(BF16), w2.astype(BF16))


def _nsa_kernel(q_ref, kc_ref, vc_ref, ks_ref, vs_ref, kw_ref, vw_ref, g_ref, wmap_ref, o_ref,
                m_ref, acc_ref, out_ref):
    i = pl.program_id(2)
    tq = NSA_TQ
    R = NSA_HEADS // NSA_KV_HEADS
    nt = (((1,), (1,)), ((), ()))
    row = lax.broadcasted_iota(jnp.int32, (tq, tq), 0)
    col = lax.broadcasted_iota(jnp.int32, (tq, tq), 1)
    t_abs = i * tq + row
    gates = jax.nn.sigmoid(g_ref[...])

    def qh(r):
        return q_ref[:, r * HEAD_DIM:(r + 1) * HEAD_DIM]

    ncp = kc_ref.shape[0]
    cmask = (lax.broadcasted_iota(jnp.int32, (tq, ncp), 1) * NSA_CMP_STRIDE + (NSA_CMP_BLOCK - 1)
             <= i * tq + lax.broadcasted_iota(jnp.int32, (tq, ncp), 0))
    imp = jnp.zeros((tq, LANES), F32)
    vc_aug = jnp.concatenate([vc_ref[...], jnp.ones((ncp, LANES), BF16), wmap_ref[...]], axis=-1)
    for r in range(R):
        s = lax.dot_general(qh(r), kc_ref[...], nt, preferred_element_type=F32)
        s = jnp.where(cmask, s, NEG)
        e = jnp.where(cmask, jnp.exp(s - jnp.max(s, axis=-1, keepdims=True)), 0.0).astype(BF16)
        acc = jnp.dot(e, vc_aug, preferred_element_type=F32)
        l = acc[:, HEAD_DIM:2 * HEAD_DIM]
        inv = jnp.where(l > 0.0, 1.0 / l, 0.0)
        imp = imp + acc[:, 2 * HEAD_DIM:] * inv
        out_ref[r] = gates[:, 3 * r:3 * r + 1] * (acc[:, :HEAD_DIM] * inv)

    ones_w = jnp.ones((tq, LANES), BF16)
    w_tiles = []
    for d in range(NSA_WINDOW // tq + 1):
        off = pl.multiple_of(jnp.maximum(i - d, 0) * tq, tq)
        gone = jnp.where(i >= d, 0, 2 * NSA_WINDOW + tq)
        dist = d * tq + row - col + gone
        w_tiles.append((off, (dist >= 0) & (dist < NSA_WINDOW)))
    for r in range(R):
        ss = []
        for off, mask in w_tiles:
            s = lax.dot_general(qh(r), kw_ref[pl.ds(off, tq), :], nt, preferred_element_type=F32)
            ss.append(jnp.where(mask, s, NEG))
        m = jnp.max(ss[0], axis=-1, keepdims=True)
        for s in ss[1:]:
            m = jnp.maximum(m, jnp.max(s, axis=-1, keepdims=True))
        acc = jnp.zeros((tq, 2 * HEAD_DIM), F32)
        for (off, _), s in zip(w_tiles, ss):
            v_aug = jnp.concatenate([vw_ref[pl.ds(off, tq), :], ones_w], axis=-1)
            acc = acc + jnp.dot(jnp.exp(s - m).astype(BF16), v_aug, preferred_element_type=F32)
        out_ref[r] = out_ref[r] + gates[:, 3 * r + 2:3 * r + 3] * (acc[:, :HEAD_DIM] / acc[:, HEAD_DIM:])

    ns = ks_ref.shape[0] // NSA_SEL_BLOCK
    blk_t = lax.broadcasted_iota(jnp.int32, (ns, tq), 0)
    q_blk_t = (i * tq + lax.broadcasted_iota(jnp.int32, (ns, tq), 1)) // NSA_SEL_BLOCK
    valid_t = blk_t <= q_blk_t
    forced_t = (blk_t == 0) | (blk_t >= q_blk_t - 1)
    impm_t = jnp.where(valid_t, imp.T[:ns] + jnp.where(forced_t, NSA_FORCE_BONUS, 0.0), NEG)
    sel_t = _first_index_topk_mask_t(impm_t, blk_t.astype(F32), NSA_SEL_TOPK)
    notsel_t = jnp.where((sel_t > 0.0) & valid_t, 0.0, 1.0)
    if ns < LANES:
        notsel_t = jnp.concatenate([notsel_t, jnp.ones((LANES - ns, tq), F32)], axis=0)
    notsel = notsel_t.T.astype(BF16)

    tk = ATT_TK
    blk_per_tile = tk // NSA_SEL_BLOCK
    n_tiles = (i * tq) // tk + 1
    key_blk = lax.broadcasted_iota(jnp.int32, (tk, LANES), 0) // NSA_SEL_BLOCK
    key_col = lax.broadcasted_iota(jnp.int32, (tk, LANES), 1)
    ones_v = jnp.ones((tk, LANES), BF16)
    m_ref[...] = jnp.full(m_ref.shape, M_INIT, F32)
    acc_ref[...] = jnp.zeros_like(acc_ref)

    def sel_tile(jt, causal):
        off = pl.multiple_of(jt * tk, tk)
        bias = jnp.where(key_col == jt * blk_per_tile + key_blk, -MASK_BIG, 0.0).astype(BF16)
        k_aug = jnp.concatenate([ks_ref[pl.ds(off, tk), :], bias], axis=-1)
        v_aug = jnp.concatenate([vs_ref[pl.ds(off, tk), :], ones_v], axis=-1)
        for r in range(R):
            q_aug = jnp.concatenate([qh(r), notsel], axis=-1)
            _flash_update(q_aug, k_aug, v_aug, m_ref.at[r], acc_ref.at[r], (i * tq, off) if causal else None)

    def body(jt, carry):
        sel_tile(jt, False)
        return carry

    lax.fori_loop(0, n_tiles - 1, body, 0)
    sel_tile(n_tiles - 1, True)
    for r in range(R):
        acc = acc_ref[r]
        o_s = acc[:, :HEAD_DIM] / acc[:, HEAD_DIM:]
        o_ref[:, r * HEAD_DIM:(r + 1) * HEAD_DIM] = (
            out_ref[r] + gates[:, 3 * r + 1:3 * r + 2] * o_s).astype(o_ref.dtype)


def nsa_attention_pallas(zq, kc, vc, zks, zvs, zkw, zvw, gate_logits, *, q_off, ks_off, vs_off, kw_off, vw_off):
    B, S, _ = zq.shape
    G, tq = NSA_KV_HEADS, NSA_TQ
    R = NSA_HEADS // G
    assert S % ATT_TK == 0 and ATT_TK % tq == 0 and S // NSA_SEL_BLOCK <= LANES
    nc = (S - NSA_CMP_BLOCK) // NSA_CMP_STRIDE + 1
    ncp = kc.shape[2]
    ns = S // NSA_SEL_BLOCK
    wmap = jnp.pad(cmp_to_sel_weights(nc, ns), ((0, ncp - nc), (0, LANES - ns))).astype(BF16)
    rw = R * HEAD_DIM
    kv_spec = lambda off: pl.BlockSpec((None, S, HEAD_DIM), lambda b, g, i: (b, 0, off + g))
    c_spec = pl.BlockSpec((None, None, ncp, HEAD_DIM), lambda b, g, i: (b, g, 0, 0))
    return pl.pallas_call(
        _nsa_kernel,
        grid=(B, G, S // tq),
        in_specs=[pl.BlockSpec((None, tq, rw), lambda b, g, i: (b, i, q_off // R + g)),
                  c_spec, c_spec, kv_spec(ks_off), kv_spec(vs_off), kv_spec(kw_off), kv_spec(vw_off),
                  pl.BlockSpec((None, tq, LANES), lambda b, g, i: (b, i, g)),
                  pl.BlockSpec((ncp, LANES), lambda b, g, i: (0, 0))],
        out_specs=pl.BlockSpec((None, tq, rw), lambda b, g, i: (b, i, g)),
        out_shape=jax.ShapeDtypeStruct((B, S, NSA_HEADS * HEAD_DIM), BF16),
        scratch_shapes=[pltpu.VMEM((R, tq, 1), F32),
                        pltpu.VMEM((R, tq, 2 * HEAD_DIM), F32), pltpu.VMEM((R, tq, HEAD_DIM), F32)],
        compiler_params=pltpu.CompilerParams(
            dimension_semantics=("parallel", "parallel", "arbitrary"),
            vmem_limit_bytes=VMEM_LIMIT_BYTES),
        name="nsa_attention",
    )(zq, kc, vc, zks, zvs, zkw, zvw, gate_logits, wmap)


MOE_TM = 256
MOE_TN = 512
ROUTER_TM = 256


def _router_kernel(x_ref, g_ref, w_ref, b_ref, t_ref, ids_ref, wts_ref):
    x = x_ref[...]
    t = x * lax.rsqrt(jnp.mean(x * x, axis=-1, keepdims=True) + NORM_EPS) * g_ref[...]
    t_ref[...] = t
    logits = jnp.dot(t.astype(BF16), w_ref[...], preferred_element_type=F32) + b_ref[...]
    coli = lax.broadcasted_iota(jnp.int32, logits.shape, 1)
    colf = coli.astype(F32)
    first = lambda hit: jnp.min(jnp.where(hit, colf, float(LANES)), axis=-1, keepdims=True)
    is_g = coli < N_GROUPS
    gl = jnp.where(is_g, logits, -jnp.inf)
    gmax = jnp.max(gl, axis=-1, keepdims=True)
    g_sel = first(gl == gmax)
    p_g = 1.0 / jnp.sum(jnp.where(is_g, jnp.exp(logits - gmax), 0.0), axis=-1, keepdims=True)
    lo = N_GROUPS + EXPERTS_PER_GROUP * g_sel
    el = jnp.where((colf >= lo) & (colf < lo + EXPERTS_PER_GROUP), logits, -jnp.inf)
    v1 = jnp.max(el, axis=-1, keepdims=True)
    i1 = first(el == v1)
    el2 = jnp.where(colf == i1, -jnp.inf, el)
    v2 = jnp.max(el2, axis=-1, keepdims=True)
    i2 = first(el2 == v2)
    e = jnp.exp(v2 - v1)
    w1 = p_g / (1.0 + e)
    w2 = p_g * e / (1.0 + e)
    ids_ref[...] = jnp.where(coli == 0, i1 - N_GROUPS, jnp.where(coli == 1, i2 - N_GROUPS, 0.0)).astype(jnp.int32)
    wts_ref[...] = jnp.where(coli == 0, w1, jnp.where(coli == 1, w2, 0.0))


def moe_router_pallas(x2, gain, w_group, b_group, w_expert, b_expert):
    T, D = x2.shape
    tm = ROUTER_TM
    nr = N_GROUPS + N_EXPERTS
    w = jnp.pad(jnp.concatenate([w_group, w_expert], axis=1), ((0, 0), (0, LANES - nr))).astype(BF16)
    b = jnp.pad(jnp.concatenate([b_group, b_expert]), (0, LANES - nr)).astype(F32)[None, :]
    t, ids, wts = pl.pallas_call(
        _router_kernel,
        grid=(T // tm,),
        in_specs=[pl.BlockSpec((tm, D), lambda i: (i, 0)),
                  pl.BlockSpec((1, D), lambda i: (0, 0)),
                  pl.BlockSpec((D, LANES), lambda i: (0, 0)),
                  pl.BlockSpec((1, LANES), lambda i: (0, 0))],
        out_specs=[pl.BlockSpec((tm, D), lambda i: (i, 0)),
                   pl.BlockSpec((tm, LANES), lambda i: (i, 0)),
                   pl.BlockSpec((tm, LANES), lambda i: (i, 0))],
        out_shape=[jax.ShapeDtypeStruct((T, D), F32),
                   jax.ShapeDtypeStruct((T, LANES), jnp.int32),
                   jax.ShapeDtypeStruct((T, LANES), F32)],
        compiler_params=pltpu.CompilerParams(
            dimension_semantics=("parallel",), vmem_limit_bytes=VMEM_LIMIT_BYTES),
        name="moe_router",
    )(x2, gain.astype(F32)[None, :], w, b)
    return t, ids[:, :TOPK_IN_GROUP], wts[:, :TOPK_IN_GROUP]


def _moe_up_kernel(tile_e_ref, n_used_ref, row_tok_ref, t_hbm, rw_ref, wg_ref, wu_ref, hid_ref,
                   xw_ref, xb_ref, sem):
    k = pl.program_id(0)
    tm = xb_ref.shape[0]
    n_used = n_used_ref[0]

    def row_copy(tok, r):
        return pltpu.make_async_copy(t_hbm.at[pl.ds(tok, 1), :], xw_ref.at[pl.ds(r, 1), :], sem.at[0])

    def start_gather(tile):
        def issue(r, carry):
            row_copy(row_tok_ref[tile * tm + r], r).start()
            return carry
        lax.fori_loop(0, tm, issue, 0, unroll=8)

    @pl.when(k == 0)
    def _():
        start_gather(0)

    @pl.when(k < n_used)
    def _():
        def wait_row(r, carry):
            row_copy(0, r).wait()
            return carry
        lax.fori_loop(0, tm, wait_row, 0, unroll=8)
        xb_ref[...] = xw_ref[...].astype(BF16)

        @pl.when(k + 1 < n_used)
        def _():
            start_gather(k + 1)

        x = xb_ref[...]
        hg = jnp.dot(x, wg_ref[...].astype(BF16), preferred_element_type=F32)
        hu = jnp.dot(x, wu_ref[...].astype(BF16), preferred_element_type=F32)
        hid_ref[...] = (jax.nn.silu(hg) * hu * rw_ref[...]).astype(hid_ref.dtype)

    @pl.when(k >= n_used)
    def _():
        hid_ref[...] = jnp.zeros_like(hid_ref)


def _moe_down_kernel(tile_e_ref, n_used_ref, hid_ref, wd_ref, o_ref):
    k = pl.program_id(0)

    @pl.when(k < n_used_ref[0])
    def _():
        hid = hid_ref[...]
        for c in range(0, o_ref.shape[1], MOE_TN):
            o_ref[:, c:c + MOE_TN] = jnp.dot(hid, wd_ref[:, c:c + MOE_TN].astype(BF16),
                                             preferred_element_type=F32)

    @pl.when(k >= n_used_ref[0])
    def _():
        o_ref[...] = jnp.zeros_like(o_ref)


COMBINE_TC = 128


def _moe_combine_kernel(pos_ref, x_ref, g_ref, y_hbm, o_ref, n_ref, buf, sem):
    i = pl.program_id(0)
    tc = x_ref.shape[0]
    nk = TOPK_IN_GROUP

    def row_copy(slot, a, p):
        return pltpu.make_async_copy(y_hbm.at[pl.ds(p, 1), :], buf.at[slot, pl.ds(a, 1), :], sem.at[slot])

    def start_gather(tile, slot):
        def issue(r, carry):
            for kk in range(nk):
                row_copy(slot, kk * tc + r, pos_ref[(tile * tc + r) * nk + kk]).start()
            return carry
        lax.fori_loop(0, tc, issue, 0, unroll=4)

    slot = i % 2

    @pl.when(i == 0)
    def _():
        start_gather(0, 0)

    def wait_row(a, carry):
        row_copy(slot, a, 0).wait()
        return carry
    lax.fori_loop(0, nk * tc, wait_row, 0, unroll=8)

    @pl.when(i + 1 < pl.num_programs(0))
    def _():
        start_gather(i + 1, 1 - slot)

    v = x_ref[...]
    for kk in range(nk):
        v = v + buf[slot, kk * tc:(kk + 1) * tc, :]
    o_ref[...] = v
    n_ref[...] = (v * lax.rsqrt(jnp.mean(v * v, axis=-1, keepdims=True) + NORM_EPS)
                  * g_ref[...]).astype(n_ref.dtype)


def moe_combine_pallas(x2, y_rows, pos, next_gain, next_dtype):
    T, D = x2.shape
    tc = COMBINE_TC
    row_blk = pl.BlockSpec((tc, D), lambda i, *_: (i, 0))
    return pl.pallas_call(
        _moe_combine_kernel,
        grid_spec=pltpu.PrefetchScalarGridSpec(
            num_scalar_prefetch=1,
            grid=(T // tc,),
            in_specs=[row_blk, pl.BlockSpec((1, D), lambda i, *_: (0, 0)),
                      pl.BlockSpec(memory_space=pl.ANY)],
            out_specs=[row_blk, row_blk],
            scratch_shapes=[pltpu.VMEM((2, TOPK_IN_GROUP * tc, D), F32),
                            pltpu.SemaphoreType.DMA((2,))]),
        out_shape=[jax.ShapeDtypeStruct((T, D), F32), jax.ShapeDtypeStruct((T, D), next_dtype)],
        compiler_params=pltpu.CompilerParams(
            dimension_semantics=("arbitrary",), vmem_limit_bytes=VMEM_LIMIT_BYTES),
        name="moe_combine",
    )(pos.reshape(-1), x2, next_gain.astype(F32)[None, :], y_rows)


def moe_experts_pallas(t_packed, row_tok, row_w, tile_e, n_used, w_gate, w_up, w_down, layer):
    P = row_tok.shape[0]
    D = w_gate.shape[-2]
    F = w_gate.shape[-1]
    tm = MOE_TM
    l = int(layer)
    w_in = pl.BlockSpec((None, None, D, F), lambda k, te, *_: (l, te[k], 0, 0))
    hid = pl.pallas_call(
        _moe_up_kernel,
        grid_spec=pltpu.PrefetchScalarGridSpec(
            num_scalar_prefetch=3,
            grid=(P // tm,),
            in_specs=[pl.BlockSpec(memory_space=pl.ANY),
                      pl.BlockSpec((tm, 1), lambda k, *_: (k, 0)),
                      w_in, w_in],
            out_specs=pl.BlockSpec((tm, F), lambda k, *_: (k, 0)),
            scratch_shapes=[pltpu.VMEM((tm, D), F32), pltpu.VMEM((tm, D), BF16),
                            pltpu.SemaphoreType.DMA((1,))]),
        out_shape=jax.ShapeDtypeStruct((P, F), BF16),
        compiler_params=pltpu.CompilerParams(
            dimension_semantics=("arbitrary",), vmem_limit_bytes=VMEM_LIMIT_BYTES),
        name="moe_up",
    )(tile_e, n_used, row_tok, t_packed, row_w, w_gate, w_up)
    return pl.pallas_call(
        _moe_down_kernel,
        grid_spec=pltpu.PrefetchScalarGridSpec(
            num_scalar_prefetch=2,
            grid=(P // tm,),
            in_specs=[pl.BlockSpec((tm, F), lambda k, *_: (k, 0)),
                      pl.BlockSpec((None, None, F, D), lambda k, te, *_: (l, te[k], 0, 0))],
            out_specs=pl.BlockSpec((tm, D), lambda k, *_: (k, 0))),
        out_shape=jax.ShapeDtypeStruct((P, D), F32),
        compiler_params=pltpu.CompilerParams(
            dimension_semantics=("parallel",), vmem_limit_bytes=VMEM_LIMIT_BYTES),
        name="moe_down",
    )(tile_e, n_used, hid, w_down)


def moe_dispatch_plan(ids, wts):
    T, K = ids.shape
    n = T * K
    P = n + N_EXPERTS * MOE_TM
    flat = ids.reshape(n)
    onehot = (flat[:, None] == jnp.arange(N_EXPERTS, dtype=jnp.int32)[None, :]).astype(jnp.int32)
    csum = jnp.cumsum(onehot, axis=0)
    counts = csum[-1]
    padded = (counts + MOE_TM - 1) // MOE_TM * MOE_TM
    pend = jnp.cumsum(padded)
    pstart = pend - padded
    pos = jnp.sum(onehot * (csum - 1 + pstart[None, :]), axis=1)
    tok = (jnp.arange(n, dtype=jnp.int32) // K).astype(F32)
    info = jnp.zeros((P, 2), F32).at[pos].set(jnp.stack([tok, wts.reshape(n)], axis=1))
    row_tok = info[:, 0].astype(jnp.int32)
    row_w = info[:, 1:2]
    tile_start = jnp.arange(P // MOE_TM, dtype=jnp.int32) * MOE_TM
    tile_e = jnp.minimum(jnp.sum((pend[None, :] <= tile_start[:, None]).astype(jnp.int32), axis=1),
                         N_EXPERTS - 1).astype(jnp.int32)
    n_used = (pend[-1] // MOE_TM).astype(jnp.int32)[None]
    return pos.reshape(T, K), row_tok, row_w, tile_e, n_used


def hierarchical_moe_pallas(x, gain, w_group, b_group, w_expert, b_expert, w_gate, w_up, w_down, layer,
                            next_gain, next_dtype):
    B, S, D = x.shape
    x2 = x.reshape(B * S, D)
    t, ids, wts = moe_router_pallas(x2, gain, w_group, b_group, w_expert, b_expert)
    pos, row_tok, row_w, tile_e, n_used = moe_dispatch_plan(ids, wts)
    y = moe_experts_pallas(t, row_tok, row_w, tile_e, n_used, w_gate, w_up, w_down, layer)
    out, normed = moe_combine_pallas(x2, y, pos, next_gain, next_dtype)
    return out.reshape(B, S, D), normed


def rms_norm(x, g):
    xf = x.astype(F32)
    y = xf * lax.rsqrt(jnp.mean(xf * xf, axis=-1, keepdims=True) + NORM_EPS)
    return (y * g.astype(F32)).astype(x.dtype)


def rope_tables(positions):
    inv = ROPE_THETA ** (-jnp.arange(0, HEAD_DIM, 2, dtype=F32) / HEAD_DIM)
    ang = positions.astype(F32)[..., None] * inv
    return jnp.cos(ang)[:, :, None, :], jnp.sin(ang)[:, :, None, :]


def rope(x, cos, sin):
    x1, x2 = jnp.split(x.astype(F32), 2, axis=-1)
    return jnp.concatenate([x1 * cos - x2 * sin, x2 * cos + x1 * sin], axis=-1).astype(x.dtype)


def split_heads(t):
    return t.reshape(t.shape[0], t.shape[1], -1, HEAD_DIM)


def split_cols(z):
    return jnp.split(z, [int(v) for v in np.cumsum(MIX_SPLITS)[:-1]], axis=-1)


def causal_depthwise_conv(x, w):
    width, c = w.shape
    xp = jnp.pad(x, ((0, 0), (width - 1, 0), (0, 0)))
    return lax.conv_general_dilated(xp, w[:, None, :].astype(x.dtype), (1,), 'VALID',
                                    dimension_numbers=('NWC', 'WIO', 'NWC'),
                                    feature_group_count=c)


def moba_attention(q, k, v):
    B, S, H, dh = q.shape
    bs = MOBA_BLOCK
    nb = -(-S // bs)
    scale = dh ** -0.5
    qh, kh, vh = (t.transpose(0, 2, 1, 3) for t in (q, k, v))
    pad = nb * bs - S
    kp = jnp.pad(kh, ((0, 0), (0, 0), (0, pad), (0, 0)))
    vp = jnp.pad(vh, ((0, 0), (0, 0), (0, pad), (0, 0)))
    kb = kp.reshape(B, H, nb, bs, dh)
    vb = vp.reshape(B, H, nb, bs, dh)
    k_mean = jnp.mean(kb.astype(F32), axis=3)
    t = jnp.arange(S)
    q_blk = t // bs
    gate = jnp.einsum('bhsd,bhnd->bhsn', qh.astype(F32), k_mean)
    past = jnp.arange(nb)[None, :] < q_blk[:, None]
    gate = jnp.where(past, gate, NEG)
    kk = min(MOBA_TOPK, nb)
    _, sel = lax.top_k(gate, kk)
    sel_ok = sel < q_blk[:, None]
    qb = MOBA_Q_BLOCK
    nqb = S // qb
    q_c = qh.reshape(B, H, nqb, qb, dh).transpose(2, 0, 1, 3, 4)
    sel_c = sel.reshape(B, H, nqb, qb, kk).transpose(2, 0, 1, 3, 4)
    ok_c = sel_ok.reshape(B, H, nqb, qb, kk).transpose(2, 0, 1, 3, 4)
    bi = jnp.arange(B)[:, None, None, None]
    hi = jnp.arange(H)[None, :, None, None]

    def chunk(args):
        qc, selc, okc, ci = args
        tq = ci * qb + jnp.arange(qb)
        own = (ci * qb) // bs * bs
        k_own = lax.dynamic_slice_in_dim(kp, own, bs, axis=2)
        v_own = lax.dynamic_slice_in_dim(vp, own, bs, axis=2)
        own_mask = (own + jnp.arange(bs))[None, :] <= tq[:, None]
        k_s = kb[bi, hi, selc].reshape(B, H, qb, kk * bs, dh)
        v_s = vb[bi, hi, selc].reshape(B, H, qb, kk * bs, dh)
        s_sel = jnp.einsum('bhqd,bhqkd->bhqk', qc, k_s, preferred_element_type=F32) * scale
        s_sel = jnp.where(jnp.repeat(okc, bs, axis=-1), s_sel, NEG)
        s_own = jnp.einsum('bhqd,bhkd->bhqk', qc, k_own, preferred_element_type=F32) * scale
        s_own = jnp.where(own_mask, s_own, NEG)
        p = jax.nn.softmax(jnp.concatenate([s_sel, s_own], axis=-1), axis=-1).astype(v.dtype)
        return (jnp.einsum('bhqk,bhqkd->bhqd', p[..., :kk * bs], v_s)
                + jnp.einsum('bhqk,bhkd->bhqd', p[..., kk * bs:], v_own))

    o = lax.map(chunk, (q_c, sel_c, ok_c, jnp.arange(nqb)))
    return o.transpose(1, 0, 3, 2, 4).reshape(B, S, H * dh)


def compress_tokens(x, pe, w1, w2):
    B, S, G, dh = x.shape
    nc = (S - NSA_CMP_BLOCK) // NSA_CMP_STRIDE + 1
    idx = jnp.arange(nc)[:, None] * NSA_CMP_STRIDE + jnp.arange(NSA_CMP_BLOCK)[None, :]
    blocks = x[:, idx] + pe[:, None, :]
    flat = blocks.transpose(0, 1, 3, 2, 4).reshape(B, nc, G, NSA_CMP_BLOCK * dh)
    return jax.nn.gelu(flat @ w1) @ w2


def cmp_to_sel_weights(nc, ns):
    r = NSA_SEL_BLOCK // NSA_CMP_STRIDE
    m = NSA_CMP_BLOCK // NSA_CMP_STRIDE
    c = jnp.arange(nc)[:, None] - r * jnp.arange(ns)[None, :]
    w = jnp.minimum(jnp.minimum(c + 1, r + m - 1 - c), min(r, m))
    return jnp.clip(w, 0, None).astype(F32)


def nsa_attention(q, k_cmp, v_cmp, k_sel, v_sel, k_win, v_win, gate_logits,
                  pe_k, w1_k, w2_k, pe_v, w1_v, w2_v):
    B, S, H, dh = q.shape
    G = k_sel.shape[2]
    R = H // G
    scale = dh ** -0.5
    t = jnp.arange(S)
    qg = q.reshape(B, S, G, R, dh)
    kc = compress_tokens(k_cmp, pe_k, w1_k, w2_k)
    vc = compress_tokens(v_cmp, pe_v, w1_v, w2_v)
    nc = kc.shape[1]
    s_c = jnp.einsum('bsgrd,bngd->bgrsn', qg, kc, preferred_element_type=F32) * scale
    c_end = jnp.arange(nc) * NSA_CMP_STRIDE + NSA_CMP_BLOCK - 1
    c_mask = c_end[None, :] <= t[:, None]
    p_c = jax.nn.softmax(jnp.where(c_mask, s_c, NEG), axis=-1) * c_mask
    o_cmp = jnp.einsum('bgrsn,bngd->bsgrd', p_c.astype(vc.dtype), vc).reshape(B, S, H, dh)
    ns = S // NSA_SEL_BLOCK
    imp = jnp.einsum('bgrsn,nm->bgsm', p_c, cmp_to_sel_weights(nc, ns))
    q_blk = t // NSA_SEL_BLOCK
    j = jnp.arange(ns)[None, :]
    valid = j <= q_blk[:, None]
    forced = (j == 0) | (j >= q_blk[:, None] - 1)
    imp = jnp.where(valid, imp + jnp.where(forced, NSA_FORCE_BONUS, 0.0), NEG)
    kk = min(NSA_SEL_TOPK, ns)
    _, sel = lax.top_k(imp, kk)
    sel_ok = sel <= q_blk[:, None]
    sb = NSA_SEL_BLOCK
    k_blocks = k_sel.transpose(0, 2, 1, 3).reshape(B, G, ns, sb, dh)
    v_blocks = v_sel.transpose(0, 2, 1, 3).reshape(B, G, ns, sb, dh)
    wpad = ((0, 0), (0, 0), (NSA_WINDOW, 0), (0, 0))
    k_wp = jnp.pad(k_win.transpose(0, 2, 1, 3), wpad)
    v_wp = jnp.pad(v_win.transpose(0, 2, 1, 3), wpad)
    nqb = S // Q_BLOCK
    q_c = qg.transpose(0, 2, 3, 1, 4).reshape(B, G, R, nqb, Q_BLOCK, dh).transpose(3, 0, 1, 2, 4, 5)
    sel_c = sel.reshape(B, G, nqb, Q_BLOCK, kk).transpose(2, 0, 1, 3, 4)
    ok_c = sel_ok.reshape(B, G, nqb, Q_BLOCK, kk).transpose(2, 0, 1, 3, 4)
    bi = jnp.arange(B)[:, None, None, None]
    gi = jnp.arange(G)[None, :, None, None]

    def chunk(args):
        qc, selc, okc, ci = args
        tq = ci * Q_BLOCK + jnp.arange(Q_BLOCK)
        ks = k_blocks[bi, gi, selc].reshape(B, G, Q_BLOCK, kk * sb, dh)
        vs = v_blocks[bi, gi, selc].reshape(B, G, Q_BLOCK, kk * sb, dh)
        kpos = (selc[..., None] * sb + jnp.arange(sb)).reshape(B, G, Q_BLOCK, kk * sb)
        m = (kpos <= tq[:, None]) & jnp.repeat(okc, sb, axis=-1)
        s = jnp.einsum('bgrqd,bgqkd->bgrqk', qc, ks, preferred_element_type=F32) * scale
        p = jax.nn.softmax(jnp.where(m[:, :, None], s, NEG), axis=-1)
        o_s = jnp.einsum('bgrqk,bgqkd->bgrqd', p.astype(vs.dtype), vs)
        kw = lax.dynamic_slice_in_dim(k_wp, ci * Q_BLOCK, Q_BLOCK + NSA_WINDOW, axis=2)
        vw = lax.dynamic_slice_in_dim(v_wp, ci * Q_BLOCK, Q_BLOCK + NSA_WINDOW, axis=2)
        wpos = ci * Q_BLOCK - NSA_WINDOW + jnp.arange(Q_BLOCK + NSA_WINDOW)
        dist = tq[:, None] - wpos[None, :]
        wm = (dist >= 0) & (dist < NSA_WINDOW) & (wpos[None, :] >= 0)
        s_w = jnp.einsum('bgrqd,bgkd->bgrqk', qc, kw, preferred_element_type=F32) * scale
        p_w = jax.nn.softmax(jnp.where(wm, s_w, NEG), axis=-1)
        o_w = jnp.einsum('bgrqk,bgkd->bgrqd', p_w.astype(vw.dtype), vw)
        return o_s, o_w

    o_sel, o_win = lax.map(chunk, (q_c, sel_c, ok_c, jnp.arange(nqb)))
    o_sel = o_sel.transpose(1, 0, 4, 2, 3, 5).reshape(B, S, H, dh)
    o_win = o_win.transpose(1, 0, 4, 2, 3, 5).reshape(B, S, H, dh)
    g = jax.nn.sigmoid(gate_logits.astype(F32)).reshape(B, S, H, 3)
    out = g[..., 0:1] * o_cmp + g[..., 1:2] * o_sel + g[..., 2:3] * o_win
    return out.reshape(B, S, H * dh).astype(q.dtype)


def rg_lru_branch(x_in, gate_in, conv_w, conv_b, w_a, b_a, w_i, b_i, lam):
    B, S, C = x_in.shape
    xc = (causal_depthwise_conv(x_in, conv_w) + conv_b).astype(F32)
    xh = xc.reshape(B, S, LRU_HEADS, C // LRU_HEADS)
    r = jax.nn.sigmoid(jnp.einsum('bshi,hij->bshj', xh, w_a.astype(F32)) + b_a).reshape(B, S, C)
    i = jax.nn.sigmoid(jnp.einsum('bshi,hij->bshj', xh, w_i.astype(F32)) + b_i).reshape(B, S, C)
    log_a = -LRU_C * r * jax.nn.softplus(-lam.astype(F32))
    a = jnp.exp(log_a)
    b = jnp.sqrt(-jnp.expm1(2.0 * log_a)) * (i * xc)

    def combine(c1, c2):
        a1, b1 = c1
        a2, b2 = c2
        return a1 * a2, a2 * b1 + b2

    _, h = lax.associative_scan(combine, (a, b), axis=1)
    return (h * jax.nn.gelu(gate_in.astype(F32))).astype(x_in.dtype)


def short_conv_branch(x_in, b_gate, c_gate, conv_w):
    return b_gate * causal_depthwise_conv(c_gate * x_in, conv_w)


def cross_attention(h, mem_n, w_q, w_k, w_v, w_o):
    B, S, _ = h.shape
    M = mem_n.shape[1]
    q = dense(h, w_q).reshape(B, S, XA_HEADS, HEAD_DIM)
    k = dense(mem_n, w_k).reshape(B, M, XA_HEADS, HEAD_DIM)
    v = dense(mem_n, w_v).reshape(B, M, XA_HEADS, HEAD_DIM)
    s = jnp.einsum('bshd,bmhd->bhsm', q, k, preferred_element_type=F32) * HEAD_DIM ** -0.5
    p = jax.nn.softmax(s, axis=-1).astype(v.dtype)
    o = jnp.einsum('bhsm,bmhd->bshd', p, v).reshape(B, S, XA_WIDTH)
    return dense(o, w_o)


def hierarchical_moe(h, w_group, b_group, w_expert, b_expert, w_gate, w_up, w_down):
    B, S, D = h.shape
    t = h.reshape(B * S, D)
    g_logits = (t @ w_group + b_group).astype(F32)
    g_prob = jax.nn.softmax(g_logits, axis=-1)
    g_sel = jnp.argmax(g_logits, axis=-1)
    e_logits = (t @ w_expert + b_expert).astype(F32).reshape(-1, N_GROUPS, EXPERTS_PER_GROUP)
    e_in = jnp.take_along_axis(e_logits, g_sel[:, None, None], axis=1)[:, 0]
    top_v, top_i = lax.top_k(e_in, TOPK_IN_GROUP)
    p_g = jnp.take_along_axis(g_prob, g_sel[:, None], axis=1)
    w_top = jax.nn.softmax(top_v, axis=-1) * p_g
    eid = g_sel[:, None] * EXPERTS_PER_GROUP + top_i
    combine = jnp.sum(jax.nn.one_hot(eid, N_EXPERTS, dtype=F32) * w_top[..., None], axis=1)
    hid = jax.nn.silu(jnp.einsum('td,edf->tef', t, w_gate)) * jnp.einsum('td,edf->tef', t, w_up)
    out = jnp.einsum('tef,efd->td', hid * combine[:, :, None].astype(hid.dtype), w_down)
    return out.reshape(B, S, D)


def kernel(x, mem, positions, norm_mix, w_mix_in, lru_conv_w, lru_conv_b, lru_w_a, lru_b_a, lru_w_i, lru_b_i, lru_lambda, sc_conv_w, nsa_pe_k, nsa_w1_k, nsa_w2_k, nsa_pe_v, nsa_w1_v, nsa_w2_v, w_merge_gate, b_merge_gate, w_branch_out, w_mix_out, norm_xattn, norm_mem, xa_w_q, xa_w_k, xa_w_v, xa_w_o, norm_moe, moe_w_group, moe_b_group, moe_w_expert, moe_b_expert, moe_w_gate, moe_w_up, moe_w_down, norm_final):
    B, S, D = x.shape
    T = B * S
    cos, sin = rope_tables(positions)
    cos_t = jnp.concatenate([cos, cos], axis=-1).reshape(T, HEAD_DIM)
    sin_t = jnp.concatenate([-sin, sin], axis=-1).reshape(T, HEAD_DIM)
    offs = [int(v) for v in np.concatenate([[0], np.cumsum(MIX_SPLITS)])]
    blk = [v // HEAD_DIM for v in offs[:11]]
    n_head_cols = offs[10]
    rope_flags = np.zeros((n_head_cols // HEAD_DIM,), np.int32)
    for k in (1, 4, 6, 8):
        rope_flags[blk[k]:blk[k + 1]] = ROPE_PLAIN
    for k in (0, 3):
        rope_flags[blk[k]:blk[k + 1]] = ROPE_SCALED
    rope_flags = jnp.asarray(rope_flags)
    G, R = NSA_KV_HEADS, NSA_HEADS // NSA_KV_HEADS
    w_in_nk = jnp.swapaxes(w_mix_in, 1, 2)
    assert offs[10] % 8 == 0 and offs[11] % 8 == 0
    M = mem.shape[1]
    h = rms_norm_pallas(x.reshape(T, D), norm_mix[0], BF16)
    for l in range(DEPTH):
        x2 = x.reshape(T, D)
        z_head = matmul(h, w_in_nk, w_lead=(l,), w_is_nk=True, n_cols=n_head_cols, out_dtype=BF16,
                        rope=(cos_t, sin_t, rope_flags)).reshape(B, S, n_head_cols)
        z_tail = matmul(h, w_in_nk, w_lead=(l,), w_is_nk=True, row0=offs[11],
                        n_cols=offs[16] - offs[11]).reshape(B, S, -1)
        z_g = matmul(h, w_in_nk, w_lead=(l,), w_is_nk=True, row0=offs[10], n_cols=LANES)
        z_gate = jnp.pad(z_g[:, :NSA_GATE_W].reshape(B, S, G, R * 3),
                         ((0, 0), (0, 0), (0, 0), (0, LANES - R * 3))).reshape(B, S, G * LANES)
        o_moba = moba_attention_pallas(z_head, z_head, z_head, n_heads=MOBA_HEADS,
                                       q_off=blk[0], k_off=blk[1], v_off=blk[2])
        kc = nsa_compress_pallas(z_head[:, :, offs[4]:offs[5]], nsa_pe_k[l], nsa_w1_k[l], nsa_w2_k[l])
        vc = nsa_compress_pallas(z_head[:, :, offs[5]:offs[6]], nsa_pe_v[l], nsa_w1_v[l], nsa_w2_v[l])
        o_nsa = nsa_attention_pallas(z_head, kc, vc, z_head, z_head, z_head, z_head, z_gate,
                                     q_off=blk[3], ks_off=blk[6], vs_off=blk[7], kw_off=blk[8], vw_off=blk[9])
        o_lru, o_conv = lru_conv_pallas(z_tail, l, lru_conv_w, lru_conv_b, lru_w_a, lru_b_a,
                                        lru_w_i, lru_b_i, lru_lambda, sc_conv_w)
        o_all = jnp.stack([o_moba, o_lru, o_conv, o_nsa]).reshape(N_BRANCH, T, -1)
        merged = merge_branches_pallas(h, o_all, w_merge_gate, b_merge_gate, w_branch_out, l)
        x2 = matmul(merged, w_mix_out, w_lead=(l,), residual=x2)
        mem_n = rms_norm_pallas(mem.reshape(B * M, D), norm_mem[l], BF16)
        xk = matmul(mem_n, xa_w_k, w_lead=(l,), out_dtype=BF16).reshape(B, M, XA_WIDTH)
        xv = matmul(mem_n, xa_w_v, w_lead=(l,), out_dtype=BF16).reshape(B, M, XA_WIDTH)
        x = cross_attention_pallas(x2.reshape(B, S, D), norm_xattn[l], xk, xv,
                                   xa_w_q[l].astype(BF16), xa_w_o[l].astype(BF16))
        last = l == DEPTH - 1
        x, h = hierarchical_moe_pallas(x, norm_moe[l], moe_w_group[l], moe_b_group[l],
                                       moe_w_expert[l], moe_b_expert[l], moe_w_gate,
                                       moe_w_up, moe_w_down, l,
                                       norm_final if last else norm_mix[l + 1], F32 if last else BF16)
    return h.reshape(B, S, D)
```

```python
import functools

import jax
import jax.numpy as jnp
import numpy as np
from jax import lax
from jax.experimental import pallas as pl
from jax.experimental.pallas import tpu as pltpu

F32 = jnp.float32
BF16 = jnp.bfloat16

D_MODEL = 4096
DEPTH = 2
HEAD_DIM = 128
ROPE_THETA = 10000.0
NORM_EPS = 1e-6
NEG = -1e30
N_BRANCH = 4
BRANCH_WIDTH = D_MODEL // 4
MOBA_HEADS = BRANCH_WIDTH // HEAD_DIM
MOBA_BLOCK = 256
MOBA_TOPK = 3
MOBA_Q_BLOCK = 64
LRU_WIDTH = BRANCH_WIDTH
LRU_HEADS = LRU_WIDTH // HEAD_DIM
LRU_C = 8.0
SC_WIDTH = BRANCH_WIDTH
NSA_HEADS = BRANCH_WIDTH // HEAD_DIM
NSA_KV_HEADS = NSA_HEADS // 4
NSA_CMP_BLOCK = 32
NSA_CMP_STRIDE = 16
NSA_SEL_BLOCK = 64
NSA_SEL_TOPK = 16
NSA_WINDOW = 512
NSA_FORCE_BONUS = 1e4
Q_BLOCK = 128
XA_HEADS = 4
XA_WIDTH = XA_HEADS * HEAD_DIM
N_GROUPS = 4
EXPERTS_PER_GROUP = 8
N_EXPERTS = N_GROUPS * EXPERTS_PER_GROUP
TOPK_IN_GROUP = 2
EXPERT_FF = D_MODEL // 8
MOBA_W = MOBA_HEADS * HEAD_DIM
NSA_Q_W = NSA_HEADS * HEAD_DIM
NSA_KV_W = NSA_KV_HEADS * HEAD_DIM
NSA_GATE_W = NSA_HEADS * 3
MIX_SPLITS = (MOBA_W, MOBA_W, MOBA_W,
              NSA_Q_W, NSA_KV_W, NSA_KV_W, NSA_KV_W, NSA_KV_W, NSA_KV_W, NSA_KV_W, NSA_GATE_W,
              LRU_WIDTH, LRU_WIDTH,
              SC_WIDTH, SC_WIDTH, SC_WIDTH)

VMEM_LIMIT_BYTES = 56 * 1024 * 1024


ROPE_PLAIN, ROPE_SCALED = 1, 2


def _mm_kernel(*refs, rope, residual, w_is_nk):
    if rope:
        flags_ref, refs = refs[0], refs[1:]
    a_ref, w_ref = refs[0], refs[1]
    o_ref = refs[-1]
    dims = (((1,), (1,)), ((), ())) if w_is_nk else (((1,), (0,)), ((), ()))
    a = a_ref[...].astype(BF16)
    if rope:
        cos, sin = refs[2][...], refs[3][...]
        tn = o_ref.shape[1]
        step = 2 * HEAD_DIM
        for c0 in range(0, tn, step):
            w_blk = w_ref[c0:c0 + step, :] if w_is_nk else w_ref[:, c0:c0 + step]
            acc = lax.dot_general(a, w_blk.astype(BF16), dims, preferred_element_type=F32)
            for c in range(c0, c0 + step, HEAD_DIM):
                blk = acc[:, c - c0:c - c0 + HEAD_DIM]
                flag = flags_ref[(pl.program_id(1) * tn + c) // HEAD_DIM]
                scl = jnp.where(flag == ROPE_SCALED, HEAD_DIM ** -0.5, 1.0)
                cos_c = jnp.where(flag != 0, cos * scl, 1.0)
                sin_c = jnp.where(flag != 0, sin * scl, 0.0)
                rot = pltpu.roll(blk, HEAD_DIM // 2, 1)
                o_ref[:, c:c + HEAD_DIM] = (blk * cos_c + rot * sin_c).astype(o_ref.dtype)
        return
    acc = lax.dot_general(a, w_ref[...].astype(BF16), dims, preferred_element_type=F32)
    if residual:
        acc = acc + refs[2][...]
    o_ref[...] = acc.astype(o_ref.dtype)


def matmul(a, w, *, w_lead=(), w_is_nk=False, col0=0, row0=None, n_cols=None, out_dtype=F32, rope=None,
           residual=None, tm=1024, tn=512):
    m, k = a.shape
    n = w.shape[-2 if w_is_nk else -1] if n_cols is None else n_cols
    assert w.shape[-1 if w_is_nk else -2] == k and not (rope is not None and residual is not None)
    tm = min(tm, m)
    tn = min(tn, n)
    lead = tuple(int(v) for v in w_lead)
    if w_is_nk and row0 is not None:
        rows_per_slab = w.shape[-2]
        slab = int(np.ravel_multi_index(lead, w.shape[:-2])) if lead else 0
        w = w.reshape(-1, k)
        w_spec = pl.BlockSpec((pl.Element(tn), pl.Element(k)),
                              lambda i, j, *_: (pl.multiple_of(slab * rows_per_slab + row0 + j * tn, 8), 0))
    elif w_is_nk:
        w_spec = pl.BlockSpec((None,) * len(lead) + (tn, k), lambda i, j, *_: lead + (j + col0, 0))
    else:
        w_spec = pl.BlockSpec((None,) * len(lead) + (k, tn), lambda i, j, *_: lead + (0, j + col0))
    in_specs = [pl.BlockSpec((tm, k), lambda i, j, *_: (i, 0)), w_spec]
    args = [a, w]
    prefetch = []
    if rope is not None:
        cos, sin, flags = rope
        assert n % tn == 0 and flags.shape == (n // HEAD_DIM,)
        in_specs += [pl.BlockSpec((tm, HEAD_DIM), lambda i, j, *_: (i, 0))] * 2
        args += [cos, sin]
        prefetch = [flags]
    if residual is not None:
        in_specs.append(pl.BlockSpec((tm, tn), lambda i, j, *_: (i, j)))
        args.append(residual)
    return pl.pallas_call(
        functools.partial(_mm_kernel, rope=rope is not None, residual=residual is not None,
                          w_is_nk=w_is_nk),
        grid_spec=pltpu.PrefetchScalarGridSpec(
            num_scalar_prefetch=len(prefetch),
            grid=(pl.cdiv(m, tm), pl.cdiv(n, tn)),
            in_specs=in_specs,
            out_specs=pl.BlockSpec((tm, tn), lambda i, j, *_: (i, j))),
        out_shape=jax.ShapeDtypeStruct((m, n), out_dtype),
        compiler_params=pltpu.CompilerParams(
            dimension_semantics=("parallel", "parallel"),
            vmem_limit_bytes=VMEM_LIMIT_BYTES),
        name="matmul",
    )(*prefetch, *args)


def _rms_kernel(x_ref, g_ref, o_ref):
    x = x_ref[...]
    y = x * lax.rsqrt(jnp.mean(x * x, axis=-1, keepdims=True) + NORM_EPS)
    o_ref[...] = (y * g_ref[...]).astype(o_ref.dtype)


def rms_norm_pallas(x2, gain, out_dtype, *, tm=512):
    m, d = x2.shape
    tm = min(tm, m)
    return pl.pallas_call(
        _rms_kernel,
        grid=(m // tm,),
        in_specs=[pl.BlockSpec((tm, d), lambda i: (i, 0)), pl.BlockSpec((1, d), lambda i: (0, 0))],
        out_specs=pl.BlockSpec((tm, d), lambda i: (i, 0)),
        out_shape=jax.ShapeDtypeStruct((m, d), out_dtype),
        compiler_params=pltpu.CompilerParams(
            dimension_semantics=("parallel",), vmem_limit_bytes=VMEM_LIMIT_BYTES),
        name="rms_norm",
    )(x2, gain.astype(F32)[None, :])


def _merge_kernel(h_ref, o_ref, wg_ref, bg_ref, wu_ref, out_ref, acc_ref):
    n = pl.program_id(2)

    @pl.when((pl.program_id(0) == 0) & (pl.program_id(1) == 0) & (n == 0))
    def _():
        acc_ref[...] = jnp.zeros_like(acc_ref)

    h, o = h_ref[...], o_ref[...]
    half = out_ref.shape[1] // 2
    for c0 in (0, half):
        cs = slice(c0, c0 + half)
        gate = jax.nn.sigmoid(
            jnp.dot(h, wg_ref[:, cs].astype(BF16), preferred_element_type=F32) + bg_ref[:, cs])
        term = gate * jnp.dot(o, wu_ref[:, cs].astype(BF16), preferred_element_type=F32)
        acc = jnp.where(n == 0, 0.0, acc_ref[:, cs]) + term
        acc_ref[:, cs] = acc
        out_ref[:, cs] = acc.astype(out_ref.dtype)


def merge_branches_pallas(h, o_all, w_gate, b_gate, w_up, layer, *, tm=1024, tn=512):
    T, D = h.shape
    N, _, W = o_all.shape
    l = int(layer)
    return pl.pallas_call(
        _merge_kernel,
        grid=(T // tm, D // tn, N),
        in_specs=[pl.BlockSpec((tm, D), lambda i, j, n: (i, 0)),
                  pl.BlockSpec((None, tm, W), lambda i, j, n: (n, i, 0)),
                  pl.BlockSpec((None, None, D, tn), lambda i, j, n: (l, n, 0, j)),
                  pl.BlockSpec((None, None, 1, tn), lambda i, j, n: (l, n, 0, j)),
                  pl.BlockSpec((None, None, W, tn), lambda i, j, n: (l, n, 0, j))],
        out_specs=pl.BlockSpec((tm, tn), lambda i, j, n: (i, j)),
        out_shape=jax.ShapeDtypeStruct((T, D), BF16),
        scratch_shapes=[pltpu.VMEM((tm, tn), F32)],
        compiler_params=pltpu.CompilerParams(
            dimension_semantics=("parallel", "parallel", "arbitrary"),
            vmem_limit_bytes=VMEM_LIMIT_BYTES),
        name="merge_branches",
    )(h, o_all, w_gate, b_gate[:, :, None, :], w_up)


LRU_TS = 256
LRU_CW = 256


def _rows_back(x, prev, s, row):
    return jnp.where(row < s, pltpu.roll(prev, s, 0), pltpu.roll(x, s, 0))


def _lane_window(blocks, shift):
    if shift == 0:
        return jnp.concatenate([b[...] for b in blocks], axis=-1)
    back = LANES - shift
    rolled = [pltpu.roll(b[...], back, 1) for b in blocks]
    lane = lax.broadcasted_iota(jnp.int32, rolled[0].shape, 1)
    return jnp.concatenate([jnp.where(lane < back, rolled[j], rolled[j + 1])
                            for j in range(len(blocks) - 1)], axis=-1)


def _lru_conv_kernel(*refs, n_lane_blocks, shift):
    nb = n_lane_blocks
    rx, rg, cb, cc, cx = (_lane_window(refs[k * nb:(k + 1) * nb], shift) for k in range(5))
    (lw_ref, lb_ref, wa_ref, ba_ref, wi_ref, bi_ref, lam_ref, sw_ref,
     olru_ref, oconv_ref, px_ref, py_ref, h_ref) = refs[5 * nb:]

    @pl.when(pl.program_id(2) == 0)
    def _():
        px_ref[...] = jnp.zeros_like(px_ref)
        py_ref[...] = jnp.zeros_like(py_ref)
        h_ref[...] = jnp.zeros_like(h_ref)

    ts, cw = px_ref.shape
    row = lax.broadcasted_iota(jnp.int32, (ts, cw), 0)

    x = rx
    prev = px_ref[...]
    nk = lw_ref.shape[0]
    xc = lb_ref[...] + lw_ref[nk - 1:nk, :] * x
    for s in range(1, nk):
        xc = xc + lw_ref[nk - 1 - s:nk - s, :] * _rows_back(x, prev, s, row)
    px_ref[...] = x
    r_parts, i_parts = [], []
    for hh in range(cw // HEAD_DIM):
        cs = slice(hh * HEAD_DIM, (hh + 1) * HEAD_DIM)
        xh = xc[:, cs].astype(BF16)
        r_parts.append(jnp.dot(xh, wa_ref[hh].astype(BF16), preferred_element_type=F32))
        i_parts.append(jnp.dot(xh, wi_ref[hh].astype(BF16), preferred_element_type=F32))
    r = jax.nn.sigmoid(jnp.concatenate(r_parts, axis=-1) + ba_ref[...])
    gi = jax.nn.sigmoid(jnp.concatenate(i_parts, axis=-1) + bi_ref[...])
    log_a = -LRU_C * r * jax.nn.softplus(-lam_ref[...])
    a = jnp.exp(log_a)
    b = jnp.sqrt(1.0 - a * a) * (gi * xc)
    d = 1
    while d < ts:
        a_back = jnp.where(row < d, 1.0, pltpu.roll(a, d, 0))
        b_back = jnp.where(row < d, 0.0, pltpu.roll(b, d, 0))
        b = a * b_back + b
        a = a * a_back
        d *= 2
    h = b + a * h_ref[...]
    h_ref[...] = h[ts - 1:ts, :]
    olru_ref[...] = (h * jax.nn.gelu(rg)).astype(olru_ref.dtype)

    y = cc * cx
    prev_y = py_ref[...]
    nk = sw_ref.shape[0]
    conv = sw_ref[nk - 1:nk, :] * y
    for s in range(1, nk):
        conv = conv + sw_ref[nk - 1 - s:nk - s, :] * _rows_back(y, prev_y, s, row)
    py_ref[...] = y
    oconv_ref[...] = (cb * conv).astype(oconv_ref.dtype)


def lru_conv_pallas(z, layer, lru_conv_w, lru_conv_b, lru_w_a, lru_b_a, lru_w_i, lru_b_i, lru_lambda,
                    sc_conv_w, *, shift=0):
    B, S, _ = z.shape
    C = lru_conv_b.shape[-1]
    ts, cw = LRU_TS, LRU_CW
    nblk = C // cw
    l = int(layer)
    per_blk = cw // LANES
    nb = per_blk + (1 if shift else 0)
    lane_blk = lambda k, j: pl.BlockSpec(
        (None, ts, LANES), lambda b, c, t: (b, t, (k * nblk + c) * per_blk + j))
    cols = [lane_blk(k, j) for k in range(5) for j in range(nb)]
    par = lambda rows: pl.BlockSpec((None, rows, cw), lambda b, c, t: (l, 0, c))
    hw = pl.BlockSpec((None, cw // HEAD_DIM, HEAD_DIM, HEAD_DIM), lambda b, c, t: (l, c, 0, 0))
    L = lru_conv_b.shape[0]
    flat = lambda p: p.reshape(L, 1, C)
    out = pl.BlockSpec((None, ts, cw), lambda b, c, t: (b, t, c))
    return pl.pallas_call(
        functools.partial(_lru_conv_kernel, n_lane_blocks=nb, shift=shift),
        grid=(B, nblk, S // ts),
        in_specs=cols + [par(lru_conv_w.shape[1]), par(1), hw, par(1), hw, par(1), par(1),
                         par(sc_conv_w.shape[1])],
        out_specs=[out, out],
        out_shape=[jax.ShapeDtypeStruct((B, S, C), BF16)] * 2,
        scratch_shapes=[pltpu.VMEM((ts, cw), F32), pltpu.VMEM((ts, cw), F32), pltpu.VMEM((1, cw), F32)],
        compiler_params=pltpu.CompilerParams(
            dimension_semantics=("parallel", "parallel", "arbitrary"),
            vmem_limit_bytes=VMEM_LIMIT_BYTES),
        name="lru_conv",
    )(*([z] * len(cols)), lru_conv_w, flat(lru_conv_b), lru_w_a, flat(lru_b_a), lru_w_i, flat(lru_b_i),
      flat(lru_lambda), sc_conv_w)


XA_TM = 256


def _xattn_kernel(x_ref, g_ref, wq_ref, k_ref, v_ref, wo_ref, o_ref, *, scale):
    x = x_ref[...]
    xn = (x * lax.rsqrt(jnp.mean(x * x, axis=-1, keepdims=True) + NORM_EPS) * g_ref[...]).astype(BF16)
    q = jnp.dot(xn, wq_ref[...], preferred_element_type=F32).astype(BF16)
    heads = []
    for hh in range(XA_HEADS):
        cs = slice(hh * HEAD_DIM, (hh + 1) * HEAD_DIM)
        s = lax.dot_general(q[:, cs], k_ref[:, cs], (((1,), (1,)), ((), ())),
                            preferred_element_type=F32) * scale
        e = jnp.exp(s - jnp.max(s, axis=-1, keepdims=True))
        p = (e / jnp.sum(e, axis=-1, keepdims=True)).astype(BF16)
        heads.append(jnp.dot(p, v_ref[:, cs], preferred_element_type=F32))
    o = jnp.concatenate(heads, axis=-1).astype(BF16)
    o_ref[...] = x + jnp.dot(o, wo_ref[...], preferred_element_type=F32)


def cross_attention_pallas(x, gain, k, v, w_q, w_o):
    B, S, D = x.shape
    M = k.shape[1]
    tm = XA_TM
    per_b = S // tm
    kv = pl.BlockSpec((None, M, XA_WIDTH), lambda i: (i // per_b, 0, 0))
    return pl.pallas_call(
        functools.partial(_xattn_kernel, scale=HEAD_DIM ** -0.5),
        grid=(B * per_b,),
        in_specs=[pl.BlockSpec((tm, D), lambda i: (i, 0)),
                  pl.BlockSpec((1, D), lambda i: (0, 0)),
                  pl.BlockSpec((D, XA_WIDTH), lambda i: (0, 0)),
                  kv, kv,
                  pl.BlockSpec((XA_WIDTH, D), lambda i: (0, 0))],
        out_specs=pl.BlockSpec((tm, D), lambda i: (i, 0)),
        out_shape=jax.ShapeDtypeStruct((B * S, D), F32),
        compiler_params=pltpu.CompilerParams(
            dimension_semantics=("parallel",), vmem_limit_bytes=VMEM_LIMIT_BYTES),
        name="cross_attention",
    )(x.reshape(B * S, D), gain.astype(F32)[None, :], w_q, k, v, w_o).reshape(B, S, D)


MASK_BIG = 1e30
LANES = 128


def _first_index_topk_mask(work, colf, k):
    sel = jnp.zeros(work.shape, F32)
    for _ in range(k):
        mx = jnp.max(work, axis=-1, keepdims=True)
        idx = jnp.min(jnp.where(work == mx, colf, float(LANES)), axis=-1, keepdims=True)
        pick = colf == idx
        sel = jnp.where(pick, 1.0, sel)
        work = jnp.where(pick, -jnp.inf, work)
    return sel


def _first_index_topk_mask_t(work, rowf, k):
    n = work.shape[0]
    sel = jnp.zeros(work.shape, F32)
    for _ in range(k):
        mx = jnp.max(work, axis=0, keepdims=True)
        idx = jnp.min(jnp.where(work == mx, rowf, float(n)), axis=0, keepdims=True)
        pick = rowf == idx
        sel = jnp.where(pick, 1.0, sel)
        work = jnp.where(pick, -jnp.inf, work)
    return sel


ATT_TK = 512
MOBA_HEADS_PER_STEP = 4
M_INIT = -1e38
NT_DIMS = (((1,), (1,)), ((), ()))


def _flash_update(q_aug, k_aug, v_aug, m_ref, acc_ref, causal):
    s = lax.dot_general(q_aug, k_aug, NT_DIMS, preferred_element_type=F32)
    if causal is not None:
        q0, k0 = causal
        row = lax.broadcasted_iota(jnp.int32, s.shape, 0)
        col = lax.broadcasted_iota(jnp.int32, s.shape, 1)
        s = jnp.where(k0 + col <= q0 + row, s, NEG)
    m_prev = m_ref[...]
    m_new = jnp.maximum(m_prev, jnp.max(s, axis=-1, keepdims=True))
    alpha = jnp.exp(m_prev - m_new)
    p = jnp.exp(s - m_new).astype(BF16)
    acc_ref[...] = alpha * acc_ref[...] + jnp.dot(p, v_aug, preferred_element_type=F32)
    m_ref[...] = m_new


def _moba_kernel(q_ref, k_ref, v_ref, o_ref, kmean_ref, qa_ref, m_ref, acc_ref):
    i = pl.program_id(2)
    bs = MOBA_BLOCK
    seq = k_ref.shape[0]
    hp = q_ref.shape[1] // HEAD_DIM
    nt = NT_DIMS
    hcol = lambda h: slice(h * HEAD_DIM, (h + 1) * HEAD_DIM)

    @pl.when(i == 0)
    def _():
        row = lax.broadcasted_iota(jnp.int32, (LANES, seq), 0)
        col = lax.broadcasted_iota(jnp.int32, (LANES, seq), 1)
        ind = jnp.where(col // bs == row, 1.0, 0.0).astype(BF16)
        kmean_ref[...] = (jnp.dot(ind, k_ref[...], preferred_element_type=F32)
                          * (1.0 / bs)).astype(BF16)

    nb = seq // bs
    nbp = -(-nb // 8) * 8
    blk_t = lax.broadcasted_iota(jnp.int32, (nbp, bs), 0)
    past_t = blk_t < i
    for h in range(hp):
        q = q_ref[:, hcol(h)]
        gate_t = lax.dot_general(kmean_ref[0:nbp, hcol(h)], q, nt, preferred_element_type=F32)
        sel_t = _first_index_topk_mask_t(jnp.where(past_t, gate_t, NEG), blk_t.astype(F32), MOBA_TOPK)
        notsel_t = jnp.where(((sel_t > 0.0) & past_t) | (blk_t == i), 0.0, 1.0)
        if nbp < LANES:
            notsel_t = jnp.concatenate([notsel_t, jnp.ones((LANES - nbp, bs), F32)], axis=0)
        qa_ref[h] = jnp.concatenate([q, notsel_t.T.astype(BF16)], axis=-1)

    tk = ATT_TK
    blk_per_tile = tk // bs
    n_tiles = (i + blk_per_tile) // blk_per_tile
    key_blk = lax.broadcasted_iota(jnp.int32, (tk, LANES), 0) // bs
    key_col = lax.broadcasted_iota(jnp.int32, (tk, LANES), 1)
    ones_v = jnp.ones((tk, LANES), BF16)
    m_ref[...] = jnp.full(m_ref.shape, M_INIT, F32)
    acc_ref[...] = jnp.zeros_like(acc_ref)

    def tile(jt, causal):
        off = pl.multiple_of(jt * tk, tk)
        bias = jnp.where(key_col == jt * blk_per_tile + key_blk, -MASK_BIG, 0.0).astype(BF16)
        for h in range(hp):
            k_aug = jnp.concatenate([k_ref[pl.ds(off, tk), hcol(h)], bias], axis=-1)
            v_aug = jnp.concatenate([v_ref[pl.ds(off, tk), hcol(h)], ones_v], axis=-1)
            _flash_update(qa_ref[h], k_aug, v_aug, m_ref.at[h], acc_ref.at[h],
                          (i * bs, off) if causal else None)

    def body(jt, carry):
        tile(jt, False)
        return carry

    lax.fori_loop(0, n_tiles - 1, body, 0)
    tile(n_tiles - 1, True)
    for h in range(hp):
        acc = acc_ref[h]
        o_ref[:, hcol(h)] = (acc[:, :HEAD_DIM] / acc[:, HEAD_DIM:]).astype(o_ref.dtype)


def moba_attention_pallas(zq, zk, zv, *, n_heads, q_off=0, k_off=0, v_off=0):
    B, S, _ = zq.shape
    bs = MOBA_BLOCK
    hp = min(MOBA_HEADS_PER_STEP, n_heads)
    hw = hp * HEAD_DIM
    assert S % ATT_TK == 0 and S // bs <= LANES and n_heads % hp == 0
    assert q_off % hp == 0 and k_off % hp == 0 and v_off % hp == 0
    return pl.pallas_call(
        _moba_kernel,
        grid=(B, n_heads // hp, S // bs),
        in_specs=[pl.BlockSpec((None, bs, hw), lambda b, h, i: (b, i, q_off // hp + h)),
                  pl.BlockSpec((None, S, hw), lambda b, h, i: (b, 0, k_off // hp + h)),
                  pl.BlockSpec((None, S, hw), lambda b, h, i: (b, 0, v_off // hp + h))],
        out_specs=pl.BlockSpec((None, bs, hw), lambda b, h, i: (b, i, h)),
        out_shape=jax.ShapeDtypeStruct((B, S, n_heads * HEAD_DIM), BF16),
        scratch_shapes=[pltpu.VMEM((LANES, hw), BF16), pltpu.VMEM((hp, bs, 2 * HEAD_DIM), BF16),
                        pltpu.VMEM((hp, bs, 1), F32), pltpu.VMEM((hp, bs, 2 * HEAD_DIM), F32)],
        compiler_params=pltpu.CompilerParams(
            dimension_semantics=("parallel", "parallel", "arbitrary"),
            vmem_limit_bytes=VMEM_LIMIT_BYTES),
        name="moba_attention",
    )(zq, zk, zv)


NSA_TQ = 256
NSA_NC_PAD = 256
CMP_ROW = NSA_CMP_STRIDE * HEAD_DIM


def _nsa_compress_kernel(x_ref, pe_ref, w1_ref, w2_ref, o_ref):
    x = x_ref[...].astype(F32)
    top = (x + pe_ref[0:1, :]).astype(BF16)
    bot = (x + pe_ref[1:2, :]).astype(BF16)
    a = jnp.dot(top, w1_ref[0:CMP_ROW, :], preferred_element_type=F32)
    b = jnp.dot(bot, w1_ref[CMP_ROW:2 * CMP_ROW, :], preferred_element_type=F32)
    pre = a + pltpu.roll(b, b.shape[0] - 1, 0)
    hid = jax.nn.gelu(pre)
    o_ref[...] = jnp.dot(hid.astype(BF16), w2_ref[...], preferred_element_type=F32).astype(o_ref.dtype)


def nsa_compress_pallas(x, pe, w1, w2):
    B, S, gw = x.shape
    G = gw // HEAD_DIM
    nrow = S // NSA_CMP_STRIDE
    xr = x.reshape(B, nrow, NSA_CMP_STRIDE, G, HEAD_DIM).transpose(0, 3, 1, 2, 4).reshape(B, G, nrow, CMP_ROW)
    pe2 = pe.astype(F32).reshape(2, CMP_ROW)
    return pl.pallas_call(
        _nsa_compress_kernel,
        grid=(B, G),
        in_specs=[pl.BlockSpec((None, None, nrow, CMP_ROW), lambda b, g: (b, g, 0, 0)),
                  pl.BlockSpec((2, CMP_ROW), lambda b, g: (0, 0)),
                  pl.BlockSpec((2 * CMP_ROW, HEAD_DIM), lambda b, g: (0, 0)),
                  pl.BlockSpec((HEAD_DIM, HEAD_DIM), lambda b, g: (0, 0))],
        out_specs=pl.BlockSpec((None, None, nrow, HEAD_DIM), lambda b, g: (b, g, 0, 0)),
        out_shape=jax.ShapeDtypeStruct((B, G, nrow, HEAD_DIM), BF16),
        compiler_params=pltpu.CompilerParams(
            dimension_semantics=("parallel", "parallel"), vmem_limit_bytes=VMEM_LIMIT_BYTES),
        name="nsa_compress",
    )(xr, pe2, w1.astype(BF16), w2.astype(BF16))


def _nsa_kernel(q_ref, kc_ref, vc_ref, ks_ref, vs_ref, kw_ref, vw_ref, g_ref, wmap_ref, o_ref,
                m_ref, acc_ref, out_ref):
    i = pl.program_id(2)
    tq = NSA_TQ
    R = NSA_HEADS // NSA_KV_HEADS
    nt = (((1,), (1,)), ((), ()))
    row = lax.broadcasted_iota(jnp.int32, (tq, tq), 0)
    col = lax.broadcasted_iota(jnp.int32, (tq, tq), 1)
    t_abs = i * tq + row
    gates = jax.nn.sigmoid(g_ref[...])

    def qh(r):
        return q_ref[:, r * HEAD_DIM:(r + 1) * HEAD_DIM]

    ncp = kc_ref.shape[0]
    cmask = (lax.broadcasted_iota(jnp.int32, (tq, ncp), 1) * NSA_CMP_STRIDE + (NSA_CMP_BLOCK - 1)
             <= i * tq + lax.broadcasted_iota(jnp.int32, (tq, ncp), 0))
    imp = jnp.zeros((tq, LANES), F32)
    vc_aug = jnp.concatenate([vc_ref[...], jnp.ones((ncp, LANES), BF16), wmap_ref[...]], axis=-1)
    for r in range(R):
        s = lax.dot_general(qh(r), kc_ref[...], nt, preferred_element_type=F32)
        s = jnp.where(cmask, s, NEG)
        e = jnp.where(cmask, jnp.exp(s - jnp.max(s, axis=-1, keepdims=True)), 0.0).astype(BF16)
        acc = jnp.dot(e, vc_aug, preferred_element_type=F32)
        l = acc[:, HEAD_DIM:2 * HEAD_DIM]
        inv = jnp.where(l > 0.0, 1.0 / l, 0.0)
        imp = imp + acc[:, 2 * HEAD_DIM:] * inv
        out_ref[r] = gates[:, 3 * r:3 * r + 1] * (acc[:, :HEAD_DIM] * inv)

    ones_w = jnp.ones((tq, LANES), BF16)
    w_tiles = []
    for d in range(NSA_WINDOW // tq + 1):
        off = pl.multiple_of(jnp.maximum(i - d, 0) * tq, tq)
        gone = jnp.where(i >= d, 0, 2 * NSA_WINDOW + tq)
        dist = d * tq + row - col + gone
        w_tiles.append((off, (dist >= 0) & (dist < NSA_WINDOW)))
    for r in range(R):
        ss = []
        for off, mask in w_tiles:
            s = lax.dot_general(qh(r), kw_ref[pl.ds(off, tq), :], nt, preferred_element_type=F32)
            ss.append(jnp.where(mask, s, NEG))
        m = jnp.max(ss[0], axis=-1, keepdims=True)
        for s in ss[1:]:
            m = jnp.maximum(m, jnp.max(s, axis=-1, keepdims=True))
        acc = jnp.zeros((tq, 2 * HEAD_DIM), F32)
        for (off, _), s in zip(w_tiles, ss):
            v_aug = jnp.concatenate([vw_ref[pl.ds(off, tq), :], ones_w], axis=-1)
            acc = acc + jnp.dot(jnp.exp(s - m).astype(BF16), v_aug, preferred_element_type=F32)
        out_ref[r] = out_ref[r] + gates[:, 3 * r + 2:3 * r + 3] * (acc[:, :HEAD_DIM] / acc[:, HEAD_DIM:])

    ns = ks_ref.shape[0] // NSA_SEL_BLOCK
    blk_t = lax.broadcasted_iota(jnp.int32, (ns, tq), 0)
    q_blk_t = (i * tq + lax.broadcasted_iota(jnp.int32, (ns, tq), 1)) // NSA_SEL_BLOCK
    valid_t = blk_t <= q_blk_t
    forced_t = (blk_t == 0) | (blk_t >= q_blk_t - 1)
    impm_t = jnp.where(valid_t, imp.T[:ns] + jnp.where(forced_t, NSA_FORCE_BONUS, 0.0), NEG)
    sel_t = _first_index_topk_mask_t(impm_t, blk_t.astype(F32), NSA_SEL_TOPK)
    notsel_t = jnp.where((sel_t > 0.0) & valid_t, 0.0, 1.0)
    if ns < LANES:
        notsel_t = jnp.concatenate([notsel_t, jnp.ones((LANES - ns, tq), F32)], axis=0)
    notsel = notsel_t.T.astype(BF16)

    tk = ATT_TK
    blk_per_tile = tk // NSA_SEL_BLOCK
    n_tiles = (i * tq) // tk + 1
    key_blk = lax.broadcasted_iota(jnp.int32, (tk, LANES), 0) // NSA_SEL_BLOCK
    key_col = lax.broadcasted_iota(jnp.int32, (tk, LANES), 1)
    ones_v = jnp.ones((tk, LANES), BF16)
    m_ref[...] = jnp.full(m_ref.shape, M_INIT, F32)
    acc_ref[...] = jnp.zeros_like(acc_ref)

    def sel_tile(jt, causal):
        off = pl.multiple_of(jt * tk, tk)
        bias = jnp.where(key_col == jt * blk_per_tile + key_blk, -MASK_BIG, 0.0).astype(BF16)
        k_aug = jnp.concatenate([ks_ref[pl.ds(off, tk), :], bias], axis=-1)
        v_aug = jnp.concatenate([vs_ref[pl.ds(off, tk), :], ones_v], axis=-1)
        for r in range(R):
            q_aug = jnp.concatenate([qh(r), notsel], axis=-1)
            _flash_update(q_aug, k_aug, v_aug, m_ref.at[r], acc_ref.at[r], (i * tq, off) if causal else None)

    def body(jt, carry):
        sel_tile(jt, False)
        return carry

    lax.fori_loop(0, n_tiles - 1, body, 0)
    sel_tile(n_tiles - 1, True)
    for r in range(R):
        acc = acc_ref[r]
        o_s = acc[:, :HEAD_DIM] / acc[:, HEAD_DIM:]
        o_ref[:, r * HEAD_DIM:(r + 1) * HEAD_DIM] = (
            out_ref[r] + gates[:, 3 * r + 1:3 * r + 2] * o_s).astype(o_ref.dtype)


def nsa_attention_pallas(zq, kc, vc, zks, zvs, zkw, zvw, gate_logits, *, q_off, ks_off, vs_off, kw_off, vw_off):
    B, S, _ = zq.shape
    G, tq = NSA_KV_HEADS, NSA_TQ
    R = NSA_HEADS // G
    assert S % ATT_TK == 0 and ATT_TK % tq == 0 and S // NSA_SEL_BLOCK <= LANES
    nc = (S - NSA_CMP_BLOCK) // NSA_CMP_STRIDE + 1
    ncp = kc.shape[2]
    ns = S // NSA_SEL_BLOCK
    wmap = jnp.pad(cmp_to_sel_weights(nc, ns), ((0, ncp - nc), (0, LANES - ns))).astype(BF16)
    rw = R * HEAD_DIM
    kv_spec = lambda off: pl.BlockSpec((None, S, HEAD_DIM), lambda b, g, i: (b, 0, off + g))
    c_spec = pl.BlockSpec((None, None, ncp, HEAD_DIM), lambda b, g, i: (b, g, 0, 0))
    return pl.pallas_call(
        _nsa_kernel,
        grid=(B, G, S // tq),
        in_specs=[pl.BlockSpec((None, tq, rw), lambda b, g, i: (b, i, q_off // R + g)),
                  c_spec, c_spec, kv_spec(ks_off), kv_spec(vs_off), kv_spec(kw_off), kv_spec(vw_off),
                  pl.BlockSpec((None, tq, LANES), lambda b, g, i: (b, i, g)),
                  pl.BlockSpec((ncp, LANES), lambda b, g, i: (0, 0))],
        out_specs=pl.BlockSpec((None, tq, rw), lambda b, g, i: (b, i, g)),
        out_shape=jax.ShapeDtypeStruct((B, S, NSA_HEADS * HEAD_DIM), BF16),
        scratch_shapes=[pltpu.VMEM((R, tq, 1), F32),
                        pltpu.VMEM((R, tq, 2 * HEAD_DIM), F32), pltpu.VMEM((R, tq, HEAD_DIM), F32)],
        compiler_params=pltpu.CompilerParams(
            dimension_semantics=("parallel", "parallel", "arbitrary"),
            vmem_limit_bytes=VMEM_LIMIT_BYTES),
        name="nsa_attention",
    )(zq, kc, vc, zks, zvs, zkw, zvw, gate_logits, wmap)


MOE_TM = 256
MOE_TN = 512
ROUTER_TM = 256


def _router_kernel(x_ref, g_ref, w_ref, b_ref, t_ref, ids_ref, wts_ref):
    x = x_ref[...]
    t = x * lax.rsqrt(jnp.mean(x * x, axis=-1, keepdims=True) + NORM_EPS) * g_ref[...]
    t_ref[...] = t
    logits = jnp.dot(t.astype(BF16), w_ref[...], preferred_element_type=F32) + b_ref[...]
    coli = lax.broadcasted_iota(jnp.int32, logits.shape, 1)
    colf = coli.astype(F32)
    first = lambda hit: jnp.min(jnp.where(hit, colf, float(LANES)), axis=-1, keepdims=True)
    is_g = coli < N_GROUPS
    gl = jnp.where(is_g, logits, -jnp.inf)
    gmax = jnp.max(gl, axis=-1, keepdims=True)
    g_sel = first(gl == gmax)
    p_g = 1.0 / jnp.sum(jnp.where(is_g, jnp.exp(logits - gmax), 0.0), axis=-1, keepdims=True)
    lo = N_GROUPS + EXPERTS_PER_GROUP * g_sel
    el = jnp.where((colf >= lo) & (colf < lo + EXPERTS_PER_GROUP), logits, -jnp.inf)
    v1 = jnp.max(el, axis=-1, keepdims=True)
    i1 = first(el == v1)
    el2 = jnp.where(colf == i1, -jnp.inf, el)
    v2 = jnp.max(el2, axis=-1, keepdims=True)
    i2 = first(el2 == v2)
    e = jnp.exp(v2 - v1)
    w1 = p_g / (1.0 + e)
    w2 = p_g * e / (1.0 + e)
    ids_ref[...] = jnp.where(coli == 0, i1 - N_GROUPS, jnp.where(coli == 1, i2 - N_GROUPS, 0.0)).astype(jnp.int32)
    wts_ref[...] = jnp.where(coli == 0, w1, jnp.where(coli == 1, w2, 0.0))


def moe_router_pallas(x2, gain, w_group, b_group, w_expert, b_expert):
    T, D = x2.shape
    tm = ROUTER_TM
    nr = N_GROUPS + N_EXPERTS
    w = jnp.pad(jnp.concatenate([w_group, w_expert], axis=1), ((0, 0), (0, LANES - nr))).astype(BF16)
    b = jnp.pad(jnp.concatenate([b_group, b_expert]), (0, LANES - nr)).astype(F32)[None, :]
    t, ids, wts = pl.pallas_call(
        _router_kernel,
        grid=(T // tm,),
        in_specs=[pl.BlockSpec((tm, D), lambda i: (i, 0)),
                  pl.BlockSpec((1, D), lambda i: (0, 0)),
                  pl.BlockSpec((D, LANES), lambda i: (0, 0)),
                  pl.BlockSpec((1, LANES), lambda i: (0, 0))],
        out_specs=[pl.BlockSpec((tm, D), lambda i: (i, 0)),
                   pl.BlockSpec((tm, LANES), lambda i: (i, 0)),
                   pl.BlockSpec((tm, LANES), lambda i: (i, 0))],
        out_shape=[jax.ShapeDtypeStruct((T, D), F32),
                   jax.ShapeDtypeStruct((T, LANES), jnp.int32),
                   jax.ShapeDtypeStruct((T, LANES), F32)],
        compiler_params=pltpu.CompilerParams(
            dimension_semantics=("parallel",), vmem_limit_bytes=VMEM_LIMIT_BYTES),
        name="moe_router",
    )(x2, gain.astype(F32)[None, :], w, b)
    return t, ids[:, :TOPK_IN_GROUP], wts[:, :TOPK_IN_GROUP]


MOE_DMA_GROUP = 8


def _moe_up_kernel(tile_e_ref, n_used_ref, tile_rows_ref, row_tok_ref, t_hbm, rw_ref, wg_ref, wu_ref, hid_ref,
                   xw_ref, xb_ref, sem):
    k = pl.program_id(0)
    tm = xb_ref.shape[0]
    n_used = n_used_ref[0]

    def row_copy(tok, r):
        return pltpu.make_async_copy(t_hbm.at[pl.ds(tok, 1), :], xw_ref.at[pl.ds(r, 1), :], sem.at[0])

    def n_groups(tile):
        return (tile_rows_ref[tile] + MOE_DMA_GROUP - 1) // MOE_DMA_GROUP

    def start_gather(tile):
        def issue(g, carry):
            for j in range(MOE_DMA_GROUP):
                r = g * MOE_DMA_GROUP + j
                row_copy(row_tok_ref[tile * tm + r], r).start()
            return carry
        lax.fori_loop(0, n_groups(tile), issue, 0)

    @pl.when(k == 0)
    def _():
        xw_ref[...] = jnp.zeros_like(xw_ref)
        start_gather(0)

    @pl.when(k < n_used)
    def _():
        def wait_group(g, carry):
            for j in range(MOE_DMA_GROUP):
                row_copy(0, g * MOE_DMA_GROUP + j).wait()
            return carry
        lax.fori_loop(0, n_groups(k), wait_group, 0)
        xb_ref[...] = xw_ref[...].astype(BF16)

        @pl.when(k + 1 < n_used)
        def _():
            start_gather(k + 1)

        x = xb_ref[...]
        hg = jnp.dot(x, wg_ref[...].astype(BF16), preferred_element_type=F32)
        hu = jnp.dot(x, wu_ref[...].astype(BF16), preferred_element_type=F32)
        hid_ref[...] = (jax.nn.silu(hg) * hu * rw_ref[...]).astype(hid_ref.dtype)

    @pl.when(k >= n_used)
    def _():
        hid_ref[...] = jnp.zeros_like(hid_ref)


def _moe_down_kernel(tile_e_ref, n_used_ref, hid_ref, wd_ref, o_ref):
    k = pl.program_id(0)

    @pl.when(k < n_used_ref[0])
    def _():
        hid = hid_ref[...]
        for c in range(0, o_ref.shape[1], MOE_TN):
            o_ref[:, c:c + MOE_TN] = jnp.dot(hid, wd_ref[:, c:c + MOE_TN].astype(BF16),
                                             preferred_element_type=F32)

    @pl.when(k >= n_used_ref[0])
    def _():
        o_ref[...] = jnp.zeros_like(o_ref)


COMBINE_TC = 128


def _moe_combine_kernel(pos_ref, x_ref, g_ref, y_hbm, o_ref, n_ref, buf, sem):
    i = pl.program_id(0)
    tc = x_ref.shape[0]
    nk = TOPK_IN_GROUP

    def row_copy(slot, a, p):
        return pltpu.make_async_copy(y_hbm.at[pl.ds(p, 1), :], buf.at[slot, pl.ds(a, 1), :], sem.at[slot])

    def start_gather(tile, slot):
        def issue(r, carry):
            for kk in range(nk):
                row_copy(slot, kk * tc + r, pos_ref[(tile * tc + r) * nk + kk]).start()
            return carry
        lax.fori_loop(0, tc, issue, 0, unroll=4)

    slot = i % 2

    @pl.when(i == 0)
    def _():
        start_gather(0, 0)

    def wait_row(a, carry):
        row_copy(slot, a, 0).wait()
        return carry
    lax.fori_loop(0, nk * tc, wait_row, 0, unroll=8)

    @pl.when(i + 1 < pl.num_programs(0))
    def _():
        start_gather(i + 1, 1 - slot)

    v = x_ref[...]
    for kk in range(nk):
        v = v + buf[slot, kk * tc:(kk + 1) * tc, :]
    o_ref[...] = v
    n_ref[...] = (v * lax.rsqrt(jnp.mean(v * v, axis=-1, keepdims=True) + NORM_EPS)
                  * g_ref[...]).astype(n_ref.dtype)


def moe_combine_pallas(x2, y_rows, pos, next_gain, next_dtype):
    T, D = x2.shape
    tc = COMBINE_TC
    row_blk = pl.BlockSpec((tc, D), lambda i, *_: (i, 0))
    return pl.pallas_call(
        _moe_combine_kernel,
        grid_spec=pltpu.PrefetchScalarGridSpec(
            num_scalar_prefetch=1,
            grid=(T // tc,),
            in_specs=[row_blk, pl.BlockSpec((1, D), lambda i, *_: (0, 0)),
                      pl.BlockSpec(memory_space=pl.ANY)],
            out_specs=[row_blk, row_blk],
            scratch_shapes=[pltpu.VMEM((2, TOPK_IN_GROUP * tc, D), F32),
                            pltpu.SemaphoreType.DMA((2,))]),
        out_shape=[jax.ShapeDtypeStruct((T, D), F32), jax.ShapeDtypeStruct((T, D), next_dtype)],
        compiler_params=pltpu.CompilerParams(
            dimension_semantics=("arbitrary",), vmem_limit_bytes=VMEM_LIMIT_BYTES),
        name="moe_combine",
    )(pos.reshape(-1), x2, next_gain.astype(F32)[None, :], y_rows)


def moe_experts_pallas(t_packed, row_tok, row_w, tile_e, n_used, tile_rows, w_gate, w_up, w_down, layer):
    P = row_tok.shape[0]
    D = w_gate.shape[-2]
    F = w_gate.shape[-1]
    tm = MOE_TM
    l = int(layer)
    w_in = pl.BlockSpec((None, None, D, F), lambda k, te, *_: (l, te[k], 0, 0))
    hid = pl.pallas_call(
        _moe_up_kernel,
        grid_spec=pltpu.PrefetchScalarGridSpec(
            num_scalar_prefetch=4,
            grid=(P // tm,),
            in_specs=[pl.BlockSpec(memory_space=pl.ANY),
                      pl.BlockSpec((tm, 1), lambda k, *_: (k, 0)),
                      w_in, w_in],
            out_specs=pl.BlockSpec((tm, F), lambda k, *_: (k, 0)),
            scratch_shapes=[pltpu.VMEM((tm, D), F32), pltpu.VMEM((tm, D), BF16),
                            pltpu.SemaphoreType.DMA((1,))]),
        out_shape=jax.ShapeDtypeStruct((P, F), BF16),
        compiler_params=pltpu.CompilerParams(
            dimension_semantics=("arbitrary",), vmem_limit_bytes=VMEM_LIMIT_BYTES),
        name="moe_up",
    )(tile_e, n_used, tile_rows, row_tok, t_packed, row_w, w_gate, w_up)
    return pl.pallas_call(
        _moe_down_kernel,
        grid_spec=pltpu.PrefetchScalarGridSpec(
            num_scalar_prefetch=2,
            grid=(P // tm,),
            in_specs=[pl.BlockSpec((tm, F), lambda k, *_: (k, 0)),
                      pl.BlockSpec((None, None, F, D), lambda k, te, *_: (l, te[k], 0, 0))],
            out_specs=pl.BlockSpec((tm, D), lambda k, *_: (k, 0))),
        out_shape=jax.ShapeDtypeStruct((P, D), F32),
        compiler_params=pltpu.CompilerParams(
            dimension_semantics=("parallel",), vmem_limit_bytes=VMEM_LIMIT_BYTES),
        name="moe_down",
    )(tile_e, n_used, hid, w_down)


def moe_dispatch_plan(ids, wts):
    T, K = ids.shape
    n = T * K
    P = n + N_EXPERTS * MOE_TM
    flat = ids.reshape(n)
    onehot = (flat[:, None] == jnp.arange(N_EXPERTS, dtype=jnp.int32)[None, :]).astype(jnp.int32)
    csum = jnp.cumsum(onehot, axis=0)
    counts = csum[-1]
    padded = (counts + MOE_TM - 1) // MOE_TM * MOE_TM
    pend = jnp.cumsum(padded)
    pstart = pend - padded
    pos = jnp.sum(onehot * (csum - 1 + pstart[None, :]), axis=1)
    tok = (jnp.arange(n, dtype=jnp.int32) // K).astype(F32)
    info = jnp.zeros((P, 2), F32).at[pos].set(jnp.stack([tok, wts.reshape(n)], axis=1))
    row_tok = info[:, 0].astype(jnp.int32)
    row_w = info[:, 1:2]
    tile_start = jnp.arange(P // MOE_TM, dtype=jnp.int32) * MOE_TM
    tile_e = jnp.minimum(jnp.sum((pend[None, :] <= tile_start[:, None]).astype(jnp.int32), axis=1),
                         N_EXPERTS - 1).astype(jnp.int32)
    n_used = (pend[-1] // MOE_TM).astype(jnp.int32)[None]
    e_hot = (tile_e[:, None] == jnp.arange(N_EXPERTS, dtype=jnp.int32)[None, :]).astype(jnp.int32)
    in_expert = tile_start - jnp.sum(e_hot * pstart[None, :], axis=1)
    tile_rows = jnp.clip(jnp.sum(e_hot * counts[None, :], axis=1) - in_expert, 0, MOE_TM).astype(jnp.int32)
    tile_rows = jnp.where(tile_start < pend[-1], tile_rows, 0)
    return pos.reshape(T, K), row_tok, row_w, tile_e, n_used, tile_rows


def hierarchical_moe_pallas(x, gain, w_group, b_group, w_expert, b_expert, w_gate, w_up, w_down, layer,
                            next_gain, next_dtype):
    B, S, D = x.shape
    x2 = x.reshape(B * S, D)
    t, ids, wts = moe_router_pallas(x2, gain, w_group, b_group, w_expert, b_expert)
    pos, row_tok, row_w, tile_e, n_used, tile_rows = moe_dispatch_plan(ids, wts)
    y = moe_experts_pallas(t, row_tok, row_w, tile_e, n_used, tile_rows, w_gate, w_up, w_down, layer)
    out, normed = moe_combine_pallas(x2, y, pos, next_gain, next_dtype)
    return out.reshape(B, S, D), normed


def rms_norm(x, g):
    xf = x.astype(F32)
    y = xf * lax.rsqrt(jnp.mean(xf * xf, axis=-1, keepdims=True) + NORM_EPS)
    return (y * g.astype(F32)).astype(x.dtype)


def rope_tables(positions):
    inv = ROPE_THETA ** (-jnp.arange(0, HEAD_DIM, 2, dtype=F32) / HEAD_DIM)
    ang = positions.astype(F32)[..., None] * inv
    return jnp.cos(ang)[:, :, None, :], jnp.sin(ang)[:, :, None, :]


def rope(x, cos, sin):
    x1, x2 = jnp.split(x.astype(F32), 2, axis=-1)
    return jnp.concatenate([x1 * cos - x2 * sin, x2 * cos + x1 * sin], axis=-1).astype(x.dtype)


def split_heads(t):
    return t.reshape(t.shape[0], t.shape[1], -1, HEAD_DIM)


def split_cols(z):
    return jnp.split(z, [int(v) for v in np.cumsum(MIX_SPLITS)[:-1]], axis=-1)


def causal_depthwise_conv(x, w):
    width, c = w.shape
    xp = jnp.pad(x, ((0, 0), (width - 1, 0), (0, 0)))
    return lax.conv_general_dilated(xp, w[:, None, :].astype(x.dtype), (1,), 'VALID',
                                    dimension_numbers=('NWC', 'WIO', 'NWC'),
                                    feature_group_count=c)


def moba_attention(q, k, v):
    B, S, H, dh = q.shape
    bs = MOBA_BLOCK
    nb = -(-S // bs)
    scale = dh ** -0.5
    qh, kh, vh = (t.transpose(0, 2, 1, 3) for t in (q, k, v))
    pad = nb * bs - S
    kp = jnp.pad(kh, ((0, 0), (0, 0), (0, pad), (0, 0)))
    vp = jnp.pad(vh, ((0, 0), (0, 0), (0, pad), (0, 0)))
    kb = kp.reshape(B, H, nb, bs, dh)
    vb = vp.reshape(B, H, nb, bs, dh)
    k_mean = jnp.mean(kb.astype(F32), axis=3)
    t = jnp.arange(S)
    q_blk = t // bs
    gate = jnp.einsum('bhsd,bhnd->bhsn', qh.astype(F32), k_mean)
    past = jnp.arange(nb)[None, :] < q_blk[:, None]
    gate = jnp.where(past, gate, NEG)
    kk = min(MOBA_TOPK, nb)
    _, sel = lax.top_k(gate, kk)
    sel_ok = sel < q_blk[:, None]
    qb = MOBA_Q_BLOCK
    nqb = S // qb
    q_c = qh.reshape(B, H, nqb, qb, dh).transpose(2, 0, 1, 3, 4)
    sel_c = sel.reshape(B, H, nqb, qb, kk).transpose(2, 0, 1, 3, 4)
    ok_c = sel_ok.reshape(B, H, nqb, qb, kk).transpose(2, 0, 1, 3, 4)
    bi = jnp.arange(B)[:, None, None, None]
    hi = jnp.arange(H)[None, :, None, None]

    def chunk(args):
        qc, selc, okc, ci = args
        tq = ci * qb + jnp.arange(qb)
        own = (ci * qb) // bs * bs
        k_own = lax.dynamic_slice_in_dim(kp, own, bs, axis=2)
        v_own = lax.dynamic_slice_in_dim(vp, own, bs, axis=2)
        own_mask = (own + jnp.arange(bs))[None, :] <= tq[:, None]
        k_s = kb[bi, hi, selc].reshape(B, H, qb, kk * bs, dh)
        v_s = vb[bi, hi, selc].reshape(B, H, qb, kk * bs, dh)
        s_sel = jnp.einsum('bhqd,bhqkd->bhqk', qc, k_s, preferred_element_type=F32) * scale
        s_sel = jnp.where(jnp.repeat(okc, bs, axis=-1), s_sel, NEG)
        s_own = jnp.einsum('bhqd,bhkd->bhqk', qc, k_own, preferred_element_type=F32) * scale
        s_own = jnp.where(own_mask, s_own, NEG)
        p = jax.nn.softmax(jnp.concatenate([s_sel, s_own], axis=-1), axis=-1).astype(v.dtype)
        return (jnp.einsum('bhqk,bhqkd->bhqd', p[..., :kk * bs], v_s)
                + jnp.einsum('bhqk,bhkd->bhqd', p[..., kk * bs:], v_own))

    o = lax.map(chunk, (q_c, sel_c, ok_c, jnp.arange(nqb)))
    return o.transpose(1, 0, 3, 2, 4).reshape(B, S, H * dh)


def compress_tokens(x, pe, w1, w2):
    B, S, G, dh = x.shape
    nc = (S - NSA_CMP_BLOCK) // NSA_CMP_STRIDE + 1
    idx = jnp.arange(nc)[:, None] * NSA_CMP_STRIDE + jnp.arange(NSA_CMP_BLOCK)[None, :]
    blocks = x[:, idx] + pe[:, None, :]
    flat = blocks.transpose(0, 1, 3, 2, 4).reshape(B, nc, G, NSA_CMP_BLOCK * dh)
    return jax.nn.gelu(flat @ w1) @ w2


def cmp_to_sel_weights(nc, ns):
    r = NSA_SEL_BLOCK // NSA_CMP_STRIDE
    m = NSA_CMP_BLOCK // NSA_CMP_STRIDE
    c = jnp.arange(nc)[:, None] - r * jnp.arange(ns)[None, :]
    w = jnp.minimum(jnp.minimum(c + 1, r + m - 1 - c), min(r, m))
    return jnp.clip(w, 0, None).astype(F32)


def nsa_attention(q, k_cmp, v_cmp, k_sel, v_sel, k_win, v_win, gate_logits,
                  pe_k, w1_k, w2_k, pe_v, w1_v, w2_v):
    B, S, H, dh = q.shape
    G = k_sel.shape[2]
    R = H // G
    scale = dh ** -0.5
    t = jnp.arange(S)
    qg = q.reshape(B, S, G, R, dh)
    kc = compress_tokens(k_cmp, pe_k, w1_k, w2_k)
    vc = compress_tokens(v_cmp, pe_v, w1_v, w2_v)
    nc = kc.shape[1]
    s_c = jnp.einsum('bsgrd,bngd->bgrsn', qg, kc, preferred_element_type=F32) * scale
    c_end = jnp.arange(nc) * NSA_CMP_STRIDE + NSA_CMP_BLOCK - 1
    c_mask = c_end[None, :] <= t[:, None]
    p_c = jax.nn.softmax(jnp.where(c_mask, s_c, NEG), axis=-1) * c_mask
    o_cmp = jnp.einsum('bgrsn,bngd->bsgrd', p_c.astype(vc.dtype), vc).reshape(B, S, H, dh)
    ns = S // NSA_SEL_BLOCK
    imp = jnp.einsum('bgrsn,nm->bgsm', p_c, cmp_to_sel_weights(nc, ns))
    q_blk = t // NSA_SEL_BLOCK
    j = jnp.arange(ns)[None, :]
    valid = j <= q_blk[:, None]
    forced = (j == 0) | (j >= q_blk[:, None] - 1)
    imp = jnp.where(valid, imp + jnp.where(forced, NSA_FORCE_BONUS, 0.0), NEG)
    kk = min(NSA_SEL_TOPK, ns)
    _, sel = lax.top_k(imp, kk)
    sel_ok = sel <= q_blk[:, None]
    sb = NSA_SEL_BLOCK
    k_blocks = k_sel.transpose(0, 2, 1, 3).reshape(B, G, ns, sb, dh)
    v_blocks = v_sel.transpose(0, 2, 1, 3).reshape(B, G, ns, sb, dh)
    wpad = ((0, 0), (0, 0), (NSA_WINDOW, 0), (0, 0))
    k_wp = jnp.pad(k_win.transpose(0, 2, 1, 3), wpad)
    v_wp = jnp.pad(v_win.transpose(0, 2, 1, 3), wpad)
    nqb = S // Q_BLOCK
    q_c = qg.transpose(0, 2, 3, 1, 4).reshape(B, G, R, nqb, Q_BLOCK, dh).transpose(3, 0, 1, 2, 4, 5)
    sel_c = sel.reshape(B, G, nqb, Q_BLOCK, kk).transpose(2, 0, 1, 3, 4)
    ok_c = sel_ok.reshape(B, G, nqb, Q_BLOCK, kk).transpose(2, 0, 1, 3, 4)
    bi = jnp.arange(B)[:, None, None, None]
    gi = jnp.arange(G)[None, :, None, None]

    def chunk(args):
        qc, selc, okc, ci = args
        tq = ci * Q_BLOCK + jnp.arange(Q_BLOCK)
        ks = k_blocks[bi, gi, selc].reshape(B, G, Q_BLOCK, kk * sb, dh)
        vs = v_blocks[bi, gi, selc].reshape(B, G, Q_BLOCK, kk * sb, dh)
        kpos = (selc[..., None] * sb + jnp.arange(sb)).reshape(B, G, Q_BLOCK, kk * sb)
        m = (kpos <= tq[:, None]) & jnp.repeat(okc, sb, axis=-1)
        s = jnp.einsum('bgrqd,bgqkd->bgrqk', qc, ks, preferred_element_type=F32) * scale
        p = jax.nn.softmax(jnp.where(m[:, :, None], s, NEG), axis=-1)
        o_s = jnp.einsum('bgrqk,bgqkd->bgrqd', p.astype(vs.dtype), vs)
        kw = lax.dynamic_slice_in_dim(k_wp, ci * Q_BLOCK, Q_BLOCK + NSA_WINDOW, axis=2)
        vw = lax.dynamic_slice_in_dim(v_wp, ci * Q_BLOCK, Q_BLOCK + NSA_WINDOW, axis=2)
        wpos = ci * Q_BLOCK - NSA_WINDOW + jnp.arange(Q_BLOCK + NSA_WINDOW)
        dist = tq[:, None] - wpos[None, :]
        wm = (dist >= 0) & (dist < NSA_WINDOW) & (wpos[None, :] >= 0)
        s_w = jnp.einsum('bgrqd,bgkd->bgrqk', qc, kw, preferred_element_type=F32) * scale
        p_w = jax.nn.softmax(jnp.where(wm, s_w, NEG), axis=-1)
        o_w = jnp.einsum('bgrqk,bgkd->bgrqd', p_w.astype(vw.dtype), vw)
        return o_s, o_w

    o_sel, o_win = lax.map(chunk, (q_c, sel_c, ok_c, jnp.arange(nqb)))
    o_sel = o_sel.transpose(1, 0, 4, 2, 3, 5).reshape(B, S, H, dh)
    o_win = o_win.transpose(1, 0, 4, 2, 3, 5).reshape(B, S, H, dh)
    g = jax.nn.sigmoid(gate_logits.astype(F32)).reshape(B, S, H, 3)
    out = g[..., 0:1] * o_cmp + g[..., 1:2] * o_sel + g[..., 2:3] * o_win
    return out.reshape(B, S, H * dh).astype(q.dtype)


def rg_lru_branch(x_in, gate_in, conv_w, conv_b, w_a, b_a, w_i, b_i, lam):
    B, S, C = x_in.shape
    xc = (causal_depthwise_conv(x_in, conv_w) + conv_b).astype(F32)
    xh = xc.reshape(B, S, LRU_HEADS, C // LRU_HEADS)
    r = jax.nn.sigmoid(jnp.einsum('bshi,hij->bshj', xh, w_a.astype(F32)) + b_a).reshape(B, S, C)
    i = jax.nn.sigmoid(jnp.einsum('bshi,hij->bshj', xh, w_i.astype(F32)) + b_i).reshape(B, S, C)
    log_a = -LRU_C * r * jax.nn.softplus(-lam.astype(F32))
    a = jnp.exp(log_a)
    b = jnp.sqrt(-jnp.expm1(2.0 * log_a)) * (i * xc)

    def combine(c1, c2):
        a1, b1 = c1
        a2, b2 = c2
        return a1 * a2, a2 * b1 + b2

    _, h = lax.associative_scan(combine, (a, b), axis=1)
    return (h * jax.nn.gelu(gate_in.astype(F32))).astype(x_in.dtype)


def short_conv_branch(x_in, b_gate, c_gate, conv_w):
    return b_gate * causal_depthwise_conv(c_gate * x_in, conv_w)


def cross_attention(h, mem_n, w_q, w_k, w_v, w_o):
    B, S, _ = h.shape
    M = mem_n.shape[1]
    q = dense(h, w_q).reshape(B, S, XA_HEADS, HEAD_DIM)
    k = dense(mem_n, w_k).reshape(B, M, XA_HEADS, HEAD_DIM)
    v = dense(mem_n, w_v).reshape(B, M, XA_HEADS, HEAD_DIM)
    s = jnp.einsum('bshd,bmhd->bhsm', q, k, preferred_element_type=F32) * HEAD_DIM ** -0.5
    p = jax.nn.softmax(s, axis=-1).astype(v.dtype)
    o = jnp.einsum('bhsm,bmhd->bshd', p, v).reshape(B, S, XA_WIDTH)
    return dense(o, w_o)


def hierarchical_moe(h, w_group, b_group, w_expert, b_expert, w_gate, w_up, w_down):
    B, S, D = h.shape
    t = h.reshape(B * S, D)
    g_logits = (t @ w_group + b_group).astype(F32)
    g_prob = jax.nn.softmax(g_logits, axis=-1)
    g_sel = jnp.argmax(g_logits, axis=-1)
    e_logits = (t @ w_expert + b_expert).astype(F32).reshape(-1, N_GROUPS, EXPERTS_PER_GROUP)
    e_in = jnp.take_along_axis(e_logits, g_sel[:, None, None], axis=1)[:, 0]
    top_v, top_i = lax.top_k(e_in, TOPK_IN_GROUP)
    p_g = jnp.take_along_axis(g_prob, g_sel[:, None], axis=1)
    w_top = jax.nn.softmax(top_v, axis=-1) * p_g
    eid = g_sel[:, None] * EXPERTS_PER_GROUP + top_i
    combine = jnp.sum(jax.nn.one_hot(eid, N_EXPERTS, dtype=F32) * w_top[..., None], axis=1)
    hid = jax.nn.silu(jnp.einsum('td,edf->tef', t, w_gate)) * jnp.einsum('td,edf->tef', t, w_up)
    out = jnp.einsum('tef,efd->td', hid * combine[:, :, None].astype(hid.dtype), w_down)
    return out.reshape(B, S, D)


def kernel(x, mem, positions, norm_mix, w_mix_in, lru_conv_w, lru_conv_b, lru_w_a, lru_b_a, lru_w_i, lru_b_i, lru_lambda, sc_conv_w, nsa_pe_k, nsa_w1_k, nsa_w2_k, nsa_pe_v, nsa_w1_v, nsa_w2_v, w_merge_gate, b_merge_gate, w_branch_out, w_mix_out, norm_xattn, norm_mem, xa_w_q, xa_w_k, xa_w_v, xa_w_o, norm_moe, moe_w_group, moe_b_group, moe_w_expert, moe_b_expert, moe_w_gate, moe_w_up, moe_w_down, norm_final):
    B, S, D = x.shape
    T = B * S
    cos, sin = rope_tables(positions)
    cos_t = jnp.concatenate([cos, cos], axis=-1).reshape(T, HEAD_DIM)
    sin_t = jnp.concatenate([-sin, sin], axis=-1).reshape(T, HEAD_DIM)
    offs = [int(v) for v in np.concatenate([[0], np.cumsum(MIX_SPLITS)])]
    blk = [v // HEAD_DIM for v in offs[:11]]
    n_head_cols = offs[10]
    rope_flags = np.zeros((n_head_cols // HEAD_DIM,), np.int32)
    for k in (1, 4, 6, 8):
        rope_flags[blk[k]:blk[k + 1]] = ROPE_PLAIN
    for k in (0, 3):
        rope_flags[blk[k]:blk[k + 1]] = ROPE_SCALED
    rope_flags = jnp.asarray(rope_flags)
    G, R = NSA_KV_HEADS, NSA_HEADS // NSA_KV_HEADS
    w_in_nk = jnp.swapaxes(w_mix_in, 1, 2)
    assert offs[10] % 8 == 0 and offs[11] % 8 == 0
    M = mem.shape[1]
    h = rms_norm_pallas(x.reshape(T, D), norm_mix[0], BF16)
    for l in range(DEPTH):
        x2 = x.reshape(T, D)
        z_head = matmul(h, w_in_nk, w_lead=(l,), w_is_nk=True, n_cols=n_head_cols, out_dtype=BF16,
                        rope=(cos_t, sin_t, rope_flags)).reshape(B, S, n_head_cols)
        z_tail = matmul(h, w_in_nk, w_lead=(l,), w_is_nk=True, row0=offs[11],
                        n_cols=offs[16] - offs[11]).reshape(B, S, -1)
        z_g = matmul(h, w_in_nk, w_lead=(l,), w_is_nk=True, row0=offs[10], n_cols=LANES)
        z_gate = jnp.pad(z_g[:, :NSA_GATE_W].reshape(B, S, G, R * 3),
                         ((0, 0), (0, 0), (0, 0), (0, LANES - R * 3))).reshape(B, S, G * LANES)
        o_moba = moba_attention_pallas(z_head, z_head, z_head, n_heads=MOBA_HEADS,
                                       q_off=blk[0], k_off=blk[1], v_off=blk[2])
        kc = nsa_compress_pallas(z_head[:, :, offs[4]:offs[5]], nsa_pe_k[l], nsa_w1_k[l], nsa_w2_k[l])
        vc = nsa_compress_pallas(z_head[:, :, offs[5]:offs[6]], nsa_pe_v[l], nsa_w1_v[l], nsa_w2_v[l])
        o_nsa = nsa_attention_pallas(z_head, kc, vc, z_head, z_head, z_head, z_head, z_gate,
                                     q_off=blk[3], ks_off=blk[6], vs_off=blk[7], kw_off=blk[8], vw_off=blk[9])
        o_lru, o_conv = lru_conv_pallas(z_tail, l, lru_conv_w, lru_conv_b, lru_w_a, lru_b_a,
                                        lru_w_i, lru_b_i, lru_lambda, sc_conv_w)
        o_all = jnp.stack([o_moba, o_lru, o_conv, o_nsa]).reshape(N_BRANCH, T, -1)
        merged = merge_branches_pallas(h, o_all, w_merge_gate, b_merge_gate, w_branch_out, l)
        x2 = matmul(merged, w_mix_out, w_lead=(l,), residual=x2)
        mem_n = rms_norm_pallas(mem.reshape(B * M, D), norm_mem[l], BF16)
        xk = matmul(mem_n, xa_w_k, w_lead=(l,), out_dtype=BF16).reshape(B, M, XA_WIDTH)
        xv = matmul(mem_n, xa_w_v, w_lead=(l,), out_dtype=BF16).reshape(B, M, XA_WIDTH)
        x = cross_attention_pallas(x2.reshape(B, S, D), norm_xattn[l], xk, xv,
                                   xa_w_q[l].astype(BF16), xa_w_o[l].astype(BF16))
        last = l == DEPTH - 1
        x, h = hierarchical_moe_pallas(x, norm_moe[l], moe_w_group[l], moe_b_group[l],
                                       moe_w_expert[l], moe_b_expert[l], moe_w_gate,
                                       moe_w_up, moe_w_down, l,
                                       norm_final if last else norm_mix[l + 1], F32 if last else BF16)
    return h.reshape(B, S, D)
```

```python
import functools

import jax
import jax.numpy as jnp
import numpy as np
from jax import lax
from jax.experimental import pallas as pl
from jax.experimental.pallas import tpu as pltpu

F32 = jnp.float32
BF16 = jnp.bfloat16

D_MODEL = 4096
DEPTH = 2
HEAD_DIM = 128
ROPE_THETA = 10000.0
NORM_EPS = 1e-6
NEG = -1e30
N_BRANCH = 4
BRANCH_WIDTH = D_MODEL // 4
MOBA_HEADS = BRANCH_WIDTH // HEAD_DIM
MOBA_BLOCK = 256
MOBA_TOPK = 3
LRU_WIDTH = BRANCH_WIDTH
LRU_C = 8.0
SC_WIDTH = BRANCH_WIDTH
NSA_HEADS = BRANCH_WIDTH // HEAD_DIM
NSA_KV_HEADS = NSA_HEADS // 4
NSA_CMP_BLOCK = 32
NSA_CMP_STRIDE = 16
NSA_SEL_BLOCK = 64
NSA_SEL_TOPK = 16
NSA_WINDOW = 512
NSA_FORCE_BONUS = 1e4
XA_HEADS = 4
XA_WIDTH = XA_HEADS * HEAD_DIM
N_GROUPS = 4
EXPERTS_PER_GROUP = 8
N_EXPERTS = N_GROUPS * EXPERTS_PER_GROUP
TOPK_IN_GROUP = 2
MOBA_W = MOBA_HEADS * HEAD_DIM
NSA_Q_W = NSA_HEADS * HEAD_DIM
NSA_KV_W = NSA_KV_HEADS * HEAD_DIM
NSA_GATE_W = NSA_HEADS * 3
MIX_SPLITS = (MOBA_W, MOBA_W, MOBA_W,
              NSA_Q_W, NSA_KV_W, NSA_KV_W, NSA_KV_W, NSA_KV_W, NSA_KV_W, NSA_KV_W, NSA_GATE_W,
              LRU_WIDTH, LRU_WIDTH,
              SC_WIDTH, SC_WIDTH, SC_WIDTH)

VMEM_LIMIT_BYTES = 56 * 1024 * 1024


ROPE_PLAIN, ROPE_SCALED = 1, 2


def _mm_kernel(*refs, rope, residual, w_is_nk):
    if rope:
        flags_ref, refs = refs[0], refs[1:]
    a_ref, w_ref = refs[0], refs[1]
    o_ref = refs[-1]
    dims = (((1,), (1,)), ((), ())) if w_is_nk else (((1,), (0,)), ((), ()))
    a = a_ref[...].astype(BF16)
    if rope:
        cos, sin = refs[2][...], refs[3][...]
        tn = o_ref.shape[1]
        step = 2 * HEAD_DIM
        for c0 in range(0, tn, step):
            w_blk = w_ref[c0:c0 + step, :] if w_is_nk else w_ref[:, c0:c0 + step]
            acc = lax.dot_general(a, w_blk.astype(BF16), dims, preferred_element_type=F32)
            for c in range(c0, c0 + step, HEAD_DIM):
                blk = acc[:, c - c0:c - c0 + HEAD_DIM]
                flag = flags_ref[(pl.program_id(1) * tn + c) // HEAD_DIM]
                scl = jnp.where(flag == ROPE_SCALED, HEAD_DIM ** -0.5, 1.0)
                cos_c = jnp.where(flag != 0, cos * scl, 1.0)
                sin_c = jnp.where(flag != 0, sin * scl, 0.0)
                rot = pltpu.roll(blk, HEAD_DIM // 2, 1)
                o_ref[:, c:c + HEAD_DIM] = (blk * cos_c + rot * sin_c).astype(o_ref.dtype)
        return
    acc = lax.dot_general(a, w_ref[...].astype(BF16), dims, preferred_element_type=F32)
    if residual:
        acc = acc + refs[2][...]
    o_ref[...] = acc.astype(o_ref.dtype)


def matmul(a, w, *, w_lead=(), w_is_nk=False, col0=0, row0=None, n_cols=None, out_dtype=F32, rope=None,
           residual=None, tm=1024, tn=512):
    m, k = a.shape
    n = w.shape[-2 if w_is_nk else -1] if n_cols is None else n_cols
    assert w.shape[-1 if w_is_nk else -2] == k and not (rope is not None and residual is not None)
    tm = min(tm, m)
    tn = min(tn, n)
    lead = tuple(int(v) for v in w_lead)
    if w_is_nk and row0 is not None:
        rows_per_slab = w.shape[-2]
        slab = int(np.ravel_multi_index(lead, w.shape[:-2])) if lead else 0
        w = w.reshape(-1, k)
        w_spec = pl.BlockSpec((pl.Element(tn), pl.Element(k)),
                              lambda i, j, *_: (pl.multiple_of(slab * rows_per_slab + row0 + j * tn, 8), 0))
    elif w_is_nk:
        w_spec = pl.BlockSpec((None,) * len(lead) + (tn, k), lambda i, j, *_: lead + (j + col0, 0))
    else:
        w_spec = pl.BlockSpec((None,) * len(lead) + (k, tn), lambda i, j, *_: lead + (0, j + col0))
    in_specs = [pl.BlockSpec((tm, k), lambda i, j, *_: (i, 0)), w_spec]
    args = [a, w]
    prefetch = []
    if rope is not None:
        cos, sin, flags = rope
        assert n % tn == 0 and flags.shape == (n // HEAD_DIM,)
        in_specs += [pl.BlockSpec((tm, HEAD_DIM), lambda i, j, *_: (i, 0))] * 2
        args += [cos, sin]
        prefetch = [flags]
    if residual is not None:
        in_specs.append(pl.BlockSpec((tm, tn), lambda i, j, *_: (i, j)))
        args.append(residual)
    return pl.pallas_call(
        functools.partial(_mm_kernel, rope=rope is not None, residual=residual is not None,
                          w_is_nk=w_is_nk),
        grid_spec=pltpu.PrefetchScalarGridSpec(
            num_scalar_prefetch=len(prefetch),
            grid=(pl.cdiv(m, tm), pl.cdiv(n, tn)),
            in_specs=in_specs,
            out_specs=pl.BlockSpec((tm, tn), lambda i, j, *_: (i, j))),
        out_shape=jax.ShapeDtypeStruct((m, n), out_dtype),
        compiler_params=pltpu.CompilerParams(
            dimension_semantics=("parallel", "parallel"),
            vmem_limit_bytes=VMEM_LIMIT_BYTES),
        name="matmul",
    )(*prefetch, *args)


def _rms_kernel(x_ref, g_ref, o_ref):
    x = x_ref[...]
    y = x * lax.rsqrt(jnp.mean(x * x, axis=-1, keepdims=True) + NORM_EPS)
    o_ref[...] = (y * g_ref[...]).astype(o_ref.dtype)


def rms_norm_pallas(x2, gain, out_dtype, *, tm=512):
    m, d = x2.shape
    tm = min(tm, m)
    return pl.pallas_call(
        _rms_kernel,
        grid=(m // tm,),
        in_specs=[pl.BlockSpec((tm, d), lambda i: (i, 0)), pl.BlockSpec((1, d), lambda i: (0, 0))],
        out_specs=pl.BlockSpec((tm, d), lambda i: (i, 0)),
        out_shape=jax.ShapeDtypeStruct((m, d), out_dtype),
        compiler_params=pltpu.CompilerParams(
            dimension_semantics=("parallel",), vmem_limit_bytes=VMEM_LIMIT_BYTES),
        name="rms_norm",
    )(x2, gain.astype(F32)[None, :])


def _merge_kernel(h_ref, o_ref, wg_ref, bg_ref, wu_ref, out_ref, acc_ref):
    n = pl.program_id(2)

    @pl.when((pl.program_id(0) == 0) & (pl.program_id(1) == 0) & (n == 0))
    def _():
        acc_ref[...] = jnp.zeros_like(acc_ref)

    h, o = h_ref[...], o_ref[...]
    half = out_ref.shape[1] // 2
    for c0 in (0, half):
        cs = slice(c0, c0 + half)
        gate = jax.nn.sigmoid(
            jnp.dot(h, wg_ref[:, cs].astype(BF16), preferred_element_type=F32) + bg_ref[:, cs])
        term = gate * jnp.dot(o, wu_ref[:, cs].astype(BF16), preferred_element_type=F32)
        acc = jnp.where(n == 0, 0.0, acc_ref[:, cs]) + term
        acc_ref[:, cs] = acc
        out_ref[:, cs] = acc.astype(out_ref.dtype)


def merge_branches_pallas(h, o_all, w_gate, b_gate, w_up, layer, *, tm=1024, tn=512):
    T, D = h.shape
    N, _, W = o_all.shape
    l = int(layer)
    return pl.pallas_call(
        _merge_kernel,
        grid=(T // tm, D // tn, N),
        in_specs=[pl.BlockSpec((tm, D), lambda i, j, n: (i, 0)),
                  pl.BlockSpec((None, tm, W), lambda i, j, n: (n, i, 0)),
                  pl.BlockSpec((None, None, D, tn), lambda i, j, n: (l, n, 0, j)),
                  pl.BlockSpec((None, None, 1, tn), lambda i, j, n: (l, n, 0, j)),
                  pl.BlockSpec((None, None, W, tn), lambda i, j, n: (l, n, 0, j))],
        out_specs=pl.BlockSpec((tm, tn), lambda i, j, n: (i, j)),
        out_shape=jax.ShapeDtypeStruct((T, D), BF16),
        scratch_shapes=[pltpu.VMEM((tm, tn), F32)],
        compiler_params=pltpu.CompilerParams(
            dimension_semantics=("parallel", "parallel", "arbitrary"),
            vmem_limit_bytes=VMEM_LIMIT_BYTES),
        name="merge_branches",
    )(h, o_all, w_gate, b_gate[:, :, None, :], w_up)


LRU_TS = 256
LRU_CW = 256


def _rows_back(x, prev, s, row):
    return jnp.where(row < s, pltpu.roll(prev, s, 0), pltpu.roll(x, s, 0))


def _lane_window(blocks, shift):
    if shift == 0:
        return jnp.concatenate([b[...] for b in blocks], axis=-1)
    back = LANES - shift
    rolled = [pltpu.roll(b[...], back, 1) for b in blocks]
    lane = lax.broadcasted_iota(jnp.int32, rolled[0].shape, 1)
    return jnp.concatenate([jnp.where(lane < back, rolled[j], rolled[j + 1])
                            for j in range(len(blocks) - 1)], axis=-1)


def _lru_conv_kernel(*refs, n_lane_blocks, shift):
    nb = n_lane_blocks
    rx, rg, cb, cc, cx = (_lane_window(refs[k * nb:(k + 1) * nb], shift) for k in range(5))
    (lw_ref, lb_ref, wa_ref, ba_ref, wi_ref, bi_ref, lam_ref, sw_ref,
     olru_ref, oconv_ref, px_ref, py_ref, h_ref) = refs[5 * nb:]

    @pl.when(pl.program_id(2) == 0)
    def _():
        px_ref[...] = jnp.zeros_like(px_ref)
        py_ref[...] = jnp.zeros_like(py_ref)
        h_ref[...] = jnp.zeros_like(h_ref)

    ts, cw = px_ref.shape
    row = lax.broadcasted_iota(jnp.int32, (ts, cw), 0)

    x = rx
    prev = px_ref[...]
    nk = lw_ref.shape[0]
    xc = lb_ref[...] + lw_ref[nk - 1:nk, :] * x
    for s in range(1, nk):
        xc = xc + lw_ref[nk - 1 - s:nk - s, :] * _rows_back(x, prev, s, row)
    px_ref[...] = x
    r_parts, i_parts = [], []
    for hh in range(cw // HEAD_DIM):
        cs = slice(hh * HEAD_DIM, (hh + 1) * HEAD_DIM)
        xh = xc[:, cs].astype(BF16)
        r_parts.append(jnp.dot(xh, wa_ref[hh].astype(BF16), preferred_element_type=F32))
        i_parts.append(jnp.dot(xh, wi_ref[hh].astype(BF16), preferred_element_type=F32))
    r = jax.nn.sigmoid(jnp.concatenate(r_parts, axis=-1) + ba_ref[...])
    gi = jax.nn.sigmoid(jnp.concatenate(i_parts, axis=-1) + bi_ref[...])
    log_a = -LRU_C * r * jax.nn.softplus(-lam_ref[...])
    a = jnp.exp(log_a)
    b = jnp.sqrt(1.0 - a * a) * (gi * xc)
    d = 1
    while d < ts:
        a_back = jnp.where(row < d, 1.0, pltpu.roll(a, d, 0))
        b_back = jnp.where(row < d, 0.0, pltpu.roll(b, d, 0))
        b = a * b_back + b
        a = a * a_back
        d *= 2
    h = b + a * h_ref[...]
    h_ref[...] = h[ts - 1:ts, :]
    olru_ref[...] = (h * jax.nn.gelu(rg)).astype(olru_ref.dtype)

    y = cc * cx
    prev_y = py_ref[...]
    nk = sw_ref.shape[0]
    conv = sw_ref[nk - 1:nk, :] * y
    for s in range(1, nk):
        conv = conv + sw_ref[nk - 1 - s:nk - s, :] * _rows_back(y, prev_y, s, row)
    py_ref[...] = y
    oconv_ref[...] = (cb * conv).astype(oconv_ref.dtype)


def lru_conv_pallas(z, layer, lru_conv_w, lru_conv_b, lru_w_a, lru_b_a, lru_w_i, lru_b_i, lru_lambda,
                    sc_conv_w, *, shift=0):
    B, S, _ = z.shape
    C = lru_conv_b.shape[-1]
    ts, cw = LRU_TS, LRU_CW
    nblk = C // cw
    l = int(layer)
    per_blk = cw // LANES
    nb = per_blk + (1 if shift else 0)
    lane_blk = lambda k, j: pl.BlockSpec(
        (None, ts, LANES), lambda b, c, t: (b, t, (k * nblk + c) * per_blk + j))
    cols = [lane_blk(k, j) for k in range(5) for j in range(nb)]
    par = lambda rows: pl.BlockSpec((None, rows, cw), lambda b, c, t: (l, 0, c))
    hw = pl.BlockSpec((None, cw // HEAD_DIM, HEAD_DIM, HEAD_DIM), lambda b, c, t: (l, c, 0, 0))
    L = lru_conv_b.shape[0]
    flat = lambda p: p.reshape(L, 1, C)
    out = pl.BlockSpec((None, ts, cw), lambda b, c, t: (b, t, c))
    return pl.pallas_call(
        functools.partial(_lru_conv_kernel, n_lane_blocks=nb, shift=shift),
        grid=(B, nblk, S // ts),
        in_specs=cols + [par(lru_conv_w.shape[1]), par(1), hw, par(1), hw, par(1), par(1),
                         par(sc_conv_w.shape[1])],
        out_specs=[out, out],
        out_shape=[jax.ShapeDtypeStruct((B, S, C), BF16)] * 2,
        scratch_shapes=[pltpu.VMEM((ts, cw), F32), pltpu.VMEM((ts, cw), F32), pltpu.VMEM((1, cw), F32)],
        compiler_params=pltpu.CompilerParams(
            dimension_semantics=("parallel", "parallel", "arbitrary"),
            vmem_limit_bytes=VMEM_LIMIT_BYTES),
        name="lru_conv",
    )(*([z] * len(cols)), lru_conv_w, flat(lru_conv_b), lru_w_a, flat(lru_b_a), lru_w_i, flat(lru_b_i),
      flat(lru_lambda), sc_conv_w)


XA_TM = 256


def _xattn_kernel(x_ref, g_ref, wq_ref, k_ref, v_ref, wo_ref, o_ref, *, scale):
    x = x_ref[...]
    xn = (x * lax.rsqrt(jnp.mean(x * x, axis=-1, keepdims=True) + NORM_EPS) * g_ref[...]).astype(BF16)
    q = jnp.dot(xn, wq_ref[...], preferred_element_type=F32).astype(BF16)
    heads = []
    for hh in range(XA_HEADS):
        cs = slice(hh * HEAD_DIM, (hh + 1) * HEAD_DIM)
        s = lax.dot_general(q[:, cs], k_ref[:, cs], (((1,), (1,)), ((), ())),
                            preferred_element_type=F32) * scale
        e = jnp.exp(s - jnp.max(s, axis=-1, keepdims=True))
        p = (e / jnp.sum(e, axis=-1, keepdims=True)).astype(BF16)
        heads.append(jnp.dot(p, v_ref[:, cs], preferred_element_type=F32))
    o = jnp.concatenate(heads, axis=-1).astype(BF16)
    o_ref[...] = x + jnp.dot(o, wo_ref[...], preferred_element_type=F32)


def cross_attention_pallas(x, gain, k, v, w_q, w_o):
    B, S, D = x.shape
    M = k.shape[1]
    tm = XA_TM
    per_b = S // tm
    kv = pl.BlockSpec((None, M, XA_WIDTH), lambda i: (i // per_b, 0, 0))
    return pl.pallas_call(
        functools.partial(_xattn_kernel, scale=HEAD_DIM ** -0.5),
        grid=(B * per_b,),
        in_specs=[pl.BlockSpec((tm, D), lambda i: (i, 0)),
                  pl.BlockSpec((1, D), lambda i: (0, 0)),
                  pl.BlockSpec((D, XA_WIDTH), lambda i: (0, 0)),
                  kv, kv,
                  pl.BlockSpec((XA_WIDTH, D), lambda i: (0, 0))],
        out_specs=pl.BlockSpec((tm, D), lambda i: (i, 0)),
        out_shape=jax.ShapeDtypeStruct((B * S, D), F32),
        compiler_params=pltpu.CompilerParams(
            dimension_semantics=("parallel",), vmem_limit_bytes=VMEM_LIMIT_BYTES),
        name="cross_attention",
    )(x.reshape(B * S, D), gain.astype(F32)[None, :], w_q, k, v, w_o).reshape(B, S, D)


MASK_BIG = 1e30
LANES = 128


def _first_index_topk_mask_t(work, rowf, k):
    n = work.shape[0]
    sel = jnp.zeros(work.shape, F32)
    for _ in range(k):
        mx = jnp.max(work, axis=0, keepdims=True)
        idx = jnp.min(jnp.where(work == mx, rowf, float(n)), axis=0, keepdims=True)
        pick = rowf == idx
        sel = jnp.where(pick, 1.0, sel)
        work = jnp.where(pick, -jnp.inf, work)
    return sel


ATT_TK = 512
MOBA_HEADS_PER_STEP = 8
M_INIT = -1e38
NT_DIMS = (((1,), (1,)), ((), ()))


def _flash_update(q_aug, k_aug, v_aug, m_ref, acc_ref, causal):
    s = lax.dot_general(q_aug, k_aug, NT_DIMS, preferred_element_type=F32)
    if causal is not None:
        q0, k0 = causal
        row = lax.broadcasted_iota(jnp.int32, s.shape, 0)
        col = lax.broadcasted_iota(jnp.int32, s.shape, 1)
        s = jnp.where(k0 + col <= q0 + row, s, NEG)
    m_prev = m_ref[...]
    m_new = jnp.maximum(m_prev, jnp.max(s, axis=-1, keepdims=True))
    alpha = jnp.exp(m_prev - m_new)
    p = jnp.exp(s - m_new).astype(BF16)
    acc_ref[...] = alpha * acc_ref[...] + jnp.dot(p, v_aug, preferred_element_type=F32)
    m_ref[...] = m_new


def _moba_kernel(q_ref, k_ref, v_ref, o_ref, kmean_ref, qa_ref, m_ref, acc_ref):
    i = pl.program_id(2)
    bs = MOBA_BLOCK
    seq = k_ref.shape[0]
    hp = q_ref.shape[1] // HEAD_DIM
    nt = NT_DIMS
    hcol = lambda h: slice(h * HEAD_DIM, (h + 1) * HEAD_DIM)

    @pl.when(i == 0)
    def _():
        row = lax.broadcasted_iota(jnp.int32, (LANES, seq), 0)
        col = lax.broadcasted_iota(jnp.int32, (LANES, seq), 1)
        ind = jnp.where(col // bs == row, 1.0, 0.0).astype(BF16)
        kmean_ref[...] = (jnp.dot(ind, k_ref[...], preferred_element_type=F32)
                          * (1.0 / bs)).astype(BF16)

    nb = seq // bs
    nbp = -(-nb // 8) * 8
    blk_t = lax.broadcasted_iota(jnp.int32, (nbp, bs), 0)
    past_t = blk_t < i
    for h in range(hp):
        q = q_ref[:, hcol(h)]
        gate_t = lax.dot_general(kmean_ref[0:nbp, hcol(h)], q, nt, preferred_element_type=F32)
        sel_t = _first_index_topk_mask_t(jnp.where(past_t, gate_t, NEG), blk_t.astype(F32), MOBA_TOPK)
        notsel_t = jnp.where(((sel_t > 0.0) & past_t) | (blk_t == i), 0.0, 1.0)
        if nbp < LANES:
            notsel_t = jnp.concatenate([notsel_t, jnp.ones((LANES - nbp, bs), F32)], axis=0)
        qa_ref[h] = jnp.concatenate([q, notsel_t.T.astype(BF16)], axis=-1)

    tk = ATT_TK
    blk_per_tile = tk // bs
    n_tiles = (i + blk_per_tile) // blk_per_tile
    key_blk = lax.broadcasted_iota(jnp.int32, (tk, LANES), 0) // bs
    key_col = lax.broadcasted_iota(jnp.int32, (tk, LANES), 1)
    ones_v = jnp.ones((tk, LANES), BF16)
    m_ref[...] = jnp.full(m_ref.shape, M_INIT, F32)
    acc_ref[...] = jnp.zeros_like(acc_ref)

    def tile(jt, causal):
        off = pl.multiple_of(jt * tk, tk)
        bias = jnp.where(key_col == jt * blk_per_tile + key_blk, -MASK_BIG, 0.0).astype(BF16)
        for h in range(hp):
            k_aug = jnp.concatenate([k_ref[pl.ds(off, tk), hcol(h)], bias], axis=-1)
            v_aug = jnp.concatenate([v_ref[pl.ds(off, tk), hcol(h)], ones_v], axis=-1)
            _flash_update(qa_ref[h], k_aug, v_aug, m_ref.at[h], acc_ref.at[h],
                          (i * bs, off) if causal else None)

    def body(jt, carry):
        tile(jt, False)
        return carry

    lax.fori_loop(0, n_tiles - 1, body, 0)
    tile(n_tiles - 1, True)
    for h in range(hp):
        acc = acc_ref[h]
        o_ref[:, hcol(h)] = (acc[:, :HEAD_DIM] / acc[:, HEAD_DIM:]).astype(o_ref.dtype)


def moba_attention_pallas(zq, zk, zv, *, n_heads, q_off=0, k_off=0, v_off=0):
    B, S, _ = zq.shape
    bs = MOBA_BLOCK
    hp = min(MOBA_HEADS_PER_STEP, n_heads)
    hw = hp * HEAD_DIM
    assert S % ATT_TK == 0 and S // bs <= LANES and n_heads % hp == 0
    assert q_off % hp == 0 and k_off % hp == 0 and v_off % hp == 0
    return pl.pallas_call(
        _moba_kernel,
        grid=(B, n_heads // hp, S // bs),
        in_specs=[pl.BlockSpec((None, bs, hw), lambda b, h, i: (b, i, q_off // hp + h)),
                  pl.BlockSpec((None, S, hw), lambda b, h, i: (b, 0, k_off // hp + h)),
                  pl.BlockSpec((None, S, hw), lambda b, h, i: (b, 0, v_off // hp + h))],
        out_specs=pl.BlockSpec((None, bs, hw), lambda b, h, i: (b, i, h)),
        out_shape=jax.ShapeDtypeStruct((B, S, n_heads * HEAD_DIM), BF16),
        scratch_shapes=[pltpu.VMEM((LANES, hw), BF16), pltpu.VMEM((hp, bs, 2 * HEAD_DIM), BF16),
                        pltpu.VMEM((hp, bs, 1), F32), pltpu.VMEM((hp, bs, 2 * HEAD_DIM), F32)],
        compiler_params=pltpu.CompilerParams(
            dimension_semantics=("parallel", "parallel", "arbitrary"),
            vmem_limit_bytes=VMEM_LIMIT_BYTES),
        name="moba_attention",
    )(zq, zk, zv)


NSA_TQ = 256
CMP_ROW = NSA_CMP_STRIDE * HEAD_DIM


def _nsa_compress_kernel(x_ref, pe_ref, w1_ref, w2_ref, o_ref):
    x = x_ref[...].astype(F32)
    top = (x + pe_ref[0:1, :]).astype(BF16)
    bot = (x + pe_ref[1:2, :]).astype(BF16)
    a = jnp.dot(top, w1_ref[0:CMP_ROW, :], preferred_element_type=F32)
    b = jnp.dot(bot, w1_ref[CMP_ROW:2 * CMP_ROW, :], preferred_element_type=F32)
    pre = a + pltpu.roll(b, b.shape[0] - 1, 0)
    hid = jax.nn.gelu(pre)
    o_ref[...] = jnp.dot(hid.astype(BF16), w2_ref[...], preferred_element_type=F32).astype(o_ref.dtype)


def nsa_compress_pallas(x, pe, w1, w2):
    B, S, gw = x.shape
    G = gw // HEAD_DIM
    nrow = S // NSA_CMP_STRIDE
    xr = x.reshape(B, nrow, NSA_CMP_STRIDE, G, HEAD_DIM).transpose(0, 3, 1, 2, 4).reshape(B, G, nrow, CMP_ROW)
    pe2 = pe.astype(F32).reshape(2, CMP_ROW)
    return pl.pallas_call(
        _nsa_compress_kernel,
        grid=(B, G),
        in_specs=[pl.BlockSpec((None, None, nrow, CMP_ROW), lambda b, g: (b, g, 0, 0)),
                  pl.BlockSpec((2, CMP_ROW), lambda b, g: (0, 0)),
                  pl.BlockSpec((2 * CMP_ROW, HEAD_DIM), lambda b, g: (0, 0)),
                  pl.BlockSpec((HEAD_DIM, HEAD_DIM), lambda b, g: (0, 0))],
        out_specs=pl.BlockSpec((None, None, nrow, HEAD_DIM), lambda b, g: (b, g, 0, 0)),
        out_shape=jax.ShapeDtypeStruct((B, G, nrow, HEAD_DIM), BF16),
        compiler_params=pltpu.CompilerParams(
            dimension_semantics=("parallel", "parallel"), vmem_limit_bytes=VMEM_LIMIT_BYTES),
        name="nsa_compress",
    )(xr, pe2, w1.astype(BF16), w2.astype(BF16))


def _nsa_kernel(q_ref, kc_ref, vc_ref, ks_ref, vs_ref, kw_ref, vw_ref, g_ref, wmap_ref, o_ref,
                m_ref, acc_ref, out_ref, ns_ref):
    i = pl.program_id(1)
    tq = NSA_TQ
    ng = kc_ref.shape[0]
    R = NSA_HEADS // NSA_KV_HEADS
    nt = NT_DIMS
    row = lax.broadcasted_iota(jnp.int32, (tq, tq), 0)
    col = lax.broadcasted_iota(jnp.int32, (tq, tq), 1)
    gates = jax.nn.sigmoid(g_ref[...])
    gcol = lambda g: slice(g * HEAD_DIM, (g + 1) * HEAD_DIM)

    def qh(g, r):
        return q_ref[:, (g * R + r) * HEAD_DIM:(g * R + r + 1) * HEAD_DIM]

    def gate(g, r, c):
        return gates[:, g * LANES + 3 * r + c:g * LANES + 3 * r + c + 1]

    ncp = kc_ref.shape[1]
    cmask = (lax.broadcasted_iota(jnp.int32, (tq, ncp), 1) * NSA_CMP_STRIDE + (NSA_CMP_BLOCK - 1)
             <= i * tq + lax.broadcasted_iota(jnp.int32, (tq, ncp), 0))
    ns = ks_ref.shape[0] // NSA_SEL_BLOCK
    blk_t = lax.broadcasted_iota(jnp.int32, (ns, tq), 0)
    q_blk_t = (i * tq + lax.broadcasted_iota(jnp.int32, (ns, tq), 1)) // NSA_SEL_BLOCK
    valid_t = blk_t <= q_blk_t
    forced_t = (blk_t == 0) | (blk_t >= q_blk_t - 1)
    for g in range(ng):
        imp = jnp.zeros((tq, LANES), F32)
        vc_aug = jnp.concatenate([vc_ref[g], jnp.ones((ncp, LANES), BF16), wmap_ref[...]], axis=-1)
        for r in range(R):
            s = lax.dot_general(qh(g, r), kc_ref[g], nt, preferred_element_type=F32)
            s = jnp.where(cmask, s, NEG)
            e = jnp.where(cmask, jnp.exp(s - jnp.max(s, axis=-1, keepdims=True)), 0.0).astype(BF16)
            acc = jnp.dot(e, vc_aug, preferred_element_type=F32)
            l = acc[:, HEAD_DIM:2 * HEAD_DIM]
            inv = jnp.where(l > 0.0, 1.0 / l, 0.0)
            imp = imp + acc[:, 2 * HEAD_DIM:] * inv
            out_ref[g * R + r] = gate(g, r, 0) * (acc[:, :HEAD_DIM] * inv)
        impm_t = jnp.where(valid_t, imp.T[:ns] + jnp.where(forced_t, NSA_FORCE_BONUS, 0.0), NEG)
        sel_t = _first_index_topk_mask_t(impm_t, blk_t.astype(F32), NSA_SEL_TOPK)
        notsel_t = jnp.where((sel_t > 0.0) & valid_t, 0.0, 1.0)
        if ns < LANES:
            notsel_t = jnp.concatenate([notsel_t, jnp.ones((LANES - ns, tq), F32)], axis=0)
        ns_ref[g] = notsel_t.T.astype(BF16)

    ones_w = jnp.ones((tq, LANES), BF16)
    w_tiles = []
    for d in range(NSA_WINDOW // tq + 1):
        off = pl.multiple_of(jnp.maximum(i - d, 0) * tq, tq)
        gone = jnp.where(i >= d, 0, 2 * NSA_WINDOW + tq)
        dist = d * tq + row - col + gone
        w_tiles.append((off, (dist >= 0) & (dist < NSA_WINDOW)))
    for g in range(ng):
        for r in range(R):
            ss = []
            for off, mask in w_tiles:
                s = lax.dot_general(qh(g, r), kw_ref[pl.ds(off, tq), gcol(g)], nt, preferred_element_type=F32)
                ss.append(jnp.where(mask, s, NEG))
            m = jnp.max(ss[0], axis=-1, keepdims=True)
            for s in ss[1:]:
                m = jnp.maximum(m, jnp.max(s, axis=-1, keepdims=True))
            acc = jnp.zeros((tq, 2 * HEAD_DIM), F32)
            for (off, _), s in zip(w_tiles, ss):
                v_aug = jnp.concatenate([vw_ref[pl.ds(off, tq), gcol(g)], ones_w], axis=-1)
                acc = acc + jnp.dot(jnp.exp(s - m).astype(BF16), v_aug, preferred_element_type=F32)
            out_ref[g * R + r] = out_ref[g * R + r] + gate(g, r, 2) * (acc[:, :HEAD_DIM] / acc[:, HEAD_DIM:])

    tk = ATT_TK
    blk_per_tile = tk // NSA_SEL_BLOCK
    n_tiles = (i * tq) // tk + 1
    key_blk = lax.broadcasted_iota(jnp.int32, (tk, LANES), 0) // NSA_SEL_BLOCK
    key_col = lax.broadcasted_iota(jnp.int32, (tk, LANES), 1)
    ones_v = jnp.ones((tk, LANES), BF16)
    m_ref[...] = jnp.full(m_ref.shape, M_INIT, F32)
    acc_ref[...] = jnp.zeros_like(acc_ref)

    def sel_tile(jt, causal):
        off = pl.multiple_of(jt * tk, tk)
        bias = jnp.where(key_col == jt * blk_per_tile + key_blk, -MASK_BIG, 0.0).astype(BF16)
        for g in range(ng):
            k_aug = jnp.concatenate([ks_ref[pl.ds(off, tk), gcol(g)], bias], axis=-1)
            v_aug = jnp.concatenate([vs_ref[pl.ds(off, tk), gcol(g)], ones_v], axis=-1)
            for r in range(R):
                q_aug = jnp.concatenate([qh(g, r), ns_ref[g]], axis=-1)
                _flash_update(q_aug, k_aug, v_aug, m_ref.at[g * R + r], acc_ref.at[g * R + r],
                              (i * tq, off) if causal else None)

    def body(jt, carry):
        sel_tile(jt, False)
        return carry

    lax.fori_loop(0, n_tiles - 1, body, 0)
    sel_tile(n_tiles - 1, True)
    for g in range(ng):
        for r in range(R):
            h = g * R + r
            acc = acc_ref[h]
            o_s = acc[:, :HEAD_DIM] / acc[:, HEAD_DIM:]
            o_ref[:, h * HEAD_DIM:(h + 1) * HEAD_DIM] = (out_ref[h] + gate(g, r, 1) * o_s).astype(o_ref.dtype)


def nsa_attention_pallas(zq, kc, vc, zks, zvs, zkw, zvw, gate_logits, *, q_off, ks_off, vs_off, kw_off, vw_off):
    B, S, _ = zq.shape
    G, tq = NSA_KV_HEADS, NSA_TQ
    R = NSA_HEADS // G
    assert S % ATT_TK == 0 and ATT_TK % tq == 0 and S // NSA_SEL_BLOCK <= LANES
    nc = (S - NSA_CMP_BLOCK) // NSA_CMP_STRIDE + 1
    ncp = kc.shape[2]
    ns = S // NSA_SEL_BLOCK
    wmap = jnp.pad(cmp_to_sel_weights(nc, ns), ((0, ncp - nc), (0, LANES - ns))).astype(BF16)
    H = NSA_HEADS
    qw = H * HEAD_DIM
    assert q_off % H == 0 and all(off % G == 0 for off in (ks_off, vs_off, kw_off, vw_off))
    kv_spec = lambda off: pl.BlockSpec((None, S, G * HEAD_DIM), lambda b, i: (b, 0, off // G))
    c_spec = pl.BlockSpec((None, G, ncp, HEAD_DIM), lambda b, i: (b, 0, 0, 0))
    return pl.pallas_call(
        _nsa_kernel,
        grid=(B, S // tq),
        in_specs=[pl.BlockSpec((None, tq, qw), lambda b, i: (b, i, q_off // H)),
                  c_spec, c_spec, kv_spec(ks_off), kv_spec(vs_off), kv_spec(kw_off), kv_spec(vw_off),
                  pl.BlockSpec((None, tq, G * LANES), lambda b, i: (b, i, 0)),
                  pl.BlockSpec((ncp, LANES), lambda b, i: (0, 0))],
        out_specs=pl.BlockSpec((None, tq, qw), lambda b, i: (b, i, 0)),
        out_shape=jax.ShapeDtypeStruct((B, S, qw), BF16),
        scratch_shapes=[pltpu.VMEM((H, tq, 1), F32), pltpu.VMEM((H, tq, 2 * HEAD_DIM), F32),
                        pltpu.VMEM((H, tq, HEAD_DIM), F32), pltpu.VMEM((G, tq, LANES), BF16)],
        compiler_params=pltpu.CompilerParams(
            dimension_semantics=("parallel", "arbitrary"),
            vmem_limit_bytes=VMEM_LIMIT_BYTES),
        name="nsa_attention",
    )(zq, kc, vc, zks, zvs, zkw, zvw, gate_logits, wmap)


MOE_TM = 256
MOE_TN = 512
ROUTER_TM = 256


def _router_kernel(x_ref, g_ref, w_ref, b_ref, t_ref, ids_ref, wts_ref):
    x = x_ref[...]
    t = x * lax.rsqrt(jnp.mean(x * x, axis=-1, keepdims=True) + NORM_EPS) * g_ref[...]
    t_ref[...] = t
    logits = jnp.dot(t.astype(BF16), w_ref[...], preferred_element_type=F32) + b_ref[...]
    coli = lax.broadcasted_iota(jnp.int32, logits.shape, 1)
    colf = coli.astype(F32)
    first = lambda hit: jnp.min(jnp.where(hit, colf, float(LANES)), axis=-1, keepdims=True)
    is_g = coli < N_GROUPS
    gl = jnp.where(is_g, logits, -jnp.inf)
    gmax = jnp.max(gl, axis=-1, keepdims=True)
    g_sel = first(gl == gmax)
    p_g = 1.0 / jnp.sum(jnp.where(is_g, jnp.exp(logits - gmax), 0.0), axis=-1, keepdims=True)
    lo = N_GROUPS + EXPERTS_PER_GROUP * g_sel
    el = jnp.where((colf >= lo) & (colf < lo + EXPERTS_PER_GROUP), logits, -jnp.inf)
    v1 = jnp.max(el, axis=-1, keepdims=True)
    i1 = first(el == v1)
    el2 = jnp.where(colf == i1, -jnp.inf, el)
    v2 = jnp.max(el2, axis=-1, keepdims=True)
    i2 = first(el2 == v2)
    e = jnp.exp(v2 - v1)
    w1 = p_g / (1.0 + e)
    w2 = p_g * e / (1.0 + e)
    ids_ref[...] = jnp.where(coli == 0, i1 - N_GROUPS, jnp.where(coli == 1, i2 - N_GROUPS, 0.0)).astype(jnp.int32)
    wts_ref[...] = jnp.where(coli == 0, w1, jnp.where(coli == 1, w2, 0.0))


def moe_router_pallas(x2, gain, w_group, b_group, w_expert, b_expert):
    T, D = x2.shape
    tm = ROUTER_TM
    nr = N_GROUPS + N_EXPERTS
    w = jnp.pad(jnp.concatenate([w_group, w_expert], axis=1), ((0, 0), (0, LANES - nr))).astype(BF16)
    b = jnp.pad(jnp.concatenate([b_group, b_expert]), (0, LANES - nr)).astype(F32)[None, :]
    t, ids, wts = pl.pallas_call(
        _router_kernel,
        grid=(T // tm,),
        in_specs=[pl.BlockSpec((tm, D), lambda i: (i, 0)),
                  pl.BlockSpec((1, D), lambda i: (0, 0)),
                  pl.BlockSpec((D, LANES), lambda i: (0, 0)),
                  pl.BlockSpec((1, LANES), lambda i: (0, 0))],
        out_specs=[pl.BlockSpec((tm, D), lambda i: (i, 0)),
                   pl.BlockSpec((tm, LANES), lambda i: (i, 0)),
                   pl.BlockSpec((tm, LANES), lambda i: (i, 0))],
        out_shape=[jax.ShapeDtypeStruct((T, D), F32),
                   jax.ShapeDtypeStruct((T, LANES), jnp.int32),
                   jax.ShapeDtypeStruct((T, LANES), F32)],
        compiler_params=pltpu.CompilerParams(
            dimension_semantics=("parallel",), vmem_limit_bytes=VMEM_LIMIT_BYTES),
        name="moe_router",
    )(x2, gain.astype(F32)[None, :], w, b)
    return t, ids[:, :TOPK_IN_GROUP], wts[:, :TOPK_IN_GROUP]


MOE_DMA_GROUP = 8


def _moe_up_kernel(tile_e_ref, n_used_ref, tile_rows_ref, row_tok_ref, t_hbm, rw_ref, wg_ref, wu_ref, hid_ref,
                   xw_ref, xb_ref, sem):
    k = pl.program_id(0)
    tm = xb_ref.shape[0]
    n_used = n_used_ref[0]

    def row_copy(tok, r):
        return pltpu.make_async_copy(t_hbm.at[pl.ds(tok, 1), :], xw_ref.at[pl.ds(r, 1), :], sem.at[0])

    def n_groups(tile):
        return (tile_rows_ref[tile] + MOE_DMA_GROUP - 1) // MOE_DMA_GROUP

    def start_gather(tile):
        def issue(g, carry):
            for j in range(MOE_DMA_GROUP):
                r = g * MOE_DMA_GROUP + j
                row_copy(row_tok_ref[tile * tm + r], r).start()
            return carry
        lax.fori_loop(0, n_groups(tile), issue, 0)

    @pl.when(k == 0)
    def _():
        xw_ref[...] = jnp.zeros_like(xw_ref)
        start_gather(0)

    @pl.when(k < n_used)
    def _():
        def wait_group(g, carry):
            for j in range(MOE_DMA_GROUP):
                row_copy(0, g * MOE_DMA_GROUP + j).wait()
            return carry
        lax.fori_loop(0, n_groups(k), wait_group, 0)
        xb_ref[...] = xw_ref[...].astype(BF16)

        @pl.when(k + 1 < n_used)
        def _():
            start_gather(k + 1)

        x = xb_ref[...]
        hg = jnp.dot(x, wg_ref[...].astype(BF16), preferred_element_type=F32)
        hu = jnp.dot(x, wu_ref[...].astype(BF16), preferred_element_type=F32)
        hid_ref[...] = (jax.nn.silu(hg) * hu * rw_ref[...]).astype(hid_ref.dtype)

    @pl.when(k >= n_used)
    def _():
        hid_ref[...] = jnp.zeros_like(hid_ref)


def _moe_down_kernel(tile_e_ref, n_used_ref, hid_ref, wd_ref, o_ref):
    k = pl.program_id(0)

    @pl.when(k < n_used_ref[0])
    def _():
        hid = hid_ref[...]
        for c in range(0, o_ref.shape[1], MOE_TN):
            o_ref[:, c:c + MOE_TN] = jnp.dot(hid, wd_ref[:, c:c + MOE_TN].astype(BF16),
                                             preferred_element_type=F32)

    @pl.when(k >= n_used_ref[0])
    def _():
        o_ref[...] = jnp.zeros_like(o_ref)


COMBINE_TC = 128


def _moe_combine_kernel(pos_ref, x_ref, g_ref, y_hbm, o_ref, n_ref, buf, sem):
    i = pl.program_id(0)
    tc = x_ref.shape[0]
    nk = TOPK_IN_GROUP

    def row_copy(slot, a, p):
        return pltpu.make_async_copy(y_hbm.at[pl.ds(p, 1), :], buf.at[slot, pl.ds(a, 1), :], sem.at[slot])

    def start_gather(tile, slot):
        def issue(r, carry):
            for kk in range(nk):
                row_copy(slot, kk * tc + r, pos_ref[(tile * tc + r) * nk + kk]).start()
            return carry
        lax.fori_loop(0, tc, issue, 0, unroll=4)

    slot = i % 2

    @pl.when(i == 0)
    def _():
        start_gather(0, 0)

    def wait_row(a, carry):
        row_copy(slot, a, 0).wait()
        return carry
    lax.fori_loop(0, nk * tc, wait_row, 0, unroll=8)

    @pl.when(i + 1 < pl.num_programs(0))
    def _():
        start_gather(i + 1, 1 - slot)

    v = x_ref[...]
    for kk in range(nk):
        v = v + buf[slot, kk * tc:(kk + 1) * tc, :]
    o_ref[...] = v
    n_ref[...] = (v * lax.rsqrt(jnp.mean(v * v, axis=-1, keepdims=True) + NORM_EPS)
                  * g_ref[...]).astype(n_ref.dtype)


def moe_combine_pallas(x2, y_rows, pos, next_gain, next_dtype):
    T, D = x2.shape
    tc = COMBINE_TC
    row_blk = pl.BlockSpec((tc, D), lambda i, *_: (i, 0))
    return pl.pallas_call(
        _moe_combine_kernel,
        grid_spec=pltpu.PrefetchScalarGridSpec(
            num_scalar_prefetch=1,
            grid=(T // tc,),
            in_specs=[row_blk, pl.BlockSpec((1, D), lambda i, *_: (0, 0)),
                      pl.BlockSpec(memory_space=pl.ANY)],
            out_specs=[row_blk, row_blk],
            scratch_shapes=[pltpu.VMEM((2, TOPK_IN_GROUP * tc, D), F32),
                            pltpu.SemaphoreType.DMA((2,))]),
        out_shape=[jax.ShapeDtypeStruct((T, D), F32), jax.ShapeDtypeStruct((T, D), next_dtype)],
        compiler_params=pltpu.CompilerParams(
            dimension_semantics=("arbitrary",), vmem_limit_bytes=VMEM_LIMIT_BYTES),
        name="moe_combine",
    )(pos.reshape(-1), x2, next_gain.astype(F32)[None, :], y_rows)


def moe_experts_pallas(t_packed, row_tok, row_w, tile_e, n_used, tile_rows, w_gate, w_up, w_down, layer):
    P = row_tok.shape[0]
    D = w_gate.shape[-2]
    F = w_gate.shape[-1]
    tm = MOE_TM
    l = int(layer)
    w_in = pl.BlockSpec((None, None, D, F), lambda k, te, *_: (l, te[k], 0, 0))
    hid = pl.pallas_call(
        _moe_up_kernel,
        grid_spec=pltpu.PrefetchScalarGridSpec(
            num_scalar_prefetch=4,
            grid=(P // tm,),
            in_specs=[pl.BlockSpec(memory_space=pl.ANY),
                      pl.BlockSpec((tm, 1), lambda k, *_: (k, 0)),
                      w_in, w_in],
            out_specs=pl.BlockSpec((tm, F), lambda k, *_: (k, 0)),
            scratch_shapes=[pltpu.VMEM((tm, D), F32), pltpu.VMEM((tm, D), BF16),
                            pltpu.SemaphoreType.DMA((1,))]),
        out_shape=jax.ShapeDtypeStruct((P, F), BF16),
        compiler_params=pltpu.CompilerParams(
            dimension_semantics=("arbitrary",), vmem_limit_bytes=VMEM_LIMIT_BYTES),
        name="moe_up",
    )(tile_e, n_used, tile_rows, row_tok, t_packed, row_w, w_gate, w_up)
    return pl.pallas_call(
        _moe_down_kernel,
        grid_spec=pltpu.PrefetchScalarGridSpec(
            num_scalar_prefetch=2,
            grid=(P // tm,),
            in_specs=[pl.BlockSpec((tm, F), lambda k, *_: (k, 0)),
                      pl.BlockSpec((None, None, F, D), lambda k, te, *_: (l, te[k], 0, 0))],
            out_specs=pl.BlockSpec((tm, D), lambda k, *_: (k, 0))),
        out_shape=jax.ShapeDtypeStruct((P, D), F32),
        compiler_params=pltpu.CompilerParams(
            dimension_semantics=("parallel",), vmem_limit_bytes=VMEM_LIMIT_BYTES),
        name="moe_down",
    )(tile_e, n_used, hid, w_down)


def moe_dispatch_plan(ids, wts):
    T, K = ids.shape
    n = T * K
    P = n + N_EXPERTS * MOE_TM
    flat = ids.reshape(n)
    onehot = (flat[:, None] == jnp.arange(N_EXPERTS, dtype=jnp.int32)[None, :]).astype(jnp.int32)
    csum = jnp.cumsum(onehot, axis=0)
    counts = csum[-1]
    padded = (counts + MOE_TM - 1) // MOE_TM * MOE_TM
    pend = jnp.cumsum(padded)
    pstart = pend - padded
    pos = jnp.sum(onehot * (csum - 1 + pstart[None, :]), axis=1)
    tok = (jnp.arange(n, dtype=jnp.int32) // K).astype(F32)
    info = jnp.zeros((P, 2), F32).at[pos].set(jnp.stack([tok, wts.reshape(n)], axis=1))
    row_tok = info[:, 0].astype(jnp.int32)
    row_w = info[:, 1:2]
    tile_start = jnp.arange(P // MOE_TM, dtype=jnp.int32) * MOE_TM
    tile_e = jnp.minimum(jnp.sum((pend[None, :] <= tile_start[:, None]).astype(jnp.int32), axis=1),
                         N_EXPERTS - 1).astype(jnp.int32)
    n_used = (pend[-1] // MOE_TM).astype(jnp.int32)[None]
    e_hot = (tile_e[:, None] == jnp.arange(N_EXPERTS, dtype=jnp.int32)[None, :]).astype(jnp.int32)
    in_expert = tile_start - jnp.sum(e_hot * pstart[None, :], axis=1)
    tile_rows = jnp.clip(jnp.sum(e_hot * counts[None, :], axis=1) - in_expert, 0, MOE_TM).astype(jnp.int32)
    tile_rows = jnp.where(tile_start < pend[-1], tile_rows, 0)
    return pos.reshape(T, K), row_tok, row_w, tile_e, n_used, tile_rows


def hierarchical_moe_pallas(x, gain, w_group, b_group, w_expert, b_expert, w_gate, w_up, w_down, layer,
                            next_gain, next_dtype):
    B, S, D = x.shape
    x2 = x.reshape(B * S, D)
    t, ids, wts = moe_router_pallas(x2, gain, w_group, b_group, w_expert, b_expert)
    pos, row_tok, row_w, tile_e, n_used, tile_rows = moe_dispatch_plan(ids, wts)
    y = moe_experts_pallas(t, row_tok, row_w, tile_e, n_used, tile_rows, w_gate, w_up, w_down, layer)
    out, normed = moe_combine_pallas(x2, y, pos, next_gain, next_dtype)
    return out.reshape(B, S, D), normed


def rope_tables(positions):
    inv = ROPE_THETA ** (-jnp.arange(0, HEAD_DIM, 2, dtype=F32) / HEAD_DIM)
    ang = positions.astype(F32)[..., None] * inv
    return jnp.cos(ang)[:, :, None, :], jnp.sin(ang)[:, :, None, :]


def cmp_to_sel_weights(nc, ns):
    r = NSA_SEL_BLOCK // NSA_CMP_STRIDE
    m = NSA_CMP_BLOCK // NSA_CMP_STRIDE
    c = jnp.arange(nc)[:, None] - r * jnp.arange(ns)[None, :]
    w = jnp.minimum(jnp.minimum(c + 1, r + m - 1 - c), min(r, m))
    return jnp.clip(w, 0, None).astype(F32)


def kernel(x, mem, positions, norm_mix, w_mix_in, lru_conv_w, lru_conv_b, lru_w_a, lru_b_a, lru_w_i, lru_b_i, lru_lambda, sc_conv_w, nsa_pe_k, nsa_w1_k, nsa_w2_k, nsa_pe_v, nsa_w1_v, nsa_w2_v, w_merge_gate, b_merge_gate, w_branch_out, w_mix_out, norm_xattn, norm_mem, xa_w_q, xa_w_k, xa_w_v, xa_w_o, norm_moe, moe_w_group, moe_b_group, moe_w_expert, moe_b_expert, moe_w_gate, moe_w_up, moe_w_down, norm_final):
    B, S, D = x.shape
    T = B * S
    cos, sin = rope_tables(positions)
    cos_t = jnp.concatenate([cos, cos], axis=-1).reshape(T, HEAD_DIM)
    sin_t = jnp.concatenate([-sin, sin], axis=-1).reshape(T, HEAD_DIM)
    offs = [int(v) for v in np.concatenate([[0], np.cumsum(MIX_SPLITS)])]
    blk = [v // HEAD_DIM for v in offs[:11]]
    n_head_cols = offs[10]
    rope_flags = np.zeros((n_head_cols // HEAD_DIM,), np.int32)
    for k in (1, 4, 6, 8):
        rope_flags[blk[k]:blk[k + 1]] = ROPE_PLAIN
    for k in (0, 3):
        rope_flags[blk[k]:blk[k + 1]] = ROPE_SCALED
    rope_flags = jnp.asarray(rope_flags)
    G, R = NSA_KV_HEADS, NSA_HEADS // NSA_KV_HEADS
    w_in_nk = jnp.swapaxes(w_mix_in, 1, 2)
    assert offs[10] % 8 == 0 and offs[11] % 8 == 0
    M = mem.shape[1]
    h = rms_norm_pallas(x.reshape(T, D), norm_mix[0], BF16)
    for l in range(DEPTH):
        x2 = x.reshape(T, D)
        z_head = matmul(h, w_in_nk, w_lead=(l,), w_is_nk=True, n_cols=n_head_cols, out_dtype=BF16,
                        rope=(cos_t, sin_t, rope_flags)).reshape(B, S, n_head_cols)
        z_tail = matmul(h, w_in_nk, w_lead=(l,), w_is_nk=True, row0=offs[11],
                        n_cols=offs[16] - offs[11]).reshape(B, S, -1)
        z_g = matmul(h, w_in_nk, w_lead=(l,), w_is_nk=True, row0=offs[10], n_cols=LANES)
        z_gate = jnp.pad(z_g[:, :NSA_GATE_W].reshape(B, S, G, R * 3),
                         ((0, 0), (0, 0), (0, 0), (0, LANES - R * 3))).reshape(B, S, G * LANES)
        o_moba = moba_attention_pallas(z_head, z_head, z_head, n_heads=MOBA_HEADS,
                                       q_off=blk[0], k_off=blk[1], v_off=blk[2])
        kc = nsa_compress_pallas(z_head[:, :, offs[4]:offs[5]], nsa_pe_k[l], nsa_w1_k[l], nsa_w2_k[l])
        vc = nsa_compress_pallas(z_head[:, :, offs[5]:offs[6]], nsa_pe_v[l], nsa_w1_v[l], nsa_w2_v[l])
        o_nsa = nsa_attention_pallas(z_head, kc, vc, z_head, z_head, z_head, z_head, z_gate,
                                     q_off=blk[3], ks_off=blk[6], vs_off=blk[7], kw_off=blk[8], vw_off=blk[9])
        o_lru, o_conv = lru_conv_pallas(z_tail, l, lru_conv_w, lru_conv_b, lru_w_a, lru_b_a,
                                        lru_w_i, lru_b_i, lru_lambda, sc_conv_w)
        o_all = jnp.stack([o_moba, o_lru, o_conv, o_nsa]).reshape(N_BRANCH, T, -1)
        merged = merge_branches_pallas(h, o_all, w_merge_gate, b_merge_gate, w_branch_out, l)
        x2 = matmul(merged, w_mix_out, w_lead=(l,), residual=x2)
        mem_n = rms_norm_pallas(mem.reshape(B * M, D), norm_mem[l], BF16)
        xk = matmul(mem_n, xa_w_k, w_lead=(l,), out_dtype=BF16).reshape(B, M, XA_WIDTH)
        xv = matmul(mem_n, xa_w_v, w_lead=(l,), out_dtype=BF16).reshape(B, M, XA_WIDTH)
        x = cross_attention_pallas(x2.reshape(B, S, D), norm_xattn[l], xk, xv,
                                   xa_w_q[l].astype(BF16), xa_w_o[l].astype(BF16))
        last = l == DEPTH - 1
        x, h = hierarchical_moe_pallas(x, norm_moe[l], moe_w_group[l], moe_b_group[l],
                                       moe_w_expert[l], moe_b_expert[l], moe_w_gate,
                                       moe_w_up, moe_w_down, l,
                                       norm_final if last else norm_mix[l + 1], F32 if last else BF16)
    return h.reshape(B, S, D)
```

```python
import functools

import jax
import jax.numpy as jnp
import numpy as np
from jax import lax
from jax.experimental import pallas as pl
from jax.experimental.pallas import tpu as pltpu

F32 = jnp.float32
BF16 = jnp.bfloat16

D_MODEL = 4096
DEPTH = 2
HEAD_DIM = 128
ROPE_THETA = 10000.0
NORM_EPS = 1e-6
NEG = -1e30
N_BRANCH = 4
BRANCH_WIDTH = D_MODEL // 4
MOBA_HEADS = BRANCH_WIDTH // HEAD_DIM
MOBA_BLOCK = 256
MOBA_TOPK = 3
LRU_WIDTH = BRANCH_WIDTH
LRU_C = 8.0
SC_WIDTH = BRANCH_WIDTH
NSA_HEADS = BRANCH_WIDTH // HEAD_DIM
NSA_KV_HEADS = NSA_HEADS // 4
NSA_CMP_BLOCK = 32
NSA_CMP_STRIDE = 16
NSA_SEL_BLOCK = 64
NSA_SEL_TOPK = 16
NSA_WINDOW = 512
NSA_FORCE_BONUS = 1e4
XA_HEADS = 4
XA_WIDTH = XA_HEADS * HEAD_DIM
N_GROUPS = 4
EXPERTS_PER_GROUP = 8
N_EXPERTS = N_GROUPS * EXPERTS_PER_GROUP
TOPK_IN_GROUP = 2
MOBA_W = MOBA_HEADS * HEAD_DIM
NSA_Q_W = NSA_HEADS * HEAD_DIM
NSA_KV_W = NSA_KV_HEADS * HEAD_DIM
NSA_GATE_W = NSA_HEADS * 3
MIX_SPLITS = (MOBA_W, MOBA_W, MOBA_W,
              NSA_Q_W, NSA_KV_W, NSA_KV_W, NSA_KV_W, NSA_KV_W, NSA_KV_W, NSA_KV_W, NSA_GATE_W,
              LRU_WIDTH, LRU_WIDTH,
              SC_WIDTH, SC_WIDTH, SC_WIDTH)

VMEM_LIMIT_BYTES = 56 * 1024 * 1024


ROPE_PLAIN, ROPE_SCALED = 1, 2


def _mm_kernel(*refs, rope, residual, w_is_nk):
    if rope:
        flags_ref, refs = refs[0], refs[1:]
    a_ref, w_ref = refs[0], refs[1]
    o_ref = refs[-1]
    dims = (((1,), (1,)), ((), ())) if w_is_nk else (((1,), (0,)), ((), ()))
    a = a_ref[...].astype(BF16)
    if rope:
        cos, sin = refs[2][...], refs[3][...]
        tn = o_ref.shape[1]
        step = 2 * HEAD_DIM
        for c0 in range(0, tn, step):
            w_blk = w_ref[c0:c0 + step, :] if w_is_nk else w_ref[:, c0:c0 + step]
            acc = lax.dot_general(a, w_blk.astype(BF16), dims, preferred_element_type=F32)
            for c in range(c0, c0 + step, HEAD_DIM):
                blk = acc[:, c - c0:c - c0 + HEAD_DIM]
                flag = flags_ref[(pl.program_id(1) * tn + c) // HEAD_DIM]
                scl = jnp.where(flag == ROPE_SCALED, HEAD_DIM ** -0.5, 1.0)
                cos_c = jnp.where(flag != 0, cos * scl, 1.0)
                sin_c = jnp.where(flag != 0, sin * scl, 0.0)
                rot = pltpu.roll(blk, HEAD_DIM // 2, 1)
                o_ref[:, c:c + HEAD_DIM] = (blk * cos_c + rot * sin_c).astype(o_ref.dtype)
        return
    acc = lax.dot_general(a, w_ref[...].astype(BF16), dims, preferred_element_type=F32)
    if residual:
        acc = acc + refs[2][...]
    o_ref[...] = acc.astype(o_ref.dtype)


def matmul(a, w, *, w_lead=(), w_is_nk=False, col0=0, row0=None, n_cols=None, out_dtype=F32, rope=None,
           residual=None, tm=1024, tn=512):
    m, k = a.shape
    n = w.shape[-2 if w_is_nk else -1] if n_cols is None else n_cols
    assert w.shape[-1 if w_is_nk else -2] == k and not (rope is not None and residual is not None)
    tm = min(tm, m)
    tn = min(tn, n)
    lead = tuple(int(v) for v in w_lead)
    if w_is_nk and row0 is not None:
        rows_per_slab = w.shape[-2]
        slab = int(np.ravel_multi_index(lead, w.shape[:-2])) if lead else 0
        w = w.reshape(-1, k)
        w_spec = pl.BlockSpec((pl.Element(tn), pl.Element(k)),
                              lambda i, j, *_: (pl.multiple_of(slab * rows_per_slab + row0 + j * tn, 8), 0))
    elif w_is_nk:
        w_spec = pl.BlockSpec((None,) * len(lead) + (tn, k), lambda i, j, *_: lead + (j + col0, 0))
    else:
        w_spec = pl.BlockSpec((None,) * len(lead) + (k, tn), lambda i, j, *_: lead + (0, j + col0))
    in_specs = [pl.BlockSpec((tm, k), lambda i, j, *_: (i, 0)), w_spec]
    args = [a, w]
    prefetch = []
    if rope is not None:
        cos, sin, flags = rope
        assert n % tn == 0 and flags.shape == (n // HEAD_DIM,)
        in_specs += [pl.BlockSpec((tm, HEAD_DIM), lambda i, j, *_: (i, 0))] * 2
        args += [cos, sin]
        prefetch = [flags]
    if residual is not None:
        in_specs.append(pl.BlockSpec((tm, tn), lambda i, j, *_: (i, j)))
        args.append(residual)
    return pl.pallas_call(
        functools.partial(_mm_kernel, rope=rope is not None, residual=residual is not None,
                          w_is_nk=w_is_nk),
        grid_spec=pltpu.PrefetchScalarGridSpec(
            num_scalar_prefetch=len(prefetch),
            grid=(pl.cdiv(m, tm), pl.cdiv(n, tn)),
            in_specs=in_specs,
            out_specs=pl.BlockSpec((tm, tn), lambda i, j, *_: (i, j))),
        out_shape=jax.ShapeDtypeStruct((m, n), out_dtype),
        compiler_params=pltpu.CompilerParams(
            dimension_semantics=("parallel", "parallel"),
            vmem_limit_bytes=VMEM_LIMIT_BYTES),
        name="matmul",
    )(*prefetch, *args)


def _rms_kernel(x_ref, g_ref, o_ref):
    x = x_ref[...]
    y = x * lax.rsqrt(jnp.mean(x * x, axis=-1, keepdims=True) + NORM_EPS)
    o_ref[...] = (y * g_ref[...]).astype(o_ref.dtype)


def rms_norm_pallas(x2, gain, out_dtype, *, tm=512):
    m, d = x2.shape
    tm = min(tm, m)
    return pl.pallas_call(
        _rms_kernel,
        grid=(m // tm,),
        in_specs=[pl.BlockSpec((tm, d), lambda i: (i, 0)), pl.BlockSpec((1, d), lambda i: (0, 0))],
        out_specs=pl.BlockSpec((tm, d), lambda i: (i, 0)),
        out_shape=jax.ShapeDtypeStruct((m, d), out_dtype),
        compiler_params=pltpu.CompilerParams(
            dimension_semantics=("parallel",), vmem_limit_bytes=VMEM_LIMIT_BYTES),
        name="rms_norm",
    )(x2, gain.astype(F32)[None, :])


def _merge_kernel(h_ref, o_ref, wg_ref, bg_ref, wu_ref, out_ref, acc_ref):
    n = pl.program_id(2)

    @pl.when((pl.program_id(0) == 0) & (pl.program_id(1) == 0) & (n == 0))
    def _():
        acc_ref[...] = jnp.zeros_like(acc_ref)

    h, o = h_ref[...], o_ref[...]
    half = out_ref.shape[1] // 2
    for c0 in (0, half):
        cs = slice(c0, c0 + half)
        gate = jax.nn.sigmoid(
            jnp.dot(h, wg_ref[:, cs].astype(BF16), preferred_element_type=F32) + bg_ref[:, cs])
        term = gate * jnp.dot(o, wu_ref[:, cs].astype(BF16), preferred_element_type=F32)
        acc = jnp.where(n == 0, 0.0, acc_ref[:, cs]) + term
        acc_ref[:, cs] = acc
        out_ref[:, cs] = acc.astype(out_ref.dtype)


def merge_branches_pallas(h, o_all, w_gate, b_gate, w_up, layer, *, tm=1024, tn=512):
    T, D = h.shape
    N, _, W = o_all.shape
    l = int(layer)
    return pl.pallas_call(
        _merge_kernel,
        grid=(T // tm, D // tn, N),
        in_specs=[pl.BlockSpec((tm, D), lambda i, j, n: (i, 0)),
                  pl.BlockSpec((None, tm, W), lambda i, j, n: (n, i, 0)),
                  pl.BlockSpec((None, None, D, tn), lambda i, j, n: (l, n, 0, j)),
                  pl.BlockSpec((None, None, 1, tn), lambda i, j, n: (l, n, 0, j)),
                  pl.BlockSpec((None, None, W, tn), lambda i, j, n: (l, n, 0, j))],
        out_specs=pl.BlockSpec((tm, tn), lambda i, j, n: (i, j)),
        out_shape=jax.ShapeDtypeStruct((T, D), BF16),
        scratch_shapes=[pltpu.VMEM((tm, tn), F32)],
        compiler_params=pltpu.CompilerParams(
            dimension_semantics=("parallel", "parallel", "arbitrary"),
            vmem_limit_bytes=VMEM_LIMIT_BYTES),
        name="merge_branches",
    )(h, o_all, w_gate, b_gate[:, :, None, :], w_up)


LRU_TS = 256
LRU_CW = 256


def _rows_back(x, prev, s, row):
    return jnp.where(row < s, pltpu.roll(prev, s, 0), pltpu.roll(x, s, 0))


def _lane_window(blocks, shift):
    if shift == 0:
        return jnp.concatenate([b[...] for b in blocks], axis=-1)
    back = LANES - shift
    rolled = [pltpu.roll(b[...], back, 1) for b in blocks]
    lane = lax.broadcasted_iota(jnp.int32, rolled[0].shape, 1)
    return jnp.concatenate([jnp.where(lane < back, rolled[j], rolled[j + 1])
                            for j in range(len(blocks) - 1)], axis=-1)


def _lru_conv_kernel(*refs, n_lane_blocks, shift):
    nb = n_lane_blocks
    rx, rg, cb, cc, cx = (_lane_window(refs[k * nb:(k + 1) * nb], shift) for k in range(5))
    (lw_ref, lb_ref, wa_ref, ba_ref, wi_ref, bi_ref, lam_ref, sw_ref,
     olru_ref, oconv_ref, px_ref, py_ref, h_ref) = refs[5 * nb:]

    @pl.when(pl.program_id(2) == 0)
    def _():
        px_ref[...] = jnp.zeros_like(px_ref)
        py_ref[...] = jnp.zeros_like(py_ref)
        h_ref[...] = jnp.zeros_like(h_ref)

    ts, cw = px_ref.shape
    row = lax.broadcasted_iota(jnp.int32, (ts, cw), 0)

    x = rx
    prev = px_ref[...]
    nk = lw_ref.shape[0]
    xc = lb_ref[...] + lw_ref[nk - 1:nk, :] * x
    for s in range(1, nk):
        xc = xc + lw_ref[nk - 1 - s:nk - s, :] * _rows_back(x, prev, s, row)
    px_ref[...] = x
    r_parts, i_parts = [], []
    for hh in range(cw // HEAD_DIM):
        cs = slice(hh * HEAD_DIM, (hh + 1) * HEAD_DIM)
        xh = xc[:, cs].astype(BF16)
        r_parts.append(jnp.dot(xh, wa_ref[hh].astype(BF16), preferred_element_type=F32))
        i_parts.append(jnp.dot(xh, wi_ref[hh].astype(BF16), preferred_element_type=F32))
    r = jax.nn.sigmoid(jnp.concatenate(r_parts, axis=-1) + ba_ref[...])
    gi = jax.nn.sigmoid(jnp.concatenate(i_parts, axis=-1) + bi_ref[...])
    log_a = -LRU_C * r * jax.nn.softplus(-lam_ref[...])
    a = jnp.exp(log_a)
    b = jnp.sqrt(1.0 - a * a) * (gi * xc)
    d = 1
    while d < ts:
        a_back = jnp.where(row < d, 1.0, pltpu.roll(a, d, 0))
        b_back = jnp.where(row < d, 0.0, pltpu.roll(b, d, 0))
        b = a * b_back + b
        a = a * a_back
        d *= 2
    h = b + a * h_ref[...]
    h_ref[...] = h[ts - 1:ts, :]
    olru_ref[...] = (h * jax.nn.gelu(rg)).astype(olru_ref.dtype)

    y = cc * cx
    prev_y = py_ref[...]
    nk = sw_ref.shape[0]
    conv = sw_ref[nk - 1:nk, :] * y
    for s in range(1, nk):
        conv = conv + sw_ref[nk - 1 - s:nk - s, :] * _rows_back(y, prev_y, s, row)
    py_ref[...] = y
    oconv_ref[...] = (cb * conv).astype(oconv_ref.dtype)


def lru_conv_pallas(z, layer, lru_conv_w, lru_conv_b, lru_w_a, lru_b_a, lru_w_i, lru_b_i, lru_lambda,
                    sc_conv_w, *, shift=0):
    B, S, _ = z.shape
    C = lru_conv_b.shape[-1]
    ts, cw = LRU_TS, LRU_CW
    nblk = C // cw
    l = int(layer)
    per_blk = cw // LANES
    nb = per_blk + (1 if shift else 0)
    lane_blk = lambda k, j: pl.BlockSpec(
        (None, ts, LANES), lambda b, c, t: (b, t, (k * nblk + c) * per_blk + j))
    cols = [lane_blk(k, j) for k in range(5) for j in range(nb)]
    par = lambda rows: pl.BlockSpec((None, rows, cw), lambda b, c, t: (l, 0, c))
    hw = pl.BlockSpec((None, cw // HEAD_DIM, HEAD_DIM, HEAD_DIM), lambda b, c, t: (l, c, 0, 0))
    L = lru_conv_b.shape[0]
    flat = lambda p: p.reshape(L, 1, C)
    out = pl.BlockSpec((None, ts, cw), lambda b, c, t: (b, t, c))
    return pl.pallas_call(
        functools.partial(_lru_conv_kernel, n_lane_blocks=nb, shift=shift),
        grid=(B, nblk, S // ts),
        in_specs=cols + [par(lru_conv_w.shape[1]), par(1), hw, par(1), hw, par(1), par(1),
                         par(sc_conv_w.shape[1])],
        out_specs=[out, out],
        out_shape=[jax.ShapeDtypeStruct((B, S, C), BF16)] * 2,
        scratch_shapes=[pltpu.VMEM((ts, cw), F32), pltpu.VMEM((ts, cw), F32), pltpu.VMEM((1, cw), F32)],
        compiler_params=pltpu.CompilerParams(
            dimension_semantics=("parallel", "parallel", "arbitrary"),
            vmem_limit_bytes=VMEM_LIMIT_BYTES),
        name="lru_conv",
    )(*([z] * len(cols)), lru_conv_w, flat(lru_conv_b), lru_w_a, flat(lru_b_a), lru_w_i, flat(lru_b_i),
      flat(lru_lambda), sc_conv_w)


XA_TM = 256


def _xattn_kernel(x_ref, g_ref, wq_ref, k_ref, v_ref, wo_ref, o_ref, *, scale):
    x = x_ref[...]
    xn = (x * lax.rsqrt(jnp.mean(x * x, axis=-1, keepdims=True) + NORM_EPS) * g_ref[...]).astype(BF16)
    q = jnp.dot(xn, wq_ref[...], preferred_element_type=F32).astype(BF16)
    heads = []
    for hh in range(XA_HEADS):
        cs = slice(hh * HEAD_DIM, (hh + 1) * HEAD_DIM)
        s = lax.dot_general(q[:, cs], k_ref[:, cs], (((1,), (1,)), ((), ())),
                            preferred_element_type=F32) * scale
        e = jnp.exp(s - jnp.max(s, axis=-1, keepdims=True))
        p = (e / jnp.sum(e, axis=-1, keepdims=True)).astype(BF16)
        heads.append(jnp.dot(p, v_ref[:, cs], preferred_element_type=F32))
    o = jnp.concatenate(heads, axis=-1).astype(BF16)
    o_ref[...] = x + jnp.dot(o, wo_ref[...], preferred_element_type=F32)


def cross_attention_pallas(x, gain, k, v, w_q, w_o):
    B, S, D = x.shape
    M = k.shape[1]
    tm = XA_TM
    per_b = S // tm
    kv = pl.BlockSpec((None, M, XA_WIDTH), lambda i: (i // per_b, 0, 0))
    return pl.pallas_call(
        functools.partial(_xattn_kernel, scale=HEAD_DIM ** -0.5),
        grid=(B * per_b,),
        in_specs=[pl.BlockSpec((tm, D), lambda i: (i, 0)),
                  pl.BlockSpec((1, D), lambda i: (0, 0)),
                  pl.BlockSpec((D, XA_WIDTH), lambda i: (0, 0)),
                  kv, kv,
                  pl.BlockSpec((XA_WIDTH, D), lambda i: (0, 0))],
        out_specs=pl.BlockSpec((tm, D), lambda i: (i, 0)),
        out_shape=jax.ShapeDtypeStruct((B * S, D), F32),
        compiler_params=pltpu.CompilerParams(
            dimension_semantics=("parallel",), vmem_limit_bytes=VMEM_LIMIT_BYTES),
        name="cross_attention",
    )(x.reshape(B * S, D), gain.astype(F32)[None, :], w_q, k, v, w_o).reshape(B, S, D)


MASK_BIG = 1e30
LANES = 128


def _first_index_topk_mask_t(work, rowf, k):
    n = work.shape[0]
    sel = jnp.zeros(work.shape, F32)
    for _ in range(k):
        mx = jnp.max(work, axis=0, keepdims=True)
        idx = jnp.min(jnp.where(work == mx, rowf, float(n)), axis=0, keepdims=True)
        pick = rowf == idx
        sel = jnp.where(pick, 1.0, sel)
        work = jnp.where(pick, -jnp.inf, work)
    return sel


ATT_TK = 1024
MOBA_HEADS_PER_STEP = 8
M_INIT = -1e38
NT_DIMS = (((1,), (1,)), ((), ()))


def _flash_update(q_aug, k_aug, v_aug, m_ref, acc_ref, causal):
    s = lax.dot_general(q_aug, k_aug, NT_DIMS, preferred_element_type=F32)
    if causal is not None:
        q0, k0 = causal
        row = lax.broadcasted_iota(jnp.int32, s.shape, 0)
        col = lax.broadcasted_iota(jnp.int32, s.shape, 1)
        s = jnp.where(k0 + col <= q0 + row, s, NEG)
    m_prev = m_ref[...]
    m_new = jnp.maximum(m_prev, jnp.max(s, axis=-1, keepdims=True))
    alpha = jnp.exp(m_prev - m_new)
    p = jnp.exp(s - m_new).astype(BF16)
    acc_ref[...] = alpha * acc_ref[...] + jnp.dot(p, v_aug, preferred_element_type=F32)
    m_ref[...] = m_new


def _moba_kernel(q_ref, k_ref, v_ref, o_ref, kmean_ref, qa_ref, m_ref, acc_ref):
    i = pl.program_id(2)
    bs = MOBA_BLOCK
    seq = k_ref.shape[0]
    hp = q_ref.shape[1] // HEAD_DIM
    nt = NT_DIMS
    hcol = lambda h: slice(h * HEAD_DIM, (h + 1) * HEAD_DIM)

    @pl.when(i == 0)
    def _():
        row = lax.broadcasted_iota(jnp.int32, (LANES, seq), 0)
        col = lax.broadcasted_iota(jnp.int32, (LANES, seq), 1)
        ind = jnp.where(col // bs == row, 1.0, 0.0).astype(BF16)
        kmean_ref[...] = (jnp.dot(ind, k_ref[...], preferred_element_type=F32)
                          * (1.0 / bs)).astype(BF16)

    nb = seq // bs
    nbp = -(-nb // 8) * 8
    blk_t = lax.broadcasted_iota(jnp.int32, (nbp, bs), 0)
    past_t = blk_t < i
    for h in range(hp):
        q = q_ref[:, hcol(h)]
        gate_t = lax.dot_general(kmean_ref[0:nbp, hcol(h)], q, nt, preferred_element_type=F32)
        sel_t = _first_index_topk_mask_t(jnp.where(past_t, gate_t, NEG), blk_t.astype(F32), MOBA_TOPK)
        notsel_t = jnp.where(((sel_t > 0.0) & past_t) | (blk_t == i), 0.0, 1.0)
        if nbp < LANES:
            notsel_t = jnp.concatenate([notsel_t, jnp.ones((LANES - nbp, bs), F32)], axis=0)
        qa_ref[h] = jnp.concatenate([q, notsel_t.T.astype(BF16)], axis=-1)

    tk = ATT_TK
    blk_per_tile = tk // bs
    n_tiles = (i + blk_per_tile) // blk_per_tile
    key_blk = lax.broadcasted_iota(jnp.int32, (tk, LANES), 0) // bs
    key_col = lax.broadcasted_iota(jnp.int32, (tk, LANES), 1)
    ones_v = jnp.ones((tk, LANES), BF16)
    m_ref[...] = jnp.full(m_ref.shape, M_INIT, F32)
    acc_ref[...] = jnp.zeros_like(acc_ref)

    def tile(jt, causal):
        off = pl.multiple_of(jt * tk, tk)
        bias = jnp.where(key_col == jt * blk_per_tile + key_blk, -MASK_BIG, 0.0).astype(BF16)
        for h in range(hp):
            k_aug = jnp.concatenate([k_ref[pl.ds(off, tk), hcol(h)], bias], axis=-1)
            v_aug = jnp.concatenate([v_ref[pl.ds(off, tk), hcol(h)], ones_v], axis=-1)
            _flash_update(qa_ref[h], k_aug, v_aug, m_ref.at[h], acc_ref.at[h],
                          (i * bs, off) if causal else None)

    def body(jt, carry):
        tile(jt, False)
        return carry

    lax.fori_loop(0, n_tiles - 1, body, 0)
    tile(n_tiles - 1, True)
    for h in range(hp):
        acc = acc_ref[h]
        o_ref[:, hcol(h)] = (acc[:, :HEAD_DIM] / acc[:, HEAD_DIM:]).astype(o_ref.dtype)


def moba_attention_pallas(zq, zk, zv, *, n_heads, q_off=0, k_off=0, v_off=0):
    B, S, _ = zq.shape
    bs = MOBA_BLOCK
    hp = min(MOBA_HEADS_PER_STEP, n_heads)
    hw = hp * HEAD_DIM
    assert S % ATT_TK == 0 and S // bs <= LANES and n_heads % hp == 0
    assert q_off % hp == 0 and k_off % hp == 0 and v_off % hp == 0
    return pl.pallas_call(
        _moba_kernel,
        grid=(B, n_heads // hp, S // bs),
        in_specs=[pl.BlockSpec((None, bs, hw), lambda b, h, i: (b, i, q_off // hp + h)),
                  pl.BlockSpec((None, S, hw), lambda b, h, i: (b, 0, k_off // hp + h)),
                  pl.BlockSpec((None, S, hw), lambda b, h, i: (b, 0, v_off // hp + h))],
        out_specs=pl.BlockSpec((None, bs, hw), lambda b, h, i: (b, i, h)),
        out_shape=jax.ShapeDtypeStruct((B, S, n_heads * HEAD_DIM), BF16),
        scratch_shapes=[pltpu.VMEM((LANES, hw), BF16), pltpu.VMEM((hp, bs, 2 * HEAD_DIM), BF16),
                        pltpu.VMEM((hp, bs, 1), F32), pltpu.VMEM((hp, bs, 2 * HEAD_DIM), F32)],
        compiler_params=pltpu.CompilerParams(
            dimension_semantics=("parallel", "parallel", "arbitrary"),
            vmem_limit_bytes=VMEM_LIMIT_BYTES),
        name="moba_attention",
    )(zq, zk, zv)


NSA_TQ = 256
CMP_ROW = NSA_CMP_STRIDE * HEAD_DIM


def _nsa_compress_kernel(x_ref, pe_ref, w1_ref, w2_ref, o_ref):
    x = x_ref[...].astype(F32)
    top = (x + pe_ref[0:1, :]).astype(BF16)
    bot = (x + pe_ref[1:2, :]).astype(BF16)
    a = jnp.dot(top, w1_ref[0:CMP_ROW, :], preferred_element_type=F32)
    b = jnp.dot(bot, w1_ref[CMP_ROW:2 * CMP_ROW, :], preferred_element_type=F32)
    pre = a + pltpu.roll(b, b.shape[0] - 1, 0)
    hid = jax.nn.gelu(pre)
    o_ref[...] = jnp.dot(hid.astype(BF16), w2_ref[...], preferred_element_type=F32).astype(o_ref.dtype)


def nsa_compress_pallas(x, pe, w1, w2):
    B, S, gw = x.shape
    G = gw // HEAD_DIM
    nrow = S // NSA_CMP_STRIDE
    xr = x.reshape(B, nrow, NSA_CMP_STRIDE, G, HEAD_DIM).transpose(0, 3, 1, 2, 4).reshape(B, G, nrow, CMP_ROW)
    pe2 = pe.astype(F32).reshape(2, CMP_ROW)
    return pl.pallas_call(
        _nsa_compress_kernel,
        grid=(B, G),
        in_specs=[pl.BlockSpec((None, None, nrow, CMP_ROW), lambda b, g: (b, g, 0, 0)),
                  pl.BlockSpec((2, CMP_ROW), lambda b, g: (0, 0)),
                  pl.BlockSpec((2 * CMP_ROW, HEAD_DIM), lambda b, g: (0, 0)),
                  pl.BlockSpec((HEAD_DIM, HEAD_DIM), lambda b, g: (0, 0))],
        out_specs=pl.BlockSpec((None, None, nrow, HEAD_DIM), lambda b, g: (b, g, 0, 0)),
        out_shape=jax.ShapeDtypeStruct((B, G, nrow, HEAD_DIM), BF16),
        compiler_params=pltpu.CompilerParams(
            dimension_semantics=("parallel", "parallel"), vmem_limit_bytes=VMEM_LIMIT_BYTES),
        name="nsa_compress",
    )(xr, pe2, w1.astype(BF16), w2.astype(BF16))


def _nsa_kernel(q_ref, kc_ref, vc_ref, ks_ref, vs_ref, kw_ref, vw_ref, g_ref, wmap_ref, o_ref,
                m_ref, acc_ref, out_ref, ns_ref):
    i = pl.program_id(1)
    tq = NSA_TQ
    ng = kc_ref.shape[0]
    R = NSA_HEADS // NSA_KV_HEADS
    nt = NT_DIMS
    row = lax.broadcasted_iota(jnp.int32, (tq, tq), 0)
    col = lax.broadcasted_iota(jnp.int32, (tq, tq), 1)
    gates = jax.nn.sigmoid(g_ref[...])
    gcol = lambda g: slice(g * HEAD_DIM, (g + 1) * HEAD_DIM)

    def qh(g, r):
        return q_ref[:, (g * R + r) * HEAD_DIM:(g * R + r + 1) * HEAD_DIM]

    def gate(g, r, c):
        return gates[:, g * LANES + 3 * r + c:g * LANES + 3 * r + c + 1]

    ncp = kc_ref.shape[1]
    cmask = (lax.broadcasted_iota(jnp.int32, (tq, ncp), 1) * NSA_CMP_STRIDE + (NSA_CMP_BLOCK - 1)
             <= i * tq + lax.broadcasted_iota(jnp.int32, (tq, ncp), 0))
    ns = ks_ref.shape[0] // NSA_SEL_BLOCK
    blk_t = lax.broadcasted_iota(jnp.int32, (ns, tq), 0)
    q_blk_t = (i * tq + lax.broadcasted_iota(jnp.int32, (ns, tq), 1)) // NSA_SEL_BLOCK
    valid_t = blk_t <= q_blk_t
    forced_t = (blk_t == 0) | (blk_t >= q_blk_t - 1)
    for g in range(ng):
        imp = jnp.zeros((tq, LANES), F32)
        vc_aug = jnp.concatenate([vc_ref[g], jnp.ones((ncp, LANES), BF16), wmap_ref[...]], axis=-1)
        for r in range(R):
            s = lax.dot_general(qh(g, r), kc_ref[g], nt, preferred_element_type=F32)
            s = jnp.where(cmask, s, NEG)
            e = jnp.where(cmask, jnp.exp(s - jnp.max(s, axis=-1, keepdims=True)), 0.0).astype(BF16)
            acc = jnp.dot(e, vc_aug, preferred_element_type=F32)
            l = acc[:, HEAD_DIM:2 * HEAD_DIM]
            inv = jnp.where(l > 0.0, 1.0 / l, 0.0)
            imp = imp + acc[:, 2 * HEAD_DIM:] * inv
            out_ref[g * R + r] = gate(g, r, 0) * (acc[:, :HEAD_DIM] * inv)
        impm_t = jnp.where(valid_t, imp.T[:ns] + jnp.where(forced_t, NSA_FORCE_BONUS, 0.0), NEG)
        sel_t = _first_index_topk_mask_t(impm_t, blk_t.astype(F32), NSA_SEL_TOPK)
        notsel_t = jnp.where((sel_t > 0.0) & valid_t, 0.0, 1.0)
        if ns < LANES:
            notsel_t = jnp.concatenate([notsel_t, jnp.ones((LANES - ns, tq), F32)], axis=0)
        ns_ref[g] = notsel_t.T.astype(BF16)

    ones_w = jnp.ones((tq, LANES), BF16)
    w_tiles = []
    for d in range(NSA_WINDOW // tq + 1):
        off = pl.multiple_of(jnp.maximum(i - d, 0) * tq, tq)
        gone = jnp.where(i >= d, 0, 2 * NSA_WINDOW + tq)
        dist = d * tq + row - col + gone
        w_tiles.append((off, (dist >= 0) & (dist < NSA_WINDOW)))
    for g in range(ng):
        for r in range(R):
            ss = []
            for off, mask in w_tiles:
                s = lax.dot_general(qh(g, r), kw_ref[pl.ds(off, tq), gcol(g)], nt, preferred_element_type=F32)
                ss.append(jnp.where(mask, s, NEG))
            m = jnp.max(ss[0], axis=-1, keepdims=True)
            for s in ss[1:]:
                m = jnp.maximum(m, jnp.max(s, axis=-1, keepdims=True))
            acc = jnp.zeros((tq, 2 * HEAD_DIM), F32)
            for (off, _), s in zip(w_tiles, ss):
                v_aug = jnp.concatenate([vw_ref[pl.ds(off, tq), gcol(g)], ones_w], axis=-1)
                acc = acc + jnp.dot(jnp.exp(s - m).astype(BF16), v_aug, preferred_element_type=F32)
            out_ref[g * R + r] = out_ref[g * R + r] + gate(g, r, 2) * (acc[:, :HEAD_DIM] / acc[:, HEAD_DIM:])

    tk = ATT_TK
    blk_per_tile = tk // NSA_SEL_BLOCK
    n_tiles = (i * tq) // tk + 1
    key_blk = lax.broadcasted_iota(jnp.int32, (tk, LANES), 0) // NSA_SEL_BLOCK
    key_col = lax.broadcasted_iota(jnp.int32, (tk, LANES), 1)
    ones_v = jnp.ones((tk, LANES), BF16)
    m_ref[...] = jnp.full(m_ref.shape, M_INIT, F32)
    acc_ref[...] = jnp.zeros_like(acc_ref)

    def sel_tile(jt, causal):
        off = pl.multiple_of(jt * tk, tk)
        bias = jnp.where(key_col == jt * blk_per_tile + key_blk, -MASK_BIG, 0.0).astype(BF16)
        for g in range(ng):
            k_aug = jnp.concatenate([ks_ref[pl.ds(off, tk), gcol(g)], bias], axis=-1)
            v_aug = jnp.concatenate([vs_ref[pl.ds(off, tk), gcol(g)], ones_v], axis=-1)
            for r in range(R):
                q_aug = jnp.concatenate([qh(g, r), ns_ref[g]], axis=-1)
                _flash_update(q_aug, k_aug, v_aug, m_ref.at[g * R + r], acc_ref.at[g * R + r],
                              (i * tq, off) if causal else None)

    def body(jt, carry):
        sel_tile(jt, False)
        return carry

    lax.fori_loop(0, n_tiles - 1, body, 0)
    sel_tile(n_tiles - 1, True)
    for g in range(ng):
        for r in range(R):
            h = g * R + r
            acc = acc_ref[h]
            o_s = acc[:, :HEAD_DIM] / acc[:, HEAD_DIM:]
            o_ref[:, h * HEAD_DIM:(h + 1) * HEAD_DIM] = (out_ref[h] + gate(g, r, 1) * o_s).astype(o_ref.dtype)


def nsa_attention_pallas(zq, kc, vc, zks, zvs, zkw, zvw, gate_logits, *, q_off, ks_off, vs_off, kw_off, vw_off):
    B, S, _ = zq.shape
    G, tq = NSA_KV_HEADS, NSA_TQ
    R = NSA_HEADS // G
    assert S % ATT_TK == 0 and ATT_TK % tq == 0 and S // NSA_SEL_BLOCK <= LANES
    nc = (S - NSA_CMP_BLOCK) // NSA_CMP_STRIDE + 1
    ncp = kc.shape[2]
    ns = S // NSA_SEL_BLOCK
    wmap = jnp.pad(cmp_to_sel_weights(nc, ns), ((0, ncp - nc), (0, LANES - ns))).astype(BF16)
    H = NSA_HEADS
    qw = H * HEAD_DIM
    assert q_off % H == 0 and all(off % G == 0 for off in (ks_off, vs_off, kw_off, vw_off))
    kv_spec = lambda off: pl.BlockSpec((None, S, G * HEAD_DIM), lambda b, i: (b, 0, off // G))
    c_spec = pl.BlockSpec((None, G, ncp, HEAD_DIM), lambda b, i: (b, 0, 0, 0))
    return pl.pallas_call(
        _nsa_kernel,
        grid=(B, S // tq),
        in_specs=[pl.BlockSpec((None, tq, qw), lambda b, i: (b, i, q_off // H)),
                  c_spec, c_spec, kv_spec(ks_off), kv_spec(vs_off), kv_spec(kw_off), kv_spec(vw_off),
                  pl.BlockSpec((None, tq, G * LANES), lambda b, i: (b, i, 0)),
                  pl.BlockSpec((ncp, LANES), lambda b, i: (0, 0))],
        out_specs=pl.BlockSpec((None, tq, qw), lambda b, i: (b, i, 0)),
        out_shape=jax.ShapeDtypeStruct((B, S, qw), BF16),
        scratch_shapes=[pltpu.VMEM((H, tq, 1), F32), pltpu.VMEM((H, tq, 2 * HEAD_DIM), F32),
                        pltpu.VMEM((H, tq, HEAD_DIM), F32), pltpu.VMEM((G, tq, LANES), BF16)],
        compiler_params=pltpu.CompilerParams(
            dimension_semantics=("parallel", "arbitrary"),
            vmem_limit_bytes=VMEM_LIMIT_BYTES),
        name="nsa_attention",
    )(zq, kc, vc, zks, zvs, zkw, zvw, gate_logits, wmap)


MOE_TM = 256
MOE_TN = 512
ROUTER_TM = 256


def _router_kernel(x_ref, g_ref, w_ref, b_ref, t_ref, ids_ref, wts_ref):
    x = x_ref[...]
    t = x * lax.rsqrt(jnp.mean(x * x, axis=-1, keepdims=True) + NORM_EPS) * g_ref[...]
    t_ref[...] = t
    logits = jnp.dot(t.astype(BF16), w_ref[...], preferred_element_type=F32) + b_ref[...]
    coli = lax.broadcasted_iota(jnp.int32, logits.shape, 1)
    colf = coli.astype(F32)
    first = lambda hit: jnp.min(jnp.where(hit, colf, float(LANES)), axis=-1, keepdims=True)
    is_g = coli < N_GROUPS
    gl = jnp.where(is_g, logits, -jnp.inf)
    gmax = jnp.max(gl, axis=-1, keepdims=True)
    g_sel = first(gl == gmax)
    p_g = 1.0 / jnp.sum(jnp.where(is_g, jnp.exp(logits - gmax), 0.0), axis=-1, keepdims=True)
    lo = N_GROUPS + EXPERTS_PER_GROUP * g_sel
    el = jnp.where((colf >= lo) & (colf < lo + EXPERTS_PER_GROUP), logits, -jnp.inf)
    v1 = jnp.max(el, axis=-1, keepdims=True)
    i1 = first(el == v1)
    el2 = jnp.where(colf == i1, -jnp.inf, el)
    v2 = jnp.max(el2, axis=-1, keepdims=True)
    i2 = first(el2 == v2)
    e = jnp.exp(v2 - v1)
    w1 = p_g / (1.0 + e)
    w2 = p_g * e / (1.0 + e)
    ids_ref[...] = jnp.where(coli == 0, i1 - N_GROUPS, jnp.where(coli == 1, i2 - N_GROUPS, 0.0)).astype(jnp.int32)
    wts_ref[...] = jnp.where(coli == 0, w1, jnp.where(coli == 1, w2, 0.0))


def moe_router_pallas(x2, gain, w_group, b_group, w_expert, b_expert):
    T, D = x2.shape
    tm = ROUTER_TM
    nr = N_GROUPS + N_EXPERTS
    w = jnp.pad(jnp.concatenate([w_group, w_expert], axis=1), ((0, 0), (0, LANES - nr))).astype(BF16)
    b = jnp.pad(jnp.concatenate([b_group, b_expert]), (0, LANES - nr)).astype(F32)[None, :]
    t, ids, wts = pl.pallas_call(
        _router_kernel,
        grid=(T // tm,),
        in_specs=[pl.BlockSpec((tm, D), lambda i: (i, 0)),
                  pl.BlockSpec((1, D), lambda i: (0, 0)),
                  pl.BlockSpec((D, LANES), lambda i: (0, 0)),
                  pl.BlockSpec((1, LANES), lambda i: (0, 0))],
        out_specs=[pl.BlockSpec((tm, D), lambda i: (i, 0)),
                   pl.BlockSpec((tm, LANES), lambda i: (i, 0)),
                   pl.BlockSpec((tm, LANES), lambda i: (i, 0))],
        out_shape=[jax.ShapeDtypeStruct((T, D), F32),
                   jax.ShapeDtypeStruct((T, LANES), jnp.int32),
                   jax.ShapeDtypeStruct((T, LANES), F32)],
        compiler_params=pltpu.CompilerParams(
            dimension_semantics=("parallel",), vmem_limit_bytes=VMEM_LIMIT_BYTES),
        name="moe_router",
    )(x2, gain.astype(F32)[None, :], w, b)
    return t, ids[:, :TOPK_IN_GROUP], wts[:, :TOPK_IN_GROUP]


MOE_DMA_GROUP = 8


def _moe_up_kernel(tile_e_ref, n_used_ref, tile_rows_ref, row_tok_ref, t_hbm, rw_ref, wg_ref, wu_ref, hid_ref,
                   xw_ref, xb_ref, sem):
    k = pl.program_id(0)
    tm = xb_ref.shape[0]
    n_used = n_used_ref[0]

    def row_copy(tok, r):
        return pltpu.make_async_copy(t_hbm.at[pl.ds(tok, 1), :], xw_ref.at[pl.ds(r, 1), :], sem.at[0])

    def n_groups(tile):
        return (tile_rows_ref[tile] + MOE_DMA_GROUP - 1) // MOE_DMA_GROUP

    def start_gather(tile):
        def issue(g, carry):
            for j in range(MOE_DMA_GROUP):
                r = g * MOE_DMA_GROUP + j
                row_copy(row_tok_ref[tile * tm + r], r).start()
            return carry
        lax.fori_loop(0, n_groups(tile), issue, 0)

    @pl.when(k == 0)
    def _():
        xw_ref[...] = jnp.zeros_like(xw_ref)
        start_gather(0)

    @pl.when(k < n_used)
    def _():
        def wait_group(g, carry):
            for j in range(MOE_DMA_GROUP):
                row_copy(0, g * MOE_DMA_GROUP + j).wait()
            return carry
        lax.fori_loop(0, n_groups(k), wait_group, 0)
        xb_ref[...] = xw_ref[...].astype(BF16)

        @pl.when(k + 1 < n_used)
        def _():
            start_gather(k + 1)

        x = xb_ref[...]
        hg = jnp.dot(x, wg_ref[...].astype(BF16), preferred_element_type=F32)
        hu = jnp.dot(x, wu_ref[...].astype(BF16), preferred_element_type=F32)
        hid_ref[...] = (jax.nn.silu(hg) * hu * rw_ref[...]).astype(hid_ref.dtype)

    @pl.when(k >= n_used)
    def _():
        hid_ref[...] = jnp.zeros_like(hid_ref)


def _moe_down_kernel(tile_e_ref, n_used_ref, hid_ref, wd_ref, o_ref):
    k = pl.program_id(0)

    @pl.when(k < n_used_ref[0])
    def _():
        hid = hid_ref[...]
        for c in range(0, o_ref.shape[1], MOE_TN):
            o_ref[:, c:c + MOE_TN] = jnp.dot(hid, wd_ref[:, c:c + MOE_TN].astype(BF16),
                                             preferred_element_type=F32)

    @pl.when(k >= n_used_ref[0])
    def _():
        o_ref[...] = jnp.zeros_like(o_ref)


COMBINE_TC = 128


def _moe_combine_kernel(pos_ref, x_ref, g_ref, y_hbm, o_ref, n_ref, buf, sem):
    i = pl.program_id(0)
    tc = x_ref.shape[0]
    nk = TOPK_IN_GROUP

    def row_copy(slot, a, p):
        return pltpu.make_async_copy(y_hbm.at[pl.ds(p, 1), :], buf.at[slot, pl.ds(a, 1), :], sem.at[slot])

    def start_gather(tile, slot):
        def issue(r, carry):
            for kk in range(nk):
                row_copy(slot, kk * tc + r, pos_ref[(tile * tc + r) * nk + kk]).start()
            return carry
        lax.fori_loop(0, tc, issue, 0, unroll=4)

    slot = i % 2

    @pl.when(i == 0)
    def _():
        start_gather(0, 0)

    def wait_row(a, carry):
        row_copy(slot, a, 0).wait()
        return carry
    lax.fori_loop(0, nk * tc, wait_row, 0, unroll=8)

    @pl.when(i + 1 < pl.num_programs(0))
    def _():
        start_gather(i + 1, 1 - slot)

    v = x_ref[...]
    for kk in range(nk):
        v = v + buf[slot, kk * tc:(kk + 1) * tc, :]
    o_ref[...] = v
    n_ref[...] = (v * lax.rsqrt(jnp.mean(v * v, axis=-1, keepdims=True) + NORM_EPS)
                  * g_ref[...]).astype(n_ref.dtype)


def moe_combine_pallas(x2, y_rows, pos, next_gain, next_dtype):
    T, D = x2.shape
    tc = COMBINE_TC
    row_blk = pl.BlockSpec((tc, D), lambda i, *_: (i, 0))
    return pl.pallas_call(
        _moe_combine_kernel,
        grid_spec=pltpu.PrefetchScalarGridSpec(
            num_scalar_prefetch=1,
            grid=(T // tc,),
            in_specs=[row_blk, pl.BlockSpec((1, D), lambda i, *_: (0, 0)),
                      pl.BlockSpec(memory_space=pl.ANY)],
            out_specs=[row_blk, row_blk],
            scratch_shapes=[pltpu.VMEM((2, TOPK_IN_GROUP * tc, D), F32),
                            pltpu.SemaphoreType.DMA((2,))]),
        out_shape=[jax.ShapeDtypeStruct((T, D), F32), jax.ShapeDtypeStruct((T, D), next_dtype)],
        compiler_params=pltpu.CompilerParams(
            dimension_semantics=("arbitrary",), vmem_limit_bytes=VMEM_LIMIT_BYTES),
        name="moe_combine",
    )(pos.reshape(-1), x2, next_gain.astype(F32)[None, :], y_rows)


def moe_experts_pallas(t_packed, row_tok, row_w, tile_e, n_used, tile_rows, w_gate, w_up, w_down, layer):
    P = row_tok.shape[0]
    D = w_gate.shape[-2]
    F = w_gate.shape[-1]
    tm = MOE_TM
    l = int(layer)
    w_in = pl.BlockSpec((None, None, D, F), lambda k, te, *_: (l, te[k], 0, 0))
    hid = pl.pallas_call(
        _moe_up_kernel,
        grid_spec=pltpu.PrefetchScalarGridSpec(
            num_scalar_prefetch=4,
            grid=(P // tm,),
            in_specs=[pl.BlockSpec(memory_space=pl.ANY),
                      pl.BlockSpec((tm, 1), lambda k, *_: (k, 0)),
                      w_in, w_in],
            out_specs=pl.BlockSpec((tm, F), lambda k, *_: (k, 0)),
            scratch_shapes=[pltpu.VMEM((tm, D), F32), pltpu.VMEM((tm, D), BF16),
                            pltpu.SemaphoreType.DMA((1,))]),
        out_shape=jax.ShapeDtypeStruct((P, F), BF16),
        compiler_params=pltpu.CompilerParams(
            dimension_semantics=("arbitrary",), vmem_limit_bytes=VMEM_LIMIT_BYTES),
        name="moe_up",
    )(tile_e, n_used, tile_rows, row_tok, t_packed, row_w, w_gate, w_up)
    return pl.pallas_call(
        _moe_down_kernel,
        grid_spec=pltpu.PrefetchScalarGridSpec(
            num_scalar_prefetch=2,
            grid=(P // tm,),
            in_specs=[pl.BlockSpec((tm, F), lambda k, *_: (k, 0)),
                      pl.BlockSpec((None, None, F, D), lambda k, te, *_: (l, te[k], 0, 0))],
            out_specs=pl.BlockSpec((tm, D), lambda k, *_: (k, 0))),
        out_shape=jax.ShapeDtypeStruct((P, D), F32),
        compiler_params=pltpu.CompilerParams(
            dimension_semantics=("parallel",), vmem_limit_bytes=VMEM_LIMIT_BYTES),
        name="moe_down",
    )(tile_e, n_used, hid, w_down)


def moe_dispatch_plan(ids, wts):
    T, K = ids.shape
    n = T * K
    P = n + N_EXPERTS * MOE_TM
    flat = ids.reshape(n)
    onehot = (flat[:, None] == jnp.arange(N_EXPERTS, dtype=jnp.int32)[None, :]).astype(jnp.int32)
    csum = jnp.cumsum(onehot, axis=0)
    counts = csum[-1]
    padded = (counts + MOE_TM - 1) // MOE_TM * MOE_TM
    pend = jnp.cumsum(padded)
    pstart = pend - padded
    pos = jnp.sum(onehot * (csum - 1 + pstart[None, :]), axis=1)
    tok = (jnp.arange(n, dtype=jnp.int32) // K).astype(F32)
    info = jnp.zeros((P, 2), F32).at[pos].set(jnp.stack([tok, wts.reshape(n)], axis=1))
    row_tok = info[:, 0].astype(jnp.int32)
    row_w = info[:, 1:2]
    tile_start = jnp.arange(P // MOE_TM, dtype=jnp.int32) * MOE_TM
    tile_e = jnp.minimum(jnp.sum((pend[None, :] <= tile_start[:, None]).astype(jnp.int32), axis=1),
                         N_EXPERTS - 1).astype(jnp.int32)
    n_used = (pend[-1] // MOE_TM).astype(jnp.int32)[None]
    e_hot = (tile_e[:, None] == jnp.arange(N_EXPERTS, dtype=jnp.int32)[None, :]).astype(jnp.int32)
    in_expert = tile_start - jnp.sum(e_hot * pstart[None, :], axis=1)
    tile_rows = jnp.clip(jnp.sum(e_hot * counts[None, :], axis=1) - in_expert, 0, MOE_TM).astype(jnp.int32)
    tile_rows = jnp.where(tile_start < pend[-1], tile_rows, 0)
    return pos.reshape(T, K), row_tok, row_w, tile_e, n_used, tile_rows


def hierarchical_moe_pallas(x, gain, w_group, b_group, w_expert, b_expert, w_gate, w_up, w_down, layer,
                            next_gain, next_dtype):
    B, S, D = x.shape
    x2 = x.reshape(B * S, D)
    t, ids, wts = moe_router_pallas(x2, gain, w_group, b_group, w_expert, b_expert)
    pos, row_tok, row_w, tile_e, n_used, tile_rows = moe_dispatch_plan(ids, wts)
    y = moe_experts_pallas(t, row_tok, row_w, tile_e, n_used, tile_rows, w_gate, w_up, w_down, layer)
    out, normed = moe_combine_pallas(x2, y, pos, next_gain, next_dtype)
    return out.reshape(B, S, D), normed


def rope_tables(positions):
    inv = ROPE_THETA ** (-jnp.arange(0, HEAD_DIM, 2, dtype=F32) / HEAD_DIM)
    ang = positions.astype(F32)[..., None] * inv
    return jnp.cos(ang)[:, :, None, :], jnp.sin(ang)[:, :, None, :]


def cmp_to_sel_weights(nc, ns):
    r = NSA_SEL_BLOCK // NSA_CMP_STRIDE
    m = NSA_CMP_BLOCK // NSA_CMP_STRIDE
    c = jnp.arange(nc)[:, None] - r * jnp.arange(ns)[None, :]
    w = jnp.minimum(jnp.minimum(c + 1, r + m - 1 - c), min(r, m))
    return jnp.clip(w, 0, None).astype(F32)


def kernel(x, mem, positions, norm_mix, w_mix_in, lru_conv_w, lru_conv_b, lru_w_a, lru_b_a, lru_w_i, lru_b_i, lru_lambda, sc_conv_w, nsa_pe_k, nsa_w1_k, nsa_w2_k, nsa_pe_v, nsa_w1_v, nsa_w2_v, w_merge_gate, b_merge_gate, w_branch_out, w_mix_out, norm_xattn, norm_mem, xa_w_q, xa_w_k, xa_w_v, xa_w_o, norm_moe, moe_w_group, moe_b_group, moe_w_expert, moe_b_expert, moe_w_gate, moe_w_up, moe_w_down, norm_final):
    B, S, D = x.shape
    T = B * S
    cos, sin = rope_tables(positions)
    cos_t = jnp.concatenate([cos, cos], axis=-1).reshape(T, HEAD_DIM)
    sin_t = jnp.concatenate([-sin, sin], axis=-1).reshape(T, HEAD_DIM)
    offs = [int(v) for v in np.concatenate([[0], np.cumsum(MIX_SPLITS)])]
    blk = [v // HEAD_DIM for v in offs[:11]]
    n_head_cols = offs[10]
    rope_flags = np.zeros((n_head_cols // HEAD_DIM,), np.int32)
    for k in (1, 4, 6, 8):
        rope_flags[blk[k]:blk[k + 1]] = ROPE_PLAIN
    for k in (0, 3):
        rope_flags[blk[k]:blk[k + 1]] = ROPE_SCALED
    rope_flags = jnp.asarray(rope_flags)
    G, R = NSA_KV_HEADS, NSA_HEADS // NSA_KV_HEADS
    w_in_nk = jnp.swapaxes(w_mix_in, 1, 2)
    assert offs[10] % 8 == 0 and offs[11] % 8 == 0
    M = mem.shape[1]
    h = rms_norm_pallas(x.reshape(T, D), norm_mix[0], BF16)
    for l in range(DEPTH):
        x2 = x.reshape(T, D)
        z_head = matmul(h, w_in_nk, w_lead=(l,), w_is_nk=True, n_cols=n_head_cols, out_dtype=BF16,
                        rope=(cos_t, sin_t, rope_flags)).reshape(B, S, n_head_cols)
        z_tail = matmul(h, w_in_nk, w_lead=(l,), w_is_nk=True, row0=offs[11],
                        n_cols=offs[16] - offs[11]).reshape(B, S, -1)
        z_g = matmul(h, w_in_nk, w_lead=(l,), w_is_nk=True, row0=offs[10], n_cols=LANES)
        z_gate = jnp.pad(z_g[:, :NSA_GATE_W].reshape(B, S, G, R * 3),
                         ((0, 0), (0, 0), (0, 0), (0, LANES - R * 3))).reshape(B, S, G * LANES)
        o_moba = moba_attention_pallas(z_head, z_head, z_head, n_heads=MOBA_HEADS,
                                       q_off=blk[0], k_off=blk[1], v_off=blk[2])
        kc = nsa_compress_pallas(z_head[:, :, offs[4]:offs[5]], nsa_pe_k[l], nsa_w1_k[l], nsa_w2_k[l])
        vc = nsa_compress_pallas(z_head[:, :, offs[5]:offs[6]], nsa_pe_v[l], nsa_w1_v[l], nsa_w2_v[l])
        o_nsa = nsa_attention_pallas(z_head, kc, vc, z_head, z_head, z_head, z_head, z_gate,
                                     q_off=blk[3], ks_off=blk[6], vs_off=blk[7], kw_off=blk[8], vw_off=blk[9])
        o_lru, o_conv = lru_conv_pallas(z_tail, l, lru_conv_w, lru_conv_b, lru_w_a, lru_b_a,
                                        lru_w_i, lru_b_i, lru_lambda, sc_conv_w)
        o_all = jnp.stack([o_moba, o_lru, o_conv, o_nsa]).reshape(N_BRANCH, T, -1)
        merged = merge_branches_pallas(h, o_all, w_merge_gate, b_merge_gate, w_branch_out, l)
        x2 = matmul(merged, w_mix_out, w_lead=(l,), residual=x2)
        mem_n = rms_norm_pallas(mem.reshape(B * M, D), norm_mem[l], BF16)
        xk = matmul(mem_n, xa_w_k, w_lead=(l,), out_dtype=BF16).reshape(B, M, XA_WIDTH)
        xv = matmul(mem_n, xa_w_v, w_lead=(l,), out_dtype=BF16).reshape(B, M, XA_WIDTH)
        x = cross_attention_pallas(x2.reshape(B, S, D), norm_xattn[l], xk, xv,
                                   xa_w_q[l].astype(BF16), xa_w_o[l].astype(BF16))
        last = l == DEPTH - 1
        x, h = hierarchical_moe_pallas(x, norm_moe[l], moe_w_group[l], moe_b_group[l],
                                       moe_w_expert[l], moe_b_expert[l], moe_w_gate,
                                       moe_w_up, moe_w_down, l,
                                       norm_final if last else norm_mix[l + 1], F32 if last else BF16)
    return h.reshape(B, S, D)
```

```python
import functools

import jax
import jax.numpy as jnp
import numpy as np
from jax import lax
from jax.experimental import pallas as pl
from jax.experimental.pallas import tpu as pltpu

F32 = jnp.float32
BF16 = jnp.bfloat16

D_MODEL = 4096
DEPTH = 2
HEAD_DIM = 128
ROPE_THETA = 10000.0
NORM_EPS = 1e-6
NEG = -1e30
N_BRANCH = 4
BRANCH_WIDTH = D_MODEL // 4
MOBA_HEADS = BRANCH_WIDTH // HEAD_DIM
MOBA_BLOCK = 256
MOBA_TOPK = 3
LRU_WIDTH = BRANCH_WIDTH
LRU_C = 8.0
SC_WIDTH = BRANCH_WIDTH
NSA_HEADS = BRANCH_WIDTH // HEAD_DIM
NSA_KV_HEADS = NSA_HEADS // 4
NSA_CMP_BLOCK = 32
NSA_CMP_STRIDE = 16
NSA_SEL_BLOCK = 64
NSA_SEL_TOPK = 16
NSA_WINDOW = 512
NSA_FORCE_BONUS = 1e4
XA_HEADS = 4
XA_WIDTH = XA_HEADS * HEAD_DIM
N_GROUPS = 4
EXPERTS_PER_GROUP = 8
N_EXPERTS = N_GROUPS * EXPERTS_PER_GROUP
TOPK_IN_GROUP = 2
MOBA_W = MOBA_HEADS * HEAD_DIM
NSA_Q_W = NSA_HEADS * HEAD_DIM
NSA_KV_W = NSA_KV_HEADS * HEAD_DIM
NSA_GATE_W = NSA_HEADS * 3
MIX_SPLITS = (MOBA_W, MOBA_W, MOBA_W,
              NSA_Q_W, NSA_KV_W, NSA_KV_W, NSA_KV_W, NSA_KV_W, NSA_KV_W, NSA_KV_W, NSA_GATE_W,
              LRU_WIDTH, LRU_WIDTH,
              SC_WIDTH, SC_WIDTH, SC_WIDTH)

VMEM_LIMIT_BYTES = 56 * 1024 * 1024


ROPE_PLAIN, ROPE_SCALED = 1, 2


def _mm_kernel(*refs, rope, residual, w_is_nk):
    if rope:
        flags_ref, refs = refs[0], refs[1:]
    a_ref, w_ref = refs[0], refs[1]
    o_ref = refs[-1]
    dims = (((1,), (1,)), ((), ())) if w_is_nk else (((1,), (0,)), ((), ()))
    a = a_ref[...].astype(BF16)
    if rope:
        cos, sin = refs[2][...], refs[3][...]
        tn = o_ref.shape[1]
        step = 2 * HEAD_DIM
        for c0 in range(0, tn, step):
            w_blk = w_ref[c0:c0 + step, :] if w_is_nk else w_ref[:, c0:c0 + step]
            acc = lax.dot_general(a, w_blk.astype(BF16), dims, preferred_element_type=F32)
            for c in range(c0, c0 + step, HEAD_DIM):
                blk = acc[:, c - c0:c - c0 + HEAD_DIM]
                flag = flags_ref[(pl.program_id(1) * tn + c) // HEAD_DIM]
                scl = jnp.where(flag == ROPE_SCALED, HEAD_DIM ** -0.5, 1.0)
                cos_c = jnp.where(flag != 0, cos * scl, 1.0)
                sin_c = jnp.where(flag != 0, sin * scl, 0.0)
                rot = pltpu.roll(blk, HEAD_DIM // 2, 1)
                o_ref[:, c:c + HEAD_DIM] = (blk * cos_c + rot * sin_c).astype(o_ref.dtype)
        return
    acc = lax.dot_general(a, w_ref[...].astype(BF16), dims, preferred_element_type=F32)
    if residual:
        acc = acc + refs[2][...]
    o_ref[...] = acc.astype(o_ref.dtype)


def matmul(a, w, *, w_lead=(), w_is_nk=False, col0=0, row0=None, n_cols=None, out_dtype=F32, rope=None,
           residual=None, tm=1024, tn=512):
    m, k = a.shape
    n = w.shape[-2 if w_is_nk else -1] if n_cols is None else n_cols
    assert w.shape[-1 if w_is_nk else -2] == k and not (rope is not None and residual is not None)
    tm = min(tm, m)
    tn = min(tn, n)
    lead = tuple(int(v) for v in w_lead)
    if w_is_nk and row0 is not None:
        rows_per_slab = w.shape[-2]
        slab = int(np.ravel_multi_index(lead, w.shape[:-2])) if lead else 0
        w = w.reshape(-1, k)
        w_spec = pl.BlockSpec((pl.Element(tn), pl.Element(k)),
                              lambda i, j, *_: (pl.multiple_of(slab * rows_per_slab + row0 + j * tn, 8), 0))
    elif w_is_nk:
        w_spec = pl.BlockSpec((None,) * len(lead) + (tn, k), lambda i, j, *_: lead + (j + col0, 0))
    else:
        w_spec = pl.BlockSpec((None,) * len(lead) + (k, tn), lambda i, j, *_: lead + (0, j + col0))
    in_specs = [pl.BlockSpec((tm, k), lambda i, j, *_: (i, 0)), w_spec]
    args = [a, w]
    prefetch = []
    if rope is not None:
        cos, sin, flags = rope
        assert n % tn == 0 and flags.shape == (n // HEAD_DIM,)
        in_specs += [pl.BlockSpec((tm, HEAD_DIM), lambda i, j, *_: (i, 0))] * 2
        args += [cos, sin]
        prefetch = [flags]
    if residual is not None:
        in_specs.append(pl.BlockSpec((tm, tn), lambda i, j, *_: (i, j)))
        args.append(residual)
    return pl.pallas_call(
        functools.partial(_mm_kernel, rope=rope is not None, residual=residual is not None,
                          w_is_nk=w_is_nk),
        grid_spec=pltpu.PrefetchScalarGridSpec(
            num_scalar_prefetch=len(prefetch),
            grid=(pl.cdiv(m, tm), pl.cdiv(n, tn)),
            in_specs=in_specs,
            out_specs=pl.BlockSpec((tm, tn), lambda i, j, *_: (i, j))),
        out_shape=jax.ShapeDtypeStruct((m, n), out_dtype),
        compiler_params=pltpu.CompilerParams(
            dimension_semantics=("parallel", "parallel"),
            vmem_limit_bytes=VMEM_LIMIT_BYTES),
        name="matmul",
    )(*prefetch, *args)


def _rms_kernel(x_ref, g_ref, o_ref):
    x = x_ref[...]
    y = x * lax.rsqrt(jnp.mean(x * x, axis=-1, keepdims=True) + NORM_EPS)
    o_ref[...] = (y * g_ref[...]).astype(o_ref.dtype)


def rms_norm_pallas(x2, gain, out_dtype, *, tm=512):
    m, d = x2.shape
    tm = min(tm, m)
    return pl.pallas_call(
        _rms_kernel,
        grid=(m // tm,),
        in_specs=[pl.BlockSpec((tm, d), lambda i: (i, 0)), pl.BlockSpec((1, d), lambda i: (0, 0))],
        out_specs=pl.BlockSpec((tm, d), lambda i: (i, 0)),
        out_shape=jax.ShapeDtypeStruct((m, d), out_dtype),
        compiler_params=pltpu.CompilerParams(
            dimension_semantics=("parallel",), vmem_limit_bytes=VMEM_LIMIT_BYTES),
        name="rms_norm",
    )(x2, gain.astype(F32)[None, :])


def _merge_kernel(h_ref, o_ref, wg_ref, bg_ref, wu_ref, out_ref, acc_ref):
    n = pl.program_id(2)

    @pl.when((pl.program_id(0) == 0) & (pl.program_id(1) == 0) & (n == 0))
    def _():
        acc_ref[...] = jnp.zeros_like(acc_ref)

    h, o = h_ref[...], o_ref[...]
    half = out_ref.shape[1] // 2
    for c0 in (0, half):
        cs = slice(c0, c0 + half)
        gate = jax.nn.sigmoid(
            jnp.dot(h, wg_ref[:, cs].astype(BF16), preferred_element_type=F32) + bg_ref[:, cs])
        term = gate * jnp.dot(o, wu_ref[:, cs].astype(BF16), preferred_element_type=F32)
        acc = jnp.where(n == 0, 0.0, acc_ref[:, cs]) + term
        acc_ref[:, cs] = acc
        out_ref[:, cs] = acc.astype(out_ref.dtype)


def merge_branches_pallas(h, o_all, w_gate, b_gate, w_up, layer, *, tm=1024, tn=512):
    T, D = h.shape
    N, _, W = o_all.shape
    l = int(layer)
    return pl.pallas_call(
        _merge_kernel,
        grid=(T // tm, D // tn, N),
        in_specs=[pl.BlockSpec((tm, D), lambda i, j, n: (i, 0)),
                  pl.BlockSpec((None, tm, W), lambda i, j, n: (n, i, 0)),
                  pl.BlockSpec((None, None, D, tn), lambda i, j, n: (l, n, 0, j)),
                  pl.BlockSpec((None, None, 1, tn), lambda i, j, n: (l, n, 0, j)),
                  pl.BlockSpec((None, None, W, tn), lambda i, j, n: (l, n, 0, j))],
        out_specs=pl.BlockSpec((tm, tn), lambda i, j, n: (i, j)),
        out_shape=jax.ShapeDtypeStruct((T, D), BF16),
        scratch_shapes=[pltpu.VMEM((tm, tn), F32)],
        compiler_params=pltpu.CompilerParams(
            dimension_semantics=("parallel", "parallel", "arbitrary"),
            vmem_limit_bytes=VMEM_LIMIT_BYTES),
        name="merge_branches",
    )(h, o_all, w_gate, b_gate[:, :, None, :], w_up)


LRU_TS = 256
LRU_CW = 256


def _rows_back(x, prev, s, row):
    return jnp.where(row < s, pltpu.roll(prev, s, 0), pltpu.roll(x, s, 0))


def _lane_window(blocks, shift):
    if shift == 0:
        return jnp.concatenate([b[...] for b in blocks], axis=-1)
    back = LANES - shift
    rolled = [pltpu.roll(b[...], back, 1) for b in blocks]
    lane = lax.broadcasted_iota(jnp.int32, rolled[0].shape, 1)
    return jnp.concatenate([jnp.where(lane < back, rolled[j], rolled[j + 1])
                            for j in range(len(blocks) - 1)], axis=-1)


def _lru_conv_kernel(*refs, n_lane_blocks, shift):
    nb = n_lane_blocks
    rx, rg, cb, cc, cx = (_lane_window(refs[k * nb:(k + 1) * nb], shift) for k in range(5))
    (lw_ref, lb_ref, wa_ref, ba_ref, wi_ref, bi_ref, lam_ref, sw_ref,
     olru_ref, oconv_ref, px_ref, py_ref, h_ref) = refs[5 * nb:]

    @pl.when(pl.program_id(2) == 0)
    def _():
        px_ref[...] = jnp.zeros_like(px_ref)
        py_ref[...] = jnp.zeros_like(py_ref)
        h_ref[...] = jnp.zeros_like(h_ref)

    ts, cw = px_ref.shape
    row = lax.broadcasted_iota(jnp.int32, (ts, cw), 0)

    x = rx
    prev = px_ref[...]
    nk = lw_ref.shape[0]
    xc = lb_ref[...] + lw_ref[nk - 1:nk, :] * x
    for s in range(1, nk):
        xc = xc + lw_ref[nk - 1 - s:nk - s, :] * _rows_back(x, prev, s, row)
    px_ref[...] = x
    r_parts, i_parts = [], []
    for hh in range(cw // HEAD_DIM):
        cs = slice(hh * HEAD_DIM, (hh + 1) * HEAD_DIM)
        xh = xc[:, cs].astype(BF16)
        r_parts.append(jnp.dot(xh, wa_ref[hh].astype(BF16), preferred_element_type=F32))
        i_parts.append(jnp.dot(xh, wi_ref[hh].astype(BF16), preferred_element_type=F32))
    r = jax.nn.sigmoid(jnp.concatenate(r_parts, axis=-1) + ba_ref[...])
    gi = jax.nn.sigmoid(jnp.concatenate(i_parts, axis=-1) + bi_ref[...])
    log_a = -LRU_C * r * jax.nn.softplus(-lam_ref[...])
    a = jnp.exp(log_a)
    b = jnp.sqrt(1.0 - a * a) * (gi * xc)
    d = 1
    while d < ts:
        a_back = jnp.where(row < d, 1.0, pltpu.roll(a, d, 0))
        b_back = jnp.where(row < d, 0.0, pltpu.roll(b, d, 0))
        b = a * b_back + b
        a = a * a_back
        d *= 2
    h = b + a * h_ref[...]
    h_ref[...] = h[ts - 1:ts, :]
    olru_ref[...] = (h * jax.nn.gelu(rg)).astype(olru_ref.dtype)

    y = cc * cx
    prev_y = py_ref[...]
    nk = sw_ref.shape[0]
    conv = sw_ref[nk - 1:nk, :] * y
    for s in range(1, nk):
        conv = conv + sw_ref[nk - 1 - s:nk - s, :] * _rows_back(y, prev_y, s, row)
    py_ref[...] = y
    oconv_ref[...] = (cb * conv).astype(oconv_ref.dtype)


def lru_conv_pallas(z, layer, lru_conv_w, lru_conv_b, lru_w_a, lru_b_a, lru_w_i, lru_b_i, lru_lambda,
                    sc_conv_w, *, shift=0):
    B, S, _ = z.shape
    C = lru_conv_b.shape[-1]
    ts, cw = LRU_TS, LRU_CW
    nblk = C // cw
    l = int(layer)
    per_blk = cw // LANES
    nb = per_blk + (1 if shift else 0)
    lane_blk = lambda k, j: pl.BlockSpec(
        (None, ts, LANES), lambda b, c, t: (b, t, (k * nblk + c) * per_blk + j))
    cols = [lane_blk(k, j) for k in range(5) for j in range(nb)]
    par = lambda rows: pl.BlockSpec((None, rows, cw), lambda b, c, t: (l, 0, c))
    hw = pl.BlockSpec((None, cw // HEAD_DIM, HEAD_DIM, HEAD_DIM), lambda b, c, t: (l, c, 0, 0))
    L = lru_conv_b.shape[0]
    flat = lambda p: p.reshape(L, 1, C)
    out = pl.BlockSpec((None, ts, cw), lambda b, c, t: (b, t, c))
    return pl.pallas_call(
        functools.partial(_lru_conv_kernel, n_lane_blocks=nb, shift=shift),
        grid=(B, nblk, S // ts),
        in_specs=cols + [par(lru_conv_w.shape[1]), par(1), hw, par(1), hw, par(1), par(1),
                         par(sc_conv_w.shape[1])],
        out_specs=[out, out],
        out_shape=[jax.ShapeDtypeStruct((B, S, C), BF16)] * 2,
        scratch_shapes=[pltpu.VMEM((ts, cw), F32), pltpu.VMEM((ts, cw), F32), pltpu.VMEM((1, cw), F32)],
        compiler_params=pltpu.CompilerParams(
            dimension_semantics=("parallel", "parallel", "arbitrary"),
            vmem_limit_bytes=VMEM_LIMIT_BYTES),
        name="lru_conv",
    )(*([z] * len(cols)), lru_conv_w, flat(lru_conv_b), lru_w_a, flat(lru_b_a), lru_w_i, flat(lru_b_i),
      flat(lru_lambda), sc_conv_w)


XA_TM = 256


def _xattn_kernel(x_ref, g_ref, wq_ref, k_ref, v_ref, wo_ref, o_ref, *, scale):
    x = x_ref[...]
    xn = (x * lax.rsqrt(jnp.mean(x * x, axis=-1, keepdims=True) + NORM_EPS) * g_ref[...]).astype(BF16)
    q = jnp.dot(xn, wq_ref[...], preferred_element_type=F32).astype(BF16)
    heads = []
    for hh in range(XA_HEADS):
        cs = slice(hh * HEAD_DIM, (hh + 1) * HEAD_DIM)
        s = lax.dot_general(q[:, cs], k_ref[:, cs], (((1,), (1,)), ((), ())),
                            preferred_element_type=F32) * scale
        e = jnp.exp(s - jnp.max(s, axis=-1, keepdims=True))
        p = (e / jnp.sum(e, axis=-1, keepdims=True)).astype(BF16)
        heads.append(jnp.dot(p, v_ref[:, cs], preferred_element_type=F32))
    o = jnp.concatenate(heads, axis=-1).astype(BF16)
    o_ref[...] = x + jnp.dot(o, wo_ref[...], preferred_element_type=F32)


def cross_attention_pallas(x, gain, k, v, w_q, w_o):
    B, S, D = x.shape
    M = k.shape[1]
    tm = XA_TM
    per_b = S // tm
    kv = pl.BlockSpec((None, M, XA_WIDTH), lambda i: (i // per_b, 0, 0))
    return pl.pallas_call(
        functools.partial(_xattn_kernel, scale=HEAD_DIM ** -0.5),
        grid=(B * per_b,),
        in_specs=[pl.BlockSpec((tm, D), lambda i: (i, 0)),
                  pl.BlockSpec((1, D), lambda i: (0, 0)),
                  pl.BlockSpec((D, XA_WIDTH), lambda i: (0, 0)),
                  kv, kv,
                  pl.BlockSpec((XA_WIDTH, D), lambda i: (0, 0))],
        out_specs=pl.BlockSpec((tm, D), lambda i: (i, 0)),
        out_shape=jax.ShapeDtypeStruct((B * S, D), F32),
        compiler_params=pltpu.CompilerParams(
            dimension_semantics=("parallel",), vmem_limit_bytes=VMEM_LIMIT_BYTES),
        name="cross_attention",
    )(x.reshape(B * S, D), gain.astype(F32)[None, :], w_q, k, v, w_o).reshape(B, S, D)


MASK_BIG = 1e30
LANES = 128


def _first_index_topk_mask_t(work, rowf, k):
    n = work.shape[0]
    sel = jnp.zeros(work.shape, F32)
    for _ in range(k):
        mx = jnp.max(work, axis=0, keepdims=True)
        idx = jnp.min(jnp.where(work == mx, rowf, float(n)), axis=0, keepdims=True)
        pick = rowf == idx
        sel = jnp.where(pick, 1.0, sel)
        work = jnp.where(pick, -jnp.inf, work)
    return sel


ATT_TK = 1024
MOBA_HEADS_PER_STEP = 8
M_INIT = -1e38
NT_DIMS = (((1,), (1,)), ((), ()))


def _flash_update(q_aug, k_aug, v_aug, m_ref, acc_ref, causal):
    s = lax.dot_general(q_aug, k_aug, NT_DIMS, preferred_element_type=F32)
    if causal is not None:
        q0, k0 = causal
        row = lax.broadcasted_iota(jnp.int32, s.shape, 0)
        col = lax.broadcasted_iota(jnp.int32, s.shape, 1)
        s = jnp.where(k0 + col <= q0 + row, s, NEG)
    m_prev = m_ref[...]
    m_new = jnp.maximum(m_prev, jnp.max(s, axis=-1, keepdims=True))
    alpha = jnp.exp(m_prev - m_new)
    p = jnp.exp(s - m_new).astype(BF16)
    acc_ref[...] = alpha * acc_ref[...] + jnp.dot(p, v_aug, preferred_element_type=F32)
    m_ref[...] = m_new


def _moba_kernel(q_ref, k_ref, v_ref, o_ref, kmean_ref, qa_ref, m_ref, acc_ref):
    i = pl.program_id(2)
    bs = MOBA_BLOCK
    seq = k_ref.shape[0]
    hp = q_ref.shape[1] // HEAD_DIM
    nt = NT_DIMS
    hcol = lambda h: slice(h * HEAD_DIM, (h + 1) * HEAD_DIM)

    @pl.when(i == 0)
    def _():
        row = lax.broadcasted_iota(jnp.int32, (LANES, seq), 0)
        col = lax.broadcasted_iota(jnp.int32, (LANES, seq), 1)
        ind = jnp.where(col // bs == row, 1.0, 0.0).astype(BF16)
        kmean_ref[...] = (jnp.dot(ind, k_ref[...], preferred_element_type=F32)
                          * (1.0 / bs)).astype(BF16)

    nb = seq // bs
    nbp = -(-nb // 8) * 8
    blk_t = lax.broadcasted_iota(jnp.int32, (nbp, bs), 0)
    past_t = blk_t < i
    for h in range(hp):
        q = q_ref[:, hcol(h)]
        gate_t = lax.dot_general(kmean_ref[0:nbp, hcol(h)], q, nt, preferred_element_type=F32)
        sel_t = _first_index_topk_mask_t(jnp.where(past_t, gate_t, NEG), blk_t.astype(F32), MOBA_TOPK)
        notsel_t = jnp.where(((sel_t > 0.0) & past_t) | (blk_t == i), 0.0, 1.0)
        if nbp < LANES:
            notsel_t = jnp.concatenate([notsel_t, jnp.ones((LANES - nbp, bs), F32)], axis=0)
        qa_ref[h] = jnp.concatenate([q, notsel_t.T.astype(BF16)], axis=-1)

    tk = ATT_TK
    blk_per_tile = tk // bs
    n_tiles = (i + blk_per_tile) // blk_per_tile
    key_blk = lax.broadcasted_iota(jnp.int32, (tk, LANES), 0) // bs
    key_col = lax.broadcasted_iota(jnp.int32, (tk, LANES), 1)
    ones_v = jnp.ones((tk, LANES), BF16)
    m_ref[...] = jnp.full(m_ref.shape, M_INIT, F32)
    acc_ref[...] = jnp.zeros_like(acc_ref)

    def tile(jt, causal):
        off = pl.multiple_of(jt * tk, tk)
        bias = jnp.where(key_col == jt * blk_per_tile + key_blk, -MASK_BIG, 0.0).astype(BF16)
        for h in range(hp):
            k_aug = jnp.concatenate([k_ref[pl.ds(off, tk), hcol(h)], bias], axis=-1)
            v_aug = jnp.concatenate([v_ref[pl.ds(off, tk), hcol(h)], ones_v], axis=-1)
            _flash_update(qa_ref[h], k_aug, v_aug, m_ref.at[h], acc_ref.at[h],
                          (i * bs, off) if causal else None)

    def body(jt, carry):
        tile(jt, False)
        return carry

    lax.fori_loop(0, n_tiles - 1, body, 0)
    tile(n_tiles - 1, True)
    for h in range(hp):
        acc = acc_ref[h]
        o_ref[:, hcol(h)] = (acc[:, :HEAD_DIM] / acc[:, HEAD_DIM:]).astype(o_ref.dtype)


def moba_attention_pallas(zq, zk, zv, *, n_heads, q_off=0, k_off=0, v_off=0):
    B, S, _ = zq.shape
    bs = MOBA_BLOCK
    hp = min(MOBA_HEADS_PER_STEP, n_heads)
    hw = hp * HEAD_DIM
    assert S % ATT_TK == 0 and S // bs <= LANES and n_heads % hp == 0
    assert q_off % hp == 0 and k_off % hp == 0 and v_off % hp == 0
    return pl.pallas_call(
        _moba_kernel,
        grid=(B, n_heads // hp, S // bs),
        in_specs=[pl.BlockSpec((None, bs, hw), lambda b, h, i: (b, i, q_off // hp + h)),
                  pl.BlockSpec((None, S, hw), lambda b, h, i: (b, 0, k_off // hp + h)),
                  pl.BlockSpec((None, S, hw), lambda b, h, i: (b, 0, v_off // hp + h))],
        out_specs=pl.BlockSpec((None, bs, hw), lambda b, h, i: (b, i, h)),
        out_shape=jax.ShapeDtypeStruct((B, S, n_heads * HEAD_DIM), BF16),
        scratch_shapes=[pltpu.VMEM((LANES, hw), BF16), pltpu.VMEM((hp, bs, 2 * HEAD_DIM), BF16),
                        pltpu.VMEM((hp, bs, 1), F32), pltpu.VMEM((hp, bs, 2 * HEAD_DIM), F32)],
        compiler_params=pltpu.CompilerParams(
            dimension_semantics=("parallel", "parallel", "arbitrary"),
            vmem_limit_bytes=VMEM_LIMIT_BYTES),
        name="moba_attention",
    )(zq, zk, zv)


NSA_TQ = 256
CMP_ROW = NSA_CMP_STRIDE * HEAD_DIM


def _nsa_compress_kernel(x_ref, pe_ref, w1_ref, w2_ref, o_ref):
    x = x_ref[...].astype(F32)
    top = (x + pe_ref[0:1, :]).astype(BF16)
    bot = (x + pe_ref[1:2, :]).astype(BF16)
    a = jnp.dot(top, w1_ref[0:CMP_ROW, :], preferred_element_type=F32)
    b = jnp.dot(bot, w1_ref[CMP_ROW:2 * CMP_ROW, :], preferred_element_type=F32)
    pre = a + pltpu.roll(b, b.shape[0] - 1, 0)
    hid = jax.nn.gelu(pre)
    o_ref[...] = jnp.dot(hid.astype(BF16), w2_ref[...], preferred_element_type=F32).astype(o_ref.dtype)


def nsa_compress_pallas(x, pe, w1, w2):
    B, S, gw = x.shape
    G = gw // HEAD_DIM
    nrow = S // NSA_CMP_STRIDE
    xr = x.reshape(B, nrow, NSA_CMP_STRIDE, G, HEAD_DIM).transpose(0, 3, 1, 2, 4).reshape(B, G, nrow, CMP_ROW)
    pe2 = pe.astype(F32).reshape(2, CMP_ROW)
    return pl.pallas_call(
        _nsa_compress_kernel,
        grid=(B, G),
        in_specs=[pl.BlockSpec((None, None, nrow, CMP_ROW), lambda b, g: (b, g, 0, 0)),
                  pl.BlockSpec((2, CMP_ROW), lambda b, g: (0, 0)),
                  pl.BlockSpec((2 * CMP_ROW, HEAD_DIM), lambda b, g: (0, 0)),
                  pl.BlockSpec((HEAD_DIM, HEAD_DIM), lambda b, g: (0, 0))],
        out_specs=pl.BlockSpec((None, None, nrow, HEAD_DIM), lambda b, g: (b, g, 0, 0)),
        out_shape=jax.ShapeDtypeStruct((B, G, nrow, HEAD_DIM), BF16),
        compiler_params=pltpu.CompilerParams(
            dimension_semantics=("parallel", "parallel"), vmem_limit_bytes=VMEM_LIMIT_BYTES),
        name="nsa_compress",
    )(xr, pe2, w1.astype(BF16), w2.astype(BF16))


def _nsa_kernel(q_ref, kc_ref, vc_ref, ks_ref, vs_ref, kw_ref, vw_ref, g_ref, wmap_ref, o_ref,
                m_ref, acc_ref, out_ref, ns_ref):
    i = pl.program_id(1)
    tq = NSA_TQ
    ng = kc_ref.shape[0]
    R = NSA_HEADS // NSA_KV_HEADS
    nt = NT_DIMS
    row = lax.broadcasted_iota(jnp.int32, (tq, tq), 0)
    col = lax.broadcasted_iota(jnp.int32, (tq, tq), 1)
    gates = jax.nn.sigmoid(g_ref[...])
    gcol = lambda g: slice(g * HEAD_DIM, (g + 1) * HEAD_DIM)

    def qh(g, r):
        return q_ref[:, (g * R + r) * HEAD_DIM:(g * R + r + 1) * HEAD_DIM]

    def gate(g, r, c):
        return gates[:, g * LANES + 3 * r + c:g * LANES + 3 * r + c + 1]

    ncp = kc_ref.shape[1]
    cmask = (lax.broadcasted_iota(jnp.int32, (tq, ncp), 1) * NSA_CMP_STRIDE + (NSA_CMP_BLOCK - 1)
             <= i * tq + lax.broadcasted_iota(jnp.int32, (tq, ncp), 0))
    ns = ks_ref.shape[0] // NSA_SEL_BLOCK
    blk_t = lax.broadcasted_iota(jnp.int32, (ns, tq), 0)
    q_blk_t = (i * tq + lax.broadcasted_iota(jnp.int32, (ns, tq), 1)) // NSA_SEL_BLOCK
    valid_t = blk_t <= q_blk_t
    forced_t = (blk_t == 0) | (blk_t >= q_blk_t - 1)
    for g in range(ng):
        imp = jnp.zeros((tq, LANES), F32)
        vc_aug = jnp.concatenate([vc_ref[g], jnp.ones((ncp, LANES), BF16), wmap_ref[...]], axis=-1)
        for r in range(R):
            s = lax.dot_general(qh(g, r), kc_ref[g], nt, preferred_element_type=F32)
            s = jnp.where(cmask, s, NEG)
            e = jnp.where(cmask, jnp.exp(s - jnp.max(s, axis=-1, keepdims=True)), 0.0).astype(BF16)
            acc = jnp.dot(e, vc_aug, preferred_element_type=F32)
            l = acc[:, HEAD_DIM:2 * HEAD_DIM]
            inv = jnp.where(l > 0.0, 1.0 / l, 0.0)
            imp = imp + acc[:, 2 * HEAD_DIM:] * inv
            out_ref[g * R + r] = gate(g, r, 0) * (acc[:, :HEAD_DIM] * inv)
        impm_t = jnp.where(valid_t, imp.T[:ns] + jnp.where(forced_t, NSA_FORCE_BONUS, 0.0), NEG)
        sel_t = _first_index_topk_mask_t(impm_t, blk_t.astype(F32), NSA_SEL_TOPK)
        notsel_t = jnp.where((sel_t > 0.0) & valid_t, 0.0, 1.0)
        if ns < LANES:
            notsel_t = jnp.concatenate([notsel_t, jnp.ones((LANES - ns, tq), F32)], axis=0)
        ns_ref[g] = notsel_t.T.astype(BF16)

    ones_w = jnp.ones((tq, LANES), BF16)
    w_tiles = []
    for d in range(NSA_WINDOW // tq + 1):
        off = pl.multiple_of(jnp.maximum(i - d, 0) * tq, tq)
        gone = jnp.where(i >= d, 0, 2 * NSA_WINDOW + tq)
        dist = d * tq + row - col + gone
        w_tiles.append((off, (dist >= 0) & (dist < NSA_WINDOW)))
    for g in range(ng):
        for r in range(R):
            ss = []
            for off, mask in w_tiles:
                s = lax.dot_general(qh(g, r), kw_ref[pl.ds(off, tq), gcol(g)], nt, preferred_element_type=F32)
                ss.append(jnp.where(mask, s, NEG))
            m = jnp.max(ss[0], axis=-1, keepdims=True)
            for s in ss[1:]:
                m = jnp.maximum(m, jnp.max(s, axis=-1, keepdims=True))
            acc = jnp.zeros((tq, 2 * HEAD_DIM), F32)
            for (off, _), s in zip(w_tiles, ss):
                v_aug = jnp.concatenate([vw_ref[pl.ds(off, tq), gcol(g)], ones_w], axis=-1)
                acc = acc + jnp.dot(jnp.exp(s - m).astype(BF16), v_aug, preferred_element_type=F32)
            out_ref[g * R + r] = out_ref[g * R + r] + gate(g, r, 2) * (acc[:, :HEAD_DIM] / acc[:, HEAD_DIM:])

    tk = ATT_TK
    blk_per_tile = tk // NSA_SEL_BLOCK
    n_tiles = (i * tq) // tk + 1
    key_blk = lax.broadcasted_iota(jnp.int32, (tk, LANES), 0) // NSA_SEL_BLOCK
    key_col = lax.broadcasted_iota(jnp.int32, (tk, LANES), 1)
    ones_v = jnp.ones((tk, LANES), BF16)
    m_ref[...] = jnp.full(m_ref.shape, M_INIT, F32)
    acc_ref[...] = jnp.zeros_like(acc_ref)

    def sel_tile(jt, causal):
        off = pl.multiple_of(jt * tk, tk)
        bias = jnp.where(key_col == jt * blk_per_tile + key_blk, -MASK_BIG, 0.0).astype(BF16)
        for g in range(ng):
            k_aug = jnp.concatenate([ks_ref[pl.ds(off, tk), gcol(g)], bias], axis=-1)
            v_aug = jnp.concatenate([vs_ref[pl.ds(off, tk), gcol(g)], ones_v], axis=-1)
            for r in range(R):
                q_aug = jnp.concatenate([qh(g, r), ns_ref[g]], axis=-1)
                _flash_update(q_aug, k_aug, v_aug, m_ref.at[g * R + r], acc_ref.at[g * R + r],
                              (i * tq, off) if causal else None)

    def body(jt, carry):
        sel_tile(jt, False)
        return carry

    lax.fori_loop(0, n_tiles - 1, body, 0)
    sel_tile(n_tiles - 1, True)
    for g in range(ng):
        for r in range(R):
            h = g * R + r
            acc = acc_ref[h]
            o_s = acc[:, :HEAD_DIM] / acc[:, HEAD_DIM:]
            o_ref[:, h * HEAD_DIM:(h + 1) * HEAD_DIM] = (out_ref[h] + gate(g, r, 1) * o_s).astype(o_ref.dtype)


def nsa_attention_pallas(zq, kc, vc, zks, zvs, zkw, zvw, gate_logits, *, q_off, ks_off, vs_off, kw_off, vw_off):
    B, S, _ = zq.shape
    G, tq = NSA_KV_HEADS, NSA_TQ
    R = NSA_HEADS // G
    assert S % ATT_TK == 0 and ATT_TK % tq == 0 and S // NSA_SEL_BLOCK <= LANES
    nc = (S - NSA_CMP_BLOCK) // NSA_CMP_STRIDE + 1
    ncp = kc.shape[2]
    ns = S // NSA_SEL_BLOCK
    wmap = jnp.pad(cmp_to_sel_weights(nc, ns), ((0, ncp - nc), (0, LANES - ns))).astype(BF16)
    H = NSA_HEADS
    qw = H * HEAD_DIM
    assert q_off % H == 0 and all(off % G == 0 for off in (ks_off, vs_off, kw_off, vw_off))
    kv_spec = lambda off: pl.BlockSpec((None, S, G * HEAD_DIM), lambda b, i: (b, 0, off // G))
    c_spec = pl.BlockSpec((None, G, ncp, HEAD_DIM), lambda b, i: (b, 0, 0, 0))
    return pl.pallas_call(
        _nsa_kernel,
        grid=(B, S // tq),
        in_specs=[pl.BlockSpec((None, tq, qw), lambda b, i: (b, i, q_off // H)),
                  c_spec, c_spec, kv_spec(ks_off), kv_spec(vs_off), kv_spec(kw_off), kv_spec(vw_off),
                  pl.BlockSpec((None, tq, G * LANES), lambda b, i: (b, i, 0)),
                  pl.BlockSpec((ncp, LANES), lambda b, i: (0, 0))],
        out_specs=pl.BlockSpec((None, tq, qw), lambda b, i: (b, i, 0)),
        out_shape=jax.ShapeDtypeStruct((B, S, qw), BF16),
        scratch_shapes=[pltpu.VMEM((H, tq, 1), F32), pltpu.VMEM((H, tq, 2 * HEAD_DIM), F32),
                        pltpu.VMEM((H, tq, HEAD_DIM), F32), pltpu.VMEM((G, tq, LANES), BF16)],
        compiler_params=pltpu.CompilerParams(
            dimension_semantics=("parallel", "arbitrary"),
            vmem_limit_bytes=VMEM_LIMIT_BYTES),
        name="nsa_attention",
    )(zq, kc, vc, zks, zvs, zkw, zvw, gate_logits, wmap)


MOE_TM = 256
MOE_TN = 512
ROUTER_TM = 256


def _router_kernel(x_ref, g_ref, w_ref, b_ref, t_ref, ids_ref, wts_ref):
    x = x_ref[...]
    t = x * lax.rsqrt(jnp.mean(x * x, axis=-1, keepdims=True) + NORM_EPS) * g_ref[...]
    t_ref[...] = t
    logits = jnp.dot(t.astype(BF16), w_ref[...], preferred_element_type=F32) + b_ref[...]
    coli = lax.broadcasted_iota(jnp.int32, logits.shape, 1)
    colf = coli.astype(F32)
    first = lambda hit: jnp.min(jnp.where(hit, colf, float(LANES)), axis=-1, keepdims=True)
    is_g = coli < N_GROUPS
    gl = jnp.where(is_g, logits, -jnp.inf)
    gmax = jnp.max(gl, axis=-1, keepdims=True)
    g_sel = first(gl == gmax)
    p_g = 1.0 / jnp.sum(jnp.where(is_g, jnp.exp(logits - gmax), 0.0), axis=-1, keepdims=True)
    lo = N_GROUPS + EXPERTS_PER_GROUP * g_sel
    el = jnp.where((colf >= lo) & (colf < lo + EXPERTS_PER_GROUP), logits, -jnp.inf)
    v1 = jnp.max(el, axis=-1, keepdims=True)
    i1 = first(el == v1)
    el2 = jnp.where(colf == i1, -jnp.inf, el)
    v2 = jnp.max(el2, axis=-1, keepdims=True)
    i2 = first(el2 == v2)
    e = jnp.exp(v2 - v1)
    w1 = p_g / (1.0 + e)
    w2 = p_g * e / (1.0 + e)
    ids_ref[...] = jnp.where(coli == 0, i1 - N_GROUPS, jnp.where(coli == 1, i2 - N_GROUPS, 0.0)).astype(jnp.int32)
    wts_ref[...] = jnp.where(coli == 0, w1, jnp.where(coli == 1, w2, 0.0))


def moe_router_pallas(x2, gain, w_group, b_group, w_expert, b_expert):
    T, D = x2.shape
    tm = ROUTER_TM
    nr = N_GROUPS + N_EXPERTS
    w = jnp.pad(jnp.concatenate([w_group, w_expert], axis=1), ((0, 0), (0, LANES - nr))).astype(BF16)
    b = jnp.pad(jnp.concatenate([b_group, b_expert]), (0, LANES - nr)).astype(F32)[None, :]
    t, ids, wts = pl.pallas_call(
        _router_kernel,
        grid=(T // tm,),
        in_specs=[pl.BlockSpec((tm, D), lambda i: (i, 0)),
                  pl.BlockSpec((1, D), lambda i: (0, 0)),
                  pl.BlockSpec((D, LANES), lambda i: (0, 0)),
                  pl.BlockSpec((1, LANES), lambda i: (0, 0))],
        out_specs=[pl.BlockSpec((tm, D), lambda i: (i, 0)),
                   pl.BlockSpec((tm, LANES), lambda i: (i, 0)),
                   pl.BlockSpec((tm, LANES), lambda i: (i, 0))],
        out_shape=[jax.ShapeDtypeStruct((T, D), F32),
                   jax.ShapeDtypeStruct((T, LANES), jnp.int32),
                   jax.ShapeDtypeStruct((T, LANES), F32)],
        compiler_params=pltpu.CompilerParams(
            dimension_semantics=("parallel",), vmem_limit_bytes=VMEM_LIMIT_BYTES),
        name="moe_router",
    )(x2, gain.astype(F32)[None, :], w, b)
    return t, ids[:, :TOPK_IN_GROUP], wts[:, :TOPK_IN_GROUP]


MOE_DMA_GROUP = 8


def _moe_up_kernel(tile_e_ref, n_used_ref, tile_rows_ref, row_tok_ref, t_hbm, rw_ref, wg_ref, wu_ref, hid_ref,
                   xw_ref, xb_ref, sem):
    k = pl.program_id(0)
    tm = xb_ref.shape[0]
    n_used = n_used_ref[0]

    def row_copy(tok, r):
        return pltpu.make_async_copy(t_hbm.at[pl.ds(tok, 1), :], xw_ref.at[pl.ds(r, 1), :], sem.at[0])

    def n_groups(tile):
        return (tile_rows_ref[tile] + MOE_DMA_GROUP - 1) // MOE_DMA_GROUP

    def start_gather(tile):
        def issue(g, carry):
            for j in range(MOE_DMA_GROUP):
                r = g * MOE_DMA_GROUP + j
                row_copy(row_tok_ref[tile * tm + r], r).start()
            return carry
        lax.fori_loop(0, n_groups(tile), issue, 0)

    @pl.when(k == 0)
    def _():
        xw_ref[...] = jnp.zeros_like(xw_ref)
        start_gather(0)

    @pl.when(k < n_used)
    def _():
        def wait_group(g, carry):
            for j in range(MOE_DMA_GROUP):
                row_copy(0, g * MOE_DMA_GROUP + j).wait()
            return carry
        lax.fori_loop(0, n_groups(k), wait_group, 0)
        xb_ref[...] = xw_ref[...].astype(BF16)

        @pl.when(k + 1 < n_used)
        def _():
            start_gather(k + 1)

        x = xb_ref[...]
        hg = jnp.dot(x, wg_ref[...].astype(BF16), preferred_element_type=F32)
        hu = jnp.dot(x, wu_ref[...].astype(BF16), preferred_element_type=F32)
        hid_ref[...] = (jax.nn.silu(hg) * hu * rw_ref[...]).astype(hid_ref.dtype)

    @pl.when(k >= n_used)
    def _():
        hid_ref[...] = jnp.zeros_like(hid_ref)


def _moe_down_kernel(tile_e_ref, n_used_ref, hid_ref, wd_ref, o_ref):
    k = pl.program_id(0)

    @pl.when(k < n_used_ref[0])
    def _():
        hid = hid_ref[...]
        for c in range(0, o_ref.shape[1], MOE_TN):
            o_ref[:, c:c + MOE_TN] = jnp.dot(hid, wd_ref[:, c:c + MOE_TN].astype(BF16),
                                             preferred_element_type=F32)

    @pl.when(k >= n_used_ref[0])
    def _():
        o_ref[...] = jnp.zeros_like(o_ref)


COMBINE_TC = 128


def _moe_combine_kernel(pos_ref, x_ref, g_ref, y_hbm, o_ref, n_ref, buf, sem):
    i = pl.program_id(0)
    tc = x_ref.shape[0]
    nk = TOPK_IN_GROUP

    def row_copy(slot, a, p):
        return pltpu.make_async_copy(y_hbm.at[pl.ds(p, 1), :], buf.at[slot, pl.ds(a, 1), :], sem.at[slot])

    def start_gather(tile, slot):
        def issue(r, carry):
            for kk in range(nk):
                row_copy(slot, kk * tc + r, pos_ref[(tile * tc + r) * nk + kk]).start(priority=kk % 2)
            return carry
        lax.fori_loop(0, tc, issue, 0, unroll=4)

    slot = i % 2

    @pl.when(i == 0)
    def _():
        start_gather(0, 0)

    def wait_row(a, carry):
        row_copy(slot, a, 0).wait()
        return carry
    lax.fori_loop(0, nk * tc, wait_row, 0, unroll=8)

    @pl.when(i + 1 < pl.num_programs(0))
    def _():
        start_gather(i + 1, 1 - slot)

    v = x_ref[...]
    for kk in range(nk):
        v = v + buf[slot, kk * tc:(kk + 1) * tc, :]
    o_ref[...] = v
    n_ref[...] = (v * lax.rsqrt(jnp.mean(v * v, axis=-1, keepdims=True) + NORM_EPS)
                  * g_ref[...]).astype(n_ref.dtype)


def moe_combine_pallas(x2, y_rows, pos, next_gain, next_dtype):
    T, D = x2.shape
    tc = COMBINE_TC
    row_blk = pl.BlockSpec((tc, D), lambda i, *_: (i, 0))
    return pl.pallas_call(
        _moe_combine_kernel,
        grid_spec=pltpu.PrefetchScalarGridSpec(
            num_scalar_prefetch=1,
            grid=(T // tc,),
            in_specs=[row_blk, pl.BlockSpec((1, D), lambda i, *_: (0, 0)),
                      pl.BlockSpec(memory_space=pl.ANY)],
            out_specs=[row_blk, row_blk],
            scratch_shapes=[pltpu.VMEM((2, TOPK_IN_GROUP * tc, D), F32),
                            pltpu.SemaphoreType.DMA((2,))]),
        out_shape=[jax.ShapeDtypeStruct((T, D), F32), jax.ShapeDtypeStruct((T, D), next_dtype)],
        compiler_params=pltpu.CompilerParams(
            dimension_semantics=("arbitrary",), vmem_limit_bytes=VMEM_LIMIT_BYTES),
        name="moe_combine",
    )(pos.reshape(-1), x2, next_gain.astype(F32)[None, :], y_rows)


def moe_experts_pallas(t_packed, row_tok, row_w, tile_e, n_used, tile_rows, w_gate, w_up, w_down, layer):
    P = row_tok.shape[0]
    D = w_gate.shape[-2]
    F = w_gate.shape[-1]
    tm = MOE_TM
    l = int(layer)
    w_in = pl.BlockSpec((None, None, D, F), lambda k, te, *_: (l, te[k], 0, 0))
    hid = pl.pallas_call(
        _moe_up_kernel,
        grid_spec=pltpu.PrefetchScalarGridSpec(
            num_scalar_prefetch=4,
            grid=(P // tm,),
            in_specs=[pl.BlockSpec(memory_space=pl.ANY),
                      pl.BlockSpec((tm, 1), lambda k, *_: (k, 0)),
                      w_in, w_in],
            out_specs=pl.BlockSpec((tm, F), lambda k, *_: (k, 0)),
            scratch_shapes=[pltpu.VMEM((tm, D), F32), pltpu.VMEM((tm, D), BF16),
                            pltpu.SemaphoreType.DMA((1,))]),
        out_shape=jax.ShapeDtypeStruct((P, F), BF16),
        compiler_params=pltpu.CompilerParams(
            dimension_semantics=("arbitrary",), vmem_limit_bytes=VMEM_LIMIT_BYTES),
        name="moe_up",
    )(tile_e, n_used, tile_rows, row_tok, t_packed, row_w, w_gate, w_up)
    return pl.pallas_call(
        _moe_down_kernel,
        grid_spec=pltpu.PrefetchScalarGridSpec(
            num_scalar_prefetch=2,
            grid=(P // tm,),
            in_specs=[pl.BlockSpec((tm, F), lambda k, *_: (k, 0)),
                      pl.BlockSpec((None, None, F, D), lambda k, te, *_: (l, te[k], 0, 0))],
            out_specs=pl.BlockSpec((tm, D), lambda k, *_: (k, 0))),
        out_shape=jax.ShapeDtypeStruct((P, D), F32),
        compiler_params=pltpu.CompilerParams(
            dimension_semantics=("parallel",), vmem_limit_bytes=VMEM_LIMIT_BYTES),
        name="moe_down",
    )(tile_e, n_used, hid, w_down)


def moe_dispatch_plan(ids, wts):
    T, K = ids.shape
    n = T * K
    P = n + N_EXPERTS * MOE_TM
    flat = ids.reshape(n)
    onehot = (flat[:, None] == jnp.arange(N_EXPERTS, dtype=jnp.int32)[None, :]).astype(jnp.int32)
    csum = jnp.cumsum(onehot, axis=0)
    counts = csum[-1]
    padded = (counts + MOE_TM - 1) // MOE_TM * MOE_TM
    pend = jnp.cumsum(padded)
    pstart = pend - padded
    pos = jnp.sum(onehot * (csum - 1 + pstart[None, :]), axis=1)
    tok = (jnp.arange(n, dtype=jnp.int32) // K).astype(F32)
    info = jnp.zeros((P, 2), F32).at[pos].set(jnp.stack([tok, wts.reshape(n)], axis=1))
    row_tok = info[:, 0].astype(jnp.int32)
    row_w = info[:, 1:2]
    tile_start = jnp.arange(P // MOE_TM, dtype=jnp.int32) * MOE_TM
    tile_e = jnp.minimum(jnp.sum((pend[None, :] <= tile_start[:, None]).astype(jnp.int32), axis=1),
                         N_EXPERTS - 1).astype(jnp.int32)
    n_used = (pend[-1] // MOE_TM).astype(jnp.int32)[None]
    e_hot = (tile_e[:, None] == jnp.arange(N_EXPERTS, dtype=jnp.int32)[None, :]).astype(jnp.int32)
    in_expert = tile_start - jnp.sum(e_hot * pstart[None, :], axis=1)
    tile_rows = jnp.clip(jnp.sum(e_hot * counts[None, :], axis=1) - in_expert, 0, MOE_TM).astype(jnp.int32)
    tile_rows = jnp.where(tile_start < pend[-1], tile_rows, 0)
    return pos.reshape(T, K), row_tok, row_w, tile_e, n_used, tile_rows


def hierarchical_moe_pallas(x, gain, w_group, b_group, w_expert, b_expert, w_gate, w_up, w_down, layer,
                            next_gain, next_dtype):
    B, S, D = x.shape
    x2 = x.reshape(B * S, D)
    t, ids, wts = moe_router_pallas(x2, gain, w_group, b_group, w_expert, b_expert)
    pos, row_tok, row_w, tile_e, n_used, tile_rows = moe_dispatch_plan(ids, wts)
    y = moe_experts_pallas(t, row_tok, row_w, tile_e, n_used, tile_rows, w_gate, w_up, w_down, layer)
    out, normed = moe_combine_pallas(x2, y, pos, next_gain, next_dtype)
    return out.reshape(B, S, D), normed


def rope_tables(positions):
    inv = ROPE_THETA ** (-jnp.arange(0, HEAD_DIM, 2, dtype=F32) / HEAD_DIM)
    ang = positions.astype(F32)[..., None] * inv
    return jnp.cos(ang)[:, :, None, :], jnp.sin(ang)[:, :, None, :]


def cmp_to_sel_weights(nc, ns):
    r = NSA_SEL_BLOCK // NSA_CMP_STRIDE
    m = NSA_CMP_BLOCK // NSA_CMP_STRIDE
    c = jnp.arange(nc)[:, None] - r * jnp.arange(ns)[None, :]
    w = jnp.minimum(jnp.minimum(c + 1, r + m - 1 - c), min(r, m))
    return jnp.clip(w, 0, None).astype(F32)


def kernel(x, mem, positions, norm_mix, w_mix_in, lru_conv_w, lru_conv_b, lru_w_a, lru_b_a, lru_w_i, lru_b_i, lru_lambda, sc_conv_w, nsa_pe_k, nsa_w1_k, nsa_w2_k, nsa_pe_v, nsa_w1_v, nsa_w2_v, w_merge_gate, b_merge_gate, w_branch_out, w_mix_out, norm_xattn, norm_mem, xa_w_q, xa_w_k, xa_w_v, xa_w_o, norm_moe, moe_w_group, moe_b_group, moe_w_expert, moe_b_expert, moe_w_gate, moe_w_up, moe_w_down, norm_final):
    B, S, D = x.shape
    T = B * S
    cos, sin = rope_tables(positions)
    cos_t = jnp.concatenate([cos, cos], axis=-1).reshape(T, HEAD_DIM)
    sin_t = jnp.concatenate([-sin, sin], axis=-1).reshape(T, HEAD_DIM)
    offs = [int(v) for v in np.concatenate([[0], np.cumsum(MIX_SPLITS)])]
    blk = [v // HEAD_DIM for v in offs[:11]]
    n_head_cols = offs[10]
    rope_flags = np.zeros((n_head_cols // HEAD_DIM,), np.int32)
    for k in (1, 4, 6, 8):
        rope_flags[blk[k]:blk[k + 1]] = ROPE_PLAIN
    for k in (0, 3):
        rope_flags[blk[k]:blk[k + 1]] = ROPE_SCALED
    rope_flags = jnp.asarray(rope_flags)
    G, R = NSA_KV_HEADS, NSA_HEADS // NSA_KV_HEADS
    w_in_nk = jnp.swapaxes(w_mix_in, 1, 2)
    assert offs[10] % 8 == 0 and offs[11] % 8 == 0
    M = mem.shape[1]
    h = rms_norm_pallas(x.reshape(T, D), norm_mix[0], BF16)
    for l in range(DEPTH):
        x2 = x.reshape(T, D)
        z_head = matmul(h, w_in_nk, w_lead=(l,), w_is_nk=True, n_cols=n_head_cols, out_dtype=BF16,
                        rope=(cos_t, sin_t, rope_flags)).reshape(B, S, n_head_cols)
        z_tail = matmul(h, w_in_nk, w_lead=(l,), w_is_nk=True, row0=offs[11],
                        n_cols=offs[16] - offs[11]).reshape(B, S, -1)
        z_g = matmul(h, w_in_nk, w_lead=(l,), w_is_nk=True, row0=offs[10], n_cols=LANES)
        z_gate = jnp.pad(z_g[:, :NSA_GATE_W].reshape(B, S, G, R * 3),
                         ((0, 0), (0, 0), (0, 0), (0, LANES - R * 3))).reshape(B, S, G * LANES)
        o_moba = moba_attention_pallas(z_head, z_head, z_head, n_heads=MOBA_HEADS,
                                       q_off=blk[0], k_off=blk[1], v_off=blk[2])
        kc = nsa_compress_pallas(z_head[:, :, offs[4]:offs[5]], nsa_pe_k[l], nsa_w1_k[l], nsa_w2_k[l])
        vc = nsa_compress_pallas(z_head[:, :, offs[5]:offs[6]], nsa_pe_v[l], nsa_w1_v[l], nsa_w2_v[l])
        o_nsa = nsa_attention_pallas(z_head, kc, vc, z_head, z_head, z_head, z_head, z_gate,
                                     q_off=blk[3], ks_off=blk[6], vs_off=blk[7], kw_off=blk[8], vw_off=blk[9])
        o_lru, o_conv = lru_conv_pallas(z_tail, l, lru_conv_w, lru_conv_b, lru_w_a, lru_b_a,
                                        lru_w_i, lru_b_i, lru_lambda, sc_conv_w)
        o_all = jnp.stack([o_moba, o_lru, o_conv, o_nsa]).reshape(N_BRANCH, T, -1)
        merged = merge_branches_pallas(h, o_all, w_merge_gate, b_merge_gate, w_branch_out, l)
        x2 = matmul(merged, w_mix_out, w_lead=(l,), residual=x2)
        mem_n = rms_norm_pallas(mem.reshape(B * M, D), norm_mem[l], BF16)
        xk = matmul(mem_n, xa_w_k, w_lead=(l,), out_dtype=BF16).reshape(B, M, XA_WIDTH)
        xv = matmul(mem_n, xa_w_v, w_lead=(l,), out_dtype=BF16).reshape(B, M, XA_WIDTH)
        x = cross_attention_pallas(x2.reshape(B, S, D), norm_xattn[l], xk, xv,
                                   xa_w_q[l].astype(BF16), xa_w_o[l].astype(BF16))
        last = l == DEPTH - 1
        x, h = hierarchical_moe_pallas(x, norm_moe[l], moe_w_group[l], moe_b_group[l],
                                       moe_w_expert[l], moe_b_expert[l], moe_w_gate,
                                       moe_w_up, moe_w_down, l,
                                       norm_final if last else norm_mix[l + 1], F32 if last else BF16)
    return h.reshape(B, S, D)
```
